```python
import math
import jax, jax.numpy as jnp
from jax import lax
import numpy as np

D_MODEL = 1024
BATCH = 32
SEQ = 256
DEPTH = 2
DEC_BATCH = 4
DEC_SEQ = 1024
PAST_LEN = 256

GRID_W = 64
HY_WIDTH = D_MODEL // 4
CF_WIDTH = D_MODEL // 4
ATT_WIDTH = D_MODEL // 2
DIFF_HEAD_DIM = 64
DIFF_HEADS = ATT_WIDTH // (2 * DIFF_HEAD_DIM)
MIX_WIDTH = HY_WIDTH + CF_WIDTH + ATT_WIDTH
IN_WIDTH = 3 * HY_WIDTH + 2 * CF_WIDTH + 3 * ATT_WIDTH
HY_SHORT_K = 3
HY_FILTER_EMB = 33
HY_FILTER_HIDDEN = 64
HY_FAST_DECAY_PCT = 0.3
HY_SLOW_DECAY_PCT = 1.5
HY_DECAY_TARGET = 1e-2
CF_CONV_K = 31
D_FF = 4 * D_MODEL
ROPE_BASE = 10000.0
AX_DIM = DIFF_HEAD_DIM // 2
DEEPNORM_ALPHA = (2 * DEPTH) ** 0.25
DEEPNORM_BETA = (8 * DEPTH) ** -0.25
DENSE_KEY_LIMIT = 2048
Q_BLOCK = 128
LN_EPS = 1e-5

kernel_name = 'hyena_conformer_diffattn_prefix_dit'


def _layer_norm(x, g, b):
    xf = x.astype(jnp.float32)
    mu = jnp.mean(xf, -1, keepdims=True)
    var = jnp.mean(jnp.square(xf - mu), -1, keepdims=True)
    return ((xf - mu) * lax.rsqrt(var + LN_EPS) * g.astype(jnp.float32) + b.astype(jnp.float32)).astype(x.dtype)


def _rms_norm(x, g):
    xf = x.astype(jnp.float32)
    return xf * lax.rsqrt(jnp.mean(jnp.square(xf), -1, keepdims=True) + LN_EPS) * g.astype(jnp.float32)


def _dwconv(x, w, b):
    K, C = w.shape
    y = lax.conv_general_dilated(x, w.astype(x.dtype)[:, None, :], (1,), [(K // 2, K // 2)],
                                 dimension_numbers=('NWC', 'WIO', 'NWC'), feature_group_count=C)
    return y + b.astype(x.dtype)


def _hyena_filter(L, w1, b1, freq, w2, b2, w3):
    f32 = jnp.float32
    bands = (HY_FILTER_EMB - 1) // 2
    t = jnp.linspace(0.0, 1.0, L, dtype=f32)[:, None]
    w = (2.0 * math.pi / L) * jnp.arange(L, dtype=f32)[:, None]
    fr = jnp.linspace(1e-4, bands - 1, bands, dtype=f32)[None, :]
    feats = jnp.concatenate([t, jnp.cos(fr * w), -jnp.sin(fr * w)], -1)
    fq = freq.astype(f32)
    hid = jnp.sin(fq * (feats @ w1.astype(f32) + b1.astype(f32)))
    hid = jnp.sin(fq * (hid @ w2.astype(f32) + b2.astype(f32)))
    hf = hid @ w3.astype(f32)
    deltas = jnp.abs(jnp.linspace(math.log(HY_DECAY_TARGET) / HY_FAST_DECAY_PCT,
                                  math.log(HY_DECAY_TARGET) / HY_SLOW_DECAY_PCT, HY_WIDTH, dtype=f32))
    decay = jnp.exp(-t * deltas[None, :])
    h_fwd = hf[:, :HY_WIDTH] * decay
    h_bwd = hf[:, HY_WIDTH:] * decay
    return jnp.concatenate([h_fwd, jnp.zeros((1, HY_WIDTH), f32), h_bwd[:0:-1]], 0)


def _long_conv(u, filt, bias):
    L = u.shape[1]
    uf = u.astype(jnp.float32)
    U = jnp.fft.rfft(uf, n=2 * L, axis=1)
    G = jnp.fft.rfft(filt, n=2 * L, axis=0)
    y = jnp.fft.irfft(U * G[None], n=2 * L, axis=1)[:, :L]
    return (y + uf * bias.astype(jnp.float32)).astype(u.dtype)


def _axial_rope_tables(n):
    rows = n // GRID_W
    row = jnp.repeat(jnp.arange(rows, dtype=jnp.float32), GRID_W)
    col = jnp.tile(jnp.arange(GRID_W, dtype=jnp.float32), rows)
    inv = ROPE_BASE ** (-jnp.arange(0, AX_DIM, 2, dtype=jnp.float32) / AX_DIM)
    ang_r = row[:, None] * inv[None, :]
    ang_c = col[:, None] * inv[None, :]
    return (jnp.cos(ang_r), jnp.sin(ang_r), jnp.cos(ang_c), jnp.sin(ang_c))


def _rot_half(x, cos, sin):
    x1, x2 = jnp.split(x, 2, axis=-1)
    cos = cos[None, :, None, :]
    sin = sin[None, :, None, :]
    return jnp.concatenate([x1 * cos - x2 * sin, x2 * cos + x1 * sin], -1)


def _apply_axial_rope(x, tables):
    cr, sr, cc, sc = tables
    xr, xc = jnp.split(x, 2, axis=-1)
    return jnp.concatenate([_rot_half(xr, cr, sr), _rot_half(xc, cc, sc)], -1).astype(x.dtype)


def _diff_attention(q, k, v, lam, lambda_init, subln_g):
    B, Sq = q.shape[:2]
    Sk = k.shape[1]
    f32 = jnp.float32
    qh = q.reshape(B, Sq, DIFF_HEADS, 2, DIFF_HEAD_DIM).astype(f32)
    kh = k.reshape(B, Sk, DIFF_HEADS, 2, DIFF_HEAD_DIM).astype(f32)
    vf = v.astype(f32)
    scale = DIFF_HEAD_DIM ** -0.5

    def attend(qb):
        s = jnp.einsum('bqhmd,bkhmd->bhmqk', qb, kh) * scale
        p = jax.nn.softmax(s, axis=-1)
        a = p[:, :, 0] - lam * p[:, :, 1]
        return jnp.einsum('bhqk,bkhe->bqhe', a, vf)

    if Sk >= DENSE_KEY_LIMIT:
        nb = Sq // Q_BLOCK
        blocks = jnp.swapaxes(qh.reshape(B, nb, Q_BLOCK, DIFF_HEADS, 2, DIFF_HEAD_DIM), 0, 1)
        out = lax.map(attend, blocks)
        out = jnp.swapaxes(out, 0, 1).reshape(B, Sq, DIFF_HEADS, 2 * DIFF_HEAD_DIM)
    else:
        out = attend(qh)
    out = _rms_norm(out, subln_g) * (1.0 - lambda_init)
    return out.reshape(B, Sq, ATT_WIDTH).astype(q.dtype)


def _token_mixers(h, pl, lambda_init, rope, ctx_kv):
    B, L, _ = h.shape
    dt = h.dtype
    z = h @ pl['w_in']
    o1 = 3 * HY_WIDTH
    o2 = o1 + 2 * CF_WIDTH
    hy = z[..., :o1]
    cf = z[..., o1:o2]
    q = z[..., o2:o2 + ATT_WIDTH].reshape(B, L, 2 * DIFF_HEADS, DIFF_HEAD_DIM)
    k = z[..., o2 + ATT_WIDTH:o2 + 2 * ATT_WIDTH].reshape(B, L, 2 * DIFF_HEADS, DIFF_HEAD_DIM)
    v = z[..., o2 + 2 * ATT_WIDTH:].reshape(B, L, DIFF_HEADS, 2 * DIFF_HEAD_DIM)
    hy = _dwconv(hy, pl['hy_conv_w'], pl['hy_conv_b'])
    x0, x1, hv = jnp.split(hy, 3, axis=-1)
    filt = _hyena_filter(L, pl['hf_w1'], pl['hf_b1'], pl['hf_freq'], pl['hf_w2'], pl['hf_b2'], pl['hf_w3'])
    y_hy = x0 * _long_conv(x1 * hv, filt, pl['hy_bias'])
    a, g = jnp.split(cf, 2, axis=-1)
    u = _dwconv(a * jax.nn.sigmoid(g), pl['cf_conv_w'], pl['cf_conv_b'])
    y_cf = jax.nn.silu(_layer_norm(u, pl['cf_ln_g'], pl['cf_ln_b']))
    f32 = jnp.float32
    lam = (jnp.exp(jnp.sum(pl['lam_q1'].astype(f32) * pl['lam_k1'].astype(f32)))
           - jnp.exp(jnp.sum(pl['lam_q2'].astype(f32) * pl['lam_k2'].astype(f32))) + lambda_init)
    if ctx_kv is None:
        k_all, v_all = k, v
    else:
        q = _apply_axial_rope(q, rope)
        k = _apply_axial_rope(k, rope)
        k_all = jnp.concatenate([ctx_kv[0].astype(dt), k], axis=1)
        v_all = jnp.concatenate([ctx_kv[1].astype(dt), v], axis=1)
    y_at = _diff_attention(q, k_all, v_all, lam, lambda_init, pl['subln_g'])
    y = jnp.concatenate([y_hy, y_cf.astype(dt), y_at], axis=-1) @ pl['w_out']
    return y, k, v


def _layer(x, cond, pl, lambda_init, rope=None, ctx_kv=None):
    ada = (jax.nn.silu(cond) @ pl['w_ada'] + pl['b_ada']).astype(x.dtype)
    sh1, sc1, g1, sh2, sc2, g2 = jnp.split(ada[:, None, :], 6, axis=-1)
    h = x * (1 + sc1) + sh1
    y, k, v = _token_mixers(h, pl, lambda_init, rope, ctx_kv)
    x = _layer_norm(DEEPNORM_ALPHA * x + g1 * y, pl['ln1_g'], pl['ln1_b'])
    h = x * (1 + sc2) + sh2
    m = jnp.square(jax.nn.relu(h @ pl['w_mlp1'])) @ pl['w_mlp2']
    x = _layer_norm(DEEPNORM_ALPHA * x + g2 * m, pl['ln2_g'], pl['ln2_b'])
    return x, k, v


def _lambda_init(l):
    return 0.8 - 0.6 * math.exp(-0.3 * l)


def setup_inputs(seed: int = 0) -> dict:
    key = jax.random.key(seed)
    keys = iter(jax.random.split(key, 40))
    nrm = lambda shape, s: jax.random.normal(next(keys), shape, jnp.float32) * s
    D = D_MODEL
    return {
        'x_prompt': nrm((BATCH, SEQ, D), 1.0),
        'x_sample': nrm((DEC_BATCH, DEC_SEQ, D), 1.0),
        'cache_k': nrm((DEC_BATCH, DEPTH, PAST_LEN, 2 * DIFF_HEADS, DIFF_HEAD_DIM), 1.0),
        'cache_v': nrm((DEC_BATCH, DEPTH, PAST_LEN, DIFF_HEADS, 2 * DIFF_HEAD_DIM), 1.0),
        'c': nrm((DEC_BATCH, D), 1.0),
        'c_ctx': nrm((D,), 1.0),
        'w_ada': nrm((DEPTH, D, 6 * D), D ** -0.5),
        'b_ada': nrm((DEPTH, 6 * D), 0.02),
        'w_in': nrm((DEPTH, D, IN_WIDTH), D ** -0.5),
        'hy_conv_w': nrm((DEPTH, HY_SHORT_K, 3 * HY_WIDTH), HY_SHORT_K ** -0.5),
        'hy_conv_b': nrm((DEPTH, 3 * HY_WIDTH), 0.02),
        'hf_w1': nrm((DEPTH, HY_FILTER_EMB, HY_FILTER_HIDDEN), HY_FILTER_EMB ** -0.5),
        'hf_b1': nrm((DEPTH, HY_FILTER_HIDDEN), 0.02),
        'hf_freq': 1.0 + nrm((DEPTH, HY_FILTER_HIDDEN), 0.05),
        'hf_w2': nrm((DEPTH, HY_FILTER_HIDDEN, HY_FILTER_HIDDEN), HY_FILTER_HIDDEN ** -0.5),
        'hf_b2': nrm((DEPTH, HY_FILTER_HIDDEN), 0.02),
        'hf_w3': nrm((DEPTH, HY_FILTER_HIDDEN, 2 * HY_WIDTH), HY_FILTER_HIDDEN ** -0.5),
        'hy_bias': nrm((DEPTH, HY_WIDTH), 0.1),
        'cf_conv_w': nrm((DEPTH, CF_CONV_K, CF_WIDTH), CF_CONV_K ** -0.5),
        'cf_conv_b': nrm((DEPTH, CF_WIDTH), 0.02),
        'cf_ln_g': 1.0 + nrm((DEPTH, CF_WIDTH), 0.02),
        'cf_ln_b': nrm((DEPTH, CF_WIDTH), 0.02),
        'lam_q1': nrm((DEPTH, DIFF_HEAD_DIM), 0.1),
        'lam_k1': nrm((DEPTH, DIFF_HEAD_DIM), 0.1),
        'lam_q2': nrm((DEPTH, DIFF_HEAD_DIM), 0.1),
        'lam_k2': nrm((DEPTH, DIFF_HEAD_DIM), 0.1),
        'subln_g': 1.0 + nrm((DEPTH, 2 * DIFF_HEAD_DIM), 0.02),
        'w_out': nrm((DEPTH, MIX_WIDTH, D), MIX_WIDTH ** -0.5 * DEEPNORM_BETA),
        'ln1_g': 1.0 + nrm((DEPTH, D), 0.02),
        'ln1_b': nrm((DEPTH, D), 0.02),
        'w_mlp1': nrm((DEPTH, D, D_FF), D ** -0.5),
        'w_mlp2': nrm((DEPTH, D_FF, D), D_FF ** -0.5 * DEEPNORM_BETA),
        'ln2_g': 1.0 + nrm((DEPTH, D), 0.02),
        'ln2_b': nrm((DEPTH, D), 0.02),
    }


def reference(x_prompt, x_sample, cache_k, cache_v, c, c_ctx, w_ada, b_ada, w_in,
              hy_conv_w, hy_conv_b, hf_w1, hf_b1, hf_freq, hf_w2, hf_b2, hf_w3, hy_bias,
              cf_conv_w, cf_conv_b, cf_ln_g, cf_ln_b, lam_q1, lam_k1, lam_q2, lam_k2, subln_g,
              w_out, ln1_g, ln1_b, w_mlp1, w_mlp2, ln2_g, ln2_b):
    params = {
        'w_ada': w_ada, 'b_ada': b_ada, 'w_in': w_in,
        'hy_conv_w': hy_conv_w, 'hy_conv_b': hy_conv_b,
        'hf_w1': hf_w1, 'hf_b1': hf_b1, 'hf_freq': hf_freq, 'hf_w2': hf_w2, 'hf_b2': hf_b2,
        'hf_w3': hf_w3, 'hy_bias': hy_bias,
        'cf_conv_w': cf_conv_w, 'cf_conv_b': cf_conv_b, 'cf_ln_g': cf_ln_g, 'cf_ln_b': cf_ln_b,
        'lam_q1': lam_q1, 'lam_k1': lam_k1, 'lam_q2': lam_q2, 'lam_k2': lam_k2, 'subln_g': subln_g,
        'w_out': w_out, 'ln1_g': ln1_g, 'ln1_b': ln1_b,
        'w_mlp1': w_mlp1, 'w_mlp2': w_mlp2, 'ln2_g': ln2_g, 'ln2_b': ln2_b,
    }
    xp = x_prompt
    cond_ctx = c_ctx[None, :]
    ks, vs = [], []
    for l in range(DEPTH):
        pl = {n: a[l] for n, a in params.items()}
        xp, k_l, v_l = _layer(xp, cond_ctx, pl, _lambda_init(l))
        ks.append(k_l)
        vs.append(v_l)
    new_cache_k = jnp.stack(ks, axis=1)
    new_cache_v = jnp.stack(vs, axis=1)
    rope = _axial_rope_tables(x_sample.shape[1])
    xs = x_sample
    for l in range(DEPTH):
        pl = {n: a[l] for n, a in params.items()}
        xs, _, _ = _layer(xs, c, pl, _lambda_init(l), rope, (cache_k[:, l], cache_v[:, l]))
    return (xp, xs, new_cache_k, new_cache_v)
```

```python
import functools
import math

import numpy as np
import jax
import jax.numpy as jnp
from jax import lax
from jax.experimental import pallas as pl
from jax.experimental.pallas import tpu as pltpu

D_MODEL = 1024
DEPTH = 2
GRID_W = 64
HY_WIDTH = D_MODEL // 4
CF_WIDTH = D_MODEL // 4
ATT_WIDTH = D_MODEL // 2
DIFF_HEAD_DIM = 64
DIFF_HEADS = ATT_WIDTH // (2 * DIFF_HEAD_DIM)
HEAD_PAIR = 2 * DIFF_HEAD_DIM
MIX_WIDTH = HY_WIDTH + CF_WIDTH + ATT_WIDTH
IN_WIDTH = 3 * HY_WIDTH + 2 * CF_WIDTH + 3 * ATT_WIDTH
HY_SHORT_K = 3
HY_FILTER_EMB = 33
HY_FILTER_HIDDEN = 64
HY_FAST_DECAY_PCT = 0.3
HY_SLOW_DECAY_PCT = 1.5
HY_DECAY_TARGET = 1e-2
CF_CONV_K = 31
D_FF = 4 * D_MODEL
ROPE_BASE = 10000.0
AX_DIM = DIFF_HEAD_DIM // 2
DEEPNORM_ALPHA = (2 * DEPTH) ** 0.25
LN_EPS = 1e-5
N_ADA = 6

LANES = 128
SUBLANES = 8
COND_ROWS = SUBLANES
VMEM_LIMIT = 56 * 1024 * 1024

ROW_TILE = 512
ADA_COL_TILE = 1536
Q_TILE = 256
CF_ROW_CHUNK = 128

F32 = jnp.float32
BF16 = jnp.bfloat16


def _lambda_init(l):
    return 0.8 - 0.6 * math.exp(-0.3 * l)


def _params(**kw):
    return pltpu.CompilerParams(vmem_limit_bytes=VMEM_LIMIT, **kw)


def _const_spec(shape):
    zeros = (0,) * len(shape)
    return pl.BlockSpec(shape, lambda *_: zeros, pipeline_mode=pl.Buffered(1))


def _dot(a, b):
    return jnp.dot(a, b, preferred_element_type=F32)


def _dot_f32(a, b):
    return jnp.dot(a, b, preferred_element_type=F32, precision=lax.Precision.HIGHEST)


def _layer_norm_rows(x, g, b):
    mu = jnp.mean(x, axis=-1, keepdims=True)
    xc = x - mu
    var = jnp.mean(xc * xc, axis=-1, keepdims=True)
    return xc * lax.rsqrt(var + LN_EPS) * g + b


@functools.lru_cache(maxsize=None)
def _dft_tables(L):
    n = 2 * L
    k = np.arange(L, dtype=np.float64)[:, None]
    s = np.arange(L, dtype=np.float64)[None, :]
    ang = 2.0 * np.pi * k * s / n
    fwd = np.concatenate([np.cos(ang), -np.sin(ang)], axis=0)
    fwd[L, :] = np.cos(np.pi * np.arange(L))
    t = np.arange(L, dtype=np.float64)[:, None]
    kk = np.arange(L, dtype=np.float64)[None, :]
    ang_i = 2.0 * np.pi * t * kk / n
    inv_re = (2.0 / n) * np.cos(ang_i)
    inv_re[:, 0] = 1.0 / n
    inv_im = -(2.0 / n) * np.sin(ang_i)
    inv_im[:, 0] = np.cos(np.pi * np.arange(L)) / n
    inv = np.concatenate([inv_re, inv_im], axis=1)
    return fwd.astype(np.float32), inv.astype(np.float32)


@functools.lru_cache(maxsize=None)
def _filter_tables(L):
    bands = (HY_FILTER_EMB - 1) // 2
    t = np.linspace(0.0, 1.0, L)[:, None]
    w = (2.0 * np.pi / L) * np.arange(L, dtype=np.float64)[:, None]
    fr = np.linspace(1e-4, bands - 1, bands)[None, :]
    feats = np.concatenate([t, np.cos(fr * w), -np.sin(fr * w)], -1)
    feats_p = np.zeros((L, LANES), np.float64)
    feats_p[:, :HY_FILTER_EMB] = feats
    deltas = np.abs(np.linspace(math.log(HY_DECAY_TARGET) / HY_FAST_DECAY_PCT,
                                math.log(HY_DECAY_TARGET) / HY_SLOW_DECAY_PCT, HY_WIDTH))
    decay = np.exp(-t * deltas[None, :])
    return feats_p.astype(np.float32), decay.astype(np.float32)


@functools.lru_cache(maxsize=None)
def _rope_tables(n):
    half = AX_DIM // 2
    pos = np.arange(n)
    row = (pos // GRID_W).astype(np.float64)[:, None]
    col = (pos % GRID_W).astype(np.float64)[:, None]
    inv = ROPE_BASE ** (-np.arange(0, AX_DIM, 2, dtype=np.float64) / AX_DIM)[None, :]
    zero = np.zeros((n, half))
    cr, sr = np.cos(row * inv), np.sin(row * inv)
    cc, sc = np.cos(col * inv), np.sin(col * inv)
    c = np.concatenate([cr, cr, cc, cc], -1)
    s_up = np.concatenate([-sr, zero, -sc, zero], -1)
    s_dn = np.concatenate([zero, sr, zero, sc], -1)
    reps = LANES // DIFF_HEAD_DIM
    tile = lambda a: np.tile(a, (1, reps)).astype(np.float32)
    return tile(c), tile(s_up), tile(s_dn)


def _ada_kernel(cond_ref, w_ref, b_ref, o_ref):
    c = cond_ref[...]
    s = c * jax.nn.sigmoid(c)
    o_ref[0] = _dot(s.astype(BF16), w_ref[0].astype(BF16)) + b_ref[0]


def _ada_call(cond, w_ada, b_ada):
    n_out = N_ADA * D_MODEL
    return pl.pallas_call(
        _ada_kernel,
        grid=(DEPTH, n_out // ADA_COL_TILE),
        in_specs=[
            pl.BlockSpec((COND_ROWS, D_MODEL), lambda l, j: (0, 0)),
            pl.BlockSpec((1, D_MODEL, ADA_COL_TILE), lambda l, j: (l, 0, j)),
            pl.BlockSpec((1, 1, ADA_COL_TILE), lambda l, j: (l, 0, j)),
        ],
        out_specs=pl.BlockSpec((1, COND_ROWS, ADA_COL_TILE), lambda l, j: (l, 0, j)),
        out_shape=jax.ShapeDtypeStruct((DEPTH, COND_ROWS, n_out), F32),
        compiler_params=_params(dimension_semantics=("arbitrary", "arbitrary")),
        name="ada",
    )(cond, w_ada, b_ada.reshape(DEPTH, 1, n_out))


def _rope_lanes(x, c, s_up, s_dn):
    outs = []
    for p in range(ATT_WIDTH // LANES):
        xb = x[:, p * LANES:(p + 1) * LANES]
        up = pltpu.roll(xb, LANES - AX_DIM // 2, 1)
        dn = pltpu.roll(xb, AX_DIM // 2, 1)
        outs.append(xb * c + up * s_up + dn * s_dn)
    return jnp.concatenate(outs, axis=1)


def _inproj_kernel(*refs, rope):
    if rope:
        x_ref, ada_ref, w_ref, c_ref, su_ref, sd_ref, hy_ref, cf_ref, q_ref, k_ref, v_ref = refs
    else:
        x_ref, ada_ref, w_ref, hy_ref, cf_ref, q_ref, k_ref, v_ref = refs
    sh1 = ada_ref[0, 0, 0:1, :]
    sc1 = ada_ref[0, 0, 1:2, :]
    h = (x_ref[...] * (1.0 + sc1) + sh1).astype(BF16)
    o1 = 3 * HY_WIDTH
    o2 = o1 + 2 * CF_WIDTH
    o3 = o2 + ATT_WIDTH
    o4 = o3 + ATT_WIDTH
    hy_ref[...] = _dot(h, w_ref[:, 0:o1])
    cf_ref[...] = _dot(h, w_ref[:, o1:o2])
    q = _dot(h, w_ref[:, o2:o3])
    k = _dot(h, w_ref[:, o3:o4])
    if rope:
        c, su, sd = c_ref[...], su_ref[...], sd_ref[...]
        q = _rope_lanes(q, c, su, sd)
        k = _rope_lanes(k, c, su, sd)
    q_ref[...] = q
    k_ref[...] = k
    v_ref[...] = _dot(h, w_ref[:, o4:IN_WIDTH])


def _inproj_call(x2d, ada4, w_in_bf, layer, seq, cond_row_of_tile, rope_tabs):
    T = x2d.shape[0]
    tiles_per_seq = seq // ROW_TILE if seq >= ROW_TILE else None
    in_specs = [
        pl.BlockSpec((ROW_TILE, D_MODEL), lambda i: (i, 0)),
        pl.BlockSpec((1, 1, N_ADA, D_MODEL), lambda i: (layer, cond_row_of_tile(i), 0, 0)),
        _const_spec((D_MODEL, IN_WIDTH)),
    ]
    args = [x2d, ada4, w_in_bf]
    if rope_tabs is not None:
        for tab in rope_tabs:
            in_specs.append(pl.BlockSpec((ROW_TILE, LANES), lambda i: (i % tiles_per_seq, 0)))
            args.append(tab)
    widths = (3 * HY_WIDTH, 2 * CF_WIDTH, ATT_WIDTH, ATT_WIDTH, ATT_WIDTH)
    return pl.pallas_call(
        functools.partial(_inproj_kernel, rope=rope_tabs is not None),
        grid=(T // ROW_TILE,),
        in_specs=in_specs,
        out_specs=[pl.BlockSpec((ROW_TILE, w), lambda i: (i, 0)) for w in widths],
        out_shape=[jax.ShapeDtypeStruct((T, w), F32) for w in widths],
        compiler_params=_params(dimension_semantics=("arbitrary",)),
        name="inproj",
    )(*args)


def _filter_kernel(feats_ref, w1_ref, b1_ref, fq_ref, w2_ref, b2_ref, w3_ref, decay_ref, fwd_ref,
                   ga_ref, gbc_ref, gd_ref):
    L = feats_ref.shape[0]
    fq = fq_ref[...]
    hid = jnp.sin(fq * (_dot_f32(feats_ref[...], w1_ref[...]) + b1_ref[...]))
    hid = jnp.sin(fq * (_dot_f32(hid, w2_ref[...]) + b2_ref[...]))
    hf = _dot_f32(hid, w3_ref[...])
    decay = decay_ref[...]
    row = lax.broadcasted_iota(jnp.int32, (L, HY_WIDTH), 0)
    h_fwd = hf[:, :HY_WIDTH] * decay
    h_bwd = jnp.where(row == 0, 0.0, hf[:, HY_WIDTH:] * decay)
    spec = _dot_f32(fwd_ref[...], jnp.concatenate([h_fwd, h_bwd], axis=1))
    p = spec[:L]
    q = spec[L:]
    g_re = p[:, :HY_WIDTH] + p[:, HY_WIDTH:]
    g_im = q[:, :HY_WIDTH] - q[:, HY_WIDTH:]
    nyq = q[0:1, :HY_WIDTH] + q[0:1, HY_WIDTH:]
    ga_ref[...] = g_re
    gbc_ref[...] = jnp.where(row == 0, 0.0, g_im)
    gd_ref[...] = jnp.where(row == 0, nyq, g_re)


def _pad_to(a, shape):
    return jnp.pad(a, [(0, s - d) for s, d in zip(shape, a.shape)])


def _filter_call(L, w1, b1, fq, w2, b2, w3):
    feats, decay = _filter_tables(L)
    fwd, _ = _dft_tables(L)
    args = [
        jnp.asarray(feats),
        _pad_to(w1, (LANES, LANES)),
        _pad_to(b1.reshape(1, -1), (1, LANES)),
        _pad_to(fq.reshape(1, -1), (1, LANES)),
        _pad_to(w2, (LANES, LANES)),
        _pad_to(b2.reshape(1, -1), (1, LANES)),
        _pad_to(w3, (LANES, 2 * HY_WIDTH)),
        jnp.asarray(decay),
        jnp.asarray(fwd),
    ]
    out = jax.ShapeDtypeStruct((L, HY_WIDTH), F32)
    return pl.pallas_call(
        _filter_kernel,
        out_shape=[out, out, out],
        compiler_params=_params(),
        name="hyena_filter",
    )(*args)


def _hyena_kernel(hy_ref, cw_ref, cb_ref, fwd_ref, inv_ref, ga_ref, gbc_ref, gd_ref, hb_ref,
                  o_ref, pad_ref):
    L = hy_ref.shape[1]
    width = hy_ref.shape[2]
    zeros = jnp.zeros((SUBLANES, width), F32)
    pad_ref[0:SUBLANES, :] = zeros
    pad_ref[SUBLANES + L:2 * SUBLANES + L, :] = zeros
    pad_ref[SUBLANES:SUBLANES + L, :] = hy_ref[0]
    conv = cb_ref[...]
    for j in range(HY_SHORT_K):
        start = SUBLANES + j - HY_SHORT_K // 2
        conv = conv + cw_ref[j:j + 1, :] * pad_ref[start:start + L, :]
    x0 = conv[:, :HY_WIDTH]
    x1 = conv[:, HY_WIDTH:2 * HY_WIDTH]
    hv = conv[:, 2 * HY_WIDTH:]
    u = x1 * hv
    spec = _dot(fwd_ref[...], u.astype(BF16))
    p = spec[:L]
    q = spec[L:]
    ga, gbc, gd = ga_ref[...], gbc_ref[...], gd_ref[...]
    y_re = p * ga - q * gbc
    y_im = p * gbc + q * gd
    y_spec = jnp.concatenate([y_re, y_im], axis=0).astype(BF16)
    y = _dot(inv_ref[...], y_spec)
    o_ref[0] = x0 * (y + u * hb_ref[...])


def _hyena_call(hy3, conv_w, conv_b, g_tabs, hy_bias):
    B, L, width = hy3.shape
    fwd, inv = _dft_tables(L)
    fwd_bf = jnp.asarray(fwd).astype(BF16)
    inv_bf = jnp.asarray(inv).astype(BF16)
    tab = _const_spec((L, HY_WIDTH))
    return pl.pallas_call(
        _hyena_kernel,
        grid=(B,),
        in_specs=[
            pl.BlockSpec((1, L, width), lambda b: (b, 0, 0)),
            _const_spec((HY_SHORT_K, width)),
            _const_spec((1, width)),
            _const_spec((2 * L, L)),
            _const_spec((L, 2 * L)),
            tab, tab, tab,
            _const_spec((1, HY_WIDTH)),
        ],
        out_specs=pl.BlockSpec((1, L, HY_WIDTH), lambda b: (b, 0, 0)),
        out_shape=jax.ShapeDtypeStruct((B, L, HY_WIDTH), F32),
        scratch_shapes=[pltpu.VMEM((L + 2 * SUBLANES, width), F32)],
        compiler_params=_params(dimension_semantics=("arbitrary",)),
        name="hyena",
    )(hy3, conv_w, conv_b.reshape(1, -1), fwd_bf, inv_bf, *g_tabs, hy_bias.reshape(1, -1))


CF_PAD = 2 * SUBLANES


def _conformer_kernel(cf_ref, cw_ref, cb_ref, g_ref, b_ref, o_ref, pad_ref):
    L = cf_ref.shape[1]
    zeros = jnp.zeros((CF_PAD, CF_WIDTH), F32)
    pad_ref[0:CF_PAD, :] = zeros
    pad_ref[CF_PAD + L:2 * CF_PAD + L, :] = zeros
    cf = cf_ref[0]
    a = cf[:, :CF_WIDTH]
    g = cf[:, CF_WIDTH:]
    pad_ref[CF_PAD:CF_PAD + L, :] = a * jax.nn.sigmoid(g)
    half = CF_CONV_K // 2
    for c0 in range(0, L, CF_ROW_CHUNK):
        acc = jnp.zeros((CF_ROW_CHUNK, CF_WIDTH), F32) + cb_ref[...]
        for j in range(CF_CONV_K):
            start = CF_PAD + c0 + j - half
            acc = acc + cw_ref[j:j + 1, :] * pad_ref[start:start + CF_ROW_CHUNK, :]
        y = _layer_norm_rows(acc, g_ref[...], b_ref[...])
        o_ref[0, c0:c0 + CF_ROW_CHUNK, :] = y * jax.nn.sigmoid(y)


def _conformer_call(cf3, conv_w, conv_b, ln_g, ln_b):
    B, L, width = cf3.shape
    row = _const_spec((1, CF_WIDTH))
    return pl.pallas_call(
        _conformer_kernel,
        grid=(B,),
        in_specs=[
            pl.BlockSpec((1, L, width), lambda b: (b, 0, 0)),
            _const_spec((CF_CONV_K, CF_WIDTH)),
            row, row, row,
        ],
        out_specs=pl.BlockSpec((1, L, CF_WIDTH), lambda b: (b, 0, 0)),
        out_shape=jax.ShapeDtypeStruct((B, L, CF_WIDTH), F32),
        scratch_shapes=[pltpu.VMEM((L + 2 * CF_PAD, CF_WIDTH), F32)],
        compiler_params=_params(dimension_semantics=("arbitrary",)),
        name="conformer",
    )(cf3, conv_w, conv_b.reshape(1, -1), ln_g.reshape(1, -1), ln_b.reshape(1, -1))


def _attn_kernel(*refs, lambda_init, past):
    if past:
        (q_ref, k_ref, v_ref, ck_ref, cv_ref, lq1, lk1, lq2, lk2, g_ref, o_ref, k_all, v_all) = refs
    else:
        (q_ref, k_ref, v_ref, lq1, lk1, lq2, lk2, g_ref, o_ref, k_all, v_all) = refs

    @pl.when(pl.program_id(2) == 0)
    def _():
        if past:
            k_all[0:past, :] = ck_ref[0, 0].astype(BF16)
            v_all[0:past, :] = cv_ref[0, 0].astype(BF16)
        k_all[past:, :] = k_ref[0].astype(BF16)
        v_all[past:, :] = v_ref[0].astype(BF16)

    lam = (jnp.exp(jnp.sum(lq1[...] * lk1[...], axis=-1, keepdims=True))
           - jnp.exp(jnp.sum(lq2[...] * lk2[...], axis=-1, keepdims=True)) + lambda_init)
    q = q_ref[0] * (DIFF_HEAD_DIM ** -0.5)
    lane = lax.broadcasted_iota(jnp.int32, q.shape, 1)
    first = lane < DIFF_HEAD_DIM
    kh = k_all[...]
    nt = (((1,), (1,)), ((), ()))

    def probs(qm):
        s = lax.dot_general(qm.astype(BF16), kh, nt, preferred_element_type=F32)
        e = jnp.exp(s - jnp.max(s, axis=-1, keepdims=True))
        return e, 1.0 / jnp.sum(e, axis=-1, keepdims=True)

    e0, r0 = probs(jnp.where(first, q, 0.0))
    e1, r1 = probs(jnp.where(first, 0.0, q))
    a = e0 * r0 - lam * (e1 * r1)
    o = _dot(a.astype(BF16), v_all[...])
    o = o * lax.rsqrt(jnp.mean(o * o, axis=-1, keepdims=True) + LN_EPS) * g_ref[...]
    o_ref[0] = o * (1.0 - lambda_init)


def _attn_call(q3, k3, v3, cache_k4, cache_v4, layer, lam_vecs, subln_g):
    B, L, _ = q3.shape
    past = 0 if cache_k4 is None else cache_k4.shape[2]
    tile = pl.BlockSpec((1, Q_TILE, HEAD_PAIR), lambda b, h, i: (b, i, h))
    seq = pl.BlockSpec((1, L, HEAD_PAIR), lambda b, h, i: (b, 0, h))
    in_specs = [tile, seq, seq]
    args = [q3, k3, v3]
    if past:
        cache = pl.BlockSpec((1, 1, past, HEAD_PAIR), lambda b, h, i: (b, layer, 0, h))
        in_specs += [cache, cache]
        args += [cache_k4, cache_v4]
    in_specs += [_const_spec((1, DIFF_HEAD_DIM))] * 4 + [_const_spec((1, HEAD_PAIR))]
    args += [v.reshape(1, -1) for v in lam_vecs] + [subln_g.reshape(1, -1)]
    return pl.pallas_call(
        functools.partial(_attn_kernel, lambda_init=_lambda_init(layer), past=past),
        grid=(B, DIFF_HEADS, L // Q_TILE),
        in_specs=in_specs,
        out_specs=tile,
        out_shape=jax.ShapeDtypeStruct((B, L, ATT_WIDTH), F32),
        scratch_shapes=[pltpu.VMEM((past + L, HEAD_PAIR), BF16),
                        pltpu.VMEM((past + L, HEAD_PAIR), BF16)],
        compiler_params=_params(dimension_semantics=("arbitrary", "arbitrary", "arbitrary")),
        name="diff_attn",
    )(*args)


def _mlp_kernel(x_ref, hy_ref, cf_ref, at_ref, ada_ref, wo_ref, w1_ref, w2_ref,
                g1_ref, b1_ref, g2_ref, b2_ref, o_ref):
    gate1 = ada_ref[0, 0, 2:3, :]
    sh2 = ada_ref[0, 0, 3:4, :]
    sc2 = ada_ref[0, 0, 4:5, :]
    gate2 = ada_ref[0, 0, 5:6, :]
    mix = jnp.concatenate([hy_ref[...], cf_ref[...], at_ref[...]], axis=1).astype(BF16)
    y = _dot(mix, wo_ref[...])
    x = _layer_norm_rows(DEEPNORM_ALPHA * x_ref[...] + gate1 * y, g1_ref[...], b1_ref[...])
    h = (x * (1.0 + sc2) + sh2).astype(BF16)
    a = jnp.maximum(_dot(h, w1_ref[...]), 0.0)
    m = _dot((a * a).astype(BF16), w2_ref[...])
    o_ref[...] = _layer_norm_rows(DEEPNORM_ALPHA * x + gate2 * m, g2_ref[...], b2_ref[...])


def _mlp_call(x2d, y_hy, y_cf, y_at, ada4, wo_bf, w1_bf, w2_bf, ln1_g, ln1_b, ln2_g, ln2_b,
              layer, cond_row_of_tile):
    T = x2d.shape[0]
    rows = lambda w: pl.BlockSpec((ROW_TILE, w), lambda i: (i, 0))
    vec = _const_spec((1, D_MODEL))
    return pl.pallas_call(
        _mlp_kernel,
        grid=(T // ROW_TILE,),
        in_specs=[
            rows(D_MODEL), rows(HY_WIDTH), rows(CF_WIDTH), rows(ATT_WIDTH),
            pl.BlockSpec((1, 1, N_ADA, D_MODEL), lambda i: (layer, cond_row_of_tile(i), 0, 0)),
            _const_spec((MIX_WIDTH, D_MODEL)),
            _const_spec((D_MODEL, D_FF)),
            _const_spec((D_FF, D_MODEL)),
            vec, vec, vec, vec,
        ],
        out_specs=rows(D_MODEL),
        out_shape=jax.ShapeDtypeStruct((T, D_MODEL), F32),
        compiler_params=_params(dimension_semantics=("arbitrary",)),
        name="outproj_mlp",
    )(x2d, y_hy, y_cf, y_at, ada4, wo_bf, w1_bf, w2_bf,
      ln1_g.reshape(1, -1), ln1_b.reshape(1, -1), ln2_g.reshape(1, -1), ln2_b.reshape(1, -1))


def _layer(x2d, batch, seq, layer, ada4, wts, g_tabs, cond_row_of_tile, rope_tabs, cache_kv):
    hy, cf, q, k, v = _inproj_call(x2d, ada4, wts["w_in"], layer, seq, cond_row_of_tile, rope_tabs)
    as_seq = lambda a: a.reshape(batch, seq, a.shape[-1])
    y_hy = _hyena_call(as_seq(hy), wts["hy_conv_w"], wts["hy_conv_b"], g_tabs, wts["hy_bias"])
    y_cf = _conformer_call(as_seq(cf), wts["cf_conv_w"], wts["cf_conv_b"], wts["cf_ln_g"], wts["cf_ln_b"])
    ck, cv = cache_kv if cache_kv is not None else (None, None)
    lam_vecs = (wts["lam_q1"], wts["lam_k1"], wts["lam_q2"], wts["lam_k2"])
    y_at = _attn_call(as_seq(q), as_seq(k), as_seq(v), ck, cv, layer, lam_vecs, wts["subln_g"])
    flat = lambda a: a.reshape(batch * seq, a.shape[-1])
    x_out = _mlp_call(x2d, flat(y_hy), flat(y_cf), flat(y_at), ada4,
                      wts["w_out"], wts["w_mlp1"], wts["w_mlp2"],
                      wts["ln1_g"], wts["ln1_b"], wts["ln2_g"], wts["ln2_b"], layer, cond_row_of_tile)
    return x_out, k, v


def kernel(x_prompt, x_sample, cache_k, cache_v, c, c_ctx, w_ada, b_ada, w_in, hy_conv_w, hy_conv_b, hf_w1, hf_b1, hf_freq, hf_w2, hf_b2, hf_w3, hy_bias, cf_conv_w, cf_conv_b, cf_ln_g, cf_ln_b, lam_q1, lam_k1, lam_q2, lam_k2, subln_g, w_out, ln1_g, ln1_b, w_mlp1, w_mlp2, ln2_g, ln2_b):
    batch, seq, _ = x_prompt.shape
    dec_batch, dec_seq, _ = x_sample.shape
    past = cache_k.shape[2]
    assert 1 + dec_batch <= COND_ROWS
    assert seq % Q_TILE == 0 and dec_seq % ROW_TILE == 0 and (batch * seq) % ROW_TILE == 0

    cond = jnp.concatenate(
        [c_ctx[None, :], c, jnp.zeros((COND_ROWS - 1 - dec_batch, D_MODEL), F32)], axis=0)
    ada4 = _ada_call(cond, w_ada, b_ada).reshape(DEPTH, COND_ROWS, N_ADA, D_MODEL)

    per_layer = dict(
        w_in=w_in.astype(BF16), w_out=w_out.astype(BF16),
        w_mlp1=w_mlp1.astype(BF16), w_mlp2=w_mlp2.astype(BF16),
        hy_conv_w=hy_conv_w, hy_conv_b=hy_conv_b, hy_bias=hy_bias,
        cf_conv_w=cf_conv_w, cf_conv_b=cf_conv_b, cf_ln_g=cf_ln_g, cf_ln_b=cf_ln_b,
        lam_q1=lam_q1, lam_k1=lam_k1, lam_q2=lam_q2, lam_k2=lam_k2, subln_g=subln_g,
        ln1_g=ln1_g, ln1_b=ln1_b, ln2_g=ln2_g, ln2_b=ln2_b)

    rope_tabs = tuple(jnp.asarray(t) for t in _rope_tables(dec_seq))
    cache_k4 = cache_k.reshape(dec_batch, DEPTH, past, ATT_WIDTH)
    cache_v4 = cache_v.reshape(dec_batch, DEPTH, past, ATT_WIDTH)
    dec_tiles_per_seq = dec_seq // ROW_TILE

    xp = x_prompt.reshape(batch * seq, D_MODEL)
    xs = x_sample.reshape(dec_batch * dec_seq, D_MODEL)
    ks, vs = [], []
    for l in range(DEPTH):
        wts = {n: a[l] for n, a in per_layer.items()}
        filt = (hf_w1[l], hf_b1[l], hf_freq[l], hf_w2[l], hf_b2[l], hf_w3[l])
        g_ctx = _filter_call(seq, *filt)
        g_dec = _filter_call(dec_seq, *filt)
        xp, k_l, v_l = _layer(xp, batch, seq, l, ada4, wts, g_ctx, lambda i: 0, None, None)
        ks.append(k_l.reshape(batch, seq, 2 * DIFF_HEADS, DIFF_HEAD_DIM))
        vs.append(v_l.reshape(batch, seq, DIFF_HEADS, 2 * DIFF_HEAD_DIM))
        xs, _, _ = _layer(xs, dec_batch, dec_seq, l, ada4, wts, g_dec,
                          lambda i: 1 + i // dec_tiles_per_seq, rope_tabs, (cache_k4, cache_v4))
    new_cache_k = jnp.stack(ks, axis=1)
    new_cache_v = jnp.stack(vs, axis=1)
    return (xp.reshape(batch, seq, D_MODEL), xs.reshape(dec_batch, dec_seq, D_MODEL),
            new_cache_k, new_cache_v)
```

```python
import functools
import math

import numpy as np
import jax
import jax.numpy as jnp
from jax import lax
from jax.experimental import pallas as pl
from jax.experimental.pallas import tpu as pltpu

D_MODEL = 1024
DEPTH = 2
GRID_W = 64
HY_WIDTH = D_MODEL // 4
CF_WIDTH = D_MODEL // 4
ATT_WIDTH = D_MODEL // 2
DIFF_HEAD_DIM = 64
DIFF_HEADS = ATT_WIDTH // (2 * DIFF_HEAD_DIM)
HEAD_PAIR = 2 * DIFF_HEAD_DIM
MIX_WIDTH = HY_WIDTH + CF_WIDTH + ATT_WIDTH
IN_WIDTH = 3 * HY_WIDTH + 2 * CF_WIDTH + 3 * ATT_WIDTH
HY_SHORT_K = 3
HY_FILTER_EMB = 33
HY_FILTER_HIDDEN = 64
HY_FAST_DECAY_PCT = 0.3
HY_SLOW_DECAY_PCT = 1.5
HY_DECAY_TARGET = 1e-2
CF_CONV_K = 31
D_FF = 4 * D_MODEL
ROPE_BASE = 10000.0
AX_DIM = DIFF_HEAD_DIM // 2
DEEPNORM_ALPHA = (2 * DEPTH) ** 0.25
LN_EPS = 1e-5
N_ADA = 6

LANES = 128
SUBLANES = 8
COND_ROWS = SUBLANES
VMEM_LIMIT = 56 * 1024 * 1024

ROW_TILE = 512
ADA_COL_TILE = 1536
Q_TILE = 256
MIXER_ROWS = 1024
CF_ROW_CHUNK = 128

F32 = jnp.float32
BF16 = jnp.bfloat16


def _lambda_init(l):
    return 0.8 - 0.6 * math.exp(-0.3 * l)


def _params(**kw):
    return pltpu.CompilerParams(vmem_limit_bytes=VMEM_LIMIT, **kw)


def _const_spec(shape):
    zeros = (0,) * len(shape)
    return pl.BlockSpec(shape, lambda *_: zeros, pipeline_mode=pl.Buffered(1))


def _layer_spec(shape, layer):
    zeros = (0,) * len(shape)
    return pl.BlockSpec((None,) + tuple(shape), lambda *_: (layer,) + zeros,
                        pipeline_mode=pl.Buffered(1))


def _dot(a, b):
    return jnp.dot(a, b, preferred_element_type=F32)


def _dot_f32(a, b):
    return jnp.dot(a, b, preferred_element_type=F32, precision=lax.Precision.HIGHEST)


def _split_bf16(x):
    hi = x.astype(BF16)
    return hi, (x - hi.astype(F32)).astype(BF16)


def _dot_bf16x3(a, b):
    a_hi, a_lo = _split_bf16(a)
    b_hi, b_lo = _split_bf16(b)
    return _dot(a_hi, b_hi) + (_dot(a_hi, b_lo) + _dot(a_lo, b_hi))


def _layer_norm_rows(x, g, b):
    mu = jnp.mean(x, axis=-1, keepdims=True)
    xc = x - mu
    var = jnp.mean(xc * xc, axis=-1, keepdims=True)
    return xc * lax.rsqrt(var + LN_EPS) * g + b


def _seqs_per_step(batch, seq):
    return max(1, min(batch, MIXER_ROWS // seq))


@functools.lru_cache(maxsize=None)
def _dft_tables(L):
    n = 2 * L
    k = np.arange(L, dtype=np.float64)[:, None]
    s = np.arange(L, dtype=np.float64)[None, :]
    ang = 2.0 * np.pi * k * s / n
    fwd = np.concatenate([np.cos(ang), -np.sin(ang)], axis=0)
    fwd[L, :] = np.cos(np.pi * np.arange(L))
    t = np.arange(L, dtype=np.float64)[:, None]
    kk = np.arange(L, dtype=np.float64)[None, :]
    ang_i = 2.0 * np.pi * t * kk / n
    inv_re = (2.0 / n) * np.cos(ang_i)
    inv_re[:, 0] = 1.0 / n
    inv_im = -(2.0 / n) * np.sin(ang_i)
    inv_im[:, 0] = np.cos(np.pi * np.arange(L)) / n
    inv = np.concatenate([inv_re, inv_im], axis=1)
    return fwd.astype(np.float32), inv.astype(np.float32)


@functools.lru_cache(maxsize=None)
def _filter_tables(L):
    bands = (HY_FILTER_EMB - 1) // 2
    t = np.linspace(0.0, 1.0, L)[:, None]
    w = (2.0 * np.pi / L) * np.arange(L, dtype=np.float64)[:, None]
    fr = np.linspace(1e-4, bands - 1, bands)[None, :]
    feats = np.concatenate([t, np.cos(fr * w), -np.sin(fr * w)], -1)
    feats_p = np.zeros((L, LANES), np.float64)
    feats_p[:, :HY_FILTER_EMB] = feats
    deltas = np.abs(np.linspace(math.log(HY_DECAY_TARGET) / HY_FAST_DECAY_PCT,
                                math.log(HY_DECAY_TARGET) / HY_SLOW_DECAY_PCT, HY_WIDTH))
    decay = np.exp(-t * deltas[None, :])
    return feats_p.astype(np.float32), decay.astype(np.float32)


@functools.lru_cache(maxsize=None)
def _rope_tables(n):
    half = AX_DIM // 2
    pos = np.arange(n)
    row = (pos // GRID_W).astype(np.float64)[:, None]
    col = (pos % GRID_W).astype(np.float64)[:, None]
    inv = ROPE_BASE ** (-np.arange(0, AX_DIM, 2, dtype=np.float64) / AX_DIM)[None, :]
    zero = np.zeros((n, half))
    cr, sr = np.cos(row * inv), np.sin(row * inv)
    cc, sc = np.cos(col * inv), np.sin(col * inv)
    c = np.concatenate([cr, cr, cc, cc], -1)
    s_up = np.concatenate([-sr, zero, -sc, zero], -1)
    s_dn = np.concatenate([zero, sr, zero, sc], -1)
    reps = LANES // DIFF_HEAD_DIM
    tile = lambda a: np.tile(a, (1, reps)).astype(np.float32)
    return tile(c), tile(s_up), tile(s_dn)


def _ada_kernel(cond_ref, w_ref, b_ref, o_ref):
    c = cond_ref[...]
    s = c * jax.nn.sigmoid(c)
    o_ref[0] = _dot(s.astype(BF16), w_ref[0].astype(BF16)) + b_ref[0]


def _ada_call(cond, w_ada, b_ada):
    n_out = N_ADA * D_MODEL
    return pl.pallas_call(
        _ada_kernel,
        grid=(DEPTH, n_out // ADA_COL_TILE),
        in_specs=[
            pl.BlockSpec((COND_ROWS, D_MODEL), lambda l, j: (0, 0)),
            pl.BlockSpec((1, D_MODEL, ADA_COL_TILE), lambda l, j: (l, 0, j)),
            pl.BlockSpec((1, 1, ADA_COL_TILE), lambda l, j: (l, 0, j)),
        ],
        out_specs=pl.BlockSpec((1, COND_ROWS, ADA_COL_TILE), lambda l, j: (l, 0, j)),
        out_shape=jax.ShapeDtypeStruct((DEPTH, COND_ROWS, n_out), F32),
        compiler_params=_params(dimension_semantics=("arbitrary", "arbitrary")),
        name="ada",
    )(cond, w_ada, b_ada.reshape(DEPTH, 1, n_out))


def _rope_lanes(x, c, s_up, s_dn):
    outs = []
    for p in range(ATT_WIDTH // LANES):
        xb = x[:, p * LANES:(p + 1) * LANES]
        up = pltpu.roll(xb, LANES - AX_DIM // 2, 1)
        dn = pltpu.roll(xb, AX_DIM // 2, 1)
        outs.append(xb * c + up * s_up + dn * s_dn)
    return jnp.concatenate(outs, axis=1)


def _inproj_kernel(*refs, rope):
    if rope:
        x_ref, ada_ref, w_ref, c_ref, su_ref, sd_ref, hy_ref, cf_ref, q_ref, k_ref, v_ref = refs
    else:
        x_ref, ada_ref, w_ref, hy_ref, cf_ref, q_ref, k_ref, v_ref = refs
    sh1 = ada_ref[0, 0, 0:1, :]
    sc1 = ada_ref[0, 0, 1:2, :]
    h = (x_ref[...] * (1.0 + sc1) + sh1).astype(BF16)
    o1 = 3 * HY_WIDTH
    o2 = o1 + 2 * CF_WIDTH
    o3 = o2 + ATT_WIDTH
    o4 = o3 + ATT_WIDTH
    hy_ref[...] = _dot(h, w_ref[:, 0:o1])
    cf_ref[...] = _dot(h, w_ref[:, o1:o2])
    q = _dot(h, w_ref[:, o2:o3])
    k = _dot(h, w_ref[:, o3:o4])
    if rope:
        c, su, sd = c_ref[...], su_ref[...], sd_ref[...]
        q = _rope_lanes(q, c, su, sd)
        k = _rope_lanes(k, c, su, sd)
    q_ref[...] = q
    k_ref[...] = k
    v_ref[...] = _dot(h, w_ref[:, o4:IN_WIDTH])


def _inproj_call(x2d, ada4, w_in_bf, layer, seq, cond_row_of_tile, rope_tabs):
    T = x2d.shape[0]
    tiles_per_seq = seq // ROW_TILE if seq >= ROW_TILE else None
    in_specs = [
        pl.BlockSpec((ROW_TILE, D_MODEL), lambda i: (i, 0)),
        pl.BlockSpec((1, 1, N_ADA, D_MODEL), lambda i: (layer, cond_row_of_tile(i), 0, 0)),
        _layer_spec((D_MODEL, IN_WIDTH), layer),
    ]
    args = [x2d, ada4, w_in_bf]
    if rope_tabs is not None:
        for tab in rope_tabs:
            in_specs.append(pl.BlockSpec((ROW_TILE, LANES), lambda i: (i % tiles_per_seq, 0)))
            args.append(tab)
    widths = (3 * HY_WIDTH, 2 * CF_WIDTH, ATT_WIDTH, ATT_WIDTH, ATT_WIDTH)
    return pl.pallas_call(
        functools.partial(_inproj_kernel, rope=rope_tabs is not None),
        grid=(T // ROW_TILE,),
        in_specs=in_specs,
        out_specs=[pl.BlockSpec((ROW_TILE, w), lambda i: (i, 0)) for w in widths],
        out_shape=[jax.ShapeDtypeStruct((T, w), F32) for w in widths],
        compiler_params=_params(dimension_semantics=("arbitrary",)),
        name="inproj",
    )(*args)


def _filter_kernel(feats_ref, w1_ref, b1_ref, fq_ref, w2_ref, b2_ref, w3_ref, decay_ref, fwd_ref,
                   ga_ref, gbc_ref, gd_ref):
    L = feats_ref.shape[0]
    fq = fq_ref[...]
    hid = jnp.sin(fq * (_dot_f32(feats_ref[...], w1_ref[...]) + b1_ref[...]))
    hid = jnp.sin(fq * (_dot_f32(hid, w2_ref[...]) + b2_ref[...]))
    hf = _dot_f32(hid, w3_ref[...])
    decay = decay_ref[...]
    row = lax.broadcasted_iota(jnp.int32, (L, HY_WIDTH), 0)
    h_fwd = hf[:, :HY_WIDTH] * decay
    h_bwd = jnp.where(row == 0, 0.0, hf[:, HY_WIDTH:] * decay)
    spec = _dot_bf16x3(fwd_ref[...], jnp.concatenate([h_fwd, h_bwd], axis=1))
    p = spec[:L]
    q = spec[L:]
    g_re = p[:, :HY_WIDTH] + p[:, HY_WIDTH:]
    g_im = q[:, :HY_WIDTH] - q[:, HY_WIDTH:]
    nyq = q[0:1, :HY_WIDTH] + q[0:1, HY_WIDTH:]
    ga_ref[...] = g_re
    gbc_ref[...] = jnp.where(row == 0, 0.0, g_im)
    gd_ref[...] = jnp.where(row == 0, nyq, g_re)


def _pad_to(a, shape):
    return jnp.pad(a, [(0, s - d) for s, d in zip(shape, a.shape)])


def _filter_call(L, w1, b1, fq, w2, b2, w3):
    feats, decay = _filter_tables(L)
    fwd, _ = _dft_tables(L)
    args = [
        jnp.asarray(feats),
        _pad_to(w1, (LANES, LANES)),
        _pad_to(b1.reshape(1, -1), (1, LANES)),
        _pad_to(fq.reshape(1, -1), (1, LANES)),
        _pad_to(w2, (LANES, LANES)),
        _pad_to(b2.reshape(1, -1), (1, LANES)),
        _pad_to(w3, (LANES, 2 * HY_WIDTH)),
        jnp.asarray(decay),
        jnp.asarray(fwd),
    ]
    out = jax.ShapeDtypeStruct((L, HY_WIDTH), F32)
    return pl.pallas_call(
        _filter_kernel,
        out_shape=[out, out, out],
        compiler_params=_params(),
        name="hyena_filter",
    )(*args)


def _hyena_kernel(hy_ref, cw_ref, cb_ref, fwd_ref, inv_ref, ga_ref, gbc_ref, gd_ref, hb_ref,
                  o_ref, pad_ref):
    nb, L, width = hy_ref.shape
    zeros = jnp.zeros((SUBLANES, width), F32)
    x0s, us = [], []
    for s in range(nb):
        pad_ref[s, 0:SUBLANES, :] = zeros
        pad_ref[s, SUBLANES + L:2 * SUBLANES + L, :] = zeros
        pad_ref[s, SUBLANES:SUBLANES + L, :] = hy_ref[s]
        conv = cb_ref[...]
        for j in range(HY_SHORT_K):
            start = SUBLANES + j - HY_SHORT_K // 2
            conv = conv + cw_ref[j:j + 1, :] * pad_ref[s, start:start + L, :]
        x0s.append(conv[:, :HY_WIDTH])
        us.append(conv[:, HY_WIDTH:2 * HY_WIDTH] * conv[:, 2 * HY_WIDTH:])
    u_all = jnp.concatenate(us, axis=1).astype(BF16)
    spec = _dot(fwd_ref[...], u_all)
    ga, gbc, gd = ga_ref[...], gbc_ref[...], gd_ref[...]
    y_re, y_im = [], []
    for s in range(nb):
        p = spec[:L, s * HY_WIDTH:(s + 1) * HY_WIDTH]
        q = spec[L:, s * HY_WIDTH:(s + 1) * HY_WIDTH]
        y_re.append(p * ga - q * gbc)
        y_im.append(p * gbc + q * gd)
    y_spec = jnp.concatenate(
        [jnp.concatenate(y_re, axis=1), jnp.concatenate(y_im, axis=1)], axis=0).astype(BF16)
    y = _dot(inv_ref[...], y_spec)
    for s in range(nb):
        o_ref[s] = x0s[s] * (y[:, s * HY_WIDTH:(s + 1) * HY_WIDTH] + us[s] * hb_ref[...])


def _hyena_call(hy3, conv_w, conv_b, g_tabs, hy_bias, layer):
    B, L, width = hy3.shape
    nb = _seqs_per_step(B, L)
    fwd, inv = _dft_tables(L)
    fwd_bf = jnp.asarray(fwd).astype(BF16)
    inv_bf = jnp.asarray(inv).astype(BF16)
    tab = _const_spec((L, HY_WIDTH))
    return pl.pallas_call(
        _hyena_kernel,
        grid=(B // nb,),
        in_specs=[
            pl.BlockSpec((nb, L, width), lambda b: (b, 0, 0)),
            _layer_spec((HY_SHORT_K, width), layer),
            _layer_spec((1, width), layer),
            _const_spec((2 * L, L)),
            _const_spec((L, 2 * L)),
            tab, tab, tab,
            _layer_spec((1, HY_WIDTH), layer),
        ],
        out_specs=pl.BlockSpec((nb, L, HY_WIDTH), lambda b: (b, 0, 0)),
        out_shape=jax.ShapeDtypeStruct((B, L, HY_WIDTH), F32),
        scratch_shapes=[pltpu.VMEM((nb, L + 2 * SUBLANES, width), F32)],
        compiler_params=_params(dimension_semantics=("arbitrary",)),
        name="hyena",
    )(hy3, conv_w, conv_b, fwd_bf, inv_bf, *g_tabs, hy_bias)


CF_PAD = 2 * SUBLANES


def _conformer_kernel(cf_ref, cw_ref, cb_ref, g_ref, b_ref, o_ref, pad_ref):
    nb, L, _ = cf_ref.shape
    padded = L + 2 * CF_PAD
    shifted = padded - SUBLANES
    zeros = jnp.zeros((CF_PAD, CF_WIDTH), F32)
    half = CF_CONV_K // 2
    for s in range(nb):
        cf = cf_ref[s]
        pad_ref[s, 0, 0:CF_PAD, :] = zeros
        pad_ref[s, 0, CF_PAD + L:padded, :] = zeros
        pad_ref[s, 0, CF_PAD:CF_PAD + L, :] = cf[:, :CF_WIDTH] * jax.nn.sigmoid(cf[:, CF_WIDTH:])
        for r in range(1, SUBLANES):
            pad_ref[s, r, 0:shifted, :] = pad_ref[s, 0, r:r + shifted, :]
        for c0 in range(0, L, CF_ROW_CHUNK):
            acc = jnp.zeros((CF_ROW_CHUNK, CF_WIDTH), F32) + cb_ref[...]
            for j in range(CF_CONV_K):
                off = CF_PAD + j - half
                start = c0 + (off // SUBLANES) * SUBLANES
                acc = acc + cw_ref[j:j + 1, :] * pad_ref[s, off % SUBLANES, start:start + CF_ROW_CHUNK, :]
            y = _layer_norm_rows(acc, g_ref[...], b_ref[...])
            o_ref[s, c0:c0 + CF_ROW_CHUNK, :] = y * jax.nn.sigmoid(y)


def _conformer_call(cf3, conv_w, conv_b, ln_g, ln_b, layer):
    B, L, width = cf3.shape
    nb = _seqs_per_step(B, L)
    row = _layer_spec((1, CF_WIDTH), layer)
    return pl.pallas_call(
        _conformer_kernel,
        grid=(B // nb,),
        in_specs=[
            pl.BlockSpec((nb, L, width), lambda b: (b, 0, 0)),
            _layer_spec((CF_CONV_K, CF_WIDTH), layer),
            row, row, row,
        ],
        out_specs=pl.BlockSpec((nb, L, CF_WIDTH), lambda b: (b, 0, 0)),
        out_shape=jax.ShapeDtypeStruct((B, L, CF_WIDTH), F32),
        scratch_shapes=[pltpu.VMEM((nb, SUBLANES, L + 2 * CF_PAD, CF_WIDTH), F32)],
        compiler_params=_params(dimension_semantics=("arbitrary",)),
        name="conformer",
    )(cf3, conv_w, conv_b, ln_g, ln_b)


def _attn_kernel(*refs, lambda_init, past):
    if past:
        (q_ref, k_ref, v_ref, ck_ref, cv_ref, lq1, lk1, lq2, lk2, g_ref, o_ref, k_all, v_all) = refs
    else:
        (q_ref, k_ref, v_ref, lq1, lk1, lq2, lk2, g_ref, o_ref, k_all, v_all) = refs
    nb = q_ref.shape[0]

    @pl.when(pl.program_id(1) == 0)
    def _():
        for s in range(nb):
            if past:
                k_all[s, 0:past, :] = ck_ref[s, 0].astype(BF16)
                v_all[s, 0:past, :] = cv_ref[s, 0].astype(BF16)
            k_all[s, past:, :] = k_ref[s].astype(BF16)
            v_all[s, past:, :] = v_ref[s].astype(BF16)

    lam = (jnp.exp(jnp.sum(lq1[...] * lk1[...], axis=-1, keepdims=True))
           - jnp.exp(jnp.sum(lq2[...] * lk2[...], axis=-1, keepdims=True)) + lambda_init)
    q_scale = DIFF_HEAD_DIM ** -0.5 * math.log2(math.e)
    lane = lax.broadcasted_iota(jnp.int32, (q_ref.shape[1], HEAD_PAIR), 1)
    first = lane < DIFF_HEAD_DIM
    nt = (((1,), (1,)), ((), ()))
    gain = g_ref[...] * (1.0 - lambda_init)

    def probs(qm, kh):
        s = lax.dot_general(qm.astype(BF16), kh, nt, preferred_element_type=F32)
        e = jnp.exp2(s - jnp.max(s, axis=-1, keepdims=True))
        return e, 1.0 / jnp.sum(e, axis=-1, keepdims=True)

    for s in range(nb):
        for h in range(DIFF_HEADS):
            cols = slice(h * HEAD_PAIR, (h + 1) * HEAD_PAIR)
            q = q_ref[s, :, cols] * q_scale
            kh = k_all[s, :, cols]
            e0, r0 = probs(jnp.where(first, q, 0.0), kh)
            e1, r1 = probs(jnp.where(first, 0.0, q), kh)
            a = e0 * r0 - e1 * (lam * r1)
            o = _dot(a.astype(BF16), v_all[s, :, cols])
            o_ref[s, :, cols] = o * lax.rsqrt(jnp.mean(o * o, axis=-1, keepdims=True) + LN_EPS) * gain


def _attn_call(q3, k3, v3, cache_k4, cache_v4, layer, lam_vecs, subln_g):
    B, L, _ = q3.shape
    past = 0 if cache_k4 is None else cache_k4.shape[2]
    nb = _seqs_per_step(B, 2 * L)
    tile = pl.BlockSpec((nb, Q_TILE, ATT_WIDTH), lambda b, i: (b, i, 0))
    seq = pl.BlockSpec((nb, L, ATT_WIDTH), lambda b, i: (b, 0, 0))
    in_specs = [tile, seq, seq]
    args = [q3, k3, v3]
    if past:
        cache = pl.BlockSpec((nb, 1, past, ATT_WIDTH), lambda b, i: (b, layer, 0, 0))
        in_specs += [cache, cache]
        args += [cache_k4, cache_v4]
    in_specs += [_layer_spec((1, DIFF_HEAD_DIM), layer)] * 4 + [_layer_spec((1, HEAD_PAIR), layer)]
    args += list(lam_vecs) + [subln_g]
    return pl.pallas_call(
        functools.partial(_attn_kernel, lambda_init=_lambda_init(layer), past=past),
        grid=(B // nb, L // Q_TILE),
        in_specs=in_specs,
        out_specs=tile,
        out_shape=jax.ShapeDtypeStruct((B, L, ATT_WIDTH), F32),
        scratch_shapes=[pltpu.VMEM((nb, past + L, ATT_WIDTH), BF16),
                        pltpu.VMEM((nb, past + L, ATT_WIDTH), BF16)],
        compiler_params=_params(dimension_semantics=("arbitrary", "arbitrary")),
        name="diff_attn",
    )(*args)


def _mlp_kernel(x_ref, hy_ref, cf_ref, at_ref, ada_ref, wo_ref, w1_ref, w2_ref,
                g1_ref, b1_ref, g2_ref, b2_ref, o_ref):
    gate1 = ada_ref[0, 0, 2:3, :]
    sh2 = ada_ref[0, 0, 3:4, :]
    sc2 = ada_ref[0, 0, 4:5, :]
    gate2 = ada_ref[0, 0, 5:6, :]
    mix = jnp.concatenate([hy_ref[...], cf_ref[...], at_ref[...]], axis=1).astype(BF16)
    y = _dot(mix, wo_ref[...])
    x = _layer_norm_rows(DEEPNORM_ALPHA * x_ref[...] + gate1 * y, g1_ref[...], b1_ref[...])
    h = (x * (1.0 + sc2) + sh2).astype(BF16)
    a = jnp.maximum(_dot(h, w1_ref[...]), 0.0)
    m = _dot((a * a).astype(BF16), w2_ref[...])
    o_ref[...] = _layer_norm_rows(DEEPNORM_ALPHA * x + gate2 * m, g2_ref[...], b2_ref[...])


def _mlp_call(x2d, y_hy, y_cf, y_at, ada4, wo_bf, w1_bf, w2_bf, ln1_g, ln1_b, ln2_g, ln2_b,
              layer, cond_row_of_tile):
    T = x2d.shape[0]
    rows = lambda w: pl.BlockSpec((ROW_TILE, w), lambda i: (i, 0))
    vec = _layer_spec((1, D_MODEL), layer)
    return pl.pallas_call(
        _mlp_kernel,
        grid=(T // ROW_TILE,),
        in_specs=[
            rows(D_MODEL), rows(HY_WIDTH), rows(CF_WIDTH), rows(ATT_WIDTH),
            pl.BlockSpec((1, 1, N_ADA, D_MODEL), lambda i: (layer, cond_row_of_tile(i), 0, 0)),
            _layer_spec((MIX_WIDTH, D_MODEL), layer),
            _layer_spec((D_MODEL, D_FF), layer),
            _layer_spec((D_FF, D_MODEL), layer),
            vec, vec, vec, vec,
        ],
        out_specs=rows(D_MODEL),
        out_shape=jax.ShapeDtypeStruct((T, D_MODEL), F32),
        compiler_params=_params(dimension_semantics=("arbitrary",)),
        name="outproj_mlp",
    )(x2d, y_hy, y_cf, y_at, ada4, wo_bf, w1_bf, w2_bf, ln1_g, ln1_b, ln2_g, ln2_b)


def _layer(x2d, batch, seq, layer, ada4, wts, g_tabs, cond_row_of_tile, rope_tabs, cache_kv):
    hy, cf, q, k, v = _inproj_call(x2d, ada4, wts["w_in"], layer, seq, cond_row_of_tile, rope_tabs)
    as_seq = lambda a: a.reshape(batch, seq, a.shape[-1])
    y_hy = _hyena_call(as_seq(hy), wts["hy_conv_w"], wts["hy_conv_b"], g_tabs, wts["hy_bias"], layer)
    y_cf = _conformer_call(as_seq(cf), wts["cf_conv_w"], wts["cf_conv_b"], wts["cf_ln_g"],
                           wts["cf_ln_b"], layer)
    ck, cv = cache_kv if cache_kv is not None else (None, None)
    lam_vecs = (wts["lam_q1"], wts["lam_k1"], wts["lam_q2"], wts["lam_k2"])
    y_at = _attn_call(as_seq(q), as_seq(k), as_seq(v), ck, cv, layer, lam_vecs, wts["subln_g"])
    flat = lambda a: a.reshape(batch * seq, a.shape[-1])
    x_out = _mlp_call(x2d, flat(y_hy), flat(y_cf), flat(y_at), ada4,
                      wts["w_out"], wts["w_mlp1"], wts["w_mlp2"],
                      wts["ln1_g"], wts["ln1_b"], wts["ln2_g"], wts["ln2_b"], layer, cond_row_of_tile)
    return x_out, k, v


def kernel(x_prompt, x_sample, cache_k, cache_v, c, c_ctx, w_ada, b_ada, w_in, hy_conv_w, hy_conv_b, hf_w1, hf_b1, hf_freq, hf_w2, hf_b2, hf_w3, hy_bias, cf_conv_w, cf_conv_b, cf_ln_g, cf_ln_b, lam_q1, lam_k1, lam_q2, lam_k2, subln_g, w_out, ln1_g, ln1_b, w_mlp1, w_mlp2, ln2_g, ln2_b):
    batch, seq, _ = x_prompt.shape
    dec_batch, dec_seq, _ = x_sample.shape
    past = cache_k.shape[2]
    assert 1 + dec_batch <= COND_ROWS
    assert seq % Q_TILE == 0 and dec_seq % ROW_TILE == 0 and (batch * seq) % ROW_TILE == 0

    cond = jnp.concatenate(
        [c_ctx[None, :], c, jnp.zeros((COND_ROWS - 1 - dec_batch, D_MODEL), F32)], axis=0)
    ada4 = _ada_call(cond, w_ada, b_ada).reshape(DEPTH, COND_ROWS, N_ADA, D_MODEL)

    row_vec = lambda a: a.reshape(DEPTH, 1, a.shape[-1])
    wts = dict(
        w_in=w_in.astype(BF16), w_out=w_out.astype(BF16),
        w_mlp1=w_mlp1.astype(BF16), w_mlp2=w_mlp2.astype(BF16),
        hy_conv_w=hy_conv_w, hy_conv_b=row_vec(hy_conv_b), hy_bias=row_vec(hy_bias),
        cf_conv_w=cf_conv_w, cf_conv_b=row_vec(cf_conv_b),
        cf_ln_g=row_vec(cf_ln_g), cf_ln_b=row_vec(cf_ln_b),
        lam_q1=row_vec(lam_q1), lam_k1=row_vec(lam_k1), lam_q2=row_vec(lam_q2), lam_k2=row_vec(lam_k2),
        subln_g=row_vec(subln_g),
        ln1_g=row_vec(ln1_g), ln1_b=row_vec(ln1_b), ln2_g=row_vec(ln2_g), ln2_b=row_vec(ln2_b))

    rope_tabs = tuple(jnp.asarray(t) for t in _rope_tables(dec_seq))
    cache_k4 = cache_k.reshape(dec_batch, DEPTH, past, ATT_WIDTH)
    cache_v4 = cache_v.reshape(dec_batch, DEPTH, past, ATT_WIDTH)
    dec_tiles_per_seq = dec_seq // ROW_TILE

    xp = x_prompt.reshape(batch * seq, D_MODEL)
    xs = x_sample.reshape(dec_batch * dec_seq, D_MODEL)
    ks, vs = [], []
    for l in range(DEPTH):
        filt = (hf_w1[l], hf_b1[l], hf_freq[l], hf_w2[l], hf_b2[l], hf_w3[l])
        g_ctx = _filter_call(seq, *filt)
        g_dec = _filter_call(dec_seq, *filt)
        xp, k_l, v_l = _layer(xp, batch, seq, l, ada4, wts, g_ctx, lambda i: 0, None, None)
        ks.append(k_l.reshape(batch, seq, 2 * DIFF_HEADS, DIFF_HEAD_DIM))
        vs.append(v_l.reshape(batch, seq, DIFF_HEADS, 2 * DIFF_HEAD_DIM))
        xs, _, _ = _layer(xs, dec_batch, dec_seq, l, ada4, wts, g_dec,
                          lambda i: 1 + i // dec_tiles_per_seq, rope_tabs, (cache_k4, cache_v4))
    new_cache_k = jnp.stack(ks, axis=1)
    new_cache_v = jnp.stack(vs, axis=1)
    return (xp.reshape(batch, seq, D_MODEL), xs.reshape(dec_batch, dec_seq, D_MODEL),
            new_cache_k, new_cache_v)
```

```python
import functools
import math

import numpy as np
import jax
import jax.numpy as jnp
from jax import lax
from jax.experimental import pallas as pl
from jax.experimental.pallas import tpu as pltpu

D_MODEL = 1024
DEPTH = 2
GRID_W = 64
HY_WIDTH = D_MODEL // 4
CF_WIDTH = D_MODEL // 4
ATT_WIDTH = D_MODEL // 2
DIFF_HEAD_DIM = 64
DIFF_HEADS = ATT_WIDTH // (2 * DIFF_HEAD_DIM)
HEAD_PAIR = 2 * DIFF_HEAD_DIM
MIX_WIDTH = HY_WIDTH + CF_WIDTH + ATT_WIDTH
IN_WIDTH = 3 * HY_WIDTH + 2 * CF_WIDTH + 3 * ATT_WIDTH
HY_SHORT_K = 3
HY_FILTER_EMB = 33
HY_FILTER_HIDDEN = 64
HY_FAST_DECAY_PCT = 0.3
HY_SLOW_DECAY_PCT = 1.5
HY_DECAY_TARGET = 1e-2
CF_CONV_K = 31
D_FF = 4 * D_MODEL
ROPE_BASE = 10000.0
AX_DIM = DIFF_HEAD_DIM // 2
DEEPNORM_ALPHA = (2 * DEPTH) ** 0.25
LN_EPS = 1e-5
N_ADA = 6

LANES = 128
SUBLANES = 8
COND_ROWS = SUBLANES
VMEM_LIMIT = 56 * 1024 * 1024

ROW_TILE = 512
ADA_COL_TILE = 1536
Q_TILE = 256
MIXER_ROWS = 1024
CF_ROW_CHUNK = 128

F32 = jnp.float32
BF16 = jnp.bfloat16


def _lambda_init(l):
    return 0.8 - 0.6 * math.exp(-0.3 * l)


def _params(**kw):
    return pltpu.CompilerParams(vmem_limit_bytes=VMEM_LIMIT, **kw)


def _const_spec(shape):
    zeros = (0,) * len(shape)
    return pl.BlockSpec(shape, lambda *_: zeros, pipeline_mode=pl.Buffered(1))


def _layer_spec(shape, layer):
    zeros = (0,) * len(shape)
    return pl.BlockSpec((None,) + tuple(shape), lambda *_: (layer,) + zeros,
                        pipeline_mode=pl.Buffered(1))


def _dot(a, b):
    return jnp.dot(a, b, preferred_element_type=F32)


def _dot_f32(a, b):
    return jnp.dot(a, b, preferred_element_type=F32, precision=lax.Precision.HIGHEST)


def _split_bf16(x):
    hi = x.astype(BF16)
    return hi, (x - hi.astype(F32)).astype(BF16)


def _dot_bf16x3(a, b):
    a_hi, a_lo = _split_bf16(a)
    b_hi, b_lo = _split_bf16(b)
    return _dot(a_hi, b_hi) + (_dot(a_hi, b_lo) + _dot(a_lo, b_hi))


def _layer_norm_rows(x, g, b):
    mu = jnp.mean(x, axis=-1, keepdims=True)
    xc = x - mu
    var = jnp.mean(xc * xc, axis=-1, keepdims=True)
    return xc * lax.rsqrt(var + LN_EPS) * g + b


def _seqs_per_step(batch, seq):
    return max(1, min(batch, MIXER_ROWS // seq))


@functools.lru_cache(maxsize=None)
def _dft_tables(L):
    n = 2 * L
    k = np.arange(L, dtype=np.float64)[:, None]
    s = np.arange(L, dtype=np.float64)[None, :]
    ang = 2.0 * np.pi * k * s / n
    fwd = np.concatenate([np.cos(ang), -np.sin(ang)], axis=0)
    fwd[L, :] = np.cos(np.pi * np.arange(L))
    t = np.arange(L, dtype=np.float64)[:, None]
    kk = np.arange(L, dtype=np.float64)[None, :]
    ang_i = 2.0 * np.pi * t * kk / n
    inv_re = (2.0 / n) * np.cos(ang_i)
    inv_re[:, 0] = 1.0 / n
    inv_im = -(2.0 / n) * np.sin(ang_i)
    inv_im[:, 0] = np.cos(np.pi * np.arange(L)) / n
    inv = np.concatenate([inv_re, inv_im], axis=1)
    return fwd.astype(np.float32), inv.astype(np.float32)


@functools.lru_cache(maxsize=None)
def _filter_tables(L):
    bands = (HY_FILTER_EMB - 1) // 2
    t = np.linspace(0.0, 1.0, L)[:, None]
    w = (2.0 * np.pi / L) * np.arange(L, dtype=np.float64)[:, None]
    fr = np.linspace(1e-4, bands - 1, bands)[None, :]
    feats = np.concatenate([t, np.cos(fr * w), -np.sin(fr * w)], -1)
    feats_p = np.zeros((L, LANES), np.float64)
    feats_p[:, :HY_FILTER_EMB] = feats
    deltas = np.abs(np.linspace(math.log(HY_DECAY_TARGET) / HY_FAST_DECAY_PCT,
                                math.log(HY_DECAY_TARGET) / HY_SLOW_DECAY_PCT, HY_WIDTH))
    decay = np.exp(-t * deltas[None, :])
    return feats_p.astype(np.float32), decay.astype(np.float32)


@functools.lru_cache(maxsize=None)
def _rope_tables(n):
    half = AX_DIM // 2
    pos = np.arange(n)
    row = (pos // GRID_W).astype(np.float64)[:, None]
    col = (pos % GRID_W).astype(np.float64)[:, None]
    inv = ROPE_BASE ** (-np.arange(0, AX_DIM, 2, dtype=np.float64) / AX_DIM)[None, :]
    zero = np.zeros((n, half))
    cr, sr = np.cos(row * inv), np.sin(row * inv)
    cc, sc = np.cos(col * inv), np.sin(col * inv)
    c = np.concatenate([cr, cr, cc, cc], -1)
    s_up = np.concatenate([-sr, zero, -sc, zero], -1)
    s_dn = np.concatenate([zero, sr, zero, sc], -1)
    reps = LANES // DIFF_HEAD_DIM
    tile = lambda a: np.tile(a, (1, reps)).astype(np.float32)
    return tile(c), tile(s_up), tile(s_dn)


def _ada_kernel(cond_ref, w_ref, b_ref, o_ref):
    c = cond_ref[...]
    s = c * jax.nn.sigmoid(c)
    o_ref[0] = _dot(s.astype(BF16), w_ref[0].astype(BF16)) + b_ref[0]


def _ada_call(cond, w_ada, b_ada):
    n_out = N_ADA * D_MODEL
    return pl.pallas_call(
        _ada_kernel,
        grid=(DEPTH, n_out // ADA_COL_TILE),
        in_specs=[
            pl.BlockSpec((COND_ROWS, D_MODEL), lambda l, j: (0, 0)),
            pl.BlockSpec((1, D_MODEL, ADA_COL_TILE), lambda l, j: (l, 0, j)),
            pl.BlockSpec((1, 1, ADA_COL_TILE), lambda l, j: (l, 0, j)),
        ],
        out_specs=pl.BlockSpec((1, COND_ROWS, ADA_COL_TILE), lambda l, j: (l, 0, j)),
        out_shape=jax.ShapeDtypeStruct((DEPTH, COND_ROWS, n_out), F32),
        compiler_params=_params(dimension_semantics=("arbitrary", "arbitrary")),
        name="ada",
    )(cond, w_ada, b_ada.reshape(DEPTH, 1, n_out))


def _rope_lanes(x, c, s_up, s_dn):
    outs = []
    for p in range(ATT_WIDTH // LANES):
        xb = x[:, p * LANES:(p + 1) * LANES]
        up = pltpu.roll(xb, LANES - AX_DIM // 2, 1)
        dn = pltpu.roll(xb, AX_DIM // 2, 1)
        outs.append(xb * c + up * s_up + dn * s_dn)
    return jnp.concatenate(outs, axis=1)


Q_SCALE = DIFF_HEAD_DIM ** -0.5 * math.log2(math.e)


def _inproj_kernel(*refs, rope, n_cache_in, seq):
    if rope:
        x_ref, ada_ref, w_ref, c_ref, su_ref, sd_ref, hy_ref, cf_ref, q_ref, k_ref, v_ref = refs
        kc_ref = vc_ref = None
    else:
        x_ref, ada_ref, w_ref = refs[:3]
        hy_ref, cf_ref, q_ref, k_ref, v_ref, kc_ref, vc_ref = refs[3 + n_cache_in:]
    sh1 = ada_ref[0, 0, 0:1, :]
    sc1 = ada_ref[0, 0, 1:2, :]
    h = (x_ref[...] * (1.0 + sc1) + sh1).astype(BF16)
    o1 = 3 * HY_WIDTH
    o2 = o1 + 2 * CF_WIDTH
    o3 = o2 + ATT_WIDTH
    o4 = o3 + ATT_WIDTH
    hy_ref[...] = _dot(h, w_ref[:, 0:o1])
    cf_ref[...] = _dot(h, w_ref[:, o1:o2])
    q = _dot(h, w_ref[:, o2:o3])
    k = _dot(h, w_ref[:, o3:o4])
    v = _dot(h, w_ref[:, o4:IN_WIDTH])
    if rope:
        c, su, sd = c_ref[...], su_ref[...], sd_ref[...]
        q = _rope_lanes(q, c, su, sd)
        k = _rope_lanes(k, c, su, sd)
    q_ref[...] = (q * Q_SCALE).astype(BF16)
    k_ref[...] = k.astype(BF16)
    v_ref[...] = v.astype(BF16)
    if kc_ref is not None:
        for b in range(kc_ref.shape[0]):
            rows = slice(b * seq, (b + 1) * seq)
            kc_ref[b, 0] = k[rows].reshape(seq, 2 * DIFF_HEADS, DIFF_HEAD_DIM)
            vc_ref[b, 0] = v[rows].reshape(seq, DIFF_HEADS, 2 * DIFF_HEAD_DIM)


def _inproj_call(x2d, ada4, w_in_bf, layer, batch, seq, cond_row_of_tile, rope_tabs, caches):
    T = x2d.shape[0]
    tiles_per_seq = seq // ROW_TILE if seq >= ROW_TILE else None
    in_specs = [
        pl.BlockSpec((ROW_TILE, D_MODEL), lambda i: (i, 0)),
        pl.BlockSpec((1, 1, N_ADA, D_MODEL), lambda i: (layer, cond_row_of_tile(i), 0, 0)),
        _layer_spec((D_MODEL, IN_WIDTH), layer),
    ]
    args = [x2d, ada4, w_in_bf]
    rows = lambda w: pl.BlockSpec((ROW_TILE, w), lambda i: (i, 0))
    out_specs = [rows(3 * HY_WIDTH), rows(2 * CF_WIDTH), rows(ATT_WIDTH), rows(ATT_WIDTH), rows(ATT_WIDTH)]
    out_shape = [jax.ShapeDtypeStruct((T, 3 * HY_WIDTH), F32), jax.ShapeDtypeStruct((T, 2 * CF_WIDTH), F32)]
    out_shape += [jax.ShapeDtypeStruct((T, ATT_WIDTH), BF16)] * 3
    aliases = {}
    n_cache_in = 0
    if rope_tabs is not None:
        for tab in rope_tabs:
            in_specs.append(pl.BlockSpec((ROW_TILE, LANES), lambda i: (i % tiles_per_seq, 0)))
            args.append(tab)
    if caches is not None:
        seqs = ROW_TILE // seq
        k_shape = (batch, DEPTH, seq, 2 * DIFF_HEADS, DIFF_HEAD_DIM)
        v_shape = (batch, DEPTH, seq, DIFF_HEADS, 2 * DIFF_HEAD_DIM)
        for shape in (k_shape, v_shape):
            out_specs.append(pl.BlockSpec((seqs, 1) + shape[2:], lambda i: (i, layer, 0, 0, 0)))
            out_shape.append(jax.ShapeDtypeStruct(shape, F32))
        for j, prev in enumerate(caches):
            aliases[len(args)] = len(out_shape) - 2 + j
            in_specs.append(pl.BlockSpec(memory_space=pl.ANY))
            args.append(prev)
        n_cache_in = len(caches)
    return pl.pallas_call(
        functools.partial(_inproj_kernel, rope=rope_tabs is not None, n_cache_in=n_cache_in, seq=seq),
        grid=(T // ROW_TILE,),
        in_specs=in_specs,
        out_specs=out_specs,
        out_shape=out_shape,
        input_output_aliases=aliases,
        compiler_params=_params(dimension_semantics=("arbitrary",)),
        name="inproj",
    )(*args)


def _filter_kernel(feats_ref, w1_ref, b1_ref, fq_ref, w2_ref, b2_ref, w3_ref, decay_ref, fwd_ref,
                   ga_ref, gbc_ref, gd_ref):
    L = feats_ref.shape[0]
    fq = fq_ref[...]
    hid = jnp.sin(fq * (_dot_f32(feats_ref[...], w1_ref[...]) + b1_ref[...]))
    hid = jnp.sin(fq * (_dot_f32(hid, w2_ref[...]) + b2_ref[...]))
    hf = _dot_f32(hid, w3_ref[...])
    decay = decay_ref[...]
    row = lax.broadcasted_iota(jnp.int32, (L, HY_WIDTH), 0)
    h_fwd = hf[:, :HY_WIDTH] * decay
    h_bwd = jnp.where(row == 0, 0.0, hf[:, HY_WIDTH:] * decay)
    spec = _dot_bf16x3(fwd_ref[...], jnp.concatenate([h_fwd, h_bwd], axis=1))
    p = spec[:L]
    q = spec[L:]
    g_re = p[:, :HY_WIDTH] + p[:, HY_WIDTH:]
    g_im = q[:, :HY_WIDTH] - q[:, HY_WIDTH:]
    nyq = q[0:1, :HY_WIDTH] + q[0:1, HY_WIDTH:]
    ga_ref[...] = g_re
    gbc_ref[...] = jnp.where(row == 0, 0.0, g_im)
    gd_ref[...] = jnp.where(row == 0, nyq, g_re)


def _pad_to(a, shape):
    return jnp.pad(a, [(0, s - d) for s, d in zip(shape, a.shape)])


def _filter_call(L, w1, b1, fq, w2, b2, w3):
    feats, decay = _filter_tables(L)
    fwd, _ = _dft_tables(L)
    args = [
        jnp.asarray(feats),
        _pad_to(w1, (LANES, LANES)),
        _pad_to(b1.reshape(1, -1), (1, LANES)),
        _pad_to(fq.reshape(1, -1), (1, LANES)),
        _pad_to(w2, (LANES, LANES)),
        _pad_to(b2.reshape(1, -1), (1, LANES)),
        _pad_to(w3, (LANES, 2 * HY_WIDTH)),
        jnp.asarray(decay),
        jnp.asarray(fwd),
    ]
    out = jax.ShapeDtypeStruct((L, HY_WIDTH), F32)
    return pl.pallas_call(
        _filter_kernel,
        out_shape=[out, out, out],
        compiler_params=_params(),
        name="hyena_filter",
    )(*args)


def _hyena_kernel(hy_ref, cw_ref, cb_ref, fwd_ref, inv_ref, ga_ref, gbc_ref, gd_ref, hb_ref,
                  o_ref, pad_ref):
    nb, L, width = hy_ref.shape
    zeros = jnp.zeros((SUBLANES, width), F32)
    x0s, us = [], []
    for s in range(nb):
        pad_ref[s, 0:SUBLANES, :] = zeros
        pad_ref[s, SUBLANES + L:2 * SUBLANES + L, :] = zeros
        pad_ref[s, SUBLANES:SUBLANES + L, :] = hy_ref[s]
        conv = cb_ref[...]
        for j in range(HY_SHORT_K):
            start = SUBLANES + j - HY_SHORT_K // 2
            conv = conv + cw_ref[j:j + 1, :] * pad_ref[s, start:start + L, :]
        x0s.append(conv[:, :HY_WIDTH])
        us.append(conv[:, HY_WIDTH:2 * HY_WIDTH] * conv[:, 2 * HY_WIDTH:])
    u_all = jnp.concatenate(us, axis=1).astype(BF16)
    spec = _dot(fwd_ref[...], u_all)
    ga, gbc, gd = ga_ref[...], gbc_ref[...], gd_ref[...]
    y_re, y_im = [], []
    for s in range(nb):
        p = spec[:L, s * HY_WIDTH:(s + 1) * HY_WIDTH]
        q = spec[L:, s * HY_WIDTH:(s + 1) * HY_WIDTH]
        y_re.append(p * ga - q * gbc)
        y_im.append(p * gbc + q * gd)
    y_spec = jnp.concatenate(
        [jnp.concatenate(y_re, axis=1), jnp.concatenate(y_im, axis=1)], axis=0).astype(BF16)
    y = _dot(inv_ref[...], y_spec)
    for s in range(nb):
        o_ref[s] = x0s[s] * (y[:, s * HY_WIDTH:(s + 1) * HY_WIDTH] + us[s] * hb_ref[...])


def _hyena_call(hy3, conv_w, conv_b, g_tabs, hy_bias, layer):
    B, L, width = hy3.shape
    nb = _seqs_per_step(B, L)
    fwd, inv = _dft_tables(L)
    fwd_bf = jnp.asarray(fwd).astype(BF16)
    inv_bf = jnp.asarray(inv).astype(BF16)
    tab = _const_spec((L, HY_WIDTH))
    return pl.pallas_call(
        _hyena_kernel,
        grid=(B // nb,),
        in_specs=[
            pl.BlockSpec((nb, L, width), lambda b: (b, 0, 0)),
            _layer_spec((HY_SHORT_K, width), layer),
            _layer_spec((1, width), layer),
            _const_spec((2 * L, L)),
            _const_spec((L, 2 * L)),
            tab, tab, tab,
            _layer_spec((1, HY_WIDTH), layer),
        ],
        out_specs=pl.BlockSpec((nb, L, HY_WIDTH), lambda b: (b, 0, 0)),
        out_shape=jax.ShapeDtypeStruct((B, L, HY_WIDTH), F32),
        scratch_shapes=[pltpu.VMEM((nb, L + 2 * SUBLANES, width), F32)],
        compiler_params=_params(dimension_semantics=("arbitrary",)),
        name="hyena",
    )(hy3, conv_w, conv_b, fwd_bf, inv_bf, *g_tabs, hy_bias)


CF_PAD = 2 * SUBLANES


def _conformer_kernel(cf_ref, cw_ref, cb_ref, g_ref, b_ref, o_ref, pad_ref):
    nb, L, _ = cf_ref.shape
    padded = L + 2 * CF_PAD
    shifted = padded - SUBLANES
    zeros = jnp.zeros((CF_PAD, CF_WIDTH), F32)
    half = CF_CONV_K // 2
    for s in range(nb):
        cf = cf_ref[s]
        pad_ref[s, 0, 0:CF_PAD, :] = zeros
        pad_ref[s, 0, CF_PAD + L:padded, :] = zeros
        pad_ref[s, 0, CF_PAD:CF_PAD + L, :] = cf[:, :CF_WIDTH] * jax.nn.sigmoid(cf[:, CF_WIDTH:])
        for r in range(1, SUBLANES):
            pad_ref[s, r, 0:shifted, :] = pad_ref[s, 0, r:r + shifted, :]
        for c0 in range(0, L, CF_ROW_CHUNK):
            acc = jnp.zeros((CF_ROW_CHUNK, CF_WIDTH), F32) + cb_ref[...]
            for j in range(CF_CONV_K):
                off = CF_PAD + j - half
                start = c0 + (off // SUBLANES) * SUBLANES
                acc = acc + cw_ref[j:j + 1, :] * pad_ref[s, off % SUBLANES, start:start + CF_ROW_CHUNK, :]
            y = _layer_norm_rows(acc, g_ref[...], b_ref[...])
            o_ref[s, c0:c0 + CF_ROW_CHUNK, :] = y * jax.nn.sigmoid(y)


def _conformer_call(cf3, conv_w, conv_b, ln_g, ln_b, layer):
    B, L, width = cf3.shape
    nb = _seqs_per_step(B, L)
    row = _layer_spec((1, CF_WIDTH), layer)
    return pl.pallas_call(
        _conformer_kernel,
        grid=(B // nb,),
        in_specs=[
            pl.BlockSpec((nb, L, width), lambda b: (b, 0, 0)),
            _layer_spec((CF_CONV_K, CF_WIDTH), layer),
            row, row, row,
        ],
        out_specs=pl.BlockSpec((nb, L, CF_WIDTH), lambda b: (b, 0, 0)),
        out_shape=jax.ShapeDtypeStruct((B, L, CF_WIDTH), F32),
        scratch_shapes=[pltpu.VMEM((nb, SUBLANES, L + 2 * CF_PAD, CF_WIDTH), F32)],
        compiler_params=_params(dimension_semantics=("arbitrary",)),
        name="conformer",
    )(cf3, conv_w, conv_b, ln_g, ln_b)


def _attn_kernel(*refs, lambda_init, past):
    if past:
        (q_ref, k_ref, v_ref, ck_ref, cv_ref, lq1, lk1, lq2, lk2, g_ref, o_ref, k_all, v_all) = refs
    else:
        (q_ref, k_ref, v_ref, lq1, lk1, lq2, lk2, g_ref, o_ref) = refs
        k_all, v_all = k_ref, v_ref
    nb = q_ref.shape[0]

    if past:
        @pl.when(pl.program_id(1) == 0)
        def _():
            for s in range(nb):
                k_all[s, 0:past, :] = ck_ref[s, 0].reshape(past, ATT_WIDTH).astype(BF16)
                v_all[s, 0:past, :] = cv_ref[s, 0].reshape(past, ATT_WIDTH).astype(BF16)
                k_all[s, past:, :] = k_ref[s]
                v_all[s, past:, :] = v_ref[s]

    lam = (jnp.exp(jnp.sum(lq1[...] * lk1[...], axis=-1, keepdims=True))
           - jnp.exp(jnp.sum(lq2[...] * lk2[...], axis=-1, keepdims=True)) + lambda_init)
    lane = lax.broadcasted_iota(jnp.int32, (q_ref.shape[1], HEAD_PAIR), 1)
    first = lane < DIFF_HEAD_DIM
    nt = (((1,), (1,)), ((), ()))
    gain = g_ref[...] * (1.0 - lambda_init)
    zero = jnp.zeros((), BF16)

    def probs(qm, kh):
        s = lax.dot_general(qm, kh, nt, preferred_element_type=F32)
        e = jnp.exp2(s - jnp.max(s, axis=-1, keepdims=True))
        return e, 1.0 / jnp.sum(e, axis=-1, keepdims=True)

    for s in range(nb):
        for h in range(DIFF_HEADS):
            cols = slice(h * HEAD_PAIR, (h + 1) * HEAD_PAIR)
            q = q_ref[s, :, cols]
            kh = k_all[s, :, cols]
            e0, r0 = probs(jnp.where(first, q, zero), kh)
            e1, r1 = probs(jnp.where(first, zero, q), kh)
            a = e0 * r0 - e1 * (lam * r1)
            o = _dot(a.astype(BF16), v_all[s, :, cols])
            o_ref[s, :, cols] = o * lax.rsqrt(jnp.mean(o * o, axis=-1, keepdims=True) + LN_EPS) * gain


def _attn_call(q3, k3, v3, cache_k, cache_v, layer, lam_vecs, subln_g):
    B, L, _ = q3.shape
    past = 0 if cache_k is None else cache_k.shape[2]
    nb = _seqs_per_step(B, 2 * L)
    tile = pl.BlockSpec((nb, Q_TILE, ATT_WIDTH), lambda b, i: (b, i, 0))
    seq = pl.BlockSpec((nb, L, ATT_WIDTH), lambda b, i: (b, 0, 0))
    in_specs = [tile, seq, seq]
    args = [q3, k3, v3]
    scratch = []
    if past:
        for cache in (cache_k, cache_v):
            in_specs.append(pl.BlockSpec((nb, 1) + cache.shape[2:], lambda b, i: (b, layer, 0, 0, 0)))
            args.append(cache)
        scratch = [pltpu.VMEM((nb, past + L, ATT_WIDTH), BF16)] * 2
    in_specs += [_layer_spec((1, DIFF_HEAD_DIM), layer)] * 4 + [_layer_spec((1, HEAD_PAIR), layer)]
    args += list(lam_vecs) + [subln_g]
    return pl.pallas_call(
        functools.partial(_attn_kernel, lambda_init=_lambda_init(layer), past=past),
        grid=(B // nb, L // Q_TILE),
        in_specs=in_specs,
        out_specs=tile,
        out_shape=jax.ShapeDtypeStruct((B, L, ATT_WIDTH), F32),
        scratch_shapes=scratch,
        compiler_params=_params(dimension_semantics=("arbitrary", "arbitrary")),
        name="diff_attn",
    )(*args)


def _mlp_kernel(x_ref, hy_ref, cf_ref, at_ref, ada_ref, wo_ref, w1_ref, w2_ref,
                g1_ref, b1_ref, g2_ref, b2_ref, o_ref):
    gate1 = ada_ref[0, 0, 2:3, :]
    sh2 = ada_ref[0, 0, 3:4, :]
    sc2 = ada_ref[0, 0, 4:5, :]
    gate2 = ada_ref[0, 0, 5:6, :]
    mix = jnp.concatenate([hy_ref[...], cf_ref[...], at_ref[...]], axis=1).astype(BF16)
    y = _dot(mix, wo_ref[...])
    x = _layer_norm_rows(DEEPNORM_ALPHA * x_ref[...] + gate1 * y, g1_ref[...], b1_ref[...])
    h = (x * (1.0 + sc2) + sh2).astype(BF16)
    a = jnp.maximum(_dot(h, w1_ref[...]), 0.0)
    m = _dot((a * a).astype(BF16), w2_ref[...])
    o_ref[...] = _layer_norm_rows(DEEPNORM_ALPHA * x + gate2 * m, g2_ref[...], b2_ref[...])


def _mlp_call(x2d, y_hy, y_cf, y_at, ada4, wo_bf, w1_bf, w2_bf, ln1_g, ln1_b, ln2_g, ln2_b,
              layer, cond_row_of_tile):
    T = x2d.shape[0]
    rows = lambda w: pl.BlockSpec((ROW_TILE, w), lambda i: (i, 0))
    vec = _layer_spec((1, D_MODEL), layer)
    return pl.pallas_call(
        _mlp_kernel,
        grid=(T // ROW_TILE,),
        in_specs=[
            rows(D_MODEL), rows(HY_WIDTH), rows(CF_WIDTH), rows(ATT_WIDTH),
            pl.BlockSpec((1, 1, N_ADA, D_MODEL), lambda i: (layer, cond_row_of_tile(i), 0, 0)),
            _layer_spec((MIX_WIDTH, D_MODEL), layer),
            _layer_spec((D_MODEL, D_FF), layer),
            _layer_spec((D_FF, D_MODEL), layer),
            vec, vec, vec, vec,
        ],
        out_specs=rows(D_MODEL),
        out_shape=jax.ShapeDtypeStruct((T, D_MODEL), F32),
        compiler_params=_params(dimension_semantics=("arbitrary",)),
        name="outproj_mlp",
    )(x2d, y_hy, y_cf, y_at, ada4, wo_bf, w1_bf, w2_bf, ln1_g, ln1_b, ln2_g, ln2_b)


def _layer(x2d, batch, seq, layer, ada4, wts, g_tabs, cond_row_of_tile, rope_tabs, cache_kv, new_caches):
    hy, cf, q, k, v, *new_caches = _inproj_call(
        x2d, ada4, wts["w_in"], layer, batch, seq, cond_row_of_tile, rope_tabs, new_caches)
    as_seq = lambda a: a.reshape(batch, seq, a.shape[-1])
    y_hy = _hyena_call(as_seq(hy), wts["hy_conv_w"], wts["hy_conv_b"], g_tabs, wts["hy_bias"], layer)
    y_cf = _conformer_call(as_seq(cf), wts["cf_conv_w"], wts["cf_conv_b"], wts["cf_ln_g"],
                           wts["cf_ln_b"], layer)
    ck, cv = cache_kv if cache_kv is not None else (None, None)
    lam_vecs = (wts["lam_q1"], wts["lam_k1"], wts["lam_q2"], wts["lam_k2"])
    y_at = _attn_call(as_seq(q), as_seq(k), as_seq(v), ck, cv, layer, lam_vecs, wts["subln_g"])
    flat = lambda a: a.reshape(batch * seq, a.shape[-1])
    x_out = _mlp_call(x2d, flat(y_hy), flat(y_cf), flat(y_at), ada4,
                      wts["w_out"], wts["w_mlp1"], wts["w_mlp2"],
                      wts["ln1_g"], wts["ln1_b"], wts["ln2_g"], wts["ln2_b"], layer, cond_row_of_tile)
    return x_out, tuple(new_caches)


def kernel(x_prompt, x_sample, cache_k, cache_v, c, c_ctx, w_ada, b_ada, w_in, hy_conv_w, hy_conv_b, hf_w1, hf_b1, hf_freq, hf_w2, hf_b2, hf_w3, hy_bias, cf_conv_w, cf_conv_b, cf_ln_g, cf_ln_b, lam_q1, lam_k1, lam_q2, lam_k2, subln_g, w_out, ln1_g, ln1_b, w_mlp1, w_mlp2, ln2_g, ln2_b):
    batch, seq, _ = x_prompt.shape
    dec_batch, dec_seq, _ = x_sample.shape
    past = cache_k.shape[2]
    assert 1 + dec_batch <= COND_ROWS
    assert seq % Q_TILE == 0 and dec_seq % ROW_TILE == 0 and (batch * seq) % ROW_TILE == 0

    cond = jnp.concatenate(
        [c_ctx[None, :], c, jnp.zeros((COND_ROWS - 1 - dec_batch, D_MODEL), F32)], axis=0)
    ada4 = _ada_call(cond, w_ada, b_ada).reshape(DEPTH, COND_ROWS, N_ADA, D_MODEL)

    row_vec = lambda a: a.reshape(DEPTH, 1, a.shape[-1])
    wts = dict(
        w_in=w_in.astype(BF16), w_out=w_out.astype(BF16),
        w_mlp1=w_mlp1.astype(BF16), w_mlp2=w_mlp2.astype(BF16),
        hy_conv_w=hy_conv_w, hy_conv_b=row_vec(hy_conv_b), hy_bias=row_vec(hy_bias),
        cf_conv_w=cf_conv_w, cf_conv_b=row_vec(cf_conv_b),
        cf_ln_g=row_vec(cf_ln_g), cf_ln_b=row_vec(cf_ln_b),
        lam_q1=row_vec(lam_q1), lam_k1=row_vec(lam_k1), lam_q2=row_vec(lam_q2), lam_k2=row_vec(lam_k2),
        subln_g=row_vec(subln_g),
        ln1_g=row_vec(ln1_g), ln1_b=row_vec(ln1_b), ln2_g=row_vec(ln2_g), ln2_b=row_vec(ln2_b))

    rope_tabs = tuple(jnp.asarray(t) for t in _rope_tables(dec_seq))
    dec_tiles_per_seq = dec_seq // ROW_TILE

    xp = x_prompt.reshape(batch * seq, D_MODEL)
    xs = x_sample.reshape(dec_batch * dec_seq, D_MODEL)
    new_caches = ()
    for l in range(DEPTH):
        filt = (hf_w1[l], hf_b1[l], hf_freq[l], hf_w2[l], hf_b2[l], hf_w3[l])
        g_ctx = _filter_call(seq, *filt)
        g_dec = _filter_call(dec_seq, *filt)
        xp, new_caches = _layer(xp, batch, seq, l, ada4, wts, g_ctx, lambda i: 0, None, None, new_caches)
        xs, _ = _layer(xs, dec_batch, dec_seq, l, ada4, wts, g_dec,
                       lambda i: 1 + i // dec_tiles_per_seq, rope_tabs, (cache_k, cache_v), None)
    new_cache_k, new_cache_v = new_caches
    return (xp.reshape(batch, seq, D_MODEL), xs.reshape(dec_batch, dec_seq, D_MODEL),
            new_cache_k, new_cache_v)
```

```python
import functools
import math

import numpy as np
import jax
import jax.numpy as jnp
from jax import lax
from jax.experimental import pallas as pl
from jax.experimental.pallas import tpu as pltpu

D_MODEL = 1024
DEPTH = 2
GRID_W = 64
HY_WIDTH = D_MODEL // 4
CF_WIDTH = D_MODEL // 4
ATT_WIDTH = D_MODEL // 2
DIFF_HEAD_DIM = 64
DIFF_HEADS = ATT_WIDTH // (2 * DIFF_HEAD_DIM)
HEAD_PAIR = 2 * DIFF_HEAD_DIM
MIX_WIDTH = HY_WIDTH + CF_WIDTH + ATT_WIDTH
IN_WIDTH = 3 * HY_WIDTH + 2 * CF_WIDTH + 3 * ATT_WIDTH
HY_SHORT_K = 3
HY_FILTER_EMB = 33
HY_FILTER_HIDDEN = 64
HY_FAST_DECAY_PCT = 0.3
HY_SLOW_DECAY_PCT = 1.5
HY_DECAY_TARGET = 1e-2
CF_CONV_K = 31
D_FF = 4 * D_MODEL
ROPE_BASE = 10000.0
AX_DIM = DIFF_HEAD_DIM // 2
DEEPNORM_ALPHA = (2 * DEPTH) ** 0.25
LN_EPS = 1e-5
N_ADA = 6

LANES = 128
SUBLANES = 8
COND_ROWS = SUBLANES
VMEM_LIMIT = 56 * 1024 * 1024

ROW_TILE = 512
ADA_COL_TILE = 1536
Q_TILE = 512
MIXER_ROWS = 1024
CF_ROW_CHUNK = 128

F32 = jnp.float32
BF16 = jnp.bfloat16


def _lambda_init(l):
    return 0.8 - 0.6 * math.exp(-0.3 * l)


def _params(**kw):
    return pltpu.CompilerParams(vmem_limit_bytes=VMEM_LIMIT, **kw)


def _const_spec(shape):
    zeros = (0,) * len(shape)
    return pl.BlockSpec(shape, lambda *_: zeros, pipeline_mode=pl.Buffered(1))


def _layer_spec(shape, layer):
    zeros = (0,) * len(shape)
    return pl.BlockSpec((None,) + tuple(shape), lambda *_: (layer,) + zeros,
                        pipeline_mode=pl.Buffered(1))


def _dot(a, b):
    return jnp.dot(a, b, preferred_element_type=F32)


def _dot_f32(a, b):
    return jnp.dot(a, b, preferred_element_type=F32, precision=lax.Precision.HIGHEST)


def _split_bf16(x):
    hi = x.astype(BF16)
    return hi, (x - hi.astype(F32)).astype(BF16)


def _layer_norm_rows(x, g, b):
    mu = jnp.mean(x, axis=-1, keepdims=True)
    xc = x - mu
    var = jnp.mean(xc * xc, axis=-1, keepdims=True)
    return xc * lax.rsqrt(var + LN_EPS) * g + b


def _seqs_per_step(batch, seq):
    return max(1, min(batch, MIXER_ROWS // seq))


@functools.lru_cache(maxsize=None)
def _dft_tables(L):
    n = 2 * L
    k = np.arange(L, dtype=np.float64)[:, None]
    s = np.arange(L, dtype=np.float64)[None, :]
    ang = 2.0 * np.pi * k * s / n
    fwd = np.concatenate([np.cos(ang), -np.sin(ang)], axis=0)
    fwd[L, :] = np.cos(np.pi * np.arange(L))
    t = np.arange(L, dtype=np.float64)[:, None]
    kk = np.arange(L, dtype=np.float64)[None, :]
    ang_i = 2.0 * np.pi * t * kk / n
    inv_re = (2.0 / n) * np.cos(ang_i)
    inv_re[:, 0] = 1.0 / n
    inv_im = -(2.0 / n) * np.sin(ang_i)
    inv_im[:, 0] = np.cos(np.pi * np.arange(L)) / n
    inv = np.concatenate([inv_re, inv_im], axis=1)
    return fwd.astype(np.float32), inv.astype(np.float32)


@functools.lru_cache(maxsize=None)
def _filter_tables(L):
    bands = (HY_FILTER_EMB - 1) // 2
    t = np.linspace(0.0, 1.0, L)[:, None]
    w = (2.0 * np.pi / L) * np.arange(L, dtype=np.float64)[:, None]
    fr = np.linspace(1e-4, bands - 1, bands)[None, :]
    feats = np.concatenate([t, np.cos(fr * w), -np.sin(fr * w)], -1)
    feats_p = np.zeros((L, LANES), np.float64)
    feats_p[:, :HY_FILTER_EMB] = feats
    deltas = np.abs(np.linspace(math.log(HY_DECAY_TARGET) / HY_FAST_DECAY_PCT,
                                math.log(HY_DECAY_TARGET) / HY_SLOW_DECAY_PCT, HY_WIDTH))
    decay = np.exp(-t * deltas[None, :])
    return feats_p.astype(np.float32), decay.astype(np.float32)


@functools.lru_cache(maxsize=None)
def _rope_tables(n):
    half = AX_DIM // 2
    pos = np.arange(n)
    row = (pos // GRID_W).astype(np.float64)[:, None]
    col = (pos % GRID_W).astype(np.float64)[:, None]
    inv = ROPE_BASE ** (-np.arange(0, AX_DIM, 2, dtype=np.float64) / AX_DIM)[None, :]
    zero = np.zeros((n, half))
    cr, sr = np.cos(row * inv), np.sin(row * inv)
    cc, sc = np.cos(col * inv), np.sin(col * inv)
    c = np.concatenate([cr, cr, cc, cc], -1)
    s_up = np.concatenate([-sr, zero, -sc, zero], -1)
    s_dn = np.concatenate([zero, sr, zero, sc], -1)
    reps = LANES // DIFF_HEAD_DIM
    tile = lambda a: np.tile(a, (1, reps)).astype(np.float32)
    return tile(c), tile(s_up), tile(s_dn)


def _ada_kernel(cond_ref, w_ref, b_ref, o_ref):
    c = cond_ref[...]
    s = c * jax.nn.sigmoid(c)
    o_ref[0] = _dot(s.astype(BF16), w_ref[0].astype(BF16)) + b_ref[0]


def _ada_call(cond, w_ada, b_ada):
    n_out = N_ADA * D_MODEL
    return pl.pallas_call(
        _ada_kernel,
        grid=(DEPTH, n_out // ADA_COL_TILE),
        in_specs=[
            pl.BlockSpec((COND_ROWS, D_MODEL), lambda l, j: (0, 0)),
            pl.BlockSpec((1, D_MODEL, ADA_COL_TILE), lambda l, j: (l, 0, j)),
            pl.BlockSpec((1, 1, ADA_COL_TILE), lambda l, j: (l, 0, j)),
        ],
        out_specs=pl.BlockSpec((1, COND_ROWS, ADA_COL_TILE), lambda l, j: (l, 0, j)),
        out_shape=jax.ShapeDtypeStruct((DEPTH, COND_ROWS, n_out), F32),
        compiler_params=_params(dimension_semantics=("arbitrary", "arbitrary")),
        name="ada",
    )(cond, w_ada, b_ada.reshape(DEPTH, 1, n_out))


def _rope_lanes(x, c, s_up, s_dn):
    outs = []
    for p in range(ATT_WIDTH // LANES):
        xb = x[:, p * LANES:(p + 1) * LANES]
        up = pltpu.roll(xb, LANES - AX_DIM // 2, 1)
        dn = pltpu.roll(xb, AX_DIM // 2, 1)
        outs.append(xb * c + up * s_up + dn * s_dn)
    return jnp.concatenate(outs, axis=1)


Q_SCALE = DIFF_HEAD_DIM ** -0.5 * math.log2(math.e)


def _inproj_kernel(*refs, rope, n_cache_in, seq):
    if rope:
        x_ref, ada_ref, w_ref, c_ref, su_ref, sd_ref, hy_ref, cf_ref, q_ref, k_ref, v_ref = refs
        kc_ref = vc_ref = None
    else:
        x_ref, ada_ref, w_ref = refs[:3]
        hy_ref, cf_ref, q_ref, k_ref, v_ref, kc_ref, vc_ref = refs[3 + n_cache_in:]
    sh1 = ada_ref[0, 0, 0:1, :]
    sc1 = ada_ref[0, 0, 1:2, :]
    h = (x_ref[...] * (1.0 + sc1) + sh1).astype(BF16)
    o1 = 3 * HY_WIDTH
    o2 = o1 + 2 * CF_WIDTH
    o3 = o2 + ATT_WIDTH
    o4 = o3 + ATT_WIDTH
    hy_ref[...] = _dot(h, w_ref[:, 0:o1])
    cf_ref[...] = _dot(h, w_ref[:, o1:o2])
    q = _dot(h, w_ref[:, o2:o3])
    k = _dot(h, w_ref[:, o3:o4])
    v = _dot(h, w_ref[:, o4:IN_WIDTH])
    if rope:
        c, su, sd = c_ref[...], su_ref[...], sd_ref[...]
        q = _rope_lanes(q, c, su, sd)
        k = _rope_lanes(k, c, su, sd)
    q_ref[...] = (q * Q_SCALE).astype(BF16)
    k_ref[...] = k.astype(BF16)
    v_ref[...] = v.astype(BF16)
    if kc_ref is not None:
        for b in range(kc_ref.shape[0]):
            rows = slice(b * seq, (b + 1) * seq)
            kc_ref[b, 0] = k[rows].reshape(seq, 2 * DIFF_HEADS, DIFF_HEAD_DIM)
            vc_ref[b, 0] = v[rows].reshape(seq, DIFF_HEADS, 2 * DIFF_HEAD_DIM)


def _inproj_call(x2d, ada4, w_in_bf, layer, batch, seq, cond_row_of_tile, rope_tabs, caches):
    T = x2d.shape[0]
    tiles_per_seq = seq // ROW_TILE if seq >= ROW_TILE else None
    in_specs = [
        pl.BlockSpec((ROW_TILE, D_MODEL), lambda i: (i, 0)),
        pl.BlockSpec((1, 1, N_ADA, D_MODEL), lambda i: (layer, cond_row_of_tile(i), 0, 0)),
        _layer_spec((D_MODEL, IN_WIDTH), layer),
    ]
    args = [x2d, ada4, w_in_bf]
    rows = lambda w: pl.BlockSpec((ROW_TILE, w), lambda i: (i, 0))
    out_specs = [rows(3 * HY_WIDTH), rows(2 * CF_WIDTH), rows(ATT_WIDTH), rows(ATT_WIDTH), rows(ATT_WIDTH)]
    out_shape = [jax.ShapeDtypeStruct((T, 3 * HY_WIDTH), F32), jax.ShapeDtypeStruct((T, 2 * CF_WIDTH), F32)]
    out_shape += [jax.ShapeDtypeStruct((T, ATT_WIDTH), BF16)] * 3
    aliases = {}
    n_cache_in = 0
    if rope_tabs is not None:
        for tab in rope_tabs:
            in_specs.append(pl.BlockSpec((ROW_TILE, LANES), lambda i: (i % tiles_per_seq, 0)))
            args.append(tab)
    if caches is not None:
        seqs = ROW_TILE // seq
        k_shape = (batch, DEPTH, seq, 2 * DIFF_HEADS, DIFF_HEAD_DIM)
        v_shape = (batch, DEPTH, seq, DIFF_HEADS, 2 * DIFF_HEAD_DIM)
        for shape in (k_shape, v_shape):
            out_specs.append(pl.BlockSpec((seqs, 1) + shape[2:], lambda i: (i, layer, 0, 0, 0)))
            out_shape.append(jax.ShapeDtypeStruct(shape, F32))
        for j, prev in enumerate(caches):
            aliases[len(args)] = len(out_shape) - 2 + j
            in_specs.append(pl.BlockSpec(memory_space=pl.ANY))
            args.append(prev)
        n_cache_in = len(caches)
    return pl.pallas_call(
        functools.partial(_inproj_kernel, rope=rope_tabs is not None, n_cache_in=n_cache_in, seq=seq),
        grid=(T // ROW_TILE,),
        in_specs=in_specs,
        out_specs=out_specs,
        out_shape=out_shape,
        input_output_aliases=aliases,
        compiler_params=_params(dimension_semantics=("arbitrary",)),
        name="inproj",
    )(*args)


def _filter_kernel(feats_ref, w1_ref, b1_ref, fq_ref, w2_ref, b2_ref, w3_ref, decay_ref,
                   fwd_hi_ref, fwd_lo_ref, ga_ref, gbc_ref, gd_ref):
    L = feats_ref.shape[0]
    fq = fq_ref[...]
    hid = jnp.sin(fq * (_dot_f32(feats_ref[...], w1_ref[...]) + b1_ref[...]))
    hid = jnp.sin(fq * (_dot_f32(hid, w2_ref[...]) + b2_ref[...]))
    hf = _dot_f32(hid, w3_ref[...])
    decay = decay_ref[...]
    row = lax.broadcasted_iota(jnp.int32, (L, HY_WIDTH), 0)
    h_fwd = hf[:, :HY_WIDTH] * decay
    h_bwd = jnp.where(row == 0, 0.0, hf[:, HY_WIDTH:] * decay)
    h_hi, h_lo = _split_bf16(jnp.concatenate([h_fwd, h_bwd], axis=1))
    fwd_hi = fwd_hi_ref[...]
    spec = _dot(fwd_hi, h_hi) + (_dot(fwd_hi, h_lo) + _dot(fwd_lo_ref[...], h_hi))
    p = spec[:L]
    q = spec[L:]
    g_re = p[:, :HY_WIDTH] + p[:, HY_WIDTH:]
    g_im = q[:, :HY_WIDTH] - q[:, HY_WIDTH:]
    nyq = q[0:1, :HY_WIDTH] + q[0:1, HY_WIDTH:]
    ga_ref[...] = g_re
    gbc_ref[...] = jnp.where(row == 0, 0.0, g_im)
    gd_ref[...] = jnp.where(row == 0, nyq, g_re)


def _pad_to(a, shape):
    return jnp.pad(a, [(0, s - d) for s, d in zip(shape, a.shape)])


def _filter_call(L, w1, b1, fq, w2, b2, w3):
    feats, decay = _filter_tables(L)
    fwd, _ = _dft_tables(L)
    args = [
        jnp.asarray(feats),
        _pad_to(w1, (LANES, LANES)),
        _pad_to(b1.reshape(1, -1), (1, LANES)),
        _pad_to(fq.reshape(1, -1), (1, LANES)),
        _pad_to(w2, (LANES, LANES)),
        _pad_to(b2.reshape(1, -1), (1, LANES)),
        _pad_to(w3, (LANES, 2 * HY_WIDTH)),
        jnp.asarray(decay),
        *_split_bf16(jnp.asarray(fwd)),
    ]
    out = jax.ShapeDtypeStruct((L, HY_WIDTH), F32)
    return pl.pallas_call(
        _filter_kernel,
        out_shape=[out, out, out],
        compiler_params=_params(),
        name="hyena_filter",
    )(*args)


def _hyena_kernel(hy_ref, cw_ref, cb_ref, fwd_ref, inv_ref, ga_ref, gbc_ref, gd_ref, hb_ref,
                  o_ref, pad_ref):
    nb, L, width = hy_ref.shape
    zeros = jnp.zeros((SUBLANES, width), F32)
    x0s, us = [], []
    for s in range(nb):
        pad_ref[s, 0:SUBLANES, :] = zeros
        pad_ref[s, SUBLANES + L:2 * SUBLANES + L, :] = zeros
        pad_ref[s, SUBLANES:SUBLANES + L, :] = hy_ref[s]
        conv = cb_ref[...]
        for j in range(HY_SHORT_K):
            start = SUBLANES + j - HY_SHORT_K // 2
            conv = conv + cw_ref[j:j + 1, :] * pad_ref[s, start:start + L, :]
        x0s.append(conv[:, :HY_WIDTH])
        us.append(conv[:, HY_WIDTH:2 * HY_WIDTH] * conv[:, 2 * HY_WIDTH:])
    u_all = jnp.concatenate(us, axis=1).astype(BF16)
    spec = _dot(fwd_ref[...], u_all)
    ga, gbc, gd = ga_ref[...], gbc_ref[...], gd_ref[...]
    y_re, y_im = [], []
    for s in range(nb):
        p = spec[:L, s * HY_WIDTH:(s + 1) * HY_WIDTH]
        q = spec[L:, s * HY_WIDTH:(s + 1) * HY_WIDTH]
        y_re.append(p * ga - q * gbc)
        y_im.append(p * gbc + q * gd)
    y_spec = jnp.concatenate(
        [jnp.concatenate(y_re, axis=1), jnp.concatenate(y_im, axis=1)], axis=0).astype(BF16)
    y = _dot(inv_ref[...], y_spec)
    for s in range(nb):
        o = x0s[s] * (y[:, s * HY_WIDTH:(s + 1) * HY_WIDTH] + us[s] * hb_ref[...])
        o_ref[s] = o.astype(o_ref.dtype)


def _hyena_call(hy3, conv_w, conv_b, g_tabs, hy_bias, layer):
    B, L, width = hy3.shape
    nb = _seqs_per_step(B, L)
    fwd, inv = _dft_tables(L)
    fwd_bf = jnp.asarray(fwd).astype(BF16)
    inv_bf = jnp.asarray(inv).astype(BF16)
    tab = _const_spec((L, HY_WIDTH))
    return pl.pallas_call(
        _hyena_kernel,
        grid=(B // nb,),
        in_specs=[
            pl.BlockSpec((nb, L, width), lambda b: (b, 0, 0)),
            _layer_spec((HY_SHORT_K, width), layer),
            _layer_spec((1, width), layer),
            _const_spec((2 * L, L)),
            _const_spec((L, 2 * L)),
            tab, tab, tab,
            _layer_spec((1, HY_WIDTH), layer),
        ],
        out_specs=pl.BlockSpec((nb, L, HY_WIDTH), lambda b: (b, 0, 0)),
        out_shape=jax.ShapeDtypeStruct((B, L, HY_WIDTH), BF16),
        scratch_shapes=[pltpu.VMEM((nb, L + 2 * SUBLANES, width), F32)],
        compiler_params=_params(dimension_semantics=("arbitrary",)),
        name="hyena",
    )(hy3, conv_w, conv_b, fwd_bf, inv_bf, *g_tabs, hy_bias)


CF_PAD = 2 * SUBLANES


def _conformer_kernel(cf_ref, cw_ref, cb_ref, g_ref, b_ref, o_ref, pad_ref):
    nb, L, _ = cf_ref.shape
    padded = L + 2 * CF_PAD
    shifted = padded - SUBLANES
    zeros = jnp.zeros((CF_PAD, CF_WIDTH), F32)
    half = CF_CONV_K // 2
    for s in range(nb):
        cf = cf_ref[s]
        pad_ref[s, 0, 0:CF_PAD, :] = zeros
        pad_ref[s, 0, CF_PAD + L:padded, :] = zeros
        pad_ref[s, 0, CF_PAD:CF_PAD + L, :] = cf[:, :CF_WIDTH] * jax.nn.sigmoid(cf[:, CF_WIDTH:])
        for r in range(1, SUBLANES):
            pad_ref[s, r, 0:shifted, :] = pad_ref[s, 0, r:r + shifted, :]
        for c0 in range(0, L, CF_ROW_CHUNK):
            acc = jnp.zeros((CF_ROW_CHUNK, CF_WIDTH), F32) + cb_ref[...]
            for j in range(CF_CONV_K):
                off = CF_PAD + j - half
                start = c0 + (off // SUBLANES) * SUBLANES
                acc = acc + cw_ref[j:j + 1, :] * pad_ref[s, off % SUBLANES, start:start + CF_ROW_CHUNK, :]
            y = _layer_norm_rows(acc, g_ref[...], b_ref[...])
            o_ref[s, c0:c0 + CF_ROW_CHUNK, :] = (y * jax.nn.sigmoid(y)).astype(o_ref.dtype)


def _conformer_call(cf3, conv_w, conv_b, ln_g, ln_b, layer):
    B, L, width = cf3.shape
    nb = _seqs_per_step(B, L)
    row = _layer_spec((1, CF_WIDTH), layer)
    return pl.pallas_call(
        _conformer_kernel,
        grid=(B // nb,),
        in_specs=[
            pl.BlockSpec((nb, L, width), lambda b: (b, 0, 0)),
            _layer_spec((CF_CONV_K, CF_WIDTH), layer),
            row, row, row,
        ],
        out_specs=pl.BlockSpec((nb, L, CF_WIDTH), lambda b: (b, 0, 0)),
        out_shape=jax.ShapeDtypeStruct((B, L, CF_WIDTH), BF16),
        scratch_shapes=[pltpu.VMEM((nb, SUBLANES, L + 2 * CF_PAD, CF_WIDTH), F32)],
        compiler_params=_params(dimension_semantics=("arbitrary",)),
        name="conformer",
    )(cf3, conv_w, conv_b, ln_g, ln_b)


def _attn_kernel(*refs, lambda_init, past):
    if past:
        (q_ref, k_ref, v_ref, ck_ref, cv_ref, lq1, lk1, lq2, lk2, g_ref, o_ref, k_all, v_aug) = refs
    else:
        (q_ref, k_ref, v_ref, lq1, lk1, lq2, lk2, g_ref, o_ref) = refs
    nb, tq, _ = q_ref.shape

    if past:
        keys = v_aug.shape[1]

        @pl.when(pl.program_id(1) == 0)
        def _():
            ones = jnp.ones((keys, HEAD_PAIR), BF16)
            for s in range(nb):
                k_all[s, 0:past, :] = ck_ref[s, 0].reshape(past, ATT_WIDTH).astype(BF16)
                k_all[s, past:keys, :] = k_ref[s]
                v_old = cv_ref[s, 0].reshape(past, ATT_WIDTH).astype(BF16)
                for h in range(DIFF_HEADS):
                    cols = slice(h * HEAD_PAIR, (h + 1) * HEAD_PAIR)
                    base = 2 * h * HEAD_PAIR
                    v_aug[s, 0:past, base:base + HEAD_PAIR] = v_old[:, cols]
                    v_aug[s, past:keys, base:base + HEAD_PAIR] = v_ref[s, :, cols]
                    v_aug[s, :, base + HEAD_PAIR:base + 2 * HEAD_PAIR] = ones

    lam = (jnp.exp(jnp.sum(lq1[...] * lk1[...], axis=-1, keepdims=True))
           - jnp.exp(jnp.sum(lq2[...] * lk2[...], axis=-1, keepdims=True)) + lambda_init)
    lane = lax.broadcasted_iota(jnp.int32, (tq, HEAD_PAIR), 1)
    first = lane < DIFF_HEAD_DIM
    nt = (((1,), (1,)), ((), ()))
    gain = g_ref[...] * (1.0 - lambda_init)
    zero = jnp.zeros((), BF16)

    for s in range(nb):
        for h in range(DIFF_HEADS):
            cols = slice(h * HEAD_PAIR, (h + 1) * HEAD_PAIR)
            q = q_ref[s, :, cols]
            q2 = jnp.concatenate([jnp.where(first, q, zero), jnp.where(first, zero, q)], axis=0)
            kh = k_all[s, :, cols] if past else k_ref[s, :, cols]
            sc = lax.dot_general(q2, kh, nt, preferred_element_type=F32)
            e = jnp.exp2(sc - jnp.max(sc, axis=-1, keepdims=True))
            if past:
                p = _dot(e.astype(BF16), v_aug[s, :, 2 * h * HEAD_PAIR:2 * (h + 1) * HEAD_PAIR])
                p0, p1 = p[:tq], p[tq:]
                o = (p0[:, :HEAD_PAIR] * (1.0 / p0[:, HEAD_PAIR:HEAD_PAIR + 1])
                     - p1[:, :HEAD_PAIR] * (lam / p1[:, HEAD_PAIR:HEAD_PAIR + 1]))
            else:
                r = 1.0 / jnp.sum(e, axis=-1, keepdims=True)
                a = e[:tq] * r[:tq] - e[tq:] * (lam * r[tq:])
                o = _dot(a.astype(BF16), v_ref[s, :, cols])
            o = o * lax.rsqrt(jnp.mean(o * o, axis=-1, keepdims=True) + LN_EPS) * gain
            o_ref[s, :, cols] = o.astype(o_ref.dtype)


def _attn_call(q3, k3, v3, cache_k, cache_v, layer, lam_vecs, subln_g):
    B, L, _ = q3.shape
    past = 0 if cache_k is None else cache_k.shape[2]
    nb = _seqs_per_step(B, 2 * L)
    tq = min(Q_TILE, L)
    tile = pl.BlockSpec((nb, tq, ATT_WIDTH), lambda b, i: (b, i, 0))
    seq = pl.BlockSpec((nb, L, ATT_WIDTH), lambda b, i: (b, 0, 0))
    in_specs = [tile, seq, seq]
    args = [q3, k3, v3]
    scratch = []
    if past:
        for cache in (cache_k, cache_v):
            in_specs.append(pl.BlockSpec((nb, 1) + cache.shape[2:], lambda b, i: (b, layer, 0, 0, 0)))
            args.append(cache)
        scratch = [pltpu.VMEM((nb, past + L, ATT_WIDTH), BF16),
                   pltpu.VMEM((nb, past + L, 2 * ATT_WIDTH), BF16)]
    in_specs += [_layer_spec((1, DIFF_HEAD_DIM), layer)] * 4 + [_layer_spec((1, HEAD_PAIR), layer)]
    args += list(lam_vecs) + [subln_g]
    return pl.pallas_call(
        functools.partial(_attn_kernel, lambda_init=_lambda_init(layer), past=past),
        grid=(B // nb, L // tq),
        in_specs=in_specs,
        out_specs=tile,
        out_shape=jax.ShapeDtypeStruct((B, L, ATT_WIDTH), BF16),
        scratch_shapes=scratch,
        compiler_params=_params(dimension_semantics=("arbitrary", "arbitrary")),
        name="diff_attn",
    )(*args)


def _mlp_kernel(x_ref, hy_ref, cf_ref, at_ref, ada_ref, wo_ref, w1_ref, w2_ref,
                g1_ref, b1_ref, g2_ref, b2_ref, o_ref):
    gate1 = ada_ref[0, 0, 2:3, :]
    sh2 = ada_ref[0, 0, 3:4, :]
    sc2 = ada_ref[0, 0, 4:5, :]
    gate2 = ada_ref[0, 0, 5:6, :]
    mix = jnp.concatenate([hy_ref[...], cf_ref[...], at_ref[...]], axis=1)
    y = _dot(mix, wo_ref[...])
    x = _layer_norm_rows(DEEPNORM_ALPHA * x_ref[...] + gate1 * y, g1_ref[...], b1_ref[...])
    h = (x * (1.0 + sc2) + sh2).astype(BF16)
    a = jnp.maximum(_dot(h, w1_ref[...]), 0.0)
    m = _dot((a * a).astype(BF16), w2_ref[...])
    o_ref[...] = _layer_norm_rows(DEEPNORM_ALPHA * x + gate2 * m, g2_ref[...], b2_ref[...])


def _mlp_call(x2d, y_hy, y_cf, y_at, ada4, wo_bf, w1_bf, w2_bf, ln1_g, ln1_b, ln2_g, ln2_b,
              layer, cond_row_of_tile):
    T = x2d.shape[0]
    rows = lambda w: pl.BlockSpec((ROW_TILE, w), lambda i: (i, 0))
    vec = _layer_spec((1, D_MODEL), layer)
    return pl.pallas_call(
        _mlp_kernel,
        grid=(T // ROW_TILE,),
        in_specs=[
            rows(D_MODEL), rows(HY_WIDTH), rows(CF_WIDTH), rows(ATT_WIDTH),
            pl.BlockSpec((1, 1, N_ADA, D_MODEL), lambda i: (layer, cond_row_of_tile(i), 0, 0)),
            _layer_spec((MIX_WIDTH, D_MODEL), layer),
            _layer_spec((D_MODEL, D_FF), layer),
            _layer_spec((D_FF, D_MODEL), layer),
            vec, vec, vec, vec,
        ],
        out_specs=rows(D_MODEL),
        out_shape=jax.ShapeDtypeStruct((T, D_MODEL), F32),
        compiler_params=_params(dimension_semantics=("arbitrary",)),
        name="outproj_mlp",
    )(x2d, y_hy, y_cf, y_at, ada4, wo_bf, w1_bf, w2_bf, ln1_g, ln1_b, ln2_g, ln2_b)


def _layer(x2d, batch, seq, layer, ada4, wts, g_tabs, cond_row_of_tile, rope_tabs, cache_kv, new_caches):
    hy, cf, q, k, v, *new_caches = _inproj_call(
        x2d, ada4, wts["w_in"], layer, batch, seq, cond_row_of_tile, rope_tabs, new_caches)
    as_seq = lambda a: a.reshape(batch, seq, a.shape[-1])
    y_hy = _hyena_call(as_seq(hy), wts["hy_conv_w"], wts["hy_conv_b"], g_tabs, wts["hy_bias"], layer)
    y_cf = _conformer_call(as_seq(cf), wts["cf_conv_w"], wts["cf_conv_b"], wts["cf_ln_g"],
                           wts["cf_ln_b"], layer)
    ck, cv = cache_kv if cache_kv is not None else (None, None)
    lam_vecs = (wts["lam_q1"], wts["lam_k1"], wts["lam_q2"], wts["lam_k2"])
    y_at = _attn_call(as_seq(q), as_seq(k), as_seq(v), ck, cv, layer, lam_vecs, wts["subln_g"])
    flat = lambda a: a.reshape(batch * seq, a.shape[-1])
    x_out = _mlp_call(x2d, flat(y_hy), flat(y_cf), flat(y_at), ada4,
                      wts["w_out"], wts["w_mlp1"], wts["w_mlp2"],
                      wts["ln1_g"], wts["ln1_b"], wts["ln2_g"], wts["ln2_b"], layer, cond_row_of_tile)
    return x_out, tuple(new_caches)


def kernel(x_prompt, x_sample, cache_k, cache_v, c, c_ctx, w_ada, b_ada, w_in, hy_conv_w, hy_conv_b, hf_w1, hf_b1, hf_freq, hf_w2, hf_b2, hf_w3, hy_bias, cf_conv_w, cf_conv_b, cf_ln_g, cf_ln_b, lam_q1, lam_k1, lam_q2, lam_k2, subln_g, w_out, ln1_g, ln1_b, w_mlp1, w_mlp2, ln2_g, ln2_b):
    batch, seq, _ = x_prompt.shape
    dec_batch, dec_seq, _ = x_sample.shape
    assert 1 + dec_batch <= COND_ROWS
    assert ROW_TILE % seq == 0 and (batch * seq) % ROW_TILE == 0
    assert dec_seq % ROW_TILE == 0 and dec_seq % Q_TILE == 0

    cond = jnp.concatenate(
        [c_ctx[None, :], c, jnp.zeros((COND_ROWS - 1 - dec_batch, D_MODEL), F32)], axis=0)
    ada4 = _ada_call(cond, w_ada, b_ada).reshape(DEPTH, COND_ROWS, N_ADA, D_MODEL)

    row_vec = lambda a: a.reshape(DEPTH, 1, a.shape[-1])
    wts = dict(
        w_in=w_in.astype(BF16), w_out=w_out.astype(BF16),
        w_mlp1=w_mlp1.astype(BF16), w_mlp2=w_mlp2.astype(BF16),
        hy_conv_w=hy_conv_w, hy_conv_b=row_vec(hy_conv_b), hy_bias=row_vec(hy_bias),
        cf_conv_w=cf_conv_w, cf_conv_b=row_vec(cf_conv_b),
        cf_ln_g=row_vec(cf_ln_g), cf_ln_b=row_vec(cf_ln_b),
        lam_q1=row_vec(lam_q1), lam_k1=row_vec(lam_k1), lam_q2=row_vec(lam_q2), lam_k2=row_vec(lam_k2),
        subln_g=row_vec(subln_g),
        ln1_g=row_vec(ln1_g), ln1_b=row_vec(ln1_b), ln2_g=row_vec(ln2_g), ln2_b=row_vec(ln2_b))

    rope_tabs = tuple(jnp.asarray(t) for t in _rope_tables(dec_seq))
    dec_tiles_per_seq = dec_seq // ROW_TILE

    xp = x_prompt.reshape(batch * seq, D_MODEL)
    xs = x_sample.reshape(dec_batch * dec_seq, D_MODEL)
    new_caches = ()
    for l in range(DEPTH):
        filt = (hf_w1[l], hf_b1[l], hf_freq[l], hf_w2[l], hf_b2[l], hf_w3[l])
        g_ctx = _filter_call(seq, *filt)
        g_dec = _filter_call(dec_seq, *filt)
        xp, new_caches = _layer(xp, batch, seq, l, ada4, wts, g_ctx, lambda i: 0, None, None, new_caches)
        xs, _ = _layer(xs, dec_batch, dec_seq, l, ada4, wts, g_dec,
                       lambda i: 1 + i // dec_tiles_per_seq, rope_tabs, (cache_k, cache_v), None)
    new_cache_k, new_cache_v = new_caches
    return (xp.reshape(batch, seq, D_MODEL), xs.reshape(dec_batch, dec_seq, D_MODEL),
            new_cache_k, new_cache_v)
```

```python
import functools
import math

import numpy as np
import jax
import jax.numpy as jnp
from jax import lax
from jax.experimental import pallas as pl
from jax.experimental.pallas import tpu as pltpu

D_MODEL = 1024
DEPTH = 2
GRID_W = 64
HY_WIDTH = D_MODEL // 4
CF_WIDTH = D_MODEL // 4
ATT_WIDTH = D_MODEL // 2
DIFF_HEAD_DIM = 64
DIFF_HEADS = ATT_WIDTH // (2 * DIFF_HEAD_DIM)
HEAD_PAIR = 2 * DIFF_HEAD_DIM
MIX_WIDTH = HY_WIDTH + CF_WIDTH + ATT_WIDTH
IN_WIDTH = 3 * HY_WIDTH + 2 * CF_WIDTH + 3 * ATT_WIDTH
HY_SHORT_K = 3
HY_FILTER_EMB = 33
HY_FILTER_HIDDEN = 64
HY_FAST_DECAY_PCT = 0.3
HY_SLOW_DECAY_PCT = 1.5
HY_DECAY_TARGET = 1e-2
CF_CONV_K = 31
D_FF = 4 * D_MODEL
ROPE_BASE = 10000.0
AX_DIM = DIFF_HEAD_DIM // 2
DEEPNORM_ALPHA = (2 * DEPTH) ** 0.25
LN_EPS = 1e-5
N_ADA = 6

LANES = 128
SUBLANES = 8
COND_ROWS = SUBLANES
VMEM_LIMIT = 56 * 1024 * 1024

ROW_TILE = 512
ADA_COL_TILE = 1536
Q_TILE = 512
MIXER_ROWS = 1024
CF_ROW_CHUNK = 128

F32 = jnp.float32
BF16 = jnp.bfloat16


def _lambda_init(l):
    return 0.8 - 0.6 * math.exp(-0.3 * l)


def _params(**kw):
    return pltpu.CompilerParams(vmem_limit_bytes=VMEM_LIMIT, **kw)


def _const_spec(shape):
    zeros = (0,) * len(shape)
    return pl.BlockSpec(shape, lambda *_: zeros, pipeline_mode=pl.Buffered(1))


def _layer_spec(shape, layer):
    zeros = (0,) * len(shape)
    return pl.BlockSpec((None,) + tuple(shape), lambda *_: (layer,) + zeros,
                        pipeline_mode=pl.Buffered(1))


def _dot(a, b):
    return jnp.dot(a, b, preferred_element_type=F32)


def _dot_f32(a, b):
    return jnp.dot(a, b, preferred_element_type=F32, precision=lax.Precision.HIGHEST)


def _split_bf16(x):
    hi = x.astype(BF16)
    return hi, (x - hi.astype(F32)).astype(BF16)


def _layer_norm_rows(x, g, b):
    mu = jnp.mean(x, axis=-1, keepdims=True)
    xc = x - mu
    var = jnp.mean(xc * xc, axis=-1, keepdims=True)
    return xc * lax.rsqrt(var + LN_EPS) * g + b


def _seqs_per_step(batch, seq):
    return max(1, min(batch, MIXER_ROWS // seq))


@functools.lru_cache(maxsize=None)
def _dft_tables(L):
    n = 2 * L
    k = np.arange(L, dtype=np.float64)[:, None]
    s = np.arange(L, dtype=np.float64)[None, :]
    ang = 2.0 * np.pi * k * s / n
    fwd = np.concatenate([np.cos(ang), -np.sin(ang)], axis=0)
    fwd[L, :] = np.cos(np.pi * np.arange(L))
    t = np.arange(L, dtype=np.float64)[:, None]
    kk = np.arange(L, dtype=np.float64)[None, :]
    ang_i = 2.0 * np.pi * t * kk / n
    inv_re = (2.0 / n) * np.cos(ang_i)
    inv_re[:, 0] = 1.0 / n
    inv_im = -(2.0 / n) * np.sin(ang_i)
    inv_im[:, 0] = np.cos(np.pi * np.arange(L)) / n
    inv = np.concatenate([inv_re, inv_im], axis=1)
    return fwd.astype(np.float32), inv.astype(np.float32)


@functools.lru_cache(maxsize=None)
def _filter_tables(L):
    bands = (HY_FILTER_EMB - 1) // 2
    t = np.linspace(0.0, 1.0, L)[:, None]
    w = (2.0 * np.pi / L) * np.arange(L, dtype=np.float64)[:, None]
    fr = np.linspace(1e-4, bands - 1, bands)[None, :]
    feats = np.concatenate([t, np.cos(fr * w), -np.sin(fr * w)], -1)
    feats_p = np.zeros((L, LANES), np.float64)
    feats_p[:, :HY_FILTER_EMB] = feats
    deltas = np.abs(np.linspace(math.log(HY_DECAY_TARGET) / HY_FAST_DECAY_PCT,
                                math.log(HY_DECAY_TARGET) / HY_SLOW_DECAY_PCT, HY_WIDTH))
    decay = np.exp(-t * deltas[None, :])
    return feats_p.astype(np.float32), decay.astype(np.float32)


@functools.lru_cache(maxsize=None)
def _rope_tables(n):
    half = AX_DIM // 2
    pos = np.arange(n)
    row = (pos // GRID_W).astype(np.float64)[:, None]
    col = (pos % GRID_W).astype(np.float64)[:, None]
    inv = ROPE_BASE ** (-np.arange(0, AX_DIM, 2, dtype=np.float64) / AX_DIM)[None, :]
    zero = np.zeros((n, half))
    cr, sr = np.cos(row * inv), np.sin(row * inv)
    cc, sc = np.cos(col * inv), np.sin(col * inv)
    c = np.concatenate([cr, cr, cc, cc], -1)
    s_up = np.concatenate([-sr, zero, -sc, zero], -1)
    s_dn = np.concatenate([zero, sr, zero, sc], -1)
    reps = LANES // DIFF_HEAD_DIM
    tile = lambda a: np.tile(a, (1, reps)).astype(np.float32)
    return tile(c), tile(s_up), tile(s_dn)


def _ada_kernel(cond_ref, w_ref, b_ref, o_ref):
    c = cond_ref[...]
    s = c * jax.nn.sigmoid(c)
    o_ref[0] = _dot(s.astype(BF16), w_ref[0].astype(BF16)) + b_ref[0]


def _ada_call(cond, w_ada, b_ada):
    n_out = N_ADA * D_MODEL
    return pl.pallas_call(
        _ada_kernel,
        grid=(DEPTH, n_out // ADA_COL_TILE),
        in_specs=[
            pl.BlockSpec((COND_ROWS, D_MODEL), lambda l, j: (0, 0)),
            pl.BlockSpec((1, D_MODEL, ADA_COL_TILE), lambda l, j: (l, 0, j)),
            pl.BlockSpec((1, 1, ADA_COL_TILE), lambda l, j: (l, 0, j)),
        ],
        out_specs=pl.BlockSpec((1, COND_ROWS, ADA_COL_TILE), lambda l, j: (l, 0, j)),
        out_shape=jax.ShapeDtypeStruct((DEPTH, COND_ROWS, n_out), F32),
        compiler_params=_params(dimension_semantics=("arbitrary", "arbitrary")),
        name="ada",
    )(cond, w_ada, b_ada.reshape(DEPTH, 1, n_out))


def _rope_lanes(x, c, s_up, s_dn):
    outs = []
    for p in range(ATT_WIDTH // LANES):
        xb = x[:, p * LANES:(p + 1) * LANES]
        up = pltpu.roll(xb, LANES - AX_DIM // 2, 1)
        dn = pltpu.roll(xb, AX_DIM // 2, 1)
        outs.append(xb * c + up * s_up + dn * s_dn)
    return jnp.concatenate(outs, axis=1)


Q_SCALE = DIFF_HEAD_DIM ** -0.5 * math.log2(math.e)


def _inproj_kernel(*refs, rope, n_cache_in, seq):
    if rope:
        x_ref, ada_ref, w_ref, c_ref, su_ref, sd_ref, hy_ref, cf_ref, q_ref, k_ref, v_ref = refs
        kc_ref = vc_ref = None
    else:
        x_ref, ada_ref, w_ref = refs[:3]
        hy_ref, cf_ref, q_ref, k_ref, v_ref, kc_ref, vc_ref = refs[3 + n_cache_in:]
    sh1 = ada_ref[0, 0, 0:1, :]
    sc1 = ada_ref[0, 0, 1:2, :]
    h = (x_ref[...] * (1.0 + sc1) + sh1).astype(BF16)
    o1 = 3 * HY_WIDTH
    o2 = o1 + 2 * CF_WIDTH
    o3 = o2 + ATT_WIDTH
    o4 = o3 + ATT_WIDTH
    hy_ref[...] = _dot(h, w_ref[:, 0:o1])
    cf_ref[...] = _dot(h, w_ref[:, o1:o2])
    q = _dot(h, w_ref[:, o2:o3])
    k = _dot(h, w_ref[:, o3:o4])
    v = _dot(h, w_ref[:, o4:IN_WIDTH])
    if rope:
        c, su, sd = c_ref[...], su_ref[...], sd_ref[...]
        q = _rope_lanes(q, c, su, sd)
        k = _rope_lanes(k, c, su, sd)
    q_ref[...] = (q * Q_SCALE).astype(BF16)
    k_ref[...] = k.astype(BF16)
    v_ref[...] = v.astype(BF16)
    if kc_ref is not None:
        for b in range(kc_ref.shape[0]):
            rows = slice(b * seq, (b + 1) * seq)
            kc_ref[b, 0] = k[rows].reshape(seq, 2 * DIFF_HEADS, DIFF_HEAD_DIM)
            vc_ref[b, 0] = v[rows].reshape(seq, DIFF_HEADS, 2 * DIFF_HEAD_DIM)


def _inproj_call(x2d, ada4, w_in_bf, layer, batch, seq, cond_row_of_tile, rope_tabs, caches):
    T = x2d.shape[0]
    tiles_per_seq = seq // ROW_TILE if seq >= ROW_TILE else None
    in_specs = [
        pl.BlockSpec((ROW_TILE, D_MODEL), lambda i: (i, 0)),
        pl.BlockSpec((1, 1, N_ADA, D_MODEL), lambda i: (layer, cond_row_of_tile(i), 0, 0)),
        _layer_spec((D_MODEL, IN_WIDTH), layer),
    ]
    args = [x2d, ada4, w_in_bf]
    rows = lambda w: pl.BlockSpec((ROW_TILE, w), lambda i: (i, 0))
    out_specs = [rows(3 * HY_WIDTH), rows(2 * CF_WIDTH), rows(ATT_WIDTH), rows(ATT_WIDTH), rows(ATT_WIDTH)]
    out_shape = [jax.ShapeDtypeStruct((T, 3 * HY_WIDTH), F32), jax.ShapeDtypeStruct((T, 2 * CF_WIDTH), F32)]
    out_shape += [jax.ShapeDtypeStruct((T, ATT_WIDTH), BF16)] * 3
    aliases = {}
    n_cache_in = 0
    if rope_tabs is not None:
        for tab in rope_tabs:
            in_specs.append(pl.BlockSpec((ROW_TILE, LANES), lambda i: (i % tiles_per_seq, 0)))
            args.append(tab)
    if caches is not None:
        seqs = ROW_TILE // seq
        k_shape = (batch, DEPTH, seq, 2 * DIFF_HEADS, DIFF_HEAD_DIM)
        v_shape = (batch, DEPTH, seq, DIFF_HEADS, 2 * DIFF_HEAD_DIM)
        for shape in (k_shape, v_shape):
            out_specs.append(pl.BlockSpec((seqs, 1) + shape[2:], lambda i: (i, layer, 0, 0, 0)))
            out_shape.append(jax.ShapeDtypeStruct(shape, F32))
        for j, prev in enumerate(caches):
            aliases[len(args)] = len(out_shape) - 2 + j
            in_specs.append(pl.BlockSpec(memory_space=pl.ANY))
            args.append(prev)
        n_cache_in = len(caches)
    return pl.pallas_call(
        functools.partial(_inproj_kernel, rope=rope_tabs is not None, n_cache_in=n_cache_in, seq=seq),
        grid=(T // ROW_TILE,),
        in_specs=in_specs,
        out_specs=out_specs,
        out_shape=out_shape,
        input_output_aliases=aliases,
        compiler_params=_params(dimension_semantics=("arbitrary",)),
        name="inproj",
    )(*args)


def _filter_kernel(feats_ref, w1_ref, b1_ref, fq_ref, w2_ref, b2_ref, w3_ref, decay_ref,
                   fwd_hi_ref, fwd_lo_ref, ga_ref, gbc_ref, gd_ref):
    L = feats_ref.shape[0]
    fq = fq_ref[...]
    hid = jnp.sin(fq * (_dot_f32(feats_ref[...], w1_ref[...]) + b1_ref[...]))
    hid = jnp.sin(fq * (_dot_f32(hid, w2_ref[...]) + b2_ref[...]))
    hf = _dot_f32(hid, w3_ref[...])
    decay = decay_ref[...]
    row = lax.broadcasted_iota(jnp.int32, (L, HY_WIDTH), 0)
    h_fwd = hf[:, :HY_WIDTH] * decay
    h_bwd = jnp.where(row == 0, 0.0, hf[:, HY_WIDTH:] * decay)
    h_hi, h_lo = _split_bf16(jnp.concatenate([h_fwd, h_bwd], axis=1))
    fwd_hi = fwd_hi_ref[...]
    spec = _dot(fwd_hi, h_hi) + (_dot(fwd_hi, h_lo) + _dot(fwd_lo_ref[...], h_hi))
    p = spec[:L]
    q = spec[L:]
    g_re = p[:, :HY_WIDTH] + p[:, HY_WIDTH:]
    g_im = q[:, :HY_WIDTH] - q[:, HY_WIDTH:]
    nyq = q[0:1, :HY_WIDTH] + q[0:1, HY_WIDTH:]
    ga_ref[...] = g_re
    gbc_ref[...] = jnp.where(row == 0, 0.0, g_im)
    gd_ref[...] = jnp.where(row == 0, nyq, g_re)


def _pad_to(a, shape):
    return jnp.pad(a, [(0, s - d) for s, d in zip(shape, a.shape)])


def _filter_call(L, w1, b1, fq, w2, b2, w3):
    feats, decay = _filter_tables(L)
    fwd, _ = _dft_tables(L)
    args = [
        jnp.asarray(feats),
        _pad_to(w1, (LANES, LANES)),
        _pad_to(b1.reshape(1, -1), (1, LANES)),
        _pad_to(fq.reshape(1, -1), (1, LANES)),
        _pad_to(w2, (LANES, LANES)),
        _pad_to(b2.reshape(1, -1), (1, LANES)),
        _pad_to(w3, (LANES, 2 * HY_WIDTH)),
        jnp.asarray(decay),
        *_split_bf16(jnp.asarray(fwd)),
    ]
    out = jax.ShapeDtypeStruct((L, HY_WIDTH), F32)
    return pl.pallas_call(
        _filter_kernel,
        out_shape=[out, out, out],
        compiler_params=_params(),
        name="hyena_filter",
    )(*args)


def _hyena_kernel(hy_ref, cw_ref, cb_ref, fwd_ref, inv_ref, ga_ref, gbc_ref, gd_ref, hb_ref,
                  o_ref, pad_ref):
    nb, L, width = hy_ref.shape
    zeros = jnp.zeros((SUBLANES, width), F32)
    x0s, us = [], []
    for s in range(nb):
        pad_ref[s, 0:SUBLANES, :] = zeros
        pad_ref[s, SUBLANES + L:2 * SUBLANES + L, :] = zeros
        pad_ref[s, SUBLANES:SUBLANES + L, :] = hy_ref[s]
        conv = cb_ref[...]
        for j in range(HY_SHORT_K):
            start = SUBLANES + j - HY_SHORT_K // 2
            conv = conv + cw_ref[j:j + 1, :] * pad_ref[s, start:start + L, :]
        x0s.append(conv[:, :HY_WIDTH])
        us.append(conv[:, HY_WIDTH:2 * HY_WIDTH] * conv[:, 2 * HY_WIDTH:])
    u_all = jnp.concatenate(us, axis=1).astype(BF16)
    spec = _dot(fwd_ref[...], u_all)
    ga, gbc, gd = ga_ref[...], gbc_ref[...], gd_ref[...]
    y_re, y_im = [], []
    for s in range(nb):
        p = spec[:L, s * HY_WIDTH:(s + 1) * HY_WIDTH]
        q = spec[L:, s * HY_WIDTH:(s + 1) * HY_WIDTH]
        y_re.append(p * ga - q * gbc)
        y_im.append(p * gbc + q * gd)
    y_spec = jnp.concatenate(
        [jnp.concatenate(y_re, axis=1), jnp.concatenate(y_im, axis=1)], axis=0).astype(BF16)
    y = _dot(inv_ref[...], y_spec)
    for s in range(nb):
        o = x0s[s] * (y[:, s * HY_WIDTH:(s + 1) * HY_WIDTH] + us[s] * hb_ref[...])
        o_ref[s] = o.astype(o_ref.dtype)


def _hyena_call(hy3, conv_w, conv_b, g_tabs, hy_bias, layer):
    B, L, width = hy3.shape
    nb = _seqs_per_step(B, L)
    fwd, inv = _dft_tables(L)
    fwd_bf = jnp.asarray(fwd).astype(BF16)
    inv_bf = jnp.asarray(inv).astype(BF16)
    tab = _const_spec((L, HY_WIDTH))
    return pl.pallas_call(
        _hyena_kernel,
        grid=(B // nb,),
        in_specs=[
            pl.BlockSpec((nb, L, width), lambda b: (b, 0, 0)),
            _layer_spec((HY_SHORT_K, width), layer),
            _layer_spec((1, width), layer),
            _const_spec((2 * L, L)),
            _const_spec((L, 2 * L)),
            tab, tab, tab,
            _layer_spec((1, HY_WIDTH), layer),
        ],
        out_specs=pl.BlockSpec((nb, L, HY_WIDTH), lambda b: (b, 0, 0)),
        out_shape=jax.ShapeDtypeStruct((B, L, HY_WIDTH), BF16),
        scratch_shapes=[pltpu.VMEM((nb, L + 2 * SUBLANES, width), F32)],
        compiler_params=_params(dimension_semantics=("arbitrary",)),
        name="hyena",
    )(hy3, conv_w, conv_b, fwd_bf, inv_bf, *g_tabs, hy_bias)


CF_PAD = 2 * SUBLANES


def _conformer_kernel(cf_ref, cw_ref, cb_ref, g_ref, b_ref, o_ref, pad_ref):
    nb, L, _ = cf_ref.shape
    padded = L + 2 * CF_PAD
    shifted = padded - SUBLANES
    zeros = jnp.zeros((CF_PAD, CF_WIDTH), F32)
    half = CF_CONV_K // 2
    for s in range(nb):
        cf = cf_ref[s]
        pad_ref[s, 0, 0:CF_PAD, :] = zeros
        pad_ref[s, 0, CF_PAD + L:padded, :] = zeros
        pad_ref[s, 0, CF_PAD:CF_PAD + L, :] = cf[:, :CF_WIDTH] * jax.nn.sigmoid(cf[:, CF_WIDTH:])
        for r in range(1, SUBLANES):
            pad_ref[s, r, 0:shifted, :] = pad_ref[s, 0, r:r + shifted, :]
        for c0 in range(0, L, CF_ROW_CHUNK):
            acc = jnp.zeros((CF_ROW_CHUNK, CF_WIDTH), F32) + cb_ref[...]
            for j in range(CF_CONV_K):
                off = CF_PAD + j - half
                start = c0 + (off // SUBLANES) * SUBLANES
                acc = acc + cw_ref[j:j + 1, :] * pad_ref[s, off % SUBLANES, start:start + CF_ROW_CHUNK, :]
            y = _layer_norm_rows(acc, g_ref[...], b_ref[...])
            o_ref[s, c0:c0 + CF_ROW_CHUNK, :] = (y * jax.nn.sigmoid(y)).astype(o_ref.dtype)


def _conformer_call(cf3, conv_w, conv_b, ln_g, ln_b, layer):
    B, L, width = cf3.shape
    nb = _seqs_per_step(B, L)
    row = _layer_spec((1, CF_WIDTH), layer)
    return pl.pallas_call(
        _conformer_kernel,
        grid=(B // nb,),
        in_specs=[
            pl.BlockSpec((nb, L, width), lambda b: (b, 0, 0)),
            _layer_spec((CF_CONV_K, CF_WIDTH), layer),
            row, row, row,
        ],
        out_specs=pl.BlockSpec((nb, L, CF_WIDTH), lambda b: (b, 0, 0)),
        out_shape=jax.ShapeDtypeStruct((B, L, CF_WIDTH), BF16),
        scratch_shapes=[pltpu.VMEM((nb, SUBLANES, L + 2 * CF_PAD, CF_WIDTH), F32)],
        compiler_params=_params(dimension_semantics=("arbitrary",)),
        name="conformer",
    )(cf3, conv_w, conv_b, ln_g, ln_b)


def _attn_kernel(*refs, lambda_init, past):
    if past:
        (q_ref, k_ref, v_ref, ck_ref, cv_ref, lq1, lk1, lq2, lk2, g_ref, o_ref, k_all, v_aug) = refs
    else:
        (q_ref, k_ref, v_ref, lq1, lk1, lq2, lk2, g_ref, o_ref) = refs
    nb, tq, _ = q_ref.shape

    if past:
        keys = v_aug.shape[1]

        @pl.when(pl.program_id(1) == 0)
        def _():
            ones = jnp.ones((keys, HEAD_PAIR), BF16)
            for s in range(nb):
                k_all[s, 0:past, :] = ck_ref[s, 0].reshape(past, ATT_WIDTH).astype(BF16)
                k_all[s, past:keys, :] = k_ref[s]
                v_old = cv_ref[s, 0].reshape(past, ATT_WIDTH).astype(BF16)
                for h in range(DIFF_HEADS):
                    cols = slice(h * HEAD_PAIR, (h + 1) * HEAD_PAIR)
                    base = 2 * h * HEAD_PAIR
                    v_aug[s, 0:past, base:base + HEAD_PAIR] = v_old[:, cols]
                    v_aug[s, past:keys, base:base + HEAD_PAIR] = v_ref[s, :, cols]
                    v_aug[s, :, base + HEAD_PAIR:base + 2 * HEAD_PAIR] = ones

    lam = (jnp.exp(jnp.sum(lq1[...] * lk1[...], axis=-1, keepdims=True))
           - jnp.exp(jnp.sum(lq2[...] * lk2[...], axis=-1, keepdims=True)) + lambda_init)
    lane = lax.broadcasted_iota(jnp.int32, (tq, HEAD_PAIR), 1)
    first = lane < DIFF_HEAD_DIM
    nt = (((1,), (1,)), ((), ()))
    gain = g_ref[...] * (1.0 - lambda_init)
    zero = jnp.zeros((), BF16)

    def scores(s, h):
        cols = slice(h * HEAD_PAIR, (h + 1) * HEAD_PAIR)
        q = q_ref[s, :, cols]
        q2 = jnp.concatenate([jnp.where(first, q, zero), jnp.where(first, zero, q)], axis=0)
        kh = k_all[s, :, cols] if past else k_ref[s, :, cols]
        return lax.dot_general(q2, kh, nt, preferred_element_type=F32)

    def finish(s, h, sc):
        cols = slice(h * HEAD_PAIR, (h + 1) * HEAD_PAIR)
        e = jnp.exp2(sc - jnp.max(sc, axis=-1, keepdims=True))
        if past:
            p = _dot(e.astype(BF16), v_aug[s, :, 2 * h * HEAD_PAIR:2 * (h + 1) * HEAD_PAIR])
            p0, p1 = p[:tq], p[tq:]
            o = (p0[:, :HEAD_PAIR] * (1.0 / p0[:, HEAD_PAIR:HEAD_PAIR + 1])
                 - p1[:, :HEAD_PAIR] * (lam / p1[:, HEAD_PAIR:HEAD_PAIR + 1]))
        else:
            r = 1.0 / jnp.sum(e, axis=-1, keepdims=True)
            a = e[:tq] * r[:tq] - e[tq:] * (lam * r[tq:])
            o = _dot(a.astype(BF16), v_ref[s, :, cols])
        o = o * lax.rsqrt(jnp.mean(o * o, axis=-1, keepdims=True) + LN_EPS) * gain
        o_ref[s, :, cols] = o.astype(o_ref.dtype)

    lookahead = 0 if past else 1
    pending = []
    for s in range(nb):
        for h in range(DIFF_HEADS):
            pending.append((s, h, scores(s, h)))
            if len(pending) > lookahead:
                finish(*pending.pop(0))
    for unit in pending:
        finish(*unit)


def _attn_call(q3, k3, v3, cache_k, cache_v, layer, lam_vecs, subln_g):
    B, L, _ = q3.shape
    past = 0 if cache_k is None else cache_k.shape[2]
    nb = _seqs_per_step(B, 2 * L)
    tq = min(Q_TILE, L)
    tile = pl.BlockSpec((nb, tq, ATT_WIDTH), lambda b, i: (b, i, 0))
    seq = pl.BlockSpec((nb, L, ATT_WIDTH), lambda b, i: (b, 0, 0))
    in_specs = [tile, seq, seq]
    args = [q3, k3, v3]
    scratch = []
    if past:
        for cache in (cache_k, cache_v):
            in_specs.append(pl.BlockSpec((nb, 1) + cache.shape[2:], lambda b, i: (b, layer, 0, 0, 0)))
            args.append(cache)
        scratch = [pltpu.VMEM((nb, past + L, ATT_WIDTH), BF16),
                   pltpu.VMEM((nb, past + L, 2 * ATT_WIDTH), BF16)]
    in_specs += [_layer_spec((1, DIFF_HEAD_DIM), layer)] * 4 + [_layer_spec((1, HEAD_PAIR), layer)]
    args += list(lam_vecs) + [subln_g]
    return pl.pallas_call(
        functools.partial(_attn_kernel, lambda_init=_lambda_init(layer), past=past),
        grid=(B // nb, L // tq),
        in_specs=in_specs,
        out_specs=tile,
        out_shape=jax.ShapeDtypeStruct((B, L, ATT_WIDTH), BF16),
        scratch_shapes=scratch,
        compiler_params=_params(dimension_semantics=("arbitrary", "arbitrary")),
        name="diff_attn",
    )(*args)


def _mlp_kernel(x_ref, hy_ref, cf_ref, at_ref, ada_ref, wo_ref, w1_ref, w2_ref,
                g1_ref, b1_ref, g2_ref, b2_ref, o_ref):
    gate1 = ada_ref[0, 0, 2:3, :]
    sh2 = ada_ref[0, 0, 3:4, :]
    sc2 = ada_ref[0, 0, 4:5, :]
    gate2 = ada_ref[0, 0, 5:6, :]
    mix = jnp.concatenate([hy_ref[...], cf_ref[...], at_ref[...]], axis=1)
    y = _dot(mix, wo_ref[...])
    x = _layer_norm_rows(DEEPNORM_ALPHA * x_ref[...] + gate1 * y, g1_ref[...], b1_ref[...])
    h = (x * (1.0 + sc2) + sh2).astype(BF16)
    a = jnp.maximum(_dot(h, w1_ref[...]), 0.0)
    m = _dot((a * a).astype(BF16), w2_ref[...])
    o_ref[...] = _layer_norm_rows(DEEPNORM_ALPHA * x + gate2 * m, g2_ref[...], b2_ref[...])


def _mlp_call(x2d, y_hy, y_cf, y_at, ada4, wo_bf, w1_bf, w2_bf, ln1_g, ln1_b, ln2_g, ln2_b,
              layer, cond_row_of_tile):
    T = x2d.shape[0]
    rows = lambda w: pl.BlockSpec((ROW_TILE, w), lambda i: (i, 0))
    vec = _layer_spec((1, D_MODEL), layer)
    return pl.pallas_call(
        _mlp_kernel,
        grid=(T // ROW_TILE,),
        in_specs=[
            rows(D_MODEL), rows(HY_WIDTH), rows(CF_WIDTH), rows(ATT_WIDTH),
            pl.BlockSpec((1, 1, N_ADA, D_MODEL), lambda i: (layer, cond_row_of_tile(i), 0, 0)),
            _layer_spec((MIX_WIDTH, D_MODEL), layer),
            _layer_spec((D_MODEL, D_FF), layer),
            _layer_spec((D_FF, D_MODEL), layer),
            vec, vec, vec, vec,
        ],
        out_specs=rows(D_MODEL),
        out_shape=jax.ShapeDtypeStruct((T, D_MODEL), F32),
        compiler_params=_params(dimension_semantics=("arbitrary",)),
        name="outproj_mlp",
    )(x2d, y_hy, y_cf, y_at, ada4, wo_bf, w1_bf, w2_bf, ln1_g, ln1_b, ln2_g, ln2_b)


def _layer(x2d, batch, seq, layer, ada4, wts, g_tabs, cond_row_of_tile, rope_tabs, cache_kv, new_caches):
    hy, cf, q, k, v, *new_caches = _inproj_call(
        x2d, ada4, wts["w_in"], layer, batch, seq, cond_row_of_tile, rope_tabs, new_caches)
    as_seq = lambda a: a.reshape(batch, seq, a.shape[-1])
    y_hy = _hyena_call(as_seq(hy), wts["hy_conv_w"], wts["hy_conv_b"], g_tabs, wts["hy_bias"], layer)
    y_cf = _conformer_call(as_seq(cf), wts["cf_conv_w"], wts["cf_conv_b"], wts["cf_ln_g"],
                           wts["cf_ln_b"], layer)
    ck, cv = cache_kv if cache_kv is not None else (None, None)
    lam_vecs = (wts["lam_q1"], wts["lam_k1"], wts["lam_q2"], wts["lam_k2"])
    y_at = _attn_call(as_seq(q), as_seq(k), as_seq(v), ck, cv, layer, lam_vecs, wts["subln_g"])
    flat = lambda a: a.reshape(batch * seq, a.shape[-1])
    x_out = _mlp_call(x2d, flat(y_hy), flat(y_cf), flat(y_at), ada4,
                      wts["w_out"], wts["w_mlp1"], wts["w_mlp2"],
                      wts["ln1_g"], wts["ln1_b"], wts["ln2_g"], wts["ln2_b"], layer, cond_row_of_tile)
    return x_out, tuple(new_caches)


def kernel(x_prompt, x_sample, cache_k, cache_v, c, c_ctx, w_ada, b_ada, w_in, hy_conv_w, hy_conv_b, hf_w1, hf_b1, hf_freq, hf_w2, hf_b2, hf_w3, hy_bias, cf_conv_w, cf_conv_b, cf_ln_g, cf_ln_b, lam_q1, lam_k1, lam_q2, lam_k2, subln_g, w_out, ln1_g, ln1_b, w_mlp1, w_mlp2, ln2_g, ln2_b):
    batch, seq, _ = x_prompt.shape
    dec_batch, dec_seq, _ = x_sample.shape
    assert 1 + dec_batch <= COND_ROWS
    assert ROW_TILE % seq == 0 and (batch * seq) % ROW_TILE == 0
    assert dec_seq % ROW_TILE == 0 and dec_seq % Q_TILE == 0

    cond = jnp.concatenate(
        [c_ctx[None, :], c, jnp.zeros((COND_ROWS - 1 - dec_batch, D_MODEL), F32)], axis=0)
    ada4 = _ada_call(cond, w_ada, b_ada).reshape(DEPTH, COND_ROWS, N_ADA, D_MODEL)

    row_vec = lambda a: a.reshape(DEPTH, 1, a.shape[-1])
    wts = dict(
        w_in=w_in.astype(BF16), w_out=w_out.astype(BF16),
        w_mlp1=w_mlp1.astype(BF16), w_mlp2=w_mlp2.astype(BF16),
        hy_conv_w=hy_conv_w, hy_conv_b=row_vec(hy_conv_b), hy_bias=row_vec(hy_bias),
        cf_conv_w=cf_conv_w, cf_conv_b=row_vec(cf_conv_b),
        cf_ln_g=row_vec(cf_ln_g), cf_ln_b=row_vec(cf_ln_b),
        lam_q1=row_vec(lam_q1), lam_k1=row_vec(lam_k1), lam_q2=row_vec(lam_q2), lam_k2=row_vec(lam_k2),
        subln_g=row_vec(subln_g),
        ln1_g=row_vec(ln1_g), ln1_b=row_vec(ln1_b), ln2_g=row_vec(ln2_g), ln2_b=row_vec(ln2_b))

    rope_tabs = tuple(jnp.asarray(t) for t in _rope_tables(dec_seq))
    dec_tiles_per_seq = dec_seq // ROW_TILE

    xp = x_prompt.reshape(batch * seq, D_MODEL)
    xs = x_sample.reshape(dec_batch * dec_seq, D_MODEL)
    new_caches = ()
    for l in range(DEPTH):
        filt = (hf_w1[l], hf_b1[l], hf_freq[l], hf_w2[l], hf_b2[l], hf_w3[l])
        g_ctx = _filter_call(seq, *filt)
        g_dec = _filter_call(dec_seq, *filt)
        xp, new_caches = _layer(xp, batch, seq, l, ada4, wts, g_ctx, lambda i: 0, None, None, new_caches)
        xs, _ = _layer(xs, dec_batch, dec_seq, l, ada4, wts, g_dec,
                       lambda i: 1 + i // dec_tiles_per_seq, rope_tabs, (cache_k, cache_v), None)
    new_cache_k, new_cache_v = new_caches
    return (xp.reshape(batch, seq, D_MODEL), xs.reshape(dec_batch, dec_seq, D_MODEL),
            new_cache_k, new_cache_v)
```

```python
import functools
import math

import numpy as np
import jax
import jax.numpy as jnp
from jax import lax
from jax.experimental import pallas as pl
from jax.experimental.pallas import tpu as pltpu

D_MODEL = 1024
DEPTH = 2
GRID_W = 64
HY_WIDTH = D_MODEL // 4
CF_WIDTH = D_MODEL // 4
ATT_WIDTH = D_MODEL // 2
DIFF_HEAD_DIM = 64
DIFF_HEADS = ATT_WIDTH // (2 * DIFF_HEAD_DIM)
HEAD_PAIR = 2 * DIFF_HEAD_DIM
MIX_WIDTH = HY_WIDTH + CF_WIDTH + ATT_WIDTH
IN_WIDTH = 3 * HY_WIDTH + 2 * CF_WIDTH + 3 * ATT_WIDTH
HY_SHORT_K = 3
HY_FILTER_EMB = 33
HY_FILTER_HIDDEN = 64
HY_FAST_DECAY_PCT = 0.3
HY_SLOW_DECAY_PCT = 1.5
HY_DECAY_TARGET = 1e-2
CF_CONV_K = 31
D_FF = 4 * D_MODEL
ROPE_BASE = 10000.0
AX_DIM = DIFF_HEAD_DIM // 2
DEEPNORM_ALPHA = (2 * DEPTH) ** 0.25
LN_EPS = 1e-5
N_ADA = 6

LANES = 128
SUBLANES = 8
COND_ROWS = SUBLANES
VMEM_LIMIT = 56 * 1024 * 1024

ROW_TILE = 512
ADA_COL_TILE = 1536
Q_TILE = 512
MIXER_ROWS = 1024
HYENA_ROWS = 2048
CF_DFT_MAX_SEQ = 256
CF_ROW_CHUNK = 128

F32 = jnp.float32
BF16 = jnp.bfloat16


def _lambda_init(l):
    return 0.8 - 0.6 * math.exp(-0.3 * l)


def _params(**kw):
    return pltpu.CompilerParams(vmem_limit_bytes=VMEM_LIMIT, **kw)


def _const_spec(shape):
    zeros = (0,) * len(shape)
    return pl.BlockSpec(shape, lambda *_: zeros, pipeline_mode=pl.Buffered(1))


def _layer_spec(shape, layer):
    zeros = (0,) * len(shape)
    return pl.BlockSpec((None,) + tuple(shape), lambda *_: (layer,) + zeros,
                        pipeline_mode=pl.Buffered(1))


def _dot(a, b):
    return jnp.dot(a, b, preferred_element_type=F32)


def _dot_f32(a, b):
    return jnp.dot(a, b, preferred_element_type=F32, precision=lax.Precision.HIGHEST)


def _split_bf16(x):
    hi = x.astype(BF16)
    return hi, (x - hi.astype(F32)).astype(BF16)


def _layer_norm_rows(x, g, b):
    mu = jnp.mean(x, axis=-1, keepdims=True)
    xc = x - mu
    var = jnp.mean(xc * xc, axis=-1, keepdims=True)
    return xc * lax.rsqrt(var + LN_EPS) * g + b


def _seqs_per_step(batch, seq):
    return max(1, min(batch, MIXER_ROWS // seq))


@functools.lru_cache(maxsize=None)
def _dft_tables(L):
    n = 2 * L
    k = np.arange(L, dtype=np.float64)[:, None]
    s = np.arange(L, dtype=np.float64)[None, :]
    ang = 2.0 * np.pi * k * s / n
    fwd = np.concatenate([np.cos(ang), -np.sin(ang)], axis=0)
    fwd[L, :] = np.cos(np.pi * np.arange(L))
    t = np.arange(L, dtype=np.float64)[:, None]
    kk = np.arange(L, dtype=np.float64)[None, :]
    ang_i = 2.0 * np.pi * t * kk / n
    inv_re = (2.0 / n) * np.cos(ang_i)
    inv_re[:, 0] = 1.0 / n
    inv_im = -(2.0 / n) * np.sin(ang_i)
    inv_im[:, 0] = np.cos(np.pi * np.arange(L)) / n
    inv = np.concatenate([inv_re, inv_im], axis=1)
    return fwd.astype(np.float32), inv.astype(np.float32)


@functools.lru_cache(maxsize=None)
def _filter_tables(L):
    bands = (HY_FILTER_EMB - 1) // 2
    t = np.linspace(0.0, 1.0, L)[:, None]
    w = (2.0 * np.pi / L) * np.arange(L, dtype=np.float64)[:, None]
    fr = np.linspace(1e-4, bands - 1, bands)[None, :]
    feats = np.concatenate([t, np.cos(fr * w), -np.sin(fr * w)], -1)
    feats_p = np.zeros((L, LANES), np.float64)
    feats_p[:, :HY_FILTER_EMB] = feats
    deltas = np.abs(np.linspace(math.log(HY_DECAY_TARGET) / HY_FAST_DECAY_PCT,
                                math.log(HY_DECAY_TARGET) / HY_SLOW_DECAY_PCT, HY_WIDTH))
    decay = np.exp(-t * deltas[None, :])
    return feats_p.astype(np.float32), decay.astype(np.float32)


@functools.lru_cache(maxsize=None)
def _rope_tables(n):
    half = AX_DIM // 2
    pos = np.arange(n)
    row = (pos // GRID_W).astype(np.float64)[:, None]
    col = (pos % GRID_W).astype(np.float64)[:, None]
    inv = ROPE_BASE ** (-np.arange(0, AX_DIM, 2, dtype=np.float64) / AX_DIM)[None, :]
    zero = np.zeros((n, half))
    cr, sr = np.cos(row * inv), np.sin(row * inv)
    cc, sc = np.cos(col * inv), np.sin(col * inv)
    c = np.concatenate([cr, cr, cc, cc], -1)
    s_up = np.concatenate([-sr, zero, -sc, zero], -1)
    s_dn = np.concatenate([zero, sr, zero, sc], -1)
    reps = LANES // DIFF_HEAD_DIM
    tile = lambda a: np.tile(a, (1, reps)).astype(np.float32)
    return tile(c), tile(s_up), tile(s_dn)


def _ada_kernel(cond_ref, w_ref, b_ref, o_ref):
    c = cond_ref[...]
    s = c * jax.nn.sigmoid(c)
    o_ref[0] = _dot(s.astype(BF16), w_ref[0].astype(BF16)) + b_ref[0]


def _ada_call(cond, w_ada, b_ada):
    n_out = N_ADA * D_MODEL
    return pl.pallas_call(
        _ada_kernel,
        grid=(DEPTH, n_out // ADA_COL_TILE),
        in_specs=[
            pl.BlockSpec((COND_ROWS, D_MODEL), lambda l, j: (0, 0)),
            pl.BlockSpec((1, D_MODEL, ADA_COL_TILE), lambda l, j: (l, 0, j)),
            pl.BlockSpec((1, 1, ADA_COL_TILE), lambda l, j: (l, 0, j)),
        ],
        out_specs=pl.BlockSpec((1, COND_ROWS, ADA_COL_TILE), lambda l, j: (l, 0, j)),
        out_shape=jax.ShapeDtypeStruct((DEPTH, COND_ROWS, n_out), F32),
        compiler_params=_params(dimension_semantics=("arbitrary", "arbitrary")),
        name="ada",
    )(cond, w_ada, b_ada.reshape(DEPTH, 1, n_out))


def _rope_lanes(x, c, s_up, s_dn):
    outs = []
    for p in range(ATT_WIDTH // LANES):
        xb = x[:, p * LANES:(p + 1) * LANES]
        up = pltpu.roll(xb, LANES - AX_DIM // 2, 1)
        dn = pltpu.roll(xb, AX_DIM // 2, 1)
        outs.append(xb * c + up * s_up + dn * s_dn)
    return jnp.concatenate(outs, axis=1)


Q_SCALE = DIFF_HEAD_DIM ** -0.5 * math.log2(math.e)


def _inproj_kernel(*refs, rope, n_cache_in, seq, cache_slot):
    if rope:
        x_ref, ada_ref, w_ref, c_ref, su_ref, sd_ref, hy_ref, cf_ref, q_ref, k_ref, v_ref = refs
        kc_ref = vc_ref = None
    else:
        x_ref, ada_ref, w_ref = refs[:3]
        hy_ref, cf_ref, q_ref, k_ref, v_ref, kc_ref, vc_ref = refs[3 + n_cache_in:]
    sh1 = ada_ref[0, 0, 0:1, :]
    sc1 = ada_ref[0, 0, 1:2, :]
    h = (x_ref[...] * (1.0 + sc1) + sh1).astype(BF16)
    o1 = 3 * HY_WIDTH
    o2 = o1 + 2 * CF_WIDTH
    o3 = o2 + ATT_WIDTH
    o4 = o3 + ATT_WIDTH
    hy_ref[...] = _dot(h, w_ref[:, 0:o1])
    cf_ref[...] = _dot(h, w_ref[:, o1:o2])
    q = _dot(h, w_ref[:, o2:o3])
    k = _dot(h, w_ref[:, o3:o4])
    v = _dot(h, w_ref[:, o4:IN_WIDTH])
    if rope:
        c, su, sd = c_ref[...], su_ref[...], sd_ref[...]
        q = _rope_lanes(q, c, su, sd)
        k = _rope_lanes(k, c, su, sd)
    q_ref[...] = (q * Q_SCALE).astype(BF16)
    k_ref[...] = k.astype(BF16)
    v_ref[...] = v.astype(BF16)
    if kc_ref is not None:
        for b in range(kc_ref.shape[0]):
            rows = slice(b * seq, (b + 1) * seq)
            for slot in range(kc_ref.shape[1]):
                if slot == cache_slot:
                    kc_ref[b, slot] = k[rows].reshape(seq, 2 * DIFF_HEADS, DIFF_HEAD_DIM)
                    vc_ref[b, slot] = v[rows].reshape(seq, DIFF_HEADS, 2 * DIFF_HEAD_DIM)
                else:
                    kc_ref[b, slot] = jnp.zeros(kc_ref.shape[2:], F32)
                    vc_ref[b, slot] = jnp.zeros(vc_ref.shape[2:], F32)


def _inproj_call(x2d, ada4, w_in_bf, layer, batch, seq, cond_row_of_tile, rope_tabs, caches):
    T = x2d.shape[0]
    tiles_per_seq = seq // ROW_TILE if seq >= ROW_TILE else None
    in_specs = [
        pl.BlockSpec((ROW_TILE, D_MODEL), lambda i: (i, 0)),
        pl.BlockSpec((1, 1, N_ADA, D_MODEL), lambda i: (layer, cond_row_of_tile(i), 0, 0)),
        _layer_spec((D_MODEL, IN_WIDTH), layer),
    ]
    args = [x2d, ada4, w_in_bf]
    rows = lambda w: pl.BlockSpec((ROW_TILE, w), lambda i: (i, 0))
    out_specs = [rows(3 * HY_WIDTH), rows(2 * CF_WIDTH), rows(ATT_WIDTH), rows(ATT_WIDTH), rows(ATT_WIDTH)]
    out_shape = [jax.ShapeDtypeStruct((T, 3 * HY_WIDTH), F32), jax.ShapeDtypeStruct((T, 2 * CF_WIDTH), F32)]
    out_shape += [jax.ShapeDtypeStruct((T, ATT_WIDTH), BF16)] * 3
    aliases = {}
    n_cache_in = 0
    cache_slot = 0
    if rope_tabs is not None:
        for tab in rope_tabs:
            in_specs.append(pl.BlockSpec((ROW_TILE, LANES), lambda i: (i % tiles_per_seq, 0)))
            args.append(tab)
    if caches is not None:
        seqs = ROW_TILE // seq
        k_shape = (batch, DEPTH, seq, 2 * DIFF_HEADS, DIFF_HEAD_DIM)
        v_shape = (batch, DEPTH, seq, DIFF_HEADS, 2 * DIFF_HEAD_DIM)
        first_call = len(caches) == 0
        cache_slot = layer if first_call else 0
        for shape in (k_shape, v_shape):
            if first_call:
                out_specs.append(pl.BlockSpec((seqs, DEPTH) + shape[2:], lambda i: (i, 0, 0, 0, 0)))
            else:
                out_specs.append(pl.BlockSpec((seqs, 1) + shape[2:], lambda i: (i, layer, 0, 0, 0)))
            out_shape.append(jax.ShapeDtypeStruct(shape, F32))
        for j, prev in enumerate(caches):
            aliases[len(args)] = len(out_shape) - 2 + j
            in_specs.append(pl.BlockSpec(memory_space=pl.ANY))
            args.append(prev)
        n_cache_in = len(caches)
    return pl.pallas_call(
        functools.partial(_inproj_kernel, rope=rope_tabs is not None, n_cache_in=n_cache_in, seq=seq,
                          cache_slot=cache_slot),
        grid=(T // ROW_TILE,),
        in_specs=in_specs,
        out_specs=out_specs,
        out_shape=out_shape,
        input_output_aliases=aliases,
        compiler_params=_params(dimension_semantics=("arbitrary",)),
        name="inproj",
    )(*args)


def _store_spectrum_tables(spec, layer, ga_ref, gbc_ref, gd_ref):
    L = spec.shape[0] // 2
    W = spec.shape[1] // 2
    row = lax.broadcasted_iota(jnp.int32, (L, W), 0)
    p = spec[:L]
    q = spec[L:]
    g_re = p[:, :W] + p[:, W:]
    g_im = q[:, :W] - q[:, W:]
    nyq = q[0:1, :W] + q[0:1, W:]
    ga_ref[layer] = g_re
    gbc_ref[layer] = jnp.where(row == 0, 0.0, g_im)
    gd_ref[layer] = jnp.where(row == 0, nyq, g_re)


def _filter_kernel(feats_ref, w1_ref, b1_ref, fq_ref, w2_ref, b2_ref, w3_ref, decay_ref,
                   fwd_hi_ref, fwd_lo_ref, ga_ref, gbc_ref, gd_ref):
    L = feats_ref.shape[0]
    decay = decay_ref[...]
    row = lax.broadcasted_iota(jnp.int32, (L, HY_WIDTH), 0)
    taps = []
    for l in range(DEPTH):
        fq = fq_ref[l]
        hid = jnp.sin(fq * (_dot_f32(feats_ref[...], w1_ref[l]) + b1_ref[l]))
        hid = jnp.sin(fq * (_dot_f32(hid, w2_ref[l]) + b2_ref[l]))
        hf = _dot_f32(hid, w3_ref[l])
        taps.append(hf[:, :HY_WIDTH] * decay)
        taps.append(jnp.where(row == 0, 0.0, hf[:, HY_WIDTH:] * decay))
    h_hi, h_lo = _split_bf16(jnp.concatenate(taps, axis=1))
    fwd_hi = fwd_hi_ref[...]
    spec = _dot(fwd_hi, h_hi) + (_dot(fwd_hi, h_lo) + _dot(fwd_lo_ref[...], h_hi))
    for l in range(DEPTH):
        _store_spectrum_tables(spec[:, 2 * l * HY_WIDTH:2 * (l + 1) * HY_WIDTH], l, ga_ref, gbc_ref, gd_ref)


def _pad_to(a, shape):
    return jnp.pad(a, [(0, s - d) for s, d in zip(shape, a.shape)])


def _filter_call(L, w1, b1, fq, w2, b2, w3):
    feats, decay = _filter_tables(L)
    fwd, _ = _dft_tables(L)
    row_vec = lambda a: _pad_to(a.reshape(DEPTH, 1, -1), (DEPTH, 1, LANES))
    args = [
        jnp.asarray(feats),
        _pad_to(w1, (DEPTH, LANES, LANES)),
        row_vec(b1),
        row_vec(fq),
        _pad_to(w2, (DEPTH, LANES, LANES)),
        row_vec(b2),
        _pad_to(w3, (DEPTH, LANES, 2 * HY_WIDTH)),
        jnp.asarray(decay),
        *_split_bf16(jnp.asarray(fwd)),
    ]
    out = jax.ShapeDtypeStruct((DEPTH, L, HY_WIDTH), F32)
    return pl.pallas_call(
        _filter_kernel,
        out_shape=[out, out, out],
        compiler_params=_params(),
        name="hyena_filter",
    )(*args)


def _conv_filter_kernel(taps_ref, fwd_ref, ga_ref, gbc_ref, gd_ref):
    spec = _dot_f32(fwd_ref[...], taps_ref[...])
    for l in range(DEPTH):
        _store_spectrum_tables(spec[:, 2 * l * CF_WIDTH:2 * (l + 1) * CF_WIDTH], l, ga_ref, gbc_ref, gd_ref)


def _conv_filter_call(L, conv_w):
    half = CF_CONV_K // 2
    fwd, _ = _dft_tables(L)
    lag_pos = conv_w[:, half::-1, :]
    lag_neg = _pad_to(conv_w[:, half + 1:, :], (DEPTH, half + 1, CF_WIDTH))
    lag_neg = jnp.roll(lag_neg, 1, axis=1)
    taps = jnp.concatenate([lag_pos, lag_neg], axis=2)
    taps = _pad_to(taps, (DEPTH, LANES, 2 * CF_WIDTH))
    taps = jnp.transpose(taps, (1, 0, 2)).reshape(LANES, DEPTH * 2 * CF_WIDTH)
    out = jax.ShapeDtypeStruct((DEPTH, L, CF_WIDTH), F32)
    return pl.pallas_call(
        _conv_filter_kernel,
        out_shape=[out, out, out],
        compiler_params=_params(),
        name="conv_filter",
    )(taps, jnp.asarray(fwd[:, :LANES]))


def _dft_conv(us, fwd_ref, inv_ref, ga_ref, gbc_ref, gd_ref):
    L, W = us[0].shape
    u_all = jnp.concatenate(us, axis=1).astype(BF16)
    spec = _dot(fwd_ref[...], u_all)
    ga, gbc, gd = ga_ref[...], gbc_ref[...], gd_ref[...]
    y_re, y_im = [], []
    for s in range(len(us)):
        p = spec[:L, s * W:(s + 1) * W]
        q = spec[L:, s * W:(s + 1) * W]
        y_re.append(p * ga - q * gbc)
        y_im.append(p * gbc + q * gd)
    y_spec = jnp.concatenate(
        [jnp.concatenate(y_re, axis=1), jnp.concatenate(y_im, axis=1)], axis=0).astype(BF16)
    return _dot(inv_ref[...], y_spec)


def _hyena_kernel(hy_ref, cw_ref, cb_ref, fwd_ref, inv_ref, ga_ref, gbc_ref, gd_ref, hb_ref,
                  o_ref, pad_ref):
    nb, L, width = hy_ref.shape
    zeros = jnp.zeros((SUBLANES, width), F32)
    x0s, us = [], []
    for s in range(nb):
        pad_ref[s, 0:SUBLANES, :] = zeros
        pad_ref[s, SUBLANES + L:2 * SUBLANES + L, :] = zeros
        pad_ref[s, SUBLANES:SUBLANES + L, :] = hy_ref[s]
        conv = cb_ref[...]
        for j in range(HY_SHORT_K):
            start = SUBLANES + j - HY_SHORT_K // 2
            conv = conv + cw_ref[j:j + 1, :] * pad_ref[s, start:start + L, :]
        x0s.append(conv[:, :HY_WIDTH])
        us.append(conv[:, HY_WIDTH:2 * HY_WIDTH] * conv[:, 2 * HY_WIDTH:])
    y = _dft_conv(us, fwd_ref, inv_ref, ga_ref, gbc_ref, gd_ref)
    for s in range(nb):
        o = x0s[s] * (y[:, s * HY_WIDTH:(s + 1) * HY_WIDTH] + us[s] * hb_ref[...])
        o_ref[s] = o.astype(o_ref.dtype)


def _hyena_call(hy3, conv_w, conv_b, g_tabs, hy_bias, layer):
    B, L, width = hy3.shape
    nb = max(1, min(B, HYENA_ROWS // L))
    fwd, inv = _dft_tables(L)
    fwd_bf = jnp.asarray(fwd).astype(BF16)
    inv_bf = jnp.asarray(inv).astype(BF16)
    tab = _layer_spec((L, HY_WIDTH), layer)
    return pl.pallas_call(
        _hyena_kernel,
        grid=(B // nb,),
        in_specs=[
            pl.BlockSpec((nb, L, width), lambda b: (b, 0, 0)),
            _layer_spec((HY_SHORT_K, width), layer),
            _layer_spec((1, width), layer),
            _const_spec((2 * L, L)),
            _const_spec((L, 2 * L)),
            tab, tab, tab,
            _layer_spec((1, HY_WIDTH), layer),
        ],
        out_specs=pl.BlockSpec((nb, L, HY_WIDTH), lambda b: (b, 0, 0)),
        out_shape=jax.ShapeDtypeStruct((B, L, HY_WIDTH), BF16),
        scratch_shapes=[pltpu.VMEM((nb, L + 2 * SUBLANES, width), F32)],
        compiler_params=_params(dimension_semantics=("arbitrary",)),
        name="hyena",
    )(hy3, conv_w, conv_b, fwd_bf, inv_bf, *g_tabs, hy_bias)


CF_PAD = 2 * SUBLANES


def _conformer_kernel(cf_ref, cw_ref, cb_ref, g_ref, b_ref, o_ref, pad_ref):
    nb, L, _ = cf_ref.shape
    padded = L + 2 * CF_PAD
    shifted = padded - SUBLANES
    zeros = jnp.zeros((CF_PAD, CF_WIDTH), F32)
    half = CF_CONV_K // 2
    for s in range(nb):
        cf = cf_ref[s]
        pad_ref[s, 0, 0:CF_PAD, :] = zeros
        pad_ref[s, 0, CF_PAD + L:padded, :] = zeros
        pad_ref[s, 0, CF_PAD:CF_PAD + L, :] = cf[:, :CF_WIDTH] * jax.nn.sigmoid(cf[:, CF_WIDTH:])
        for r in range(1, SUBLANES):
            pad_ref[s, r, 0:shifted, :] = pad_ref[s, 0, r:r + shifted, :]
        for c0 in range(0, L, CF_ROW_CHUNK):
            acc = jnp.zeros((CF_ROW_CHUNK, CF_WIDTH), F32) + cb_ref[...]
            for j in range(CF_CONV_K):
                off = CF_PAD + j - half
                start = c0 + (off // SUBLANES) * SUBLANES
                acc = acc + cw_ref[j:j + 1, :] * pad_ref[s, off % SUBLANES, start:start + CF_ROW_CHUNK, :]
            y = _layer_norm_rows(acc, g_ref[...], b_ref[...])
            o_ref[s, c0:c0 + CF_ROW_CHUNK, :] = (y * jax.nn.sigmoid(y)).astype(o_ref.dtype)


def _conformer_dft_kernel(cf_ref, fwd_ref, inv_ref, ga_ref, gbc_ref, gd_ref, cb_ref, g_ref, b_ref, o_ref):
    nb = cf_ref.shape[0]
    us = []
    for s in range(nb):
        cf = cf_ref[s]
        us.append(cf[:, :CF_WIDTH] * jax.nn.sigmoid(cf[:, CF_WIDTH:]))
    conv = _dft_conv(us, fwd_ref, inv_ref, ga_ref, gbc_ref, gd_ref)
    for s in range(nb):
        y = _layer_norm_rows(conv[:, s * CF_WIDTH:(s + 1) * CF_WIDTH] + cb_ref[...], g_ref[...], b_ref[...])
        o_ref[s] = (y * jax.nn.sigmoid(y)).astype(o_ref.dtype)


def _conformer_call(cf3, conv_w, conv_b, ln_g, ln_b, layer, g_tabs):
    B, L, width = cf3.shape
    nb = _seqs_per_step(B, L)
    row = _layer_spec((1, CF_WIDTH), layer)
    if L <= CF_DFT_MAX_SEQ:
        fwd, inv = _dft_tables(L)
        tab = _layer_spec((L, CF_WIDTH), layer)
        return pl.pallas_call(
            _conformer_dft_kernel,
            grid=(B // nb,),
            in_specs=[
                pl.BlockSpec((nb, L, width), lambda b: (b, 0, 0)),
                _const_spec((2 * L, L)),
                _const_spec((L, 2 * L)),
                tab, tab, tab,
                row, row, row,
            ],
            out_specs=pl.BlockSpec((nb, L, CF_WIDTH), lambda b: (b, 0, 0)),
            out_shape=jax.ShapeDtypeStruct((B, L, CF_WIDTH), BF16),
            compiler_params=_params(dimension_semantics=("arbitrary",)),
            name="conformer_dft",
        )(cf3, jnp.asarray(fwd).astype(BF16), jnp.asarray(inv).astype(BF16), *g_tabs, conv_b, ln_g, ln_b)
    return pl.pallas_call(
        _conformer_kernel,
        grid=(B // nb,),
        in_specs=[
            pl.BlockSpec((nb, L, width), lambda b: (b, 0, 0)),
            _layer_spec((CF_CONV_K, CF_WIDTH), layer),
            row, row, row,
        ],
        out_specs=pl.BlockSpec((nb, L, CF_WIDTH), lambda b: (b, 0, 0)),
        out_shape=jax.ShapeDtypeStruct((B, L, CF_WIDTH), BF16),
        scratch_shapes=[pltpu.VMEM((nb, SUBLANES, L + 2 * CF_PAD, CF_WIDTH), F32)],
        compiler_params=_params(dimension_semantics=("arbitrary",)),
        name="conformer",
    )(cf3, conv_w, conv_b, ln_g, ln_b)


def _attn_kernel(*refs, lambda_init, past):
    if past:
        (q_ref, k_ref, v_ref, ck_ref, cv_ref, lq1, lk1, lq2, lk2, g_ref, o_ref, k_all, v_aug) = refs
    else:
        (q_ref, k_ref, v_ref, lq1, lk1, lq2, lk2, g_ref, o_ref) = refs
    nb, tq, _ = q_ref.shape

    if past:
        keys = v_aug.shape[1]

        @pl.when(pl.program_id(1) == 0)
        def _():
            ones = jnp.ones((keys, HEAD_PAIR), BF16)
            for s in range(nb):
                k_all[s, 0:past, :] = ck_ref[s, 0].reshape(past, ATT_WIDTH).astype(BF16)
                k_all[s, past:keys, :] = k_ref[s]
                v_old = cv_ref[s, 0].reshape(past, ATT_WIDTH).astype(BF16)
                for h in range(DIFF_HEADS):
                    cols = slice(h * HEAD_PAIR, (h + 1) * HEAD_PAIR)
                    base = 2 * h * HEAD_PAIR
                    v_aug[s, 0:past, base:base + HEAD_PAIR] = v_old[:, cols]
                    v_aug[s, past:keys, base:base + HEAD_PAIR] = v_ref[s, :, cols]
                    v_aug[s, :, base + HEAD_PAIR:base + 2 * HEAD_PAIR] = ones

    lam = (jnp.exp(jnp.sum(lq1[...] * lk1[...], axis=-1, keepdims=True))
           - jnp.exp(jnp.sum(lq2[...] * lk2[...], axis=-1, keepdims=True)) + lambda_init)
    lane = lax.broadcasted_iota(jnp.int32, (tq, HEAD_PAIR), 1)
    first = lane < DIFF_HEAD_DIM
    nt = (((1,), (1,)), ((), ()))
    gain = g_ref[...] * (1.0 - lambda_init)
    zero = jnp.zeros((), BF16)

    def scores(s, h):
        cols = slice(h * HEAD_PAIR, (h + 1) * HEAD_PAIR)
        q = q_ref[s, :, cols]
        q2 = jnp.concatenate([jnp.where(first, q, zero), jnp.where(first, zero, q)], axis=0)
        kh = k_all[s, :, cols] if past else k_ref[s, :, cols]
        return lax.dot_general(q2, kh, nt, preferred_element_type=F32)

    def finish(s, h, sc):
        cols = slice(h * HEAD_PAIR, (h + 1) * HEAD_PAIR)
        e = jnp.exp2(sc - jnp.max(sc, axis=-1, keepdims=True))
        if past:
            p = _dot(e.astype(BF16), v_aug[s, :, 2 * h * HEAD_PAIR:2 * (h + 1) * HEAD_PAIR])
            p0, p1 = p[:tq], p[tq:]
            o = (p0[:, :HEAD_PAIR] * (1.0 / p0[:, HEAD_PAIR:HEAD_PAIR + 1])
                 - p1[:, :HEAD_PAIR] * (lam / p1[:, HEAD_PAIR:HEAD_PAIR + 1]))
        else:
            r = 1.0 / jnp.sum(e, axis=-1, keepdims=True)
            a = e[:tq] * r[:tq] - e[tq:] * (lam * r[tq:])
            o = _dot(a.astype(BF16), v_ref[s, :, cols])
        o = o * lax.rsqrt(jnp.mean(o * o, axis=-1, keepdims=True) + LN_EPS) * gain
        o_ref[s, :, cols] = o.astype(o_ref.dtype)

    lookahead = 0 if past else 1
    pending = []
    for s in range(nb):
        for h in range(DIFF_HEADS):
            pending.append((s, h, scores(s, h)))
            if len(pending) > lookahead:
                finish(*pending.pop(0))
    for unit in pending:
        finish(*unit)


def _attn_call(q3, k3, v3, cache_k, cache_v, layer, lam_vecs, subln_g):
    B, L, _ = q3.shape
    past = 0 if cache_k is None else cache_k.shape[2]
    nb = _seqs_per_step(B, 2 * L)
    tq = min(Q_TILE, L)
    tile = pl.BlockSpec((nb, tq, ATT_WIDTH), lambda b, i: (b, i, 0))
    seq = pl.BlockSpec((nb, L, ATT_WIDTH), lambda b, i: (b, 0, 0))
    in_specs = [tile, seq, seq]
    args = [q3, k3, v3]
    scratch = []
    if past:
        for cache in (cache_k, cache_v):
            in_specs.append(pl.BlockSpec((nb, 1) + cache.shape[2:], lambda b, i: (b, layer, 0, 0, 0)))
            args.append(cache)
        scratch = [pltpu.VMEM((nb, past + L, ATT_WIDTH), BF16),
                   pltpu.VMEM((nb, past + L, 2 * ATT_WIDTH), BF16)]
    in_specs += [_layer_spec((1, DIFF_HEAD_DIM), layer)] * 4 + [_layer_spec((1, HEAD_PAIR), layer)]
    args += list(lam_vecs) + [subln_g]
    return pl.pallas_call(
        functools.partial(_attn_kernel, lambda_init=_lambda_init(layer), past=past),
        grid=(B // nb, L // tq),
        in_specs=in_specs,
        out_specs=tile,
        out_shape=jax.ShapeDtypeStruct((B, L, ATT_WIDTH), BF16),
        scratch_shapes=scratch,
        compiler_params=_params(dimension_semantics=("arbitrary", "arbitrary")),
        name="diff_attn",
    )(*args)


def _mlp_kernel(x_ref, hy_ref, cf_ref, at_ref, ada_ref, wo_ref, w1_ref, w2_ref,
                g1_ref, b1_ref, g2_ref, b2_ref, o_ref):
    gate1 = ada_ref[0, 0, 2:3, :]
    sh2 = ada_ref[0, 0, 3:4, :]
    sc2 = ada_ref[0, 0, 4:5, :]
    gate2 = ada_ref[0, 0, 5:6, :]
    mix = jnp.concatenate([hy_ref[...], cf_ref[...], at_ref[...]], axis=1)
    y = _dot(mix, wo_ref[...])
    x = _layer_norm_rows(DEEPNORM_ALPHA * x_ref[...] + gate1 * y, g1_ref[...], b1_ref[...])
    h = (x * (1.0 + sc2) + sh2).astype(BF16)
    a = jnp.maximum(_dot(h, w1_ref[...]), 0.0)
    m = _dot((a * a).astype(BF16), w2_ref[...])
    o_ref[...] = _layer_norm_rows(DEEPNORM_ALPHA * x + gate2 * m, g2_ref[...], b2_ref[...])


def _mlp_call(x2d, y_hy, y_cf, y_at, ada4, wo_bf, w1_bf, w2_bf, ln1_g, ln1_b, ln2_g, ln2_b,
              layer, cond_row_of_tile):
    T = x2d.shape[0]
    rows = lambda w: pl.BlockSpec((ROW_TILE, w), lambda i: (i, 0))
    vec = _layer_spec((1, D_MODEL), layer)
    return pl.pallas_call(
        _mlp_kernel,
        grid=(T // ROW_TILE,),
        in_specs=[
            rows(D_MODEL), rows(HY_WIDTH), rows(CF_WIDTH), rows(ATT_WIDTH),
            pl.BlockSpec((1, 1, N_ADA, D_MODEL), lambda i: (layer, cond_row_of_tile(i), 0, 0)),
            _layer_spec((MIX_WIDTH, D_MODEL), layer),
            _layer_spec((D_MODEL, D_FF), layer),
            _layer_spec((D_FF, D_MODEL), layer),
            vec, vec, vec, vec,
        ],
        out_specs=rows(D_MODEL),
        out_shape=jax.ShapeDtypeStruct((T, D_MODEL), F32),
        compiler_params=_params(dimension_semantics=("arbitrary",)),
        name="outproj_mlp",
    )(x2d, y_hy, y_cf, y_at, ada4, wo_bf, w1_bf, w2_bf, ln1_g, ln1_b, ln2_g, ln2_b)


def _layer(x2d, batch, seq, layer, ada4, wts, hy_tabs, cf_tabs, cond_row_of_tile, rope_tabs, cache_kv,
           new_caches):
    hy, cf, q, k, v, *new_caches = _inproj_call(
        x2d, ada4, wts["w_in"], layer, batch, seq, cond_row_of_tile, rope_tabs, new_caches)
    as_seq = lambda a: a.reshape(batch, seq, a.shape[-1])
    y_hy = _hyena_call(as_seq(hy), wts["hy_conv_w"], wts["hy_conv_b"], hy_tabs, wts["hy_bias"], layer)
    y_cf = _conformer_call(as_seq(cf), wts["cf_conv_w"], wts["cf_conv_b"], wts["cf_ln_g"],
                           wts["cf_ln_b"], layer, cf_tabs)
    ck, cv = cache_kv if cache_kv is not None else (None, None)
    lam_vecs = (wts["lam_q1"], wts["lam_k1"], wts["lam_q2"], wts["lam_k2"])
    y_at = _attn_call(as_seq(q), as_seq(k), as_seq(v), ck, cv, layer, lam_vecs, wts["subln_g"])
    flat = lambda a: a.reshape(batch * seq, a.shape[-1])
    x_out = _mlp_call(x2d, flat(y_hy), flat(y_cf), flat(y_at), ada4,
                      wts["w_out"], wts["w_mlp1"], wts["w_mlp2"],
                      wts["ln1_g"], wts["ln1_b"], wts["ln2_g"], wts["ln2_b"], layer, cond_row_of_tile)
    return x_out, tuple(new_caches)


def kernel(x_prompt, x_sample, cache_k, cache_v, c, c_ctx, w_ada, b_ada, w_in, hy_conv_w, hy_conv_b, hf_w1, hf_b1, hf_freq, hf_w2, hf_b2, hf_w3, hy_bias, cf_conv_w, cf_conv_b, cf_ln_g, cf_ln_b, lam_q1, lam_k1, lam_q2, lam_k2, subln_g, w_out, ln1_g, ln1_b, w_mlp1, w_mlp2, ln2_g, ln2_b):
    batch, seq, _ = x_prompt.shape
    dec_batch, dec_seq, _ = x_sample.shape
    assert 1 + dec_batch <= COND_ROWS
    assert ROW_TILE % seq == 0 and (batch * seq) % ROW_TILE == 0
    assert dec_seq % ROW_TILE == 0 and dec_seq % Q_TILE == 0

    cond = jnp.concatenate(
        [c_ctx[None, :], c, jnp.zeros((COND_ROWS - 1 - dec_batch, D_MODEL), F32)], axis=0)
    ada4 = _ada_call(cond, w_ada, b_ada).reshape(DEPTH, COND_ROWS, N_ADA, D_MODEL)

    row_vec = lambda a: a.reshape(DEPTH, 1, a.shape[-1])
    wts = dict(
        w_in=w_in.astype(BF16), w_out=w_out.astype(BF16),
        w_mlp1=w_mlp1.astype(BF16), w_mlp2=w_mlp2.astype(BF16),
        hy_conv_w=hy_conv_w, hy_conv_b=row_vec(hy_conv_b), hy_bias=row_vec(hy_bias),
        cf_conv_w=cf_conv_w, cf_conv_b=row_vec(cf_conv_b),
        cf_ln_g=row_vec(cf_ln_g), cf_ln_b=row_vec(cf_ln_b),
        lam_q1=row_vec(lam_q1), lam_k1=row_vec(lam_k1), lam_q2=row_vec(lam_q2), lam_k2=row_vec(lam_k2),
        subln_g=row_vec(subln_g),
        ln1_g=row_vec(ln1_g), ln1_b=row_vec(ln1_b), ln2_g=row_vec(ln2_g), ln2_b=row_vec(ln2_b))

    rope_tabs = tuple(jnp.asarray(t) for t in _rope_tables(dec_seq))
    dec_tiles_per_seq = dec_seq // ROW_TILE

    xp = x_prompt.reshape(batch * seq, D_MODEL)
    xs = x_sample.reshape(dec_batch * dec_seq, D_MODEL)
    new_caches = ()
    filt = (hf_w1, hf_b1, hf_freq, hf_w2, hf_b2, hf_w3)
    hy_ctx = _filter_call(seq, *filt)
    hy_dec = _filter_call(dec_seq, *filt)
    cf_ctx = _conv_filter_call(seq, cf_conv_w) if seq <= CF_DFT_MAX_SEQ else None
    cf_dec = _conv_filter_call(dec_seq, cf_conv_w) if dec_seq <= CF_DFT_MAX_SEQ else None
    for l in range(DEPTH):
        xp, new_caches = _layer(xp, batch, seq, l, ada4, wts, hy_ctx, cf_ctx, lambda i: 0, None, None,
                                new_caches)
        xs, _ = _layer(xs, dec_batch, dec_seq, l, ada4, wts, hy_dec, cf_dec,
                       lambda i: 1 + i // dec_tiles_per_seq, rope_tabs, (cache_k, cache_v), None)
    new_cache_k, new_cache_v = new_caches
    return (xp.reshape(batch, seq, D_MODEL), xs.reshape(dec_batch, dec_seq, D_MODEL),
            new_cache_k, new_cache_v)
```

```python
import functools
import math

import numpy as np
import jax
import jax.numpy as jnp
from jax import lax
from jax.experimental import pallas as pl
from jax.experimental.pallas import tpu as pltpu

D_MODEL = 1024
DEPTH = 2
GRID_W = 64
HY_WIDTH = D_MODEL // 4
CF_WIDTH = D_MODEL // 4
ATT_WIDTH = D_MODEL // 2
DIFF_HEAD_DIM = 64
DIFF_HEADS = ATT_WIDTH // (2 * DIFF_HEAD_DIM)
HEAD_PAIR = 2 * DIFF_HEAD_DIM
MIX_WIDTH = HY_WIDTH + CF_WIDTH + ATT_WIDTH
IN_WIDTH = 3 * HY_WIDTH + 2 * CF_WIDTH + 3 * ATT_WIDTH
HY_SHORT_K = 3
HY_FILTER_EMB = 33
HY_FILTER_HIDDEN = 64
HY_FAST_DECAY_PCT = 0.3
HY_SLOW_DECAY_PCT = 1.5
HY_DECAY_TARGET = 1e-2
CF_CONV_K = 31
D_FF = 4 * D_MODEL
ROPE_BASE = 10000.0
AX_DIM = DIFF_HEAD_DIM // 2
DEEPNORM_ALPHA = (2 * DEPTH) ** 0.25
LN_EPS = 1e-5
N_ADA = 6

LANES = 128
SUBLANES = 8
COND_ROWS = SUBLANES
VMEM_LIMIT = 56 * 1024 * 1024

ROW_TILE = 512
ADA_COL_TILE = 1536
Q_TILE = 512
MIXER_ROWS = 1024
HYENA_LONG_SEQS = 2
CF_DFT_MAX_SEQ = 256
CF_ROW_CHUNK = 128

F32 = jnp.float32
BF16 = jnp.bfloat16


def _lambda_init(l):
    return 0.8 - 0.6 * math.exp(-0.3 * l)


def _params(**kw):
    return pltpu.CompilerParams(vmem_limit_bytes=VMEM_LIMIT, **kw)


def _const_spec(shape):
    zeros = (0,) * len(shape)
    return pl.BlockSpec(shape, lambda *_: zeros, pipeline_mode=pl.Buffered(1))


def _layer_spec(shape, layer):
    zeros = (0,) * len(shape)
    return pl.BlockSpec((None,) + tuple(shape), lambda *_: (layer,) + zeros,
                        pipeline_mode=pl.Buffered(1))


def _dot(a, b):
    return jnp.dot(a, b, preferred_element_type=F32)


def _dot_f32(a, b):
    return jnp.dot(a, b, preferred_element_type=F32, precision=lax.Precision.HIGHEST)


def _split_bf16(x):
    hi = x.astype(BF16)
    return hi, (x - hi.astype(F32)).astype(BF16)


def _layer_norm_rows(x, g, b):
    mu = jnp.mean(x, axis=-1, keepdims=True)
    xc = x - mu
    var = jnp.mean(xc * xc, axis=-1, keepdims=True)
    return xc * lax.rsqrt(var + LN_EPS) * g + b


def _seqs_per_step(batch, seq):
    return max(1, min(batch, MIXER_ROWS // seq))


@functools.lru_cache(maxsize=None)
def _dft_tables(L):
    n = 2 * L
    k = np.arange(L, dtype=np.float64)[:, None]
    s = np.arange(L, dtype=np.float64)[None, :]
    ang = 2.0 * np.pi * k * s / n
    fwd = np.concatenate([np.cos(ang), -np.sin(ang)], axis=0)
    fwd[L, :] = np.cos(np.pi * np.arange(L))
    t = np.arange(L, dtype=np.float64)[:, None]
    kk = np.arange(L, dtype=np.float64)[None, :]
    ang_i = 2.0 * np.pi * t * kk / n
    inv_re = (2.0 / n) * np.cos(ang_i)
    inv_re[:, 0] = 1.0 / n
    inv_im = -(2.0 / n) * np.sin(ang_i)
    inv_im[:, 0] = np.cos(np.pi * np.arange(L)) / n
    inv = np.concatenate([inv_re, inv_im], axis=1)
    return fwd.astype(np.float32), inv.astype(np.float32)


@functools.lru_cache(maxsize=None)
def _filter_tables(L):
    bands = (HY_FILTER_EMB - 1) // 2
    t = np.linspace(0.0, 1.0, L)[:, None]
    w = (2.0 * np.pi / L) * np.arange(L, dtype=np.float64)[:, None]
    fr = np.linspace(1e-4, bands - 1, bands)[None, :]
    feats = np.concatenate([t, np.cos(fr * w), -np.sin(fr * w)], -1)
    feats_p = np.zeros((L, LANES), np.float64)
    feats_p[:, :HY_FILTER_EMB] = feats
    deltas = np.abs(np.linspace(math.log(HY_DECAY_TARGET) / HY_FAST_DECAY_PCT,
                                math.log(HY_DECAY_TARGET) / HY_SLOW_DECAY_PCT, HY_WIDTH))
    decay = np.exp(-t * deltas[None, :])
    return feats_p.astype(np.float32), decay.astype(np.float32)


@functools.lru_cache(maxsize=None)
def _rope_tables(n):
    half = AX_DIM // 2
    pos = np.arange(n)
    row = (pos // GRID_W).astype(np.float64)[:, None]
    col = (pos % GRID_W).astype(np.float64)[:, None]
    inv = ROPE_BASE ** (-np.arange(0, AX_DIM, 2, dtype=np.float64) / AX_DIM)[None, :]
    zero = np.zeros((n, half))
    cr, sr = np.cos(row * inv), np.sin(row * inv)
    cc, sc = np.cos(col * inv), np.sin(col * inv)
    c = np.concatenate([cr, cr, cc, cc], -1)
    s_up = np.concatenate([-sr, zero, -sc, zero], -1)
    s_dn = np.concatenate([zero, sr, zero, sc], -1)
    reps = LANES // DIFF_HEAD_DIM
    tile = lambda a: np.tile(a, (1, reps)).astype(np.float32)
    return tile(c), tile(s_up), tile(s_dn)


def _ada_kernel(cond_ref, w_ref, b_ref, o_ref):
    c = cond_ref[...]
    s = c * jax.nn.sigmoid(c)
    o_ref[0] = _dot(s.astype(BF16), w_ref[0].astype(BF16)) + b_ref[0]


def _ada_call(cond, w_ada, b_ada):
    n_out = N_ADA * D_MODEL
    return pl.pallas_call(
        _ada_kernel,
        grid=(DEPTH, n_out // ADA_COL_TILE),
        in_specs=[
            pl.BlockSpec((COND_ROWS, D_MODEL), lambda l, j: (0, 0)),
            pl.BlockSpec((1, D_MODEL, ADA_COL_TILE), lambda l, j: (l, 0, j)),
            pl.BlockSpec((1, 1, ADA_COL_TILE), lambda l, j: (l, 0, j)),
        ],
        out_specs=pl.BlockSpec((1, COND_ROWS, ADA_COL_TILE), lambda l, j: (l, 0, j)),
        out_shape=jax.ShapeDtypeStruct((DEPTH, COND_ROWS, n_out), F32),
        compiler_params=_params(dimension_semantics=("arbitrary", "arbitrary")),
        name="ada",
    )(cond, w_ada, b_ada.reshape(DEPTH, 1, n_out))


def _rope_lanes(x, c, s_up, s_dn):
    outs = []
    for p in range(ATT_WIDTH // LANES):
        xb = x[:, p * LANES:(p + 1) * LANES]
        up = pltpu.roll(xb, LANES - AX_DIM // 2, 1)
        dn = pltpu.roll(xb, AX_DIM // 2, 1)
        outs.append(xb * c + up * s_up + dn * s_dn)
    return jnp.concatenate(outs, axis=1)


Q_SCALE = DIFF_HEAD_DIM ** -0.5 * math.log2(math.e)


def _conformer_tail(us, fwd_ref, inv_ref, ga_ref, gbc_ref, gd_ref, cb_ref, g_ref, b_ref):
    conv = _dft_conv(us, fwd_ref, inv_ref, ga_ref, gbc_ref, gd_ref)
    outs = []
    for s in range(len(us)):
        y = _layer_norm_rows(conv[:, s * CF_WIDTH:(s + 1) * CF_WIDTH] + cb_ref[...], g_ref[...], b_ref[...])
        outs.append(y * jax.nn.sigmoid(y))
    return outs


def _inproj_kernel(*refs, names, seq, cache_slot):
    r = dict(zip(names, refs))
    sh1 = r["ada"][0, 0, 0:1, :]
    sc1 = r["ada"][0, 0, 1:2, :]
    h = (r["x"][...] * (1.0 + sc1) + sh1).astype(BF16)
    w_ref = r["w"]
    o1 = 3 * HY_WIDTH
    o2 = o1 + 2 * CF_WIDTH
    o3 = o2 + ATT_WIDTH
    o4 = o3 + ATT_WIDTH
    r["hy"][...] = _dot(h, w_ref[:, 0:o1])
    cf = _dot(h, w_ref[:, o1:o2])
    q = _dot(h, w_ref[:, o2:o3])
    k = _dot(h, w_ref[:, o3:o4])
    v = _dot(h, w_ref[:, o4:IN_WIDTH])
    if "ycf" in r:
        us = []
        for b in range(cf.shape[0] // seq):
            c = cf[b * seq:(b + 1) * seq]
            us.append(c[:, :CF_WIDTH] * jax.nn.sigmoid(c[:, CF_WIDTH:]))
        ys = _conformer_tail(us, r["cf_fwd"], r["cf_inv"], r["cf_ga"], r["cf_gbc"], r["cf_gd"],
                             r["cf_cb"], r["cf_g"], r["cf_b"])
        for b, y in enumerate(ys):
            r["ycf"][b * seq:(b + 1) * seq, :] = y.astype(BF16)
    else:
        r["cf"][...] = cf
    if "rope_c" in r:
        c, su, sd = r["rope_c"][...], r["rope_up"][...], r["rope_dn"][...]
        q = _rope_lanes(q, c, su, sd)
        k = _rope_lanes(k, c, su, sd)
    r["q"][...] = (q * Q_SCALE).astype(BF16)
    r["k"][...] = k.astype(BF16)
    r["v"][...] = v.astype(BF16)
    if "kc" in r:
        kc_ref, vc_ref = r["kc"], r["vc"]
        for b in range(kc_ref.shape[0]):
            rows = slice(b * seq, (b + 1) * seq)
            for slot in range(kc_ref.shape[1]):
                if slot == cache_slot:
                    kc_ref[b, slot] = k[rows].reshape(seq, 2 * DIFF_HEADS, DIFF_HEAD_DIM)
                    vc_ref[b, slot] = v[rows].reshape(seq, DIFF_HEADS, 2 * DIFF_HEAD_DIM)
                else:
                    kc_ref[b, slot] = jnp.zeros(kc_ref.shape[2:], F32)
                    vc_ref[b, slot] = jnp.zeros(vc_ref.shape[2:], F32)


def _inproj_call(x2d, ada4, wts, layer, batch, seq, cond_row_of_tile, rope_tabs, caches, cf_tabs):
    T = x2d.shape[0]
    rows = lambda w: pl.BlockSpec((ROW_TILE, w), lambda i: (i, 0))
    ins = [
        ("x", x2d, rows(D_MODEL)),
        ("ada", ada4, pl.BlockSpec((1, 1, N_ADA, D_MODEL), lambda i: (layer, cond_row_of_tile(i), 0, 0))),
        ("w", wts["w_in"], _layer_spec((D_MODEL, IN_WIDTH), layer)),
    ]
    outs = [("hy", rows(3 * HY_WIDTH), jax.ShapeDtypeStruct((T, 3 * HY_WIDTH), F32))]
    if rope_tabs is not None:
        tiles_per_seq = seq // ROW_TILE
        for name, tab in zip(("rope_c", "rope_up", "rope_dn"), rope_tabs):
            ins.append((name, tab, pl.BlockSpec((ROW_TILE, LANES), lambda i: (i % tiles_per_seq, 0))))
    if cf_tabs is not None:
        fwd, inv = _dft_tables(seq)
        ins.append(("cf_fwd", jnp.asarray(fwd).astype(BF16), _const_spec((2 * seq, seq))))
        ins.append(("cf_inv", jnp.asarray(inv).astype(BF16), _const_spec((seq, 2 * seq))))
        for name, tab in zip(("cf_ga", "cf_gbc", "cf_gd"), cf_tabs):
            ins.append((name, tab, _layer_spec((seq, CF_WIDTH), layer)))
        for name, key in (("cf_cb", "cf_conv_b"), ("cf_g", "cf_ln_g"), ("cf_b", "cf_ln_b")):
            ins.append((name, wts[key], _layer_spec((1, CF_WIDTH), layer)))
        outs.append(("ycf", rows(CF_WIDTH), jax.ShapeDtypeStruct((T, CF_WIDTH), BF16)))
    else:
        outs.append(("cf", rows(2 * CF_WIDTH), jax.ShapeDtypeStruct((T, 2 * CF_WIDTH), F32)))
    for name in ("q", "k", "v"):
        outs.append((name, rows(ATT_WIDTH), jax.ShapeDtypeStruct((T, ATT_WIDTH), BF16)))
    aliases = {}
    cache_slot = 0
    if caches is not None:
        seqs = ROW_TILE // seq
        first_call = len(caches) == 0
        cache_slot = layer if first_call else 0
        shapes = (("kc", (batch, DEPTH, seq, 2 * DIFF_HEADS, DIFF_HEAD_DIM)),
                  ("vc", (batch, DEPTH, seq, DIFF_HEADS, 2 * DIFF_HEAD_DIM)))
        for j, (name, shape) in enumerate(shapes):
            if first_call:
                spec = pl.BlockSpec((seqs, DEPTH) + shape[2:], lambda i: (i, 0, 0, 0, 0))
            else:
                spec = pl.BlockSpec((seqs, 1) + shape[2:], lambda i: (i, layer, 0, 0, 0))
                aliases[len(ins)] = len(outs)
                ins.append((name + "_in", caches[j], pl.BlockSpec(memory_space=pl.ANY)))
            outs.append((name, spec, jax.ShapeDtypeStruct(shape, F32)))
    names = tuple(n for n, _, _ in ins) + tuple(n for n, _, _ in outs)
    results = pl.pallas_call(
        functools.partial(_inproj_kernel, names=names, seq=seq, cache_slot=cache_slot),
        grid=(T // ROW_TILE,),
        in_specs=[spec for _, _, spec in ins],
        out_specs=[spec for _, spec, _ in outs],
        out_shape=[shape for _, _, shape in outs],
        input_output_aliases=aliases,
        compiler_params=_params(dimension_semantics=("arbitrary",)),
        name="inproj",
    )(*[a for _, a, _ in ins])
    return dict(zip((n for n, _, _ in outs), results))


def _store_spectrum_tables(spec, layer, ga_ref, gbc_ref, gd_ref):
    L = spec.shape[0] // 2
    W = spec.shape[1] // 2
    row = lax.broadcasted_iota(jnp.int32, (L, W), 0)
    p = spec[:L]
    q = spec[L:]
    g_re = p[:, :W] + p[:, W:]
    g_im = q[:, :W] - q[:, W:]
    nyq = q[0:1, :W] + q[0:1, W:]
    ga_ref[layer] = g_re
    gbc_ref[layer] = jnp.where(row == 0, 0.0, g_im)
    gd_ref[layer] = jnp.where(row == 0, nyq, g_re)


def _filter_kernel(feats_ref, w1_ref, b1_ref, fq_ref, w2_ref, b2_ref, w3_ref, decay_ref,
                   fwd_hi_ref, fwd_lo_ref, ga_ref, gbc_ref, gd_ref):
    L = feats_ref.shape[0]
    decay = decay_ref[...]
    row = lax.broadcasted_iota(jnp.int32, (L, HY_WIDTH), 0)
    taps = []
    for l in range(DEPTH):
        fq = fq_ref[l]
        hid = jnp.sin(fq * (_dot_f32(feats_ref[...], w1_ref[l]) + b1_ref[l]))
        hid = jnp.sin(fq * (_dot_f32(hid, w2_ref[l]) + b2_ref[l]))
        hf = _dot_f32(hid, w3_ref[l])
        taps.append(hf[:, :HY_WIDTH] * decay)
        taps.append(jnp.where(row == 0, 0.0, hf[:, HY_WIDTH:] * decay))
    h_hi, h_lo = _split_bf16(jnp.concatenate(taps, axis=1))
    fwd_hi = fwd_hi_ref[...]
    spec = _dot(fwd_hi, h_hi) + (_dot(fwd_hi, h_lo) + _dot(fwd_lo_ref[...], h_hi))
    for l in range(DEPTH):
        _store_spectrum_tables(spec[:, 2 * l * HY_WIDTH:2 * (l + 1) * HY_WIDTH], l, ga_ref, gbc_ref, gd_ref)


def _pad_to(a, shape):
    return jnp.pad(a, [(0, s - d) for s, d in zip(shape, a.shape)])


def _filter_call(L, w1, b1, fq, w2, b2, w3):
    feats, decay = _filter_tables(L)
    fwd, _ = _dft_tables(L)
    row_vec = lambda a: _pad_to(a.reshape(DEPTH, 1, -1), (DEPTH, 1, LANES))
    args = [
        jnp.asarray(feats),
        _pad_to(w1, (DEPTH, LANES, LANES)),
        row_vec(b1),
        row_vec(fq),
        _pad_to(w2, (DEPTH, LANES, LANES)),
        row_vec(b2),
        _pad_to(w3, (DEPTH, LANES, 2 * HY_WIDTH)),
        jnp.asarray(decay),
        *_split_bf16(jnp.asarray(fwd)),
    ]
    out = jax.ShapeDtypeStruct((DEPTH, L, HY_WIDTH), F32)
    return pl.pallas_call(
        _filter_kernel,
        out_shape=[out, out, out],
        compiler_params=_params(),
        name="hyena_filter",
    )(*args)


def _conv_filter_kernel(taps_ref, fwd_ref, ga_ref, gbc_ref, gd_ref):
    spec = _dot_f32(fwd_ref[...], taps_ref[...])
    for l in range(DEPTH):
        _store_spectrum_tables(spec[:, 2 * l * CF_WIDTH:2 * (l + 1) * CF_WIDTH], l, ga_ref, gbc_ref, gd_ref)


def _conv_filter_call(L, conv_w):
    half = CF_CONV_K // 2
    fwd, _ = _dft_tables(L)
    lag_pos = conv_w[:, half::-1, :]
    lag_neg = _pad_to(conv_w[:, half + 1:, :], (DEPTH, half + 1, CF_WIDTH))
    lag_neg = jnp.roll(lag_neg, 1, axis=1)
    taps = jnp.concatenate([lag_pos, lag_neg], axis=2)
    taps = _pad_to(taps, (DEPTH, LANES, 2 * CF_WIDTH))
    taps = jnp.transpose(taps, (1, 0, 2)).reshape(LANES, DEPTH * 2 * CF_WIDTH)
    out = jax.ShapeDtypeStruct((DEPTH, L, CF_WIDTH), F32)
    return pl.pallas_call(
        _conv_filter_kernel,
        out_shape=[out, out, out],
        compiler_params=_params(),
        name="conv_filter",
    )(taps, jnp.asarray(fwd[:, :LANES]))


def _dft_conv(us, fwd_ref, inv_ref, ga_ref, gbc_ref, gd_ref):
    L, W = us[0].shape
    u_all = jnp.concatenate(us, axis=1).astype(BF16)
    spec = _dot(fwd_ref[...], u_all)
    ga, gbc, gd = ga_ref[...], gbc_ref[...], gd_ref[...]
    y_re, y_im = [], []
    for s in range(len(us)):
        p = spec[:L, s * W:(s + 1) * W]
        q = spec[L:, s * W:(s + 1) * W]
        y_re.append(p * ga - q * gbc)
        y_im.append(p * gbc + q * gd)
    y_spec = jnp.concatenate(
        [jnp.concatenate(y_re, axis=1), jnp.concatenate(y_im, axis=1)], axis=0).astype(BF16)
    return _dot(inv_ref[...], y_spec)


def _hyena_kernel(hy_ref, cw_ref, cb_ref, fwd_ref, inv_ref, ga_ref, gbc_ref, gd_ref, hb_ref,
                  o_ref, pad_ref):
    nb, L, width = hy_ref.shape
    zeros = jnp.zeros((SUBLANES, width), F32)
    x0s, us = [], []
    for s in range(nb):
        pad_ref[s, 0:SUBLANES, :] = zeros
        pad_ref[s, SUBLANES + L:2 * SUBLANES + L, :] = zeros
        pad_ref[s, SUBLANES:SUBLANES + L, :] = hy_ref[s]
        conv = cb_ref[...]
        for j in range(HY_SHORT_K):
            start = SUBLANES + j - HY_SHORT_K // 2
            conv = conv + cw_ref[j:j + 1, :] * pad_ref[s, start:start + L, :]
        x0s.append(conv[:, :HY_WIDTH])
        us.append(conv[:, HY_WIDTH:2 * HY_WIDTH] * conv[:, 2 * HY_WIDTH:])
    y = _dft_conv(us, fwd_ref, inv_ref, ga_ref, gbc_ref, gd_ref)
    for s in range(nb):
        o = x0s[s] * (y[:, s * HY_WIDTH:(s + 1) * HY_WIDTH] + us[s] * hb_ref[...])
        o_ref[s] = o.astype(o_ref.dtype)


def _hyena_call(hy3, conv_w, conv_b, g_tabs, hy_bias, layer):
    B, L, width = hy3.shape
    nb = _seqs_per_step(B, L) if L < MIXER_ROWS else min(B, HYENA_LONG_SEQS)
    fwd, inv = _dft_tables(L)
    fwd_bf = jnp.asarray(fwd).astype(BF16)
    inv_bf = jnp.asarray(inv).astype(BF16)
    tab = _layer_spec((L, HY_WIDTH), layer)
    return pl.pallas_call(
        _hyena_kernel,
        grid=(B // nb,),
        in_specs=[
            pl.BlockSpec((nb, L, width), lambda b: (b, 0, 0)),
            _layer_spec((HY_SHORT_K, width), layer),
            _layer_spec((1, width), layer),
            _const_spec((2 * L, L)),
            _const_spec((L, 2 * L)),
            tab, tab, tab,
            _layer_spec((1, HY_WIDTH), layer),
        ],
        out_specs=pl.BlockSpec((nb, L, HY_WIDTH), lambda b: (b, 0, 0)),
        out_shape=jax.ShapeDtypeStruct((B, L, HY_WIDTH), BF16),
        scratch_shapes=[pltpu.VMEM((nb, L + 2 * SUBLANES, width), F32)],
        compiler_params=_params(dimension_semantics=("arbitrary",)),
        name="hyena",
    )(hy3, conv_w, conv_b, fwd_bf, inv_bf, *g_tabs, hy_bias)


CF_PAD = 2 * SUBLANES


def _conformer_kernel(cf_ref, cw_ref, cb_ref, g_ref, b_ref, o_ref, pad_ref):
    nb, L, _ = cf_ref.shape
    padded = L + 2 * CF_PAD
    shifted = padded - SUBLANES
    zeros = jnp.zeros((CF_PAD, CF_WIDTH), F32)
    half = CF_CONV_K // 2
    for s in range(nb):
        cf = cf_ref[s]
        pad_ref[s, 0, 0:CF_PAD, :] = zeros
        pad_ref[s, 0, CF_PAD + L:padded, :] = zeros
        pad_ref[s, 0, CF_PAD:CF_PAD + L, :] = cf[:, :CF_WIDTH] * jax.nn.sigmoid(cf[:, CF_WIDTH:])
        for r in range(1, SUBLANES):
            pad_ref[s, r, 0:shifted, :] = pad_ref[s, 0, r:r + shifted, :]
        for c0 in range(0, L, CF_ROW_CHUNK):
            acc = jnp.zeros((CF_ROW_CHUNK, CF_WIDTH), F32) + cb_ref[...]
            for j in range(CF_CONV_K):
                off = CF_PAD + j - half
                start = c0 + (off // SUBLANES) * SUBLANES
                acc = acc + cw_ref[j:j + 1, :] * pad_ref[s, off % SUBLANES, start:start + CF_ROW_CHUNK, :]
            y = _layer_norm_rows(acc, g_ref[...], b_ref[...])
            o_ref[s, c0:c0 + CF_ROW_CHUNK, :] = (y * jax.nn.sigmoid(y)).astype(o_ref.dtype)


def _conformer_call(cf3, conv_w, conv_b, ln_g, ln_b, layer):
    B, L, width = cf3.shape
    nb = _seqs_per_step(B, L)
    row = _layer_spec((1, CF_WIDTH), layer)
    return pl.pallas_call(
        _conformer_kernel,
        grid=(B // nb,),
        in_specs=[
            pl.BlockSpec((nb, L, width), lambda b: (b, 0, 0)),
            _layer_spec((CF_CONV_K, CF_WIDTH), layer),
            row, row, row,
        ],
        out_specs=pl.BlockSpec((nb, L, CF_WIDTH), lambda b: (b, 0, 0)),
        out_shape=jax.ShapeDtypeStruct((B, L, CF_WIDTH), BF16),
        scratch_shapes=[pltpu.VMEM((nb, SUBLANES, L + 2 * CF_PAD, CF_WIDTH), F32)],
        compiler_params=_params(dimension_semantics=("arbitrary",)),
        name="conformer",
    )(cf3, conv_w, conv_b, ln_g, ln_b)


def _attn_kernel(*refs, lambda_init, past):
    if past:
        (q_ref, k_ref, v_ref, ck_ref, cv_ref, lq1, lk1, lq2, lk2, g_ref, o_ref, k_all, v_aug) = refs
    else:
        (q_ref, k_ref, v_ref, lq1, lk1, lq2, lk2, g_ref, o_ref) = refs
    nb, tq, _ = q_ref.shape

    if past:
        keys = v_aug.shape[1]

        @pl.when(pl.program_id(1) == 0)
        def _():
            ones = jnp.ones((keys, HEAD_PAIR), BF16)
            for s in range(nb):
                k_all[s, 0:past, :] = ck_ref[s, 0].reshape(past, ATT_WIDTH).astype(BF16)
                k_all[s, past:keys, :] = k_ref[s]
                v_old = cv_ref[s, 0].reshape(past, ATT_WIDTH).astype(BF16)
                for h in range(DIFF_HEADS):
                    cols = slice(h * HEAD_PAIR, (h + 1) * HEAD_PAIR)
                    base = 2 * h * HEAD_PAIR
                    v_aug[s, 0:past, base:base + HEAD_PAIR] = v_old[:, cols]
                    v_aug[s, past:keys, base:base + HEAD_PAIR] = v_ref[s, :, cols]
                    v_aug[s, :, base + HEAD_PAIR:base + 2 * HEAD_PAIR] = ones

    lam = (jnp.exp(jnp.sum(lq1[...] * lk1[...], axis=-1, keepdims=True))
           - jnp.exp(jnp.sum(lq2[...] * lk2[...], axis=-1, keepdims=True)) + lambda_init)
    lane = lax.broadcasted_iota(jnp.int32, (tq, HEAD_PAIR), 1)
    first = lane < DIFF_HEAD_DIM
    nt = (((1,), (1,)), ((), ()))
    gain = g_ref[...] * (1.0 - lambda_init)
    zero = jnp.zeros((), BF16)

    def scores(s, h):
        cols = slice(h * HEAD_PAIR, (h + 1) * HEAD_PAIR)
        q = q_ref[s, :, cols]
        q2 = jnp.concatenate([jnp.where(first, q, zero), jnp.where(first, zero, q)], axis=0)
        kh = k_all[s, :, cols] if past else k_ref[s, :, cols]
        return lax.dot_general(q2, kh, nt, preferred_element_type=F32)

    def finish(s, h, sc):
        cols = slice(h * HEAD_PAIR, (h + 1) * HEAD_PAIR)
        e = jnp.exp2(sc - jnp.max(sc, axis=-1, keepdims=True))
        if past:
            p = _dot(e.astype(BF16), v_aug[s, :, 2 * h * HEAD_PAIR:2 * (h + 1) * HEAD_PAIR])
            p0, p1 = p[:tq], p[tq:]
            o = (p0[:, :HEAD_PAIR] * (1.0 / p0[:, HEAD_PAIR:HEAD_PAIR + 1])
                 - p1[:, :HEAD_PAIR] * (lam / p1[:, HEAD_PAIR:HEAD_PAIR + 1]))
        else:
            r = 1.0 / jnp.sum(e, axis=-1, keepdims=True)
            a = e[:tq] * r[:tq] - e[tq:] * (lam * r[tq:])
            o = _dot(a.astype(BF16), v_ref[s, :, cols])
        o = o * lax.rsqrt(jnp.mean(o * o, axis=-1, keepdims=True) + LN_EPS) * gain
        o_ref[s, :, cols] = o.astype(o_ref.dtype)

    lookahead = 0 if past else 1
    pending = []
    for s in range(nb):
        for h in range(DIFF_HEADS):
            pending.append((s, h, scores(s, h)))
            if len(pending) > lookahead:
                finish(*pending.pop(0))
    for unit in pending:
        finish(*unit)


def _attn_call(q3, k3, v3, cache_k, cache_v, layer, lam_vecs, subln_g):
    B, L, _ = q3.shape
    past = 0 if cache_k is None else cache_k.shape[2]
    nb = _seqs_per_step(B, 2 * L)
    tq = min(Q_TILE, L)
    tile = pl.BlockSpec((nb, tq, ATT_WIDTH), lambda b, i: (b, i, 0))
    seq = pl.BlockSpec((nb, L, ATT_WIDTH), lambda b, i: (b, 0, 0))
    in_specs = [tile, seq, seq]
    args = [q3, k3, v3]
    scratch = []
    if past:
        for cache in (cache_k, cache_v):
            in_specs.append(pl.BlockSpec((nb, 1) + cache.shape[2:], lambda b, i: (b, layer, 0, 0, 0)))
            args.append(cache)
        scratch = [pltpu.VMEM((nb, past + L, ATT_WIDTH), BF16),
                   pltpu.VMEM((nb, past + L, 2 * ATT_WIDTH), BF16)]
    in_specs += [_layer_spec((1, DIFF_HEAD_DIM), layer)] * 4 + [_layer_spec((1, HEAD_PAIR), layer)]
    args += list(lam_vecs) + [subln_g]
    return pl.pallas_call(
        functools.partial(_attn_kernel, lambda_init=_lambda_init(layer), past=past),
        grid=(B // nb, L // tq),
        in_specs=in_specs,
        out_specs=tile,
        out_shape=jax.ShapeDtypeStruct((B, L, ATT_WIDTH), BF16),
        scratch_shapes=scratch,
        compiler_params=_params(dimension_semantics=("arbitrary", "arbitrary")),
        name="diff_attn",
    )(*args)


def _mlp_kernel(x_ref, hy_ref, cf_ref, at_ref, ada_ref, wo_ref, w1_ref, w2_ref,
                g1_ref, b1_ref, g2_ref, b2_ref, o_ref):
    gate1 = ada_ref[0, 0, 2:3, :]
    sh2 = ada_ref[0, 0, 3:4, :]
    sc2 = ada_ref[0, 0, 4:5, :]
    gate2 = ada_ref[0, 0, 5:6, :]
    mix = jnp.concatenate([hy_ref[...], cf_ref[...], at_ref[...]], axis=1)
    y = _dot(mix, wo_ref[...])
    x = _layer_norm_rows(DEEPNORM_ALPHA * x_ref[...] + gate1 * y, g1_ref[...], b1_ref[...])
    h = (x * (1.0 + sc2) + sh2).astype(BF16)
    a = jnp.maximum(_dot(h, w1_ref[...]), 0.0)
    m = _dot((a * a).astype(BF16), w2_ref[...])
    o_ref[...] = _layer_norm_rows(DEEPNORM_ALPHA * x + gate2 * m, g2_ref[...], b2_ref[...])


def _mlp_call(x2d, y_hy, y_cf, y_at, ada4, wo_bf, w1_bf, w2_bf, ln1_g, ln1_b, ln2_g, ln2_b,
              layer, cond_row_of_tile):
    T = x2d.shape[0]
    rows = lambda w: pl.BlockSpec((ROW_TILE, w), lambda i: (i, 0))
    vec = _layer_spec((1, D_MODEL), layer)
    return pl.pallas_call(
        _mlp_kernel,
        grid=(T // ROW_TILE,),
        in_specs=[
            rows(D_MODEL), rows(HY_WIDTH), rows(CF_WIDTH), rows(ATT_WIDTH),
            pl.BlockSpec((1, 1, N_ADA, D_MODEL), lambda i: (layer, cond_row_of_tile(i), 0, 0)),
            _layer_spec((MIX_WIDTH, D_MODEL), layer),
            _layer_spec((D_MODEL, D_FF), layer),
            _layer_spec((D_FF, D_MODEL), layer),
            vec, vec, vec, vec,
        ],
        out_specs=rows(D_MODEL),
        out_shape=jax.ShapeDtypeStruct((T, D_MODEL), F32),
        compiler_params=_params(dimension_semantics=("arbitrary",)),
        name="outproj_mlp",
    )(x2d, y_hy, y_cf, y_at, ada4, wo_bf, w1_bf, w2_bf, ln1_g, ln1_b, ln2_g, ln2_b)


def _layer(x2d, batch, seq, layer, ada4, wts, hy_tabs, cf_tabs, cond_row_of_tile, rope_tabs, cache_kv,
           new_caches):
    z = _inproj_call(x2d, ada4, wts, layer, batch, seq, cond_row_of_tile, rope_tabs, new_caches, cf_tabs)
    as_seq = lambda a: a.reshape(batch, seq, a.shape[-1])
    flat = lambda a: a.reshape(batch * seq, a.shape[-1])
    y_hy = _hyena_call(as_seq(z["hy"]), wts["hy_conv_w"], wts["hy_conv_b"], hy_tabs, wts["hy_bias"], layer)
    if "ycf" in z:
        y_cf = z["ycf"]
    else:
        y_cf = flat(_conformer_call(as_seq(z["cf"]), wts["cf_conv_w"], wts["cf_conv_b"], wts["cf_ln_g"],
                                    wts["cf_ln_b"], layer))
    ck, cv = cache_kv if cache_kv is not None else (None, None)
    lam_vecs = (wts["lam_q1"], wts["lam_k1"], wts["lam_q2"], wts["lam_k2"])
    y_at = _attn_call(as_seq(z["q"]), as_seq(z["k"]), as_seq(z["v"]), ck, cv, layer, lam_vecs,
                      wts["subln_g"])
    x_out = _mlp_call(x2d, flat(y_hy), y_cf, flat(y_at), ada4,
                      wts["w_out"], wts["w_mlp1"], wts["w_mlp2"],
                      wts["ln1_g"], wts["ln1_b"], wts["ln2_g"], wts["ln2_b"], layer, cond_row_of_tile)
    return x_out, (z["kc"], z["vc"]) if "kc" in z else None


def kernel(x_prompt, x_sample, cache_k, cache_v, c, c_ctx, w_ada, b_ada, w_in, hy_conv_w, hy_conv_b, hf_w1, hf_b1, hf_freq, hf_w2, hf_b2, hf_w3, hy_bias, cf_conv_w, cf_conv_b, cf_ln_g, cf_ln_b, lam_q1, lam_k1, lam_q2, lam_k2, subln_g, w_out, ln1_g, ln1_b, w_mlp1, w_mlp2, ln2_g, ln2_b):
    batch, seq, _ = x_prompt.shape
    dec_batch, dec_seq, _ = x_sample.shape
    assert 1 + dec_batch <= COND_ROWS
    assert ROW_TILE % seq == 0 and (batch * seq) % ROW_TILE == 0
    assert dec_seq % ROW_TILE == 0 and dec_seq % Q_TILE == 0

    cond = jnp.concatenate(
        [c_ctx[None, :], c, jnp.zeros((COND_ROWS - 1 - dec_batch, D_MODEL), F32)], axis=0)
    ada4 = _ada_call(cond, w_ada, b_ada).reshape(DEPTH, COND_ROWS, N_ADA, D_MODEL)

    row_vec = lambda a: a.reshape(DEPTH, 1, a.shape[-1])
    wts = dict(
        w_in=w_in.astype(BF16), w_out=w_out.astype(BF16),
        w_mlp1=w_mlp1.astype(BF16), w_mlp2=w_mlp2.astype(BF16),
        hy_conv_w=hy_conv_w, hy_conv_b=row_vec(hy_conv_b), hy_bias=row_vec(hy_bias),
        cf_conv_w=cf_conv_w, cf_conv_b=row_vec(cf_conv_b),
        cf_ln_g=row_vec(cf_ln_g), cf_ln_b=row_vec(cf_ln_b),
        lam_q1=row_vec(lam_q1), lam_k1=row_vec(lam_k1), lam_q2=row_vec(lam_q2), lam_k2=row_vec(lam_k2),
        subln_g=row_vec(subln_g),
        ln1_g=row_vec(ln1_g), ln1_b=row_vec(ln1_b), ln2_g=row_vec(ln2_g), ln2_b=row_vec(ln2_b))

    rope_tabs = tuple(jnp.asarray(t) for t in _rope_tables(dec_seq))
    dec_tiles_per_seq = dec_seq // ROW_TILE

    xp = x_prompt.reshape(batch * seq, D_MODEL)
    xs = x_sample.reshape(dec_batch * dec_seq, D_MODEL)
    new_caches = ()
    filt = (hf_w1, hf_b1, hf_freq, hf_w2, hf_b2, hf_w3)
    hy_ctx = _filter_call(seq, *filt)
    hy_dec = _filter_call(dec_seq, *filt)
    fused_cf = lambda n: n <= CF_DFT_MAX_SEQ and ROW_TILE % n == 0
    cf_ctx = _conv_filter_call(seq, cf_conv_w) if fused_cf(seq) else None
    cf_dec = _conv_filter_call(dec_seq, cf_conv_w) if fused_cf(dec_seq) else None
    for l in range(DEPTH):
        xp, new_caches = _layer(xp, batch, seq, l, ada4, wts, hy_ctx, cf_ctx, lambda i: 0, None, None,
                                new_caches)
        xs, _ = _layer(xs, dec_batch, dec_seq, l, ada4, wts, hy_dec, cf_dec,
                       lambda i: 1 + i // dec_tiles_per_seq, rope_tabs, (cache_k, cache_v), None)
    new_cache_k, new_cache_v = new_caches
    return (xp.reshape(batch, seq, D_MODEL), xs.reshape(dec_batch, dec_seq, D_MODEL),
            new_cache_k, new_cache_v)
```

```python
import functools
import math

import numpy as np
import jax
import jax.numpy as jnp
from jax import lax
from jax.experimental import pallas as pl
from jax.experimental.pallas import tpu as pltpu

D_MODEL = 1024
DEPTH = 2
GRID_W = 64
HY_WIDTH = D_MODEL // 4
CF_WIDTH = D_MODEL // 4
ATT_WIDTH = D_MODEL // 2
DIFF_HEAD_DIM = 64
DIFF_HEADS = ATT_WIDTH // (2 * DIFF_HEAD_DIM)
HEAD_PAIR = 2 * DIFF_HEAD_DIM
MIX_WIDTH = HY_WIDTH + CF_WIDTH + ATT_WIDTH
IN_WIDTH = 3 * HY_WIDTH + 2 * CF_WIDTH + 3 * ATT_WIDTH
HY_SHORT_K = 3
HY_FILTER_EMB = 33
HY_FILTER_HIDDEN = 64
HY_FAST_DECAY_PCT = 0.3
HY_SLOW_DECAY_PCT = 1.5
HY_DECAY_TARGET = 1e-2
CF_CONV_K = 31
D_FF = 4 * D_MODEL
ROPE_BASE = 10000.0
AX_DIM = DIFF_HEAD_DIM // 2
DEEPNORM_ALPHA = (2 * DEPTH) ** 0.25
LN_EPS = 1e-5
N_ADA = 6

LANES = 128
SUBLANES = 8
COND_ROWS = SUBLANES
VMEM_LIMIT = 56 * 1024 * 1024

ROW_TILE = 512
ADA_COL_TILE = 1536
Q_TILE = 512
MIXER_ROWS = 1024
HYENA_LONG_SEQS = 2
FUSED_MIXER_MAX_SEQ = 256
CF_ROW_CHUNK = 128

F32 = jnp.float32
BF16 = jnp.bfloat16


def _lambda_init(l):
    return 0.8 - 0.6 * math.exp(-0.3 * l)


def _params(**kw):
    return pltpu.CompilerParams(vmem_limit_bytes=VMEM_LIMIT, **kw)


def _const_spec(shape):
    zeros = (0,) * len(shape)
    return pl.BlockSpec(shape, lambda *_: zeros, pipeline_mode=pl.Buffered(1))


def _layer_spec(shape, layer):
    zeros = (0,) * len(shape)
    return pl.BlockSpec((None,) + tuple(shape), lambda *_: (layer,) + zeros,
                        pipeline_mode=pl.Buffered(1))


def _dot(a, b):
    return jnp.dot(a, b, preferred_element_type=F32)


def _dot_f32(a, b):
    return jnp.dot(a, b, preferred_element_type=F32, precision=lax.Precision.HIGHEST)


def _split_bf16(x):
    hi = x.astype(BF16)
    return hi, (x - hi.astype(F32)).astype(BF16)


def _layer_norm_rows(x, g, b):
    mu = jnp.mean(x, axis=-1, keepdims=True)
    xc = x - mu
    var = jnp.mean(xc * xc, axis=-1, keepdims=True)
    return xc * lax.rsqrt(var + LN_EPS) * g + b


def _seqs_per_step(batch, seq):
    return max(1, min(batch, MIXER_ROWS // seq))


@functools.lru_cache(maxsize=None)
def _dft_tables(L):
    n = 2 * L
    k = np.arange(L, dtype=np.float64)[:, None]
    s = np.arange(L, dtype=np.float64)[None, :]
    ang = 2.0 * np.pi * k * s / n
    fwd = np.concatenate([np.cos(ang), -np.sin(ang)], axis=0)
    fwd[L, :] = np.cos(np.pi * np.arange(L))
    t = np.arange(L, dtype=np.float64)[:, None]
    kk = np.arange(L, dtype=np.float64)[None, :]
    ang_i = 2.0 * np.pi * t * kk / n
    inv_re = (2.0 / n) * np.cos(ang_i)
    inv_re[:, 0] = 1.0 / n
    inv_im = -(2.0 / n) * np.sin(ang_i)
    inv_im[:, 0] = np.cos(np.pi * np.arange(L)) / n
    inv = np.concatenate([inv_re, inv_im], axis=1)
    return fwd.astype(np.float32), inv.astype(np.float32)


@functools.lru_cache(maxsize=None)
def _filter_tables(L):
    bands = (HY_FILTER_EMB - 1) // 2
    t = np.linspace(0.0, 1.0, L)[:, None]
    w = (2.0 * np.pi / L) * np.arange(L, dtype=np.float64)[:, None]
    fr = np.linspace(1e-4, bands - 1, bands)[None, :]
    feats = np.concatenate([t, np.cos(fr * w), -np.sin(fr * w)], -1)
    feats_p = np.zeros((L, LANES), np.float64)
    feats_p[:, :HY_FILTER_EMB] = feats
    deltas = np.abs(np.linspace(math.log(HY_DECAY_TARGET) / HY_FAST_DECAY_PCT,
                                math.log(HY_DECAY_TARGET) / HY_SLOW_DECAY_PCT, HY_WIDTH))
    decay = np.exp(-t * deltas[None, :])
    return feats_p.astype(np.float32), decay.astype(np.float32)


@functools.lru_cache(maxsize=None)
def _rope_tables(n):
    half = AX_DIM // 2
    pos = np.arange(n)
    row = (pos // GRID_W).astype(np.float64)[:, None]
    col = (pos % GRID_W).astype(np.float64)[:, None]
    inv = ROPE_BASE ** (-np.arange(0, AX_DIM, 2, dtype=np.float64) / AX_DIM)[None, :]
    zero = np.zeros((n, half))
    cr, sr = np.cos(row * inv), np.sin(row * inv)
    cc, sc = np.cos(col * inv), np.sin(col * inv)
    c = np.concatenate([cr, cr, cc, cc], -1)
    s_up = np.concatenate([-sr, zero, -sc, zero], -1)
    s_dn = np.concatenate([zero, sr, zero, sc], -1)
    reps = LANES // DIFF_HEAD_DIM
    tile = lambda a: np.tile(a, (1, reps)).astype(np.float32)
    return tile(c), tile(s_up), tile(s_dn)


def _ada_kernel(cond_ref, w_ref, b_ref, o_ref):
    c = cond_ref[...]
    s = c * jax.nn.sigmoid(c)
    o_ref[0] = _dot(s.astype(BF16), w_ref[0].astype(BF16)) + b_ref[0]


def _ada_call(cond, w_ada, b_ada):
    n_out = N_ADA * D_MODEL
    return pl.pallas_call(
        _ada_kernel,
        grid=(DEPTH, n_out // ADA_COL_TILE),
        in_specs=[
            pl.BlockSpec((COND_ROWS, D_MODEL), lambda l, j: (0, 0)),
            pl.BlockSpec((1, D_MODEL, ADA_COL_TILE), lambda l, j: (l, 0, j)),
            pl.BlockSpec((1, 1, ADA_COL_TILE), lambda l, j: (l, 0, j)),
        ],
        out_specs=pl.BlockSpec((1, COND_ROWS, ADA_COL_TILE), lambda l, j: (l, 0, j)),
        out_shape=jax.ShapeDtypeStruct((DEPTH, COND_ROWS, n_out), F32),
        compiler_params=_params(dimension_semantics=("arbitrary", "arbitrary")),
        name="ada",
    )(cond, w_ada, b_ada.reshape(DEPTH, 1, n_out))


def _rope_lanes(x, c, s_up, s_dn):
    outs = []
    for p in range(ATT_WIDTH // LANES):
        xb = x[:, p * LANES:(p + 1) * LANES]
        up = pltpu.roll(xb, LANES - AX_DIM // 2, 1)
        dn = pltpu.roll(xb, AX_DIM // 2, 1)
        outs.append(xb * c + up * s_up + dn * s_dn)
    return jnp.concatenate(outs, axis=1)


Q_SCALE = DIFF_HEAD_DIM ** -0.5 * math.log2(math.e)


def _inproj_kernel(*refs, names, seq, cache_slot):
    r = dict(zip(names, refs))
    sh1 = r["ada"][0, 0, 0:1, :]
    sc1 = r["ada"][0, 0, 1:2, :]
    h = (r["x"][...] * (1.0 + sc1) + sh1).astype(BF16)
    w_ref = r["w"]
    o1 = 3 * HY_WIDTH
    o2 = o1 + 2 * CF_WIDTH
    o3 = o2 + ATT_WIDTH
    o4 = o3 + ATT_WIDTH
    hy = _dot(h, w_ref[:, 0:o1])
    cf = _dot(h, w_ref[:, o1:o2])
    q = _dot(h, w_ref[:, o2:o3])
    k = _dot(h, w_ref[:, o3:o4])
    v = _dot(h, w_ref[:, o4:IN_WIDTH])
    if "yhy" in r:
        seqs = hy.shape[0] // seq
        x0s, us, tabs = [], [], []
        for b in range(seqs):
            x0, u = _hyena_front(hy[b * seq:(b + 1) * seq], r["hy_pad"].at[b], r["hy_cw"], r["hy_cb"])
            x0s.append(x0)
            us.append(u)
            tabs.append((r["hy_ga"][...], r["hy_gbc"][...], r["hy_gd"][...]))
        for b in range(seqs):
            c = cf[b * seq:(b + 1) * seq]
            us.append(c[:, :CF_WIDTH] * jax.nn.sigmoid(c[:, CF_WIDTH:]))
            tabs.append((r["cf_ga"][...], r["cf_gbc"][...], r["cf_gd"][...]))
        ys = _dft_conv(us, tabs, r["fwd"], r["inv"])
        for b in range(seqs):
            rows = slice(b * seq, (b + 1) * seq)
            r["yhy"][rows, :] = (x0s[b] * (ys[b] + us[b] * r["hy_bias"][...])).astype(BF16)
            y = _layer_norm_rows(ys[seqs + b] + r["cf_cb"][...], r["cf_g"][...], r["cf_b"][...])
            r["ycf"][rows, :] = (y * jax.nn.sigmoid(y)).astype(BF16)
    else:
        r["hy"][...] = hy
        r["cf"][...] = cf
    if "rope_c" in r:
        c, su, sd = r["rope_c"][...], r["rope_up"][...], r["rope_dn"][...]
        q = _rope_lanes(q, c, su, sd)
        k = _rope_lanes(k, c, su, sd)
    r["q"][...] = (q * Q_SCALE).astype(BF16)
    r["k"][...] = k.astype(BF16)
    r["v"][...] = v.astype(BF16)
    if "kc" in r:
        kc_ref, vc_ref = r["kc"], r["vc"]
        for b in range(kc_ref.shape[0]):
            rows = slice(b * seq, (b + 1) * seq)
            for slot in range(kc_ref.shape[1]):
                if slot == cache_slot:
                    kc_ref[b, slot] = k[rows].reshape(seq, 2 * DIFF_HEADS, DIFF_HEAD_DIM)
                    vc_ref[b, slot] = v[rows].reshape(seq, DIFF_HEADS, 2 * DIFF_HEAD_DIM)
                else:
                    kc_ref[b, slot] = jnp.zeros(kc_ref.shape[2:], F32)
                    vc_ref[b, slot] = jnp.zeros(vc_ref.shape[2:], F32)


def _inproj_call(x2d, ada4, wts, layer, batch, seq, cond_row_of_tile, rope_tabs, caches, mix_tabs):
    T = x2d.shape[0]
    rows = lambda w: pl.BlockSpec((ROW_TILE, w), lambda i: (i, 0))
    ins = [
        ("x", x2d, rows(D_MODEL)),
        ("ada", ada4, pl.BlockSpec((1, 1, N_ADA, D_MODEL), lambda i: (layer, cond_row_of_tile(i), 0, 0))),
        ("w", wts["w_in"], _layer_spec((D_MODEL, IN_WIDTH), layer)),
    ]
    outs = []
    scratch = []
    if rope_tabs is not None:
        tiles_per_seq = seq // ROW_TILE
        for name, tab in zip(("rope_c", "rope_up", "rope_dn"), rope_tabs):
            ins.append((name, tab, pl.BlockSpec((ROW_TILE, LANES), lambda i: (i % tiles_per_seq, 0))))
    if mix_tabs is not None:
        hy_tabs, cf_tabs = mix_tabs
        fwd, inv = _dft_tables(seq)
        ins.append(("fwd", jnp.asarray(fwd).astype(BF16), _const_spec((2 * seq, seq))))
        ins.append(("inv", jnp.asarray(inv).astype(BF16), _const_spec((seq, 2 * seq))))
        for name, tab in zip(("hy_ga", "hy_gbc", "hy_gd"), hy_tabs):
            ins.append((name, tab, _layer_spec((seq, HY_WIDTH), layer)))
        for name, tab in zip(("cf_ga", "cf_gbc", "cf_gd"), cf_tabs):
            ins.append((name, tab, _layer_spec((seq, CF_WIDTH), layer)))
        ins.append(("hy_cw", wts["hy_conv_w"], _layer_spec((HY_SHORT_K, 3 * HY_WIDTH), layer)))
        ins.append(("hy_cb", wts["hy_conv_b"], _layer_spec((1, 3 * HY_WIDTH), layer)))
        ins.append(("hy_bias", wts["hy_bias"], _layer_spec((1, HY_WIDTH), layer)))
        for name, key in (("cf_cb", "cf_conv_b"), ("cf_g", "cf_ln_g"), ("cf_b", "cf_ln_b")):
            ins.append((name, wts[key], _layer_spec((1, CF_WIDTH), layer)))
        outs.append(("yhy", rows(HY_WIDTH), jax.ShapeDtypeStruct((T, HY_WIDTH), BF16)))
        outs.append(("ycf", rows(CF_WIDTH), jax.ShapeDtypeStruct((T, CF_WIDTH), BF16)))
        scratch.append(("hy_pad", pltpu.VMEM((ROW_TILE // seq, seq + 2 * SUBLANES, 3 * HY_WIDTH), F32)))
    else:
        outs.append(("hy", rows(3 * HY_WIDTH), jax.ShapeDtypeStruct((T, 3 * HY_WIDTH), F32)))
        outs.append(("cf", rows(2 * CF_WIDTH), jax.ShapeDtypeStruct((T, 2 * CF_WIDTH), F32)))
    for name in ("q", "k", "v"):
        outs.append((name, rows(ATT_WIDTH), jax.ShapeDtypeStruct((T, ATT_WIDTH), BF16)))
    aliases = {}
    cache_slot = 0
    if caches is not None:
        seqs = ROW_TILE // seq
        first_call = len(caches) == 0
        cache_slot = layer if first_call else 0
        shapes = (("kc", (batch, DEPTH, seq, 2 * DIFF_HEADS, DIFF_HEAD_DIM)),
                  ("vc", (batch, DEPTH, seq, DIFF_HEADS, 2 * DIFF_HEAD_DIM)))
        for j, (name, shape) in enumerate(shapes):
            if first_call:
                spec = pl.BlockSpec((seqs, DEPTH) + shape[2:], lambda i: (i, 0, 0, 0, 0))
            else:
                spec = pl.BlockSpec((seqs, 1) + shape[2:], lambda i: (i, layer, 0, 0, 0))
                aliases[len(ins)] = len(outs)
                ins.append((name + "_in", caches[j], pl.BlockSpec(memory_space=pl.ANY)))
            outs.append((name, spec, jax.ShapeDtypeStruct(shape, F32)))
    names = tuple(n for n, _, _ in ins) + tuple(n for n, _, _ in outs) + tuple(n for n, _ in scratch)
    results = pl.pallas_call(
        functools.partial(_inproj_kernel, names=names, seq=seq, cache_slot=cache_slot),
        grid=(T // ROW_TILE,),
        in_specs=[spec for _, _, spec in ins],
        out_specs=[spec for _, spec, _ in outs],
        out_shape=[shape for _, _, shape in outs],
        scratch_shapes=[s for _, s in scratch],
        input_output_aliases=aliases,
        compiler_params=_params(dimension_semantics=("arbitrary",)),
        name="inproj",
    )(*[a for _, a, _ in ins])
    return dict(zip((n for n, _, _ in outs), results))


def _store_spectrum_tables(spec, layer, ga_ref, gbc_ref, gd_ref):
    L = spec.shape[0] // 2
    W = spec.shape[1] // 2
    row = lax.broadcasted_iota(jnp.int32, (L, W), 0)
    p = spec[:L]
    q = spec[L:]
    g_re = p[:, :W] + p[:, W:]
    g_im = q[:, :W] - q[:, W:]
    nyq = q[0:1, :W] + q[0:1, W:]
    ga_ref[layer] = g_re
    gbc_ref[layer] = jnp.where(row == 0, 0.0, g_im)
    gd_ref[layer] = jnp.where(row == 0, nyq, g_re)


def _filter_kernel(feats_ref, w1_ref, b1_ref, fq_ref, w2_ref, b2_ref, w3_ref, decay_ref,
                   fwd_hi_ref, fwd_lo_ref, ga_ref, gbc_ref, gd_ref):
    L = feats_ref.shape[0]
    decay = decay_ref[...]
    row = lax.broadcasted_iota(jnp.int32, (L, HY_WIDTH), 0)
    taps = []
    for l in range(DEPTH):
        fq = fq_ref[l]
        hid = jnp.sin(fq * (_dot_f32(feats_ref[...], w1_ref[l]) + b1_ref[l]))
        hid = jnp.sin(fq * (_dot_f32(hid, w2_ref[l]) + b2_ref[l]))
        hf = _dot_f32(hid, w3_ref[l])
        taps.append(hf[:, :HY_WIDTH] * decay)
        taps.append(jnp.where(row == 0, 0.0, hf[:, HY_WIDTH:] * decay))
    h_hi, h_lo = _split_bf16(jnp.concatenate(taps, axis=1))
    fwd_hi = fwd_hi_ref[...]
    spec = _dot(fwd_hi, h_hi) + (_dot(fwd_hi, h_lo) + _dot(fwd_lo_ref[...], h_hi))
    for l in range(DEPTH):
        _store_spectrum_tables(spec[:, 2 * l * HY_WIDTH:2 * (l + 1) * HY_WIDTH], l, ga_ref, gbc_ref, gd_ref)


def _pad_to(a, shape):
    return jnp.pad(a, [(0, s - d) for s, d in zip(shape, a.shape)])


def _filter_call(L, w1, b1, fq, w2, b2, w3):
    feats, decay = _filter_tables(L)
    fwd, _ = _dft_tables(L)
    row_vec = lambda a: _pad_to(a.reshape(DEPTH, 1, -1), (DEPTH, 1, LANES))
    args = [
        jnp.asarray(feats),
        _pad_to(w1, (DEPTH, LANES, LANES)),
        row_vec(b1),
        row_vec(fq),
        _pad_to(w2, (DEPTH, LANES, LANES)),
        row_vec(b2),
        _pad_to(w3, (DEPTH, LANES, 2 * HY_WIDTH)),
        jnp.asarray(decay),
        *_split_bf16(jnp.asarray(fwd)),
    ]
    out = jax.ShapeDtypeStruct((DEPTH, L, HY_WIDTH), F32)
    return pl.pallas_call(
        _filter_kernel,
        out_shape=[out, out, out],
        compiler_params=_params(),
        name="hyena_filter",
    )(*args)


def _conv_filter_kernel(taps_ref, fwd_ref, ga_ref, gbc_ref, gd_ref):
    spec = _dot_f32(fwd_ref[...], taps_ref[...])
    for l in range(DEPTH):
        _store_spectrum_tables(spec[:, 2 * l * CF_WIDTH:2 * (l + 1) * CF_WIDTH], l, ga_ref, gbc_ref, gd_ref)


def _conv_filter_call(L, conv_w):
    half = CF_CONV_K // 2
    fwd, _ = _dft_tables(L)
    lag_pos = conv_w[:, half::-1, :]
    lag_neg = _pad_to(conv_w[:, half + 1:, :], (DEPTH, half + 1, CF_WIDTH))
    lag_neg = jnp.roll(lag_neg, 1, axis=1)
    taps = jnp.concatenate([lag_pos, lag_neg], axis=2)
    taps = _pad_to(taps, (DEPTH, LANES, 2 * CF_WIDTH))
    taps = jnp.transpose(taps, (1, 0, 2)).reshape(LANES, DEPTH * 2 * CF_WIDTH)
    out = jax.ShapeDtypeStruct((DEPTH, L, CF_WIDTH), F32)
    return pl.pallas_call(
        _conv_filter_kernel,
        out_shape=[out, out, out],
        compiler_params=_params(),
        name="conv_filter",
    )(taps, jnp.asarray(fwd[:, :LANES]))


def _dft_conv(us, tabs, fwd_ref, inv_ref):
    L, W = us[0].shape
    u_all = jnp.concatenate(us, axis=1).astype(BF16)
    spec = _dot(fwd_ref[...], u_all)
    y_re, y_im = [], []
    for s, (ga, gbc, gd) in enumerate(tabs):
        p = spec[:L, s * W:(s + 1) * W]
        q = spec[L:, s * W:(s + 1) * W]
        y_re.append(p * ga - q * gbc)
        y_im.append(p * gbc + q * gd)
    y_spec = jnp.concatenate(
        [jnp.concatenate(y_re, axis=1), jnp.concatenate(y_im, axis=1)], axis=0).astype(BF16)
    y = _dot(inv_ref[...], y_spec)
    return [y[:, s * W:(s + 1) * W] for s in range(len(us))]


def _hyena_front(hy, pad_ref, cw_ref, cb_ref):
    L, width = hy.shape
    zeros = jnp.zeros((SUBLANES, width), F32)
    pad_ref[0:SUBLANES, :] = zeros
    pad_ref[SUBLANES + L:2 * SUBLANES + L, :] = zeros
    pad_ref[SUBLANES:SUBLANES + L, :] = hy
    conv = cb_ref[...]
    for j in range(HY_SHORT_K):
        start = SUBLANES + j - HY_SHORT_K // 2
        conv = conv + cw_ref[j:j + 1, :] * pad_ref[start:start + L, :]
    return conv[:, :HY_WIDTH], conv[:, HY_WIDTH:2 * HY_WIDTH] * conv[:, 2 * HY_WIDTH:]


def _hyena_kernel(hy_ref, cw_ref, cb_ref, fwd_ref, inv_ref, ga_ref, gbc_ref, gd_ref, hb_ref,
                  o_ref, pad_ref):
    nb = hy_ref.shape[0]
    x0s, us = zip(*[_hyena_front(hy_ref[s], pad_ref.at[s], cw_ref, cb_ref) for s in range(nb)])
    tabs = [(ga_ref[...], gbc_ref[...], gd_ref[...])] * nb
    ys = _dft_conv(us, tabs, fwd_ref, inv_ref)
    for s in range(nb):
        o_ref[s] = (x0s[s] * (ys[s] + us[s] * hb_ref[...])).astype(o_ref.dtype)


def _hyena_call(hy3, conv_w, conv_b, g_tabs, hy_bias, layer):
    B, L, width = hy3.shape
    nb = _seqs_per_step(B, L) if L < MIXER_ROWS else min(B, HYENA_LONG_SEQS)
    fwd, inv = _dft_tables(L)
    fwd_bf = jnp.asarray(fwd).astype(BF16)
    inv_bf = jnp.asarray(inv).astype(BF16)
    tab = _layer_spec((L, HY_WIDTH), layer)
    return pl.pallas_call(
        _hyena_kernel,
        grid=(B // nb,),
        in_specs=[
            pl.BlockSpec((nb, L, width), lambda b: (b, 0, 0)),
            _layer_spec((HY_SHORT_K, width), layer),
            _layer_spec((1, width), layer),
            _const_spec((2 * L, L)),
            _const_spec((L, 2 * L)),
            tab, tab, tab,
            _layer_spec((1, HY_WIDTH), layer),
        ],
        out_specs=pl.BlockSpec((nb, L, HY_WIDTH), lambda b: (b, 0, 0)),
        out_shape=jax.ShapeDtypeStruct((B, L, HY_WIDTH), BF16),
        scratch_shapes=[pltpu.VMEM((nb, L + 2 * SUBLANES, width), F32)],
        compiler_params=_params(dimension_semantics=("arbitrary",)),
        name="hyena",
    )(hy3, conv_w, conv_b, fwd_bf, inv_bf, *g_tabs, hy_bias)


CF_PAD = 2 * SUBLANES


def _conformer_kernel(cf_ref, cw_ref, cb_ref, g_ref, b_ref, o_ref, pad_ref):
    nb, L, _ = cf_ref.shape
    padded = L + 2 * CF_PAD
    shifted = padded - SUBLANES
    zeros = jnp.zeros((CF_PAD, CF_WIDTH), F32)
    half = CF_CONV_K // 2
    for s in range(nb):
        cf = cf_ref[s]
        pad_ref[s, 0, 0:CF_PAD, :] = zeros
        pad_ref[s, 0, CF_PAD + L:padded, :] = zeros
        pad_ref[s, 0, CF_PAD:CF_PAD + L, :] = cf[:, :CF_WIDTH] * jax.nn.sigmoid(cf[:, CF_WIDTH:])
        for r in range(1, SUBLANES):
            pad_ref[s, r, 0:shifted, :] = pad_ref[s, 0, r:r + shifted, :]
        for c0 in range(0, L, CF_ROW_CHUNK):
            acc = jnp.zeros((CF_ROW_CHUNK, CF_WIDTH), F32) + cb_ref[...]
            for j in range(CF_CONV_K):
                off = CF_PAD + j - half
                start = c0 + (off // SUBLANES) * SUBLANES
                acc = acc + cw_ref[j:j + 1, :] * pad_ref[s, off % SUBLANES, start:start + CF_ROW_CHUNK, :]
            y = _layer_norm_rows(acc, g_ref[...], b_ref[...])
            o_ref[s, c0:c0 + CF_ROW_CHUNK, :] = (y * jax.nn.sigmoid(y)).astype(o_ref.dtype)


def _conformer_call(cf3, conv_w, conv_b, ln_g, ln_b, layer):
    B, L, width = cf3.shape
    nb = _seqs_per_step(B, L)
    row = _layer_spec((1, CF_WIDTH), layer)
    return pl.pallas_call(
        _conformer_kernel,
        grid=(B // nb,),
        in_specs=[
            pl.BlockSpec((nb, L, width), lambda b: (b, 0, 0)),
            _layer_spec((CF_CONV_K, CF_WIDTH), layer),
            row, row, row,
        ],
        out_specs=pl.BlockSpec((nb, L, CF_WIDTH), lambda b: (b, 0, 0)),
        out_shape=jax.ShapeDtypeStruct((B, L, CF_WIDTH), BF16),
        scratch_shapes=[pltpu.VMEM((nb, SUBLANES, L + 2 * CF_PAD, CF_WIDTH), F32)],
        compiler_params=_params(dimension_semantics=("arbitrary",)),
        name="conformer",
    )(cf3, conv_w, conv_b, ln_g, ln_b)


def _attn_kernel(*refs, lambda_init, past):
    if past:
        (q_ref, k_ref, v_ref, ck_ref, cv_ref, lq1, lk1, lq2, lk2, g_ref, o_ref, k_all, v_aug) = refs
    else:
        (q_ref, k_ref, v_ref, lq1, lk1, lq2, lk2, g_ref, o_ref) = refs
    nb, tq, _ = q_ref.shape

    if past:
        keys = v_aug.shape[1]

        @pl.when(pl.program_id(1) == 0)
        def _():
            ones = jnp.ones((keys, HEAD_PAIR), BF16)
            for s in range(nb):
                k_all[s, 0:past, :] = ck_ref[s, 0].reshape(past, ATT_WIDTH).astype(BF16)
                k_all[s, past:keys, :] = k_ref[s]
                v_old = cv_ref[s, 0].reshape(past, ATT_WIDTH).astype(BF16)
                for h in range(DIFF_HEADS):
                    cols = slice(h * HEAD_PAIR, (h + 1) * HEAD_PAIR)
                    base = 2 * h * HEAD_PAIR
                    v_aug[s, 0:past, base:base + HEAD_PAIR] = v_old[:, cols]
                    v_aug[s, past:keys, base:base + HEAD_PAIR] = v_ref[s, :, cols]
                    v_aug[s, :, base + HEAD_PAIR:base + 2 * HEAD_PAIR] = ones

    lam = (jnp.exp(jnp.sum(lq1[...] * lk1[...], axis=-1, keepdims=True))
           - jnp.exp(jnp.sum(lq2[...] * lk2[...], axis=-1, keepdims=True)) + lambda_init)
    lane = lax.broadcasted_iota(jnp.int32, (tq, HEAD_PAIR), 1)
    first = lane < DIFF_HEAD_DIM
    nt = (((1,), (1,)), ((), ()))
    gain = g_ref[...] * (1.0 - lambda_init)
    zero = jnp.zeros((), BF16)

    def scores(s, h):
        cols = slice(h * HEAD_PAIR, (h + 1) * HEAD_PAIR)
        q = q_ref[s, :, cols]
        q2 = jnp.concatenate([jnp.where(first, q, zero), jnp.where(first, zero, q)], axis=0)
        kh = k_all[s, :, cols] if past else k_ref[s, :, cols]
        return lax.dot_general(q2, kh, nt, preferred_element_type=F32)

    def finish(s, h, sc):
        cols = slice(h * HEAD_PAIR, (h + 1) * HEAD_PAIR)
        e = jnp.exp2(sc - jnp.max(sc, axis=-1, keepdims=True))
        if past:
            p = _dot(e.astype(BF16), v_aug[s, :, 2 * h * HEAD_PAIR:2 * (h + 1) * HEAD_PAIR])
            p0, p1 = p[:tq], p[tq:]
            o = (p0[:, :HEAD_PAIR] * (1.0 / p0[:, HEAD_PAIR:HEAD_PAIR + 1])
                 - p1[:, :HEAD_PAIR] * (lam / p1[:, HEAD_PAIR:HEAD_PAIR + 1]))
        else:
            r = 1.0 / jnp.sum(e, axis=-1, keepdims=True)
            a = e[:tq] * r[:tq] - e[tq:] * (lam * r[tq:])
            o = _dot(a.astype(BF16), v_ref[s, :, cols])
        o = o * lax.rsqrt(jnp.mean(o * o, axis=-1, keepdims=True) + LN_EPS) * gain
        o_ref[s, :, cols] = o.astype(o_ref.dtype)

    lookahead = 0 if past else 1
    pending = []
    for s in range(nb):
        for h in range(DIFF_HEADS):
            pending.append((s, h, scores(s, h)))
            if len(pending) > lookahead:
                finish(*pending.pop(0))
    for unit in pending:
        finish(*unit)


def _attn_call(q3, k3, v3, cache_k, cache_v, layer, lam_vecs, subln_g):
    B, L, _ = q3.shape
    past = 0 if cache_k is None else cache_k.shape[2]
    nb = _seqs_per_step(B, 2 * L)
    tq = min(Q_TILE, L)
    tile = pl.BlockSpec((nb, tq, ATT_WIDTH), lambda b, i: (b, i, 0))
    seq = pl.BlockSpec((nb, L, ATT_WIDTH), lambda b, i: (b, 0, 0))
    in_specs = [tile, seq, seq]
    args = [q3, k3, v3]
    scratch = []
    if past:
        for cache in (cache_k, cache_v):
            in_specs.append(pl.BlockSpec((nb, 1) + cache.shape[2:], lambda b, i: (b, layer, 0, 0, 0)))
            args.append(cache)
        scratch = [pltpu.VMEM((nb, past + L, ATT_WIDTH), BF16),
                   pltpu.VMEM((nb, past + L, 2 * ATT_WIDTH), BF16)]
    in_specs += [_layer_spec((1, DIFF_HEAD_DIM), layer)] * 4 + [_layer_spec((1, HEAD_PAIR), layer)]
    args += list(lam_vecs) + [subln_g]
    return pl.pallas_call(
        functools.partial(_attn_kernel, lambda_init=_lambda_init(layer), past=past),
        grid=(B // nb, L // tq),
        in_specs=in_specs,
        out_specs=tile,
        out_shape=jax.ShapeDtypeStruct((B, L, ATT_WIDTH), BF16),
        scratch_shapes=scratch,
        compiler_params=_params(dimension_semantics=("arbitrary", "arbitrary")),
        name="diff_attn",
    )(*args)


def _mlp_kernel(x_ref, hy_ref, cf_ref, at_ref, ada_ref, wo_ref, w1_ref, w2_ref,
                g1_ref, b1_ref, g2_ref, b2_ref, o_ref):
    gate1 = ada_ref[0, 0, 2:3, :]
    sh2 = ada_ref[0, 0, 3:4, :]
    sc2 = ada_ref[0, 0, 4:5, :]
    gate2 = ada_ref[0, 0, 5:6, :]
    mix = jnp.concatenate([hy_ref[...], cf_ref[...], at_ref[...]], axis=1)
    y = _dot(mix, wo_ref[...])
    x = _layer_norm_rows(DEEPNORM_ALPHA * x_ref[...] + gate1 * y, g1_ref[...], b1_ref[...])
    h = (x * (1.0 + sc2) + sh2).astype(BF16)
    a = jnp.maximum(_dot(h, w1_ref[...]), 0.0)
    m = _dot((a * a).astype(BF16), w2_ref[...])
    o_ref[...] = _layer_norm_rows(DEEPNORM_ALPHA * x + gate2 * m, g2_ref[...], b2_ref[...])


def _mlp_call(x2d, y_hy, y_cf, y_at, ada4, wo_bf, w1_bf, w2_bf, ln1_g, ln1_b, ln2_g, ln2_b,
              layer, cond_row_of_tile):
    T = x2d.shape[0]
    rows = lambda w: pl.BlockSpec((ROW_TILE, w), lambda i: (i, 0))
    vec = _layer_spec((1, D_MODEL), layer)
    return pl.pallas_call(
        _mlp_kernel,
        grid=(T // ROW_TILE,),
        in_specs=[
            rows(D_MODEL), rows(HY_WIDTH), rows(CF_WIDTH), rows(ATT_WIDTH),
            pl.BlockSpec((1, 1, N_ADA, D_MODEL), lambda i: (layer, cond_row_of_tile(i), 0, 0)),
            _layer_spec((MIX_WIDTH, D_MODEL), layer),
            _layer_spec((D_MODEL, D_FF), layer),
            _layer_spec((D_FF, D_MODEL), layer),
            vec, vec, vec, vec,
        ],
        out_specs=rows(D_MODEL),
        out_shape=jax.ShapeDtypeStruct((T, D_MODEL), F32),
        compiler_params=_params(dimension_semantics=("arbitrary",)),
        name="outproj_mlp",
    )(x2d, y_hy, y_cf, y_at, ada4, wo_bf, w1_bf, w2_bf, ln1_g, ln1_b, ln2_g, ln2_b)


def _layer(x2d, batch, seq, layer, ada4, wts, hy_tabs, cf_tabs, cond_row_of_tile, rope_tabs, cache_kv,
           new_caches):
    z = _inproj_call(x2d, ada4, wts, layer, batch, seq, cond_row_of_tile, rope_tabs, new_caches,
                     (hy_tabs, cf_tabs) if cf_tabs is not None else None)
    as_seq = lambda a: a.reshape(batch, seq, a.shape[-1])
    flat = lambda a: a.reshape(batch * seq, a.shape[-1])
    if "yhy" in z:
        y_hy, y_cf = z["yhy"], z["ycf"]
    else:
        y_hy = flat(_hyena_call(as_seq(z["hy"]), wts["hy_conv_w"], wts["hy_conv_b"], hy_tabs,
                                wts["hy_bias"], layer))
        y_cf = flat(_conformer_call(as_seq(z["cf"]), wts["cf_conv_w"], wts["cf_conv_b"], wts["cf_ln_g"],
                                    wts["cf_ln_b"], layer))
    ck, cv = cache_kv if cache_kv is not None else (None, None)
    lam_vecs = (wts["lam_q1"], wts["lam_k1"], wts["lam_q2"], wts["lam_k2"])
    y_at = _attn_call(as_seq(z["q"]), as_seq(z["k"]), as_seq(z["v"]), ck, cv, layer, lam_vecs,
                      wts["subln_g"])
    x_out = _mlp_call(x2d, y_hy, y_cf, flat(y_at), ada4,
                      wts["w_out"], wts["w_mlp1"], wts["w_mlp2"],
                      wts["ln1_g"], wts["ln1_b"], wts["ln2_g"], wts["ln2_b"], layer, cond_row_of_tile)
    return x_out, (z["kc"], z["vc"]) if "kc" in z else None


def kernel(x_prompt, x_sample, cache_k, cache_v, c, c_ctx, w_ada, b_ada, w_in, hy_conv_w, hy_conv_b, hf_w1, hf_b1, hf_freq, hf_w2, hf_b2, hf_w3, hy_bias, cf_conv_w, cf_conv_b, cf_ln_g, cf_ln_b, lam_q1, lam_k1, lam_q2, lam_k2, subln_g, w_out, ln1_g, ln1_b, w_mlp1, w_mlp2, ln2_g, ln2_b):
    batch, seq, _ = x_prompt.shape
    dec_batch, dec_seq, _ = x_sample.shape
    assert 1 + dec_batch <= COND_ROWS
    assert ROW_TILE % seq == 0 and (batch * seq) % ROW_TILE == 0
    assert dec_seq % ROW_TILE == 0 and dec_seq % Q_TILE == 0

    cond = jnp.concatenate(
        [c_ctx[None, :], c, jnp.zeros((COND_ROWS - 1 - dec_batch, D_MODEL), F32)], axis=0)
    ada4 = _ada_call(cond, w_ada, b_ada).reshape(DEPTH, COND_ROWS, N_ADA, D_MODEL)

    row_vec = lambda a: a.reshape(DEPTH, 1, a.shape[-1])
    wts = dict(
        w_in=w_in.astype(BF16), w_out=w_out.astype(BF16),
        w_mlp1=w_mlp1.astype(BF16), w_mlp2=w_mlp2.astype(BF16),
        hy_conv_w=hy_conv_w, hy_conv_b=row_vec(hy_conv_b), hy_bias=row_vec(hy_bias),
        cf_conv_w=cf_conv_w, cf_conv_b=row_vec(cf_conv_b),
        cf_ln_g=row_vec(cf_ln_g), cf_ln_b=row_vec(cf_ln_b),
        lam_q1=row_vec(lam_q1), lam_k1=row_vec(lam_k1), lam_q2=row_vec(lam_q2), lam_k2=row_vec(lam_k2),
        subln_g=row_vec(subln_g),
        ln1_g=row_vec(ln1_g), ln1_b=row_vec(ln1_b), ln2_g=row_vec(ln2_g), ln2_b=row_vec(ln2_b))

    rope_tabs = tuple(jnp.asarray(t) for t in _rope_tables(dec_seq))
    dec_tiles_per_seq = dec_seq // ROW_TILE

    xp = x_prompt.reshape(batch * seq, D_MODEL)
    xs = x_sample.reshape(dec_batch * dec_seq, D_MODEL)
    new_caches = ()
    filt = (hf_w1, hf_b1, hf_freq, hf_w2, hf_b2, hf_w3)
    hy_ctx = _filter_call(seq, *filt)
    hy_dec = _filter_call(dec_seq, *filt)
    fused = lambda n: n <= FUSED_MIXER_MAX_SEQ and ROW_TILE % n == 0
    cf_ctx = _conv_filter_call(seq, cf_conv_w) if fused(seq) else None
    cf_dec = _conv_filter_call(dec_seq, cf_conv_w) if fused(dec_seq) else None
    for l in range(DEPTH):
        xp, new_caches = _layer(xp, batch, seq, l, ada4, wts, hy_ctx, cf_ctx, lambda i: 0, None, None,
                                new_caches)
        xs, _ = _layer(xs, dec_batch, dec_seq, l, ada4, wts, hy_dec, cf_dec,
                       lambda i: 1 + i // dec_tiles_per_seq, rope_tabs, (cache_k, cache_v), None)
    new_cache_k, new_cache_v = new_caches
    return (xp.reshape(batch, seq, D_MODEL), xs.reshape(dec_batch, dec_seq, D_MODEL),
            new_cache_k, new_cache_v)
```

```python
import functools
import math

import numpy as np
import jax
import jax.numpy as jnp
from jax import lax
from jax.experimental import pallas as pl
from jax.experimental.pallas import tpu as pltpu

D_MODEL = 1024
DEPTH = 2
GRID_W = 64
HY_WIDTH = D_MODEL // 4
CF_WIDTH = D_MODEL // 4
ATT_WIDTH = D_MODEL // 2
DIFF_HEAD_DIM = 64
DIFF_HEADS = ATT_WIDTH // (2 * DIFF_HEAD_DIM)
HEAD_PAIR = 2 * DIFF_HEAD_DIM
MIX_WIDTH = HY_WIDTH + CF_WIDTH + ATT_WIDTH
IN_WIDTH = 3 * HY_WIDTH + 2 * CF_WIDTH + 3 * ATT_WIDTH
HY_SHORT_K = 3
HY_FILTER_EMB = 33
HY_FILTER_HIDDEN = 64
HY_FAST_DECAY_PCT = 0.3
HY_SLOW_DECAY_PCT = 1.5
HY_DECAY_TARGET = 1e-2
CF_CONV_K = 31
D_FF = 4 * D_MODEL
ROPE_BASE = 10000.0
AX_DIM = DIFF_HEAD_DIM // 2
DEEPNORM_ALPHA = (2 * DEPTH) ** 0.25
LN_EPS = 1e-5
N_ADA = 6

LANES = 128
SUBLANES = 8
COND_ROWS = SUBLANES
VMEM_LIMIT = 56 * 1024 * 1024

ROW_TILE = 512
ADA_COL_TILE = 1536
Q_TILE = 512
MIXER_ROWS = 1024
HYENA_LONG_SEQS = 2
FUSED_MIXER_MAX_SEQ = 256
CF_ROW_CHUNK = 128

F32 = jnp.float32
BF16 = jnp.bfloat16


def _lambda_init(l):
    return 0.8 - 0.6 * math.exp(-0.3 * l)


def _params(**kw):
    return pltpu.CompilerParams(vmem_limit_bytes=VMEM_LIMIT, **kw)


def _const_spec(shape):
    zeros = (0,) * len(shape)
    return pl.BlockSpec(shape, lambda *_: zeros, pipeline_mode=pl.Buffered(1))


def _layer_spec(shape, layer):
    zeros = (0,) * len(shape)
    return pl.BlockSpec((None,) + tuple(shape), lambda *_: (layer,) + zeros,
                        pipeline_mode=pl.Buffered(1))


def _dot(a, b):
    return jnp.dot(a, b, preferred_element_type=F32)


def _dot_f32(a, b):
    return jnp.dot(a, b, preferred_element_type=F32, precision=lax.Precision.HIGHEST)


def _split_bf16(x):
    hi = x.astype(BF16)
    return hi, (x - hi.astype(F32)).astype(BF16)


def _layer_norm_rows(x, g, b):
    mu = jnp.mean(x, axis=-1, keepdims=True)
    xc = x - mu
    var = jnp.mean(xc * xc, axis=-1, keepdims=True)
    return xc * lax.rsqrt(var + LN_EPS) * g + b


def _seqs_per_step(batch, seq):
    return max(1, min(batch, MIXER_ROWS // seq))


@functools.lru_cache(maxsize=None)
def _dft_tables(L):
    n = 2 * L
    k = np.arange(L, dtype=np.float64)[:, None]
    s = np.arange(L, dtype=np.float64)[None, :]
    ang = 2.0 * np.pi * k * s / n
    fwd = np.concatenate([np.cos(ang), -np.sin(ang)], axis=0)
    fwd[L, :] = np.cos(np.pi * np.arange(L))
    t = np.arange(L, dtype=np.float64)[:, None]
    kk = np.arange(L, dtype=np.float64)[None, :]
    ang_i = 2.0 * np.pi * t * kk / n
    inv_re = (2.0 / n) * np.cos(ang_i)
    inv_re[:, 0] = 1.0 / n
    inv_im = -(2.0 / n) * np.sin(ang_i)
    inv_im[:, 0] = np.cos(np.pi * np.arange(L)) / n
    inv = np.concatenate([inv_re, inv_im], axis=1)
    return fwd.astype(np.float32), inv.astype(np.float32)


@functools.lru_cache(maxsize=None)
def _filter_tables(L):
    bands = (HY_FILTER_EMB - 1) // 2
    t = np.linspace(0.0, 1.0, L)[:, None]
    w = (2.0 * np.pi / L) * np.arange(L, dtype=np.float64)[:, None]
    fr = np.linspace(1e-4, bands - 1, bands)[None, :]
    feats = np.concatenate([t, np.cos(fr * w), -np.sin(fr * w)], -1)
    feats_p = np.zeros((L, LANES), np.float64)
    feats_p[:, :HY_FILTER_EMB] = feats
    deltas = np.abs(np.linspace(math.log(HY_DECAY_TARGET) / HY_FAST_DECAY_PCT,
                                math.log(HY_DECAY_TARGET) / HY_SLOW_DECAY_PCT, HY_WIDTH))
    decay = np.exp(-t * deltas[None, :])
    return feats_p.astype(np.float32), decay.astype(np.float32)


@functools.lru_cache(maxsize=None)
def _rope_tables(n):
    half = AX_DIM // 2
    pos = np.arange(n)
    row = (pos // GRID_W).astype(np.float64)[:, None]
    col = (pos % GRID_W).astype(np.float64)[:, None]
    inv = ROPE_BASE ** (-np.arange(0, AX_DIM, 2, dtype=np.float64) / AX_DIM)[None, :]
    zero = np.zeros((n, half))
    cr, sr = np.cos(row * inv), np.sin(row * inv)
    cc, sc = np.cos(col * inv), np.sin(col * inv)
    c = np.concatenate([cr, cr, cc, cc], -1)
    s_up = np.concatenate([-sr, zero, -sc, zero], -1)
    s_dn = np.concatenate([zero, sr, zero, sc], -1)
    reps = LANES // DIFF_HEAD_DIM
    tile = lambda a: np.tile(a, (1, reps)).astype(np.float32)
    return tile(c), tile(s_up), tile(s_dn)


def _ada_kernel(cond_ref, w_ref, b_ref, o_ref):
    c = cond_ref[...]
    s = c * jax.nn.sigmoid(c)
    o_ref[0] = _dot(s.astype(BF16), w_ref[0].astype(BF16)) + b_ref[0]


def _ada_call(cond, w_ada, b_ada):
    n_out = N_ADA * D_MODEL
    return pl.pallas_call(
        _ada_kernel,
        grid=(DEPTH, n_out // ADA_COL_TILE),
        in_specs=[
            pl.BlockSpec((COND_ROWS, D_MODEL), lambda l, j: (0, 0)),
            pl.BlockSpec((1, D_MODEL, ADA_COL_TILE), lambda l, j: (l, 0, j)),
            pl.BlockSpec((1, 1, ADA_COL_TILE), lambda l, j: (l, 0, j)),
        ],
        out_specs=pl.BlockSpec((1, COND_ROWS, ADA_COL_TILE), lambda l, j: (l, 0, j)),
        out_shape=jax.ShapeDtypeStruct((DEPTH, COND_ROWS, n_out), F32),
        compiler_params=_params(dimension_semantics=("arbitrary", "arbitrary")),
        name="ada",
    )(cond, w_ada, b_ada.reshape(DEPTH, 1, n_out))


def _rope_lanes(x, c, s_up, s_dn):
    outs = []
    for p in range(ATT_WIDTH // LANES):
        xb = x[:, p * LANES:(p + 1) * LANES]
        up = pltpu.roll(xb, LANES - AX_DIM // 2, 1)
        dn = pltpu.roll(xb, AX_DIM // 2, 1)
        outs.append(xb * c + up * s_up + dn * s_dn)
    return jnp.concatenate(outs, axis=1)


Q_SCALE = DIFF_HEAD_DIM ** -0.5 * math.log2(math.e)


def _inproj_kernel(*refs, names, seq, cache_slot):
    r = dict(zip(names, refs))
    sh1 = r["ada"][0, 0, 0:1, :]
    sc1 = r["ada"][0, 0, 1:2, :]
    h = (r["x"][...] * (1.0 + sc1) + sh1).astype(BF16)
    w_ref = r["w"]
    o1 = 3 * HY_WIDTH
    o2 = o1 + 2 * CF_WIDTH
    o3 = o2 + ATT_WIDTH
    o4 = o3 + ATT_WIDTH
    hy = _dot(h, w_ref[:, 0:o1])
    cf = _dot(h, w_ref[:, o1:o2])
    q = _dot(h, w_ref[:, o2:o3])
    k = _dot(h, w_ref[:, o3:o4])
    v = _dot(h, w_ref[:, o4:IN_WIDTH])
    if "yhy" in r:
        seqs = hy.shape[0] // seq
        x0s, us, tabs = [], [], []
        for b in range(seqs):
            x0, u = _hyena_front(hy[b * seq:(b + 1) * seq], r["hy_pad"].at[b], r["hy_cw"], r["hy_cb"])
            x0s.append(x0)
            us.append(u)
            tabs.append((r["hy_ga"][...], r["hy_gbc"][...], r["hy_gd"][...]))
        for b in range(seqs):
            c = cf[b * seq:(b + 1) * seq]
            us.append(c[:, :CF_WIDTH] * jax.nn.sigmoid(c[:, CF_WIDTH:]))
            tabs.append((r["cf_ga"][...], r["cf_gbc"][...], r["cf_gd"][...]))
        ys = _dft_conv(us, tabs, r["fwd"], r["inv"])
        for b in range(seqs):
            rows = slice(b * seq, (b + 1) * seq)
            r["yhy"][rows, :] = (x0s[b] * (ys[b] + us[b] * r["hy_bias"][...])).astype(BF16)
            y = _layer_norm_rows(ys[seqs + b] + r["cf_cb"][...], r["cf_g"][...], r["cf_b"][...])
            r["ycf"][rows, :] = (y * jax.nn.sigmoid(y)).astype(BF16)
    else:
        r["hy"][...] = hy
        r["cf"][...] = cf
    if "rope_c" in r:
        c, su, sd = r["rope_c"][...], r["rope_up"][...], r["rope_dn"][...]
        q = _rope_lanes(q, c, su, sd)
        k = _rope_lanes(k, c, su, sd)
    r["q"][...] = (q * Q_SCALE).astype(BF16)
    r["k"][...] = k.astype(BF16)
    r["v"][...] = v.astype(BF16)
    if "kc" in r:
        kc_ref, vc_ref = r["kc"], r["vc"]
        for b in range(kc_ref.shape[0]):
            rows = slice(b * seq, (b + 1) * seq)
            for slot in range(kc_ref.shape[1]):
                if slot == cache_slot:
                    kc_ref[b, slot] = k[rows].T.reshape(2 * DIFF_HEADS, DIFF_HEAD_DIM, seq)
                    vc_ref[b, slot] = v[rows].reshape(seq, DIFF_HEADS, 2 * DIFF_HEAD_DIM)
                else:
                    kc_ref[b, slot] = jnp.zeros(kc_ref.shape[2:], F32)
                    vc_ref[b, slot] = jnp.zeros(vc_ref.shape[2:], F32)


def _inproj_call(x2d, ada4, wts, layer, batch, seq, cond_row_of_tile, rope_tabs, caches, mix_tabs):
    T = x2d.shape[0]
    rows = lambda w: pl.BlockSpec((ROW_TILE, w), lambda i: (i, 0))
    ins = [
        ("x", x2d, rows(D_MODEL)),
        ("ada", ada4, pl.BlockSpec((1, 1, N_ADA, D_MODEL), lambda i: (layer, cond_row_of_tile(i), 0, 0))),
        ("w", wts["w_in"], _layer_spec((D_MODEL, IN_WIDTH), layer)),
    ]
    outs = []
    scratch = []
    if rope_tabs is not None:
        tiles_per_seq = seq // ROW_TILE
        for name, tab in zip(("rope_c", "rope_up", "rope_dn"), rope_tabs):
            ins.append((name, tab, pl.BlockSpec((ROW_TILE, LANES), lambda i: (i % tiles_per_seq, 0))))
    if mix_tabs is not None:
        hy_tabs, cf_tabs = mix_tabs
        fwd, inv = _dft_tables(seq)
        ins.append(("fwd", jnp.asarray(fwd).astype(BF16), _const_spec((2 * seq, seq))))
        ins.append(("inv", jnp.asarray(inv).astype(BF16), _const_spec((seq, 2 * seq))))
        for name, tab in zip(("hy_ga", "hy_gbc", "hy_gd"), hy_tabs):
            ins.append((name, tab, _layer_spec((seq, HY_WIDTH), layer)))
        for name, tab in zip(("cf_ga", "cf_gbc", "cf_gd"), cf_tabs):
            ins.append((name, tab, _layer_spec((seq, CF_WIDTH), layer)))
        ins.append(("hy_cw", wts["hy_conv_w"], _layer_spec((HY_SHORT_K, 3 * HY_WIDTH), layer)))
        ins.append(("hy_cb", wts["hy_conv_b"], _layer_spec((1, 3 * HY_WIDTH), layer)))
        ins.append(("hy_bias", wts["hy_bias"], _layer_spec((1, HY_WIDTH), layer)))
        for name, key in (("cf_cb", "cf_conv_b"), ("cf_g", "cf_ln_g"), ("cf_b", "cf_ln_b")):
            ins.append((name, wts[key], _layer_spec((1, CF_WIDTH), layer)))
        outs.append(("yhy", rows(HY_WIDTH), jax.ShapeDtypeStruct((T, HY_WIDTH), BF16)))
        outs.append(("ycf", rows(CF_WIDTH), jax.ShapeDtypeStruct((T, CF_WIDTH), BF16)))
        scratch.append(("hy_pad", pltpu.VMEM((ROW_TILE // seq, seq + 2 * SUBLANES, 3 * HY_WIDTH), F32)))
    else:
        outs.append(("hy", rows(3 * HY_WIDTH), jax.ShapeDtypeStruct((T, 3 * HY_WIDTH), F32)))
        outs.append(("cf", rows(2 * CF_WIDTH), jax.ShapeDtypeStruct((T, 2 * CF_WIDTH), F32)))
    for name in ("q", "k", "v"):
        outs.append((name, rows(ATT_WIDTH), jax.ShapeDtypeStruct((T, ATT_WIDTH), BF16)))
    aliases = {}
    cache_slot = 0
    if caches is not None:
        seqs = ROW_TILE // seq
        first_call = len(caches) == 0
        cache_slot = layer if first_call else 0
        shapes = (("kc", (batch, DEPTH, 2 * DIFF_HEADS, DIFF_HEAD_DIM, seq)),
                  ("vc", (batch, DEPTH, seq, DIFF_HEADS, 2 * DIFF_HEAD_DIM)))
        for j, (name, shape) in enumerate(shapes):
            if first_call:
                spec = pl.BlockSpec((seqs, DEPTH) + shape[2:], lambda i: (i, 0, 0, 0, 0))
            else:
                spec = pl.BlockSpec((seqs, 1) + shape[2:], lambda i: (i, layer, 0, 0, 0))
                aliases[len(ins)] = len(outs)
                ins.append((name + "_in", caches[j], pl.BlockSpec(memory_space=pl.ANY)))
            outs.append((name, spec, jax.ShapeDtypeStruct(shape, F32)))
    names = tuple(n for n, _, _ in ins) + tuple(n for n, _, _ in outs) + tuple(n for n, _ in scratch)
    results = pl.pallas_call(
        functools.partial(_inproj_kernel, names=names, seq=seq, cache_slot=cache_slot),
        grid=(T // ROW_TILE,),
        in_specs=[spec for _, _, spec in ins],
        out_specs=[spec for _, spec, _ in outs],
        out_shape=[shape for _, _, shape in outs],
        scratch_shapes=[s for _, s in scratch],
        input_output_aliases=aliases,
        compiler_params=_params(dimension_semantics=("arbitrary",)),
        name="inproj",
    )(*[a for _, a, _ in ins])
    return dict(zip((n for n, _, _ in outs), results))


def _store_spectrum_tables(spec, layer, ga_ref, gbc_ref, gd_ref):
    L = spec.shape[0] // 2
    W = spec.shape[1] // 2
    row = lax.broadcasted_iota(jnp.int32, (L, W), 0)
    p = spec[:L]
    q = spec[L:]
    g_re = p[:, :W] + p[:, W:]
    g_im = q[:, :W] - q[:, W:]
    nyq = q[0:1, :W] + q[0:1, W:]
    ga_ref[layer] = g_re
    gbc_ref[layer] = jnp.where(row == 0, 0.0, g_im)
    gd_ref[layer] = jnp.where(row == 0, nyq, g_re)


def _filter_kernel(feats_ref, w1_ref, b1_ref, fq_ref, w2_ref, b2_ref, w3_ref, decay_ref,
                   fwd_hi_ref, fwd_lo_ref, ga_ref, gbc_ref, gd_ref):
    L = feats_ref.shape[0]
    decay = decay_ref[...]
    row = lax.broadcasted_iota(jnp.int32, (L, HY_WIDTH), 0)
    taps = []
    for l in range(DEPTH):
        fq = fq_ref[l]
        hid = jnp.sin(fq * (_dot_f32(feats_ref[...], w1_ref[l]) + b1_ref[l]))
        hid = jnp.sin(fq * (_dot_f32(hid, w2_ref[l]) + b2_ref[l]))
        hf = _dot_f32(hid, w3_ref[l])
        taps.append(hf[:, :HY_WIDTH] * decay)
        taps.append(jnp.where(row == 0, 0.0, hf[:, HY_WIDTH:] * decay))
    h_hi, h_lo = _split_bf16(jnp.concatenate(taps, axis=1))
    fwd_hi = fwd_hi_ref[...]
    spec = _dot(fwd_hi, h_hi) + (_dot(fwd_hi, h_lo) + _dot(fwd_lo_ref[...], h_hi))
    for l in range(DEPTH):
        _store_spectrum_tables(spec[:, 2 * l * HY_WIDTH:2 * (l + 1) * HY_WIDTH], l, ga_ref, gbc_ref, gd_ref)


def _pad_to(a, shape):
    return jnp.pad(a, [(0, s - d) for s, d in zip(shape, a.shape)])


def _filter_call(L, w1, b1, fq, w2, b2, w3):
    feats, decay = _filter_tables(L)
    fwd, _ = _dft_tables(L)
    row_vec = lambda a: _pad_to(a.reshape(DEPTH, 1, -1), (DEPTH, 1, LANES))
    args = [
        jnp.asarray(feats),
        _pad_to(w1, (DEPTH, LANES, LANES)),
        row_vec(b1),
        row_vec(fq),
        _pad_to(w2, (DEPTH, LANES, LANES)),
        row_vec(b2),
        _pad_to(w3, (DEPTH, LANES, 2 * HY_WIDTH)),
        jnp.asarray(decay),
        *_split_bf16(jnp.asarray(fwd)),
    ]
    out = jax.ShapeDtypeStruct((DEPTH, L, HY_WIDTH), F32)
    return pl.pallas_call(
        _filter_kernel,
        out_shape=[out, out, out],
        compiler_params=_params(),
        name="hyena_filter",
    )(*args)


def _conv_filter_kernel(taps_ref, fwd_ref, ga_ref, gbc_ref, gd_ref):
    spec = _dot_f32(fwd_ref[...], taps_ref[...])
    for l in range(DEPTH):
        _store_spectrum_tables(spec[:, 2 * l * CF_WIDTH:2 * (l + 1) * CF_WIDTH], l, ga_ref, gbc_ref, gd_ref)


def _conv_filter_call(L, conv_w):
    half = CF_CONV_K // 2
    fwd, _ = _dft_tables(L)
    lag_pos = conv_w[:, half::-1, :]
    lag_neg = _pad_to(conv_w[:, half + 1:, :], (DEPTH, half + 1, CF_WIDTH))
    lag_neg = jnp.roll(lag_neg, 1, axis=1)
    taps = jnp.concatenate([lag_pos, lag_neg], axis=2)
    taps = _pad_to(taps, (DEPTH, LANES, 2 * CF_WIDTH))
    taps = jnp.transpose(taps, (1, 0, 2)).reshape(LANES, DEPTH * 2 * CF_WIDTH)
    out = jax.ShapeDtypeStruct((DEPTH, L, CF_WIDTH), F32)
    return pl.pallas_call(
        _conv_filter_kernel,
        out_shape=[out, out, out],
        compiler_params=_params(),
        name="conv_filter",
    )(taps, jnp.asarray(fwd[:, :LANES]))


def _dft_conv(us, tabs, fwd_ref, inv_ref):
    L, W = us[0].shape
    u_all = jnp.concatenate(us, axis=1).astype(BF16)
    spec = _dot(fwd_ref[...], u_all)
    y_re, y_im = [], []
    for s, (ga, gbc, gd) in enumerate(tabs):
        p = spec[:L, s * W:(s + 1) * W]
        q = spec[L:, s * W:(s + 1) * W]
        y_re.append(p * ga - q * gbc)
        y_im.append(p * gbc + q * gd)
    y_spec = jnp.concatenate(
        [jnp.concatenate(y_re, axis=1), jnp.concatenate(y_im, axis=1)], axis=0).astype(BF16)
    y = _dot(inv_ref[...], y_spec)
    return [y[:, s * W:(s + 1) * W] for s in range(len(us))]


def _hyena_front(hy, pad_ref, cw_ref, cb_ref):
    L, width = hy.shape
    zeros = jnp.zeros((SUBLANES, width), F32)
    pad_ref[0:SUBLANES, :] = zeros
    pad_ref[SUBLANES + L:2 * SUBLANES + L, :] = zeros
    pad_ref[SUBLANES:SUBLANES + L, :] = hy
    conv = cb_ref[...]
    for j in range(HY_SHORT_K):
        start = SUBLANES + j - HY_SHORT_K // 2
        conv = conv + cw_ref[j:j + 1, :] * pad_ref[start:start + L, :]
    return conv[:, :HY_WIDTH], conv[:, HY_WIDTH:2 * HY_WIDTH] * conv[:, 2 * HY_WIDTH:]


def _hyena_kernel(hy_ref, cw_ref, cb_ref, fwd_ref, inv_ref, ga_ref, gbc_ref, gd_ref, hb_ref,
                  o_ref, pad_ref):
    nb = hy_ref.shape[0]
    x0s, us = zip(*[_hyena_front(hy_ref[s], pad_ref.at[s], cw_ref, cb_ref) for s in range(nb)])
    tabs = [(ga_ref[...], gbc_ref[...], gd_ref[...])] * nb
    ys = _dft_conv(us, tabs, fwd_ref, inv_ref)
    for s in range(nb):
        o_ref[s] = (x0s[s] * (ys[s] + us[s] * hb_ref[...])).astype(o_ref.dtype)


def _hyena_call(hy3, conv_w, conv_b, g_tabs, hy_bias, layer):
    B, L, width = hy3.shape
    nb = _seqs_per_step(B, L) if L < MIXER_ROWS else min(B, HYENA_LONG_SEQS)
    fwd, inv = _dft_tables(L)
    fwd_bf = jnp.asarray(fwd).astype(BF16)
    inv_bf = jnp.asarray(inv).astype(BF16)
    tab = _layer_spec((L, HY_WIDTH), layer)
    return pl.pallas_call(
        _hyena_kernel,
        grid=(B // nb,),
        in_specs=[
            pl.BlockSpec((nb, L, width), lambda b: (b, 0, 0)),
            _layer_spec((HY_SHORT_K, width), layer),
            _layer_spec((1, width), layer),
            _const_spec((2 * L, L)),
            _const_spec((L, 2 * L)),
            tab, tab, tab,
            _layer_spec((1, HY_WIDTH), layer),
        ],
        out_specs=pl.BlockSpec((nb, L, HY_WIDTH), lambda b: (b, 0, 0)),
        out_shape=jax.ShapeDtypeStruct((B, L, HY_WIDTH), BF16),
        scratch_shapes=[pltpu.VMEM((nb, L + 2 * SUBLANES, width), F32)],
        compiler_params=_params(dimension_semantics=("arbitrary",)),
        name="hyena",
    )(hy3, conv_w, conv_b, fwd_bf, inv_bf, *g_tabs, hy_bias)


CF_PAD = 2 * SUBLANES


def _conformer_kernel(cf_ref, cw_ref, cb_ref, g_ref, b_ref, o_ref, pad_ref):
    nb, L, _ = cf_ref.shape
    padded = L + 2 * CF_PAD
    shifted = padded - SUBLANES
    zeros = jnp.zeros((CF_PAD, CF_WIDTH), F32)
    half = CF_CONV_K // 2
    for s in range(nb):
        cf = cf_ref[s]
        pad_ref[s, 0, 0:CF_PAD, :] = zeros
        pad_ref[s, 0, CF_PAD + L:padded, :] = zeros
        pad_ref[s, 0, CF_PAD:CF_PAD + L, :] = cf[:, :CF_WIDTH] * jax.nn.sigmoid(cf[:, CF_WIDTH:])
        for r in range(1, SUBLANES):
            pad_ref[s, r, 0:shifted, :] = pad_ref[s, 0, r:r + shifted, :]
        for c0 in range(0, L, CF_ROW_CHUNK):
            acc = jnp.zeros((CF_ROW_CHUNK, CF_WIDTH), F32) + cb_ref[...]
            for j in range(CF_CONV_K):
                off = CF_PAD + j - half
                start = c0 + (off // SUBLANES) * SUBLANES
                acc = acc + cw_ref[j:j + 1, :] * pad_ref[s, off % SUBLANES, start:start + CF_ROW_CHUNK, :]
            y = _layer_norm_rows(acc, g_ref[...], b_ref[...])
            o_ref[s, c0:c0 + CF_ROW_CHUNK, :] = (y * jax.nn.sigmoid(y)).astype(o_ref.dtype)


def _conformer_call(cf3, conv_w, conv_b, ln_g, ln_b, layer):
    B, L, width = cf3.shape
    nb = _seqs_per_step(B, L)
    row = _layer_spec((1, CF_WIDTH), layer)
    return pl.pallas_call(
        _conformer_kernel,
        grid=(B // nb,),
        in_specs=[
            pl.BlockSpec((nb, L, width), lambda b: (b, 0, 0)),
            _layer_spec((CF_CONV_K, CF_WIDTH), layer),
            row, row, row,
        ],
        out_specs=pl.BlockSpec((nb, L, CF_WIDTH), lambda b: (b, 0, 0)),
        out_shape=jax.ShapeDtypeStruct((B, L, CF_WIDTH), BF16),
        scratch_shapes=[pltpu.VMEM((nb, SUBLANES, L + 2 * CF_PAD, CF_WIDTH), F32)],
        compiler_params=_params(dimension_semantics=("arbitrary",)),
        name="conformer",
    )(cf3, conv_w, conv_b, ln_g, ln_b)


def _attn_kernel(*refs, lambda_init, past):
    if past:
        (q_ref, k_ref, v_ref, ck_ref, cv_ref, lq1, lk1, lq2, lk2, g_ref, o_ref, k_all, v_aug) = refs
    else:
        (q_ref, k_ref, v_ref, lq1, lk1, lq2, lk2, g_ref, o_ref) = refs
    nb, tq, _ = q_ref.shape

    if past:
        keys = v_aug.shape[1]

        @pl.when(pl.program_id(1) == 0)
        def _():
            ones = jnp.ones((keys, HEAD_PAIR), BF16)
            for s in range(nb):
                k_all[s, 0:past, :] = ck_ref[s, 0].reshape(ATT_WIDTH, past).T.astype(BF16)
                k_all[s, past:keys, :] = k_ref[s]
                v_old = cv_ref[s, 0].reshape(past, ATT_WIDTH).astype(BF16)
                for h in range(DIFF_HEADS):
                    cols = slice(h * HEAD_PAIR, (h + 1) * HEAD_PAIR)
                    base = 2 * h * HEAD_PAIR
                    v_aug[s, 0:past, base:base + HEAD_PAIR] = v_old[:, cols]
                    v_aug[s, past:keys, base:base + HEAD_PAIR] = v_ref[s, :, cols]
                    v_aug[s, :, base + HEAD_PAIR:base + 2 * HEAD_PAIR] = ones

    lam = (jnp.exp(jnp.sum(lq1[...] * lk1[...], axis=-1, keepdims=True))
           - jnp.exp(jnp.sum(lq2[...] * lk2[...], axis=-1, keepdims=True)) + lambda_init)
    lane = lax.broadcasted_iota(jnp.int32, (tq, HEAD_PAIR), 1)
    first = lane < DIFF_HEAD_DIM
    nt = (((1,), (1,)), ((), ()))
    gain = g_ref[...] * (1.0 - lambda_init)
    zero = jnp.zeros((), BF16)

    def scores(s, h):
        cols = slice(h * HEAD_PAIR, (h + 1) * HEAD_PAIR)
        q = q_ref[s, :, cols]
        q2 = jnp.concatenate([jnp.where(first, q, zero), jnp.where(first, zero, q)], axis=0)
        kh = k_all[s, :, cols] if past else k_ref[s, :, cols]
        return lax.dot_general(q2, kh, nt, preferred_element_type=F32)

    def finish(s, h, sc):
        cols = slice(h * HEAD_PAIR, (h + 1) * HEAD_PAIR)
        e = jnp.exp2(sc - jnp.max(sc, axis=-1, keepdims=True))
        if past:
            p = _dot(e.astype(BF16), v_aug[s, :, 2 * h * HEAD_PAIR:2 * (h + 1) * HEAD_PAIR])
            p0, p1 = p[:tq], p[tq:]
            o = (p0[:, :HEAD_PAIR] * (1.0 / p0[:, HEAD_PAIR:HEAD_PAIR + 1])
                 - p1[:, :HEAD_PAIR] * (lam / p1[:, HEAD_PAIR:HEAD_PAIR + 1]))
        else:
            r = 1.0 / jnp.sum(e, axis=-1, keepdims=True)
            a = e[:tq] * r[:tq] - e[tq:] * (lam * r[tq:])
            o = _dot(a.astype(BF16), v_ref[s, :, cols])
        o = o * lax.rsqrt(jnp.mean(o * o, axis=-1, keepdims=True) + LN_EPS) * gain
        o_ref[s, :, cols] = o.astype(o_ref.dtype)

    lookahead = 0 if past else 1
    pending = []
    for s in range(nb):
        for h in range(DIFF_HEADS):
            pending.append((s, h, scores(s, h)))
            if len(pending) > lookahead:
                finish(*pending.pop(0))
    for unit in pending:
        finish(*unit)


def _attn_call(q3, k3, v3, cache_k, cache_v, layer, lam_vecs, subln_g):
    B, L, _ = q3.shape
    past = 0 if cache_k is None else cache_k.shape[-1]
    nb = _seqs_per_step(B, 2 * L)
    tq = min(Q_TILE, L)
    tile = pl.BlockSpec((nb, tq, ATT_WIDTH), lambda b, i: (b, i, 0))
    seq = pl.BlockSpec((nb, L, ATT_WIDTH), lambda b, i: (b, 0, 0))
    in_specs = [tile, seq, seq]
    args = [q3, k3, v3]
    scratch = []
    if past:
        for cache in (cache_k, cache_v):
            in_specs.append(pl.BlockSpec((nb, 1) + cache.shape[2:], lambda b, i: (b, layer, 0, 0, 0)))
            args.append(cache)
        scratch = [pltpu.VMEM((nb, past + L, ATT_WIDTH), BF16),
                   pltpu.VMEM((nb, past + L, 2 * ATT_WIDTH), BF16)]
    in_specs += [_layer_spec((1, DIFF_HEAD_DIM), layer)] * 4 + [_layer_spec((1, HEAD_PAIR), layer)]
    args += list(lam_vecs) + [subln_g]
    return pl.pallas_call(
        functools.partial(_attn_kernel, lambda_init=_lambda_init(layer), past=past),
        grid=(B // nb, L // tq),
        in_specs=in_specs,
        out_specs=tile,
        out_shape=jax.ShapeDtypeStruct((B, L, ATT_WIDTH), BF16),
        scratch_shapes=scratch,
        compiler_params=_params(dimension_semantics=("arbitrary", "arbitrary")),
        name="diff_attn",
    )(*args)


def _mlp_kernel(x_ref, hy_ref, cf_ref, at_ref, ada_ref, wo_ref, w1_ref, w2_ref,
                g1_ref, b1_ref, g2_ref, b2_ref, o_ref):
    gate1 = ada_ref[0, 0, 2:3, :]
    sh2 = ada_ref[0, 0, 3:4, :]
    sc2 = ada_ref[0, 0, 4:5, :]
    gate2 = ada_ref[0, 0, 5:6, :]
    mix = jnp.concatenate([hy_ref[...], cf_ref[...], at_ref[...]], axis=1)
    y = _dot(mix, wo_ref[...])
    x = _layer_norm_rows(DEEPNORM_ALPHA * x_ref[...] + gate1 * y, g1_ref[...], b1_ref[...])
    h = (x * (1.0 + sc2) + sh2).astype(BF16)
    a = jnp.maximum(_dot(h, w1_ref[...]), 0.0)
    m = _dot((a * a).astype(BF16), w2_ref[...])
    o_ref[...] = _layer_norm_rows(DEEPNORM_ALPHA * x + gate2 * m, g2_ref[...], b2_ref[...])


def _mlp_call(x2d, y_hy, y_cf, y_at, ada4, wo_bf, w1_bf, w2_bf, ln1_g, ln1_b, ln2_g, ln2_b,
              layer, cond_row_of_tile):
    T = x2d.shape[0]
    rows = lambda w: pl.BlockSpec((ROW_TILE, w), lambda i: (i, 0))
    vec = _layer_spec((1, D_MODEL), layer)
    return pl.pallas_call(
        _mlp_kernel,
        grid=(T // ROW_TILE,),
        in_specs=[
            rows(D_MODEL), rows(HY_WIDTH), rows(CF_WIDTH), rows(ATT_WIDTH),
            pl.BlockSpec((1, 1, N_ADA, D_MODEL), lambda i: (layer, cond_row_of_tile(i), 0, 0)),
            _layer_spec((MIX_WIDTH, D_MODEL), layer),
            _layer_spec((D_MODEL, D_FF), layer),
            _layer_spec((D_FF, D_MODEL), layer),
            vec, vec, vec, vec,
        ],
        out_specs=rows(D_MODEL),
        out_shape=jax.ShapeDtypeStruct((T, D_MODEL), F32),
        compiler_params=_params(dimension_semantics=("arbitrary",)),
        name="outproj_mlp",
    )(x2d, y_hy, y_cf, y_at, ada4, wo_bf, w1_bf, w2_bf, ln1_g, ln1_b, ln2_g, ln2_b)


def _layer(x2d, batch, seq, layer, ada4, wts, hy_tabs, cf_tabs, cond_row_of_tile, rope_tabs, cache_kv,
           new_caches):
    z = _inproj_call(x2d, ada4, wts, layer, batch, seq, cond_row_of_tile, rope_tabs, new_caches,
                     (hy_tabs, cf_tabs) if cf_tabs is not None else None)
    as_seq = lambda a: a.reshape(batch, seq, a.shape[-1])
    flat = lambda a: a.reshape(batch * seq, a.shape[-1])
    if "yhy" in z:
        y_hy, y_cf = z["yhy"], z["ycf"]
    else:
        y_hy = flat(_hyena_call(as_seq(z["hy"]), wts["hy_conv_w"], wts["hy_conv_b"], hy_tabs,
                                wts["hy_bias"], layer))
        y_cf = flat(_conformer_call(as_seq(z["cf"]), wts["cf_conv_w"], wts["cf_conv_b"], wts["cf_ln_g"],
                                    wts["cf_ln_b"], layer))
    ck, cv = cache_kv if cache_kv is not None else (None, None)
    lam_vecs = (wts["lam_q1"], wts["lam_k1"], wts["lam_q2"], wts["lam_k2"])
    y_at = _attn_call(as_seq(z["q"]), as_seq(z["k"]), as_seq(z["v"]), ck, cv, layer, lam_vecs,
                      wts["subln_g"])
    x_out = _mlp_call(x2d, y_hy, y_cf, flat(y_at), ada4,
                      wts["w_out"], wts["w_mlp1"], wts["w_mlp2"],
                      wts["ln1_g"], wts["ln1_b"], wts["ln2_g"], wts["ln2_b"], layer, cond_row_of_tile)
    return x_out, (z["kc"], z["vc"]) if "kc" in z else None


def kernel(x_prompt, x_sample, cache_k, cache_v, c, c_ctx, w_ada, b_ada, w_in, hy_conv_w, hy_conv_b, hf_w1, hf_b1, hf_freq, hf_w2, hf_b2, hf_w3, hy_bias, cf_conv_w, cf_conv_b, cf_ln_g, cf_ln_b, lam_q1, lam_k1, lam_q2, lam_k2, subln_g, w_out, ln1_g, ln1_b, w_mlp1, w_mlp2, ln2_g, ln2_b):
    batch, seq, _ = x_prompt.shape
    dec_batch, dec_seq, _ = x_sample.shape
    assert 1 + dec_batch <= COND_ROWS
    assert ROW_TILE % seq == 0 and (batch * seq) % ROW_TILE == 0
    assert dec_seq % ROW_TILE == 0 and dec_seq % Q_TILE == 0

    cond = jnp.concatenate(
        [c_ctx[None, :], c, jnp.zeros((COND_ROWS - 1 - dec_batch, D_MODEL), F32)], axis=0)
    ada4 = _ada_call(cond, w_ada, b_ada).reshape(DEPTH, COND_ROWS, N_ADA, D_MODEL)

    row_vec = lambda a: a.reshape(DEPTH, 1, a.shape[-1])
    wts = dict(
        w_in=w_in.astype(BF16), w_out=w_out.astype(BF16),
        w_mlp1=w_mlp1.astype(BF16), w_mlp2=w_mlp2.astype(BF16),
        hy_conv_w=hy_conv_w, hy_conv_b=row_vec(hy_conv_b), hy_bias=row_vec(hy_bias),
        cf_conv_w=cf_conv_w, cf_conv_b=row_vec(cf_conv_b),
        cf_ln_g=row_vec(cf_ln_g), cf_ln_b=row_vec(cf_ln_b),
        lam_q1=row_vec(lam_q1), lam_k1=row_vec(lam_k1), lam_q2=row_vec(lam_q2), lam_k2=row_vec(lam_k2),
        subln_g=row_vec(subln_g),
        ln1_g=row_vec(ln1_g), ln1_b=row_vec(ln1_b), ln2_g=row_vec(ln2_g), ln2_b=row_vec(ln2_b))

    rope_tabs = tuple(jnp.asarray(t) for t in _rope_tables(dec_seq))
    cache_kt = jnp.transpose(cache_k, (0, 1, 3, 4, 2))
    dec_tiles_per_seq = dec_seq // ROW_TILE

    xp = x_prompt.reshape(batch * seq, D_MODEL)
    xs = x_sample.reshape(dec_batch * dec_seq, D_MODEL)
    new_caches = ()
    filt = (hf_w1, hf_b1, hf_freq, hf_w2, hf_b2, hf_w3)
    hy_ctx = _filter_call(seq, *filt)
    hy_dec = _filter_call(dec_seq, *filt)
    fused = lambda n: n <= FUSED_MIXER_MAX_SEQ and ROW_TILE % n == 0
    cf_ctx = _conv_filter_call(seq, cf_conv_w) if fused(seq) else None
    cf_dec = _conv_filter_call(dec_seq, cf_conv_w) if fused(dec_seq) else None
    for l in range(DEPTH):
        xp, new_caches = _layer(xp, batch, seq, l, ada4, wts, hy_ctx, cf_ctx, lambda i: 0, None, None,
                                new_caches)
        xs, _ = _layer(xs, dec_batch, dec_seq, l, ada4, wts, hy_dec, cf_dec,
                       lambda i: 1 + i // dec_tiles_per_seq, rope_tabs, (cache_kt, cache_v), None)
    new_cache_kt, new_cache_v = new_caches
    new_cache_k = jnp.transpose(new_cache_kt, (0, 1, 4, 2, 3))
    return (xp.reshape(batch, seq, D_MODEL), xs.reshape(dec_batch, dec_seq, D_MODEL),
            new_cache_k, new_cache_v)
```

```python
import functools
import math

import numpy as np
import jax
import jax.numpy as jnp
from jax import lax
from jax.experimental import pallas as pl
from jax.experimental.pallas import tpu as pltpu

D_MODEL = 1024
DEPTH = 2
GRID_W = 64
HY_WIDTH = D_MODEL // 4
CF_WIDTH = D_MODEL // 4
ATT_WIDTH = D_MODEL // 2
DIFF_HEAD_DIM = 64
DIFF_HEADS = ATT_WIDTH // (2 * DIFF_HEAD_DIM)
HEAD_PAIR = 2 * DIFF_HEAD_DIM
MIX_WIDTH = HY_WIDTH + CF_WIDTH + ATT_WIDTH
IN_WIDTH = 3 * HY_WIDTH + 2 * CF_WIDTH + 3 * ATT_WIDTH
HY_SHORT_K = 3
HY_FILTER_EMB = 33
HY_FILTER_HIDDEN = 64
HY_FAST_DECAY_PCT = 0.3
HY_SLOW_DECAY_PCT = 1.5
HY_DECAY_TARGET = 1e-2
CF_CONV_K = 31
D_FF = 4 * D_MODEL
ROPE_BASE = 10000.0
AX_DIM = DIFF_HEAD_DIM // 2
DEEPNORM_ALPHA = (2 * DEPTH) ** 0.25
LN_EPS = 1e-5
N_ADA = 6

LANES = 128
SUBLANES = 8
COND_ROWS = SUBLANES
VMEM_LIMIT = 56 * 1024 * 1024

ROW_TILE = 512
ADA_COL_TILE = 3072
Q_TILE = 512
MIXER_ROWS = 1024
HYENA_LONG_SEQS = 2
FUSED_MIXER_MAX_SEQ = 256
CF_ROW_CHUNK = 128

F32 = jnp.float32
BF16 = jnp.bfloat16


def _lambda_init(l):
    return 0.8 - 0.6 * math.exp(-0.3 * l)


def _params(**kw):
    return pltpu.CompilerParams(vmem_limit_bytes=VMEM_LIMIT, **kw)


def _const_spec(shape):
    zeros = (0,) * len(shape)
    return pl.BlockSpec(shape, lambda *_: zeros, pipeline_mode=pl.Buffered(1))


def _layer_spec(shape, layer):
    zeros = (0,) * len(shape)
    return pl.BlockSpec((None,) + tuple(shape), lambda *_: (layer,) + zeros,
                        pipeline_mode=pl.Buffered(1))


def _dot(a, b):
    return jnp.dot(a, b, preferred_element_type=F32)


def _dot_f32(a, b):
    return jnp.dot(a, b, preferred_element_type=F32, precision=lax.Precision.HIGHEST)


def _split_bf16(x):
    hi = x.astype(BF16)
    return hi, (x - hi.astype(F32)).astype(BF16)


def _layer_norm_rows(x, g, b):
    mu = jnp.mean(x, axis=-1, keepdims=True)
    xc = x - mu
    var = jnp.mean(xc * xc, axis=-1, keepdims=True)
    return xc * lax.rsqrt(var + LN_EPS) * g + b


def _seqs_per_step(batch, seq):
    return max(1, min(batch, MIXER_ROWS // seq))


@functools.lru_cache(maxsize=None)
def _dft_tables(L):
    n = 2 * L
    k = np.arange(L, dtype=np.float64)[:, None]
    s = np.arange(L, dtype=np.float64)[None, :]
    ang = 2.0 * np.pi * k * s / n
    fwd = np.concatenate([np.cos(ang), -np.sin(ang)], axis=0)
    fwd[L, :] = np.cos(np.pi * np.arange(L))
    t = np.arange(L, dtype=np.float64)[:, None]
    kk = np.arange(L, dtype=np.float64)[None, :]
    ang_i = 2.0 * np.pi * t * kk / n
    inv_re = (2.0 / n) * np.cos(ang_i)
    inv_re[:, 0] = 1.0 / n
    inv_im = -(2.0 / n) * np.sin(ang_i)
    inv_im[:, 0] = np.cos(np.pi * np.arange(L)) / n
    inv = np.concatenate([inv_re, inv_im], axis=1)
    return fwd.astype(np.float32), inv.astype(np.float32)


@functools.lru_cache(maxsize=None)
def _filter_tables(L):
    bands = (HY_FILTER_EMB - 1) // 2
    t = np.linspace(0.0, 1.0, L)[:, None]
    w = (2.0 * np.pi / L) * np.arange(L, dtype=np.float64)[:, None]
    fr = np.linspace(1e-4, bands - 1, bands)[None, :]
    feats = np.concatenate([t, np.cos(fr * w), -np.sin(fr * w)], -1)
    feats_p = np.zeros((L, LANES), np.float64)
    feats_p[:, :HY_FILTER_EMB] = feats
    deltas = np.abs(np.linspace(math.log(HY_DECAY_TARGET) / HY_FAST_DECAY_PCT,
                                math.log(HY_DECAY_TARGET) / HY_SLOW_DECAY_PCT, HY_WIDTH))
    decay = np.exp(-t * deltas[None, :])
    return feats_p.astype(np.float32), decay.astype(np.float32)


@functools.lru_cache(maxsize=None)
def _rope_tables(n):
    half = AX_DIM // 2
    pos = np.arange(n)
    row = (pos // GRID_W).astype(np.float64)[:, None]
    col = (pos % GRID_W).astype(np.float64)[:, None]
    inv = ROPE_BASE ** (-np.arange(0, AX_DIM, 2, dtype=np.float64) / AX_DIM)[None, :]
    zero = np.zeros((n, half))
    cr, sr = np.cos(row * inv), np.sin(row * inv)
    cc, sc = np.cos(col * inv), np.sin(col * inv)
    c = np.concatenate([cr, cr, cc, cc], -1)
    s_up = np.concatenate([-sr, zero, -sc, zero], -1)
    s_dn = np.concatenate([zero, sr, zero, sc], -1)
    reps = LANES // DIFF_HEAD_DIM
    tile = lambda a: np.tile(a, (1, reps)).astype(np.float32)
    return tile(c), tile(s_up), tile(s_dn)


def _ada_kernel(cond_ref, w_ref, b_ref, o_ref):
    c = cond_ref[...]
    s = c * jax.nn.sigmoid(c)
    o_ref[0] = _dot(s.astype(BF16), w_ref[0].astype(BF16)) + b_ref[0]


def _ada_call(cond, w_ada, b_ada):
    n_out = N_ADA * D_MODEL
    return pl.pallas_call(
        _ada_kernel,
        grid=(DEPTH, n_out // ADA_COL_TILE),
        in_specs=[
            pl.BlockSpec((COND_ROWS, D_MODEL), lambda l, j: (0, 0)),
            pl.BlockSpec((1, D_MODEL, ADA_COL_TILE), lambda l, j: (l, 0, j)),
            pl.BlockSpec((1, 1, ADA_COL_TILE), lambda l, j: (l, 0, j)),
        ],
        out_specs=pl.BlockSpec((1, COND_ROWS, ADA_COL_TILE), lambda l, j: (l, 0, j)),
        out_shape=jax.ShapeDtypeStruct((DEPTH, COND_ROWS, n_out), F32),
        compiler_params=_params(dimension_semantics=("arbitrary", "arbitrary")),
        name="ada",
    )(cond, w_ada, b_ada.reshape(DEPTH, 1, n_out))


def _rope_lanes(x, c, s_up, s_dn):
    outs = []
    for p in range(ATT_WIDTH // LANES):
        xb = x[:, p * LANES:(p + 1) * LANES]
        up = pltpu.roll(xb, LANES - AX_DIM // 2, 1)
        dn = pltpu.roll(xb, AX_DIM // 2, 1)
        outs.append(xb * c + up * s_up + dn * s_dn)
    return jnp.concatenate(outs, axis=1)


Q_SCALE = DIFF_HEAD_DIM ** -0.5 * math.log2(math.e)


def _inproj_kernel(*refs, names, seq, cache_slot):
    r = dict(zip(names, refs))
    sh1 = r["ada"][0, 0, 0:1, :]
    sc1 = r["ada"][0, 0, 1:2, :]
    h = (r["x"][...] * (1.0 + sc1) + sh1).astype(BF16)
    w_ref = r["w"]
    o1 = 3 * HY_WIDTH
    o2 = o1 + 2 * CF_WIDTH
    o3 = o2 + ATT_WIDTH
    o4 = o3 + ATT_WIDTH
    hy = _dot(h, w_ref[:, 0:o1])
    cf = _dot(h, w_ref[:, o1:o2])
    q = _dot(h, w_ref[:, o2:o3])
    k = _dot(h, w_ref[:, o3:o4])
    v = _dot(h, w_ref[:, o4:IN_WIDTH])
    if "yhy" in r:
        seqs = hy.shape[0] // seq
        x0s, us, tabs = [], [], []
        for b in range(seqs):
            x0, u = _hyena_front(hy[b * seq:(b + 1) * seq], r["hy_pad"].at[b], r["hy_cw"], r["hy_cb"])
            x0s.append(x0)
            us.append(u)
            tabs.append((r["hy_ga"][...], r["hy_gbc"][...], r["hy_gd"][...]))
        for b in range(seqs):
            c = cf[b * seq:(b + 1) * seq]
            us.append(c[:, :CF_WIDTH] * jax.nn.sigmoid(c[:, CF_WIDTH:]))
            tabs.append((r["cf_ga"][...], r["cf_gbc"][...], r["cf_gd"][...]))
        ys = _dft_conv(us, tabs, r["fwd"], r["inv"])
        for b in range(seqs):
            rows = slice(b * seq, (b + 1) * seq)
            r["yhy"][rows, :] = (x0s[b] * (ys[b] + us[b] * r["hy_bias"][...])).astype(BF16)
            y = _layer_norm_rows(ys[seqs + b] + r["cf_cb"][...], r["cf_g"][...], r["cf_b"][...])
            r["ycf"][rows, :] = (y * jax.nn.sigmoid(y)).astype(BF16)
    else:
        r["hy"][...] = hy
        r["cf"][...] = cf
    if "rope_c" in r:
        c, su, sd = r["rope_c"][...], r["rope_up"][...], r["rope_dn"][...]
        q = _rope_lanes(q, c, su, sd)
        k = _rope_lanes(k, c, su, sd)
    r["q"][...] = (q * Q_SCALE).astype(BF16)
    r["k"][...] = k.astype(BF16)
    r["v"][...] = v.astype(BF16)
    if "kc" in r:
        kc_ref, vc_ref = r["kc"], r["vc"]
        for b in range(kc_ref.shape[0]):
            rows = slice(b * seq, (b + 1) * seq)
            for slot in range(kc_ref.shape[1]):
                if slot == cache_slot:
                    kc_ref[b, slot] = k[rows].T.reshape(2 * DIFF_HEADS, DIFF_HEAD_DIM, seq)
                    vc_ref[b, slot] = v[rows].reshape(seq, DIFF_HEADS, 2 * DIFF_HEAD_DIM)
                else:
                    kc_ref[b, slot] = jnp.zeros(kc_ref.shape[2:], F32)
                    vc_ref[b, slot] = jnp.zeros(vc_ref.shape[2:], F32)


def _inproj_call(x2d, ada4, wts, layer, batch, seq, cond_row_of_tile, rope_tabs, caches, mix_tabs):
    T = x2d.shape[0]
    rows = lambda w: pl.BlockSpec((ROW_TILE, w), lambda i: (i, 0))
    ins = [
        ("x", x2d, rows(D_MODEL)),
        ("ada", ada4, pl.BlockSpec((1, 1, N_ADA, D_MODEL), lambda i: (layer, cond_row_of_tile(i), 0, 0))),
        ("w", wts["w_in"], _layer_spec((D_MODEL, IN_WIDTH), layer)),
    ]
    outs = []
    scratch = []
    if rope_tabs is not None:
        tiles_per_seq = seq // ROW_TILE
        for name, tab in zip(("rope_c", "rope_up", "rope_dn"), rope_tabs):
            ins.append((name, tab, pl.BlockSpec((ROW_TILE, LANES), lambda i: (i % tiles_per_seq, 0))))
    if mix_tabs is not None:
        hy_tabs, cf_tabs = mix_tabs
        fwd, inv = _dft_tables(seq)
        ins.append(("fwd", jnp.asarray(fwd).astype(BF16), _const_spec((2 * seq, seq))))
        ins.append(("inv", jnp.asarray(inv).astype(BF16), _const_spec((seq, 2 * seq))))
        for name, tab in zip(("hy_ga", "hy_gbc", "hy_gd"), hy_tabs):
            ins.append((name, tab, _layer_spec((seq, HY_WIDTH), layer)))
        for name, tab in zip(("cf_ga", "cf_gbc", "cf_gd"), cf_tabs):
            ins.append((name, tab, _layer_spec((seq, CF_WIDTH), layer)))
        ins.append(("hy_cw", wts["hy_conv_w"], _layer_spec((HY_SHORT_K, 3 * HY_WIDTH), layer)))
        ins.append(("hy_cb", wts["hy_conv_b"], _layer_spec((1, 3 * HY_WIDTH), layer)))
        ins.append(("hy_bias", wts["hy_bias"], _layer_spec((1, HY_WIDTH), layer)))
        for name, key in (("cf_cb", "cf_conv_b"), ("cf_g", "cf_ln_g"), ("cf_b", "cf_ln_b")):
            ins.append((name, wts[key], _layer_spec((1, CF_WIDTH), layer)))
        outs.append(("yhy", rows(HY_WIDTH), jax.ShapeDtypeStruct((T, HY_WIDTH), BF16)))
        outs.append(("ycf", rows(CF_WIDTH), jax.ShapeDtypeStruct((T, CF_WIDTH), BF16)))
        scratch.append(("hy_pad", pltpu.VMEM((ROW_TILE // seq, seq + 2 * SUBLANES, 3 * HY_WIDTH), F32)))
    else:
        outs.append(("hy", rows(3 * HY_WIDTH), jax.ShapeDtypeStruct((T, 3 * HY_WIDTH), F32)))
        outs.append(("cf", rows(2 * CF_WIDTH), jax.ShapeDtypeStruct((T, 2 * CF_WIDTH), F32)))
    for name in ("q", "k", "v"):
        outs.append((name, rows(ATT_WIDTH), jax.ShapeDtypeStruct((T, ATT_WIDTH), BF16)))
    aliases = {}
    cache_slot = 0
    if caches is not None:
        seqs = ROW_TILE // seq
        first_call = len(caches) == 0
        cache_slot = layer if first_call else 0
        shapes = (("kc", (batch, DEPTH, 2 * DIFF_HEADS, DIFF_HEAD_DIM, seq)),
                  ("vc", (batch, DEPTH, seq, DIFF_HEADS, 2 * DIFF_HEAD_DIM)))
        for j, (name, shape) in enumerate(shapes):
            if first_call:
                spec = pl.BlockSpec((seqs, DEPTH) + shape[2:], lambda i: (i, 0, 0, 0, 0))
            else:
                spec = pl.BlockSpec((seqs, 1) + shape[2:], lambda i: (i, layer, 0, 0, 0))
                aliases[len(ins)] = len(outs)
                ins.append((name + "_in", caches[j], pl.BlockSpec(memory_space=pl.ANY)))
            outs.append((name, spec, jax.ShapeDtypeStruct(shape, F32)))
    names = tuple(n for n, _, _ in ins) + tuple(n for n, _, _ in outs) + tuple(n for n, _ in scratch)
    results = pl.pallas_call(
        functools.partial(_inproj_kernel, names=names, seq=seq, cache_slot=cache_slot),
        grid=(T // ROW_TILE,),
        in_specs=[spec for _, _, spec in ins],
        out_specs=[spec for _, spec, _ in outs],
        out_shape=[shape for _, _, shape in outs],
        scratch_shapes=[s for _, s in scratch],
        input_output_aliases=aliases,
        compiler_params=_params(dimension_semantics=("arbitrary",)),
        name="inproj",
    )(*[a for _, a, _ in ins])
    return dict(zip((n for n, _, _ in outs), results))


def _store_spectrum_tables(spec, layer, ga_ref, gbc_ref, gd_ref):
    L = spec.shape[0] // 2
    W = spec.shape[1] // 2
    row = lax.broadcasted_iota(jnp.int32, (L, W), 0)
    p = spec[:L]
    q = spec[L:]
    g_re = p[:, :W] + p[:, W:]
    g_im = q[:, :W] - q[:, W:]
    nyq = q[0:1, :W] + q[0:1, W:]
    ga_ref[layer] = g_re
    gbc_ref[layer] = jnp.where(row == 0, 0.0, g_im)
    gd_ref[layer] = jnp.where(row == 0, nyq, g_re)


def _filter_kernel(feats_ref, w1_ref, b1_ref, fq_ref, w2_ref, b2_ref, w3_ref, decay_ref,
                   fwd_hi_ref, fwd_lo_ref, ga_ref, gbc_ref, gd_ref):
    L = feats_ref.shape[0]
    decay = decay_ref[...]
    row = lax.broadcasted_iota(jnp.int32, (L, HY_WIDTH), 0)
    taps = []
    for l in range(DEPTH):
        fq = fq_ref[l]
        hid = jnp.sin(fq * (_dot_f32(feats_ref[...], w1_ref[l]) + b1_ref[l]))
        hid = jnp.sin(fq * (_dot_f32(hid, w2_ref[l]) + b2_ref[l]))
        hf = _dot_f32(hid, w3_ref[l])
        taps.append(hf[:, :HY_WIDTH] * decay)
        taps.append(jnp.where(row == 0, 0.0, hf[:, HY_WIDTH:] * decay))
    h_hi, h_lo = _split_bf16(jnp.concatenate(taps, axis=1))
    fwd_hi = fwd_hi_ref[...]
    spec = _dot(fwd_hi, h_hi) + (_dot(fwd_hi, h_lo) + _dot(fwd_lo_ref[...], h_hi))
    for l in range(DEPTH):
        _store_spectrum_tables(spec[:, 2 * l * HY_WIDTH:2 * (l + 1) * HY_WIDTH], l, ga_ref, gbc_ref, gd_ref)


def _pad_to(a, shape):
    return jnp.pad(a, [(0, s - d) for s, d in zip(shape, a.shape)])


def _filter_call(L, w1, b1, fq, w2, b2, w3):
    feats, decay = _filter_tables(L)
    fwd, _ = _dft_tables(L)
    row_vec = lambda a: _pad_to(a.reshape(DEPTH, 1, -1), (DEPTH, 1, LANES))
    args = [
        jnp.asarray(feats),
        _pad_to(w1, (DEPTH, LANES, LANES)),
        row_vec(b1),
        row_vec(fq),
        _pad_to(w2, (DEPTH, LANES, LANES)),
        row_vec(b2),
        _pad_to(w3, (DEPTH, LANES, 2 * HY_WIDTH)),
        jnp.asarray(decay),
        *_split_bf16(jnp.asarray(fwd)),
    ]
    out = jax.ShapeDtypeStruct((DEPTH, L, HY_WIDTH), F32)
    return pl.pallas_call(
        _filter_kernel,
        out_shape=[out, out, out],
        compiler_params=_params(),
        name="hyena_filter",
    )(*args)


def _conv_filter_kernel(taps_ref, fwd_ref, ga_ref, gbc_ref, gd_ref):
    spec = _dot_f32(fwd_ref[...], taps_ref[...])
    for l in range(DEPTH):
        _store_spectrum_tables(spec[:, 2 * l * CF_WIDTH:2 * (l + 1) * CF_WIDTH], l, ga_ref, gbc_ref, gd_ref)


def _conv_filter_call(L, conv_w):
    half = CF_CONV_K // 2
    fwd, _ = _dft_tables(L)
    lag_pos = conv_w[:, half::-1, :]
    lag_neg = _pad_to(conv_w[:, half + 1:, :], (DEPTH, half + 1, CF_WIDTH))
    lag_neg = jnp.roll(lag_neg, 1, axis=1)
    taps = jnp.concatenate([lag_pos, lag_neg], axis=2)
    taps = _pad_to(taps, (DEPTH, LANES, 2 * CF_WIDTH))
    taps = jnp.transpose(taps, (1, 0, 2)).reshape(LANES, DEPTH * 2 * CF_WIDTH)
    out = jax.ShapeDtypeStruct((DEPTH, L, CF_WIDTH), F32)
    return pl.pallas_call(
        _conv_filter_kernel,
        out_shape=[out, out, out],
        compiler_params=_params(),
        name="conv_filter",
    )(taps, jnp.asarray(fwd[:, :LANES]))


def _dft_conv(us, tabs, fwd_ref, inv_ref):
    L, W = us[0].shape
    u_all = jnp.concatenate(us, axis=1).astype(BF16)
    spec = _dot(fwd_ref[...], u_all)
    y_re, y_im = [], []
    for s, (ga, gbc, gd) in enumerate(tabs):
        p = spec[:L, s * W:(s + 1) * W]
        q = spec[L:, s * W:(s + 1) * W]
        y_re.append(p * ga - q * gbc)
        y_im.append(p * gbc + q * gd)
    y_spec = jnp.concatenate(
        [jnp.concatenate(y_re, axis=1), jnp.concatenate(y_im, axis=1)], axis=0).astype(BF16)
    y = _dot(inv_ref[...], y_spec)
    return [y[:, s * W:(s + 1) * W] for s in range(len(us))]


def _hyena_front(hy, pad_ref, cw_ref, cb_ref):
    L, width = hy.shape
    zeros = jnp.zeros((SUBLANES, width), F32)
    pad_ref[0:SUBLANES, :] = zeros
    pad_ref[SUBLANES + L:2 * SUBLANES + L, :] = zeros
    pad_ref[SUBLANES:SUBLANES + L, :] = hy
    conv = cb_ref[...]
    for j in range(HY_SHORT_K):
        start = SUBLANES + j - HY_SHORT_K // 2
        conv = conv + cw_ref[j:j + 1, :] * pad_ref[start:start + L, :]
    return conv[:, :HY_WIDTH], conv[:, HY_WIDTH:2 * HY_WIDTH] * conv[:, 2 * HY_WIDTH:]


def _hyena_kernel(hy_ref, cw_ref, cb_ref, fwd_ref, inv_ref, ga_ref, gbc_ref, gd_ref, hb_ref,
                  o_ref, pad_ref):
    nb = hy_ref.shape[0]
    x0s, us = zip(*[_hyena_front(hy_ref[s], pad_ref.at[s], cw_ref, cb_ref) for s in range(nb)])
    tabs = [(ga_ref[...], gbc_ref[...], gd_ref[...])] * nb
    ys = _dft_conv(us, tabs, fwd_ref, inv_ref)
    for s in range(nb):
        o_ref[s] = (x0s[s] * (ys[s] + us[s] * hb_ref[...])).astype(o_ref.dtype)


def _hyena_call(hy3, conv_w, conv_b, g_tabs, hy_bias, layer):
    B, L, width = hy3.shape
    nb = _seqs_per_step(B, L) if L < MIXER_ROWS else min(B, HYENA_LONG_SEQS)
    fwd, inv = _dft_tables(L)
    fwd_bf = jnp.asarray(fwd).astype(BF16)
    inv_bf = jnp.asarray(inv).astype(BF16)
    tab = _layer_spec((L, HY_WIDTH), layer)
    return pl.pallas_call(
        _hyena_kernel,
        grid=(B // nb,),
        in_specs=[
            pl.BlockSpec((nb, L, width), lambda b: (b, 0, 0)),
            _layer_spec((HY_SHORT_K, width), layer),
            _layer_spec((1, width), layer),
            _const_spec((2 * L, L)),
            _const_spec((L, 2 * L)),
            tab, tab, tab,
            _layer_spec((1, HY_WIDTH), layer),
        ],
        out_specs=pl.BlockSpec((nb, L, HY_WIDTH), lambda b: (b, 0, 0)),
        out_shape=jax.ShapeDtypeStruct((B, L, HY_WIDTH), BF16),
        scratch_shapes=[pltpu.VMEM((nb, L + 2 * SUBLANES, width), F32)],
        compiler_params=_params(dimension_semantics=("arbitrary",)),
        name="hyena",
    )(hy3, conv_w, conv_b, fwd_bf, inv_bf, *g_tabs, hy_bias)


CF_PAD = 2 * SUBLANES


def _conformer_kernel(cf_ref, cw_ref, cb_ref, g_ref, b_ref, o_ref, pad_ref):
    nb, L, _ = cf_ref.shape
    padded = L + 2 * CF_PAD
    shifted = padded - SUBLANES
    zeros = jnp.zeros((CF_PAD, CF_WIDTH), F32)
    half = CF_CONV_K // 2
    for s in range(nb):
        cf = cf_ref[s]
        pad_ref[s, 0, 0:CF_PAD, :] = zeros
        pad_ref[s, 0, CF_PAD + L:padded, :] = zeros
        pad_ref[s, 0, CF_PAD:CF_PAD + L, :] = cf[:, :CF_WIDTH] * jax.nn.sigmoid(cf[:, CF_WIDTH:])
        for r in range(1, SUBLANES):
            pad_ref[s, r, 0:shifted, :] = pad_ref[s, 0, r:r + shifted, :]
        for c0 in range(0, L, CF_ROW_CHUNK):
            acc = jnp.zeros((CF_ROW_CHUNK, CF_WIDTH), F32) + cb_ref[...]
            for j in range(CF_CONV_K):
                off = CF_PAD + j - half
                start = c0 + (off // SUBLANES) * SUBLANES
                acc = acc + cw_ref[j:j + 1, :] * pad_ref[s, off % SUBLANES, start:start + CF_ROW_CHUNK, :]
            y = _layer_norm_rows(acc, g_ref[...], b_ref[...])
            o_ref[s, c0:c0 + CF_ROW_CHUNK, :] = (y * jax.nn.sigmoid(y)).astype(o_ref.dtype)


def _conformer_call(cf3, conv_w, conv_b, ln_g, ln_b, layer):
    B, L, width = cf3.shape
    nb = _seqs_per_step(B, L)
    row = _layer_spec((1, CF_WIDTH), layer)
    return pl.pallas_call(
        _conformer_kernel,
        grid=(B // nb,),
        in_specs=[
            pl.BlockSpec((nb, L, width), lambda b: (b, 0, 0)),
            _layer_spec((CF_CONV_K, CF_WIDTH), layer),
            row, row, row,
        ],
        out_specs=pl.BlockSpec((nb, L, CF_WIDTH), lambda b: (b, 0, 0)),
        out_shape=jax.ShapeDtypeStruct((B, L, CF_WIDTH), BF16),
        scratch_shapes=[pltpu.VMEM((nb, SUBLANES, L + 2 * CF_PAD, CF_WIDTH), F32)],
        compiler_params=_params(dimension_semantics=("arbitrary",)),
        name="conformer",
    )(cf3, conv_w, conv_b, ln_g, ln_b)


def _attn_kernel(*refs, lambda_init, past):
    if past:
        (q_ref, k_ref, v_ref, ck_ref, cv_ref, lq1, lk1, lq2, lk2, g_ref, o_ref, k_all, v_aug) = refs
    else:
        (q_ref, k_ref, v_ref, lq1, lk1, lq2, lk2, g_ref, o_ref) = refs
    nb, tq, _ = q_ref.shape

    if past:
        keys = v_aug.shape[1]

        @pl.when(pl.program_id(1) == 0)
        def _():
            ones = jnp.ones((keys, HEAD_PAIR), BF16)
            for s in range(nb):
                k_all[s, 0:past, :] = ck_ref[s, 0].reshape(ATT_WIDTH, past).T.astype(BF16)
                k_all[s, past:keys, :] = k_ref[s]
                v_old = cv_ref[s, 0].reshape(past, ATT_WIDTH).astype(BF16)
                for h in range(DIFF_HEADS):
                    cols = slice(h * HEAD_PAIR, (h + 1) * HEAD_PAIR)
                    base = 2 * h * HEAD_PAIR
                    v_aug[s, 0:past, base:base + HEAD_PAIR] = v_old[:, cols]
                    v_aug[s, past:keys, base:base + HEAD_PAIR] = v_ref[s, :, cols]
                    v_aug[s, :, base + HEAD_PAIR:base + 2 * HEAD_PAIR] = ones

    lam = (jnp.exp(jnp.sum(lq1[...] * lk1[...], axis=-1, keepdims=True))
           - jnp.exp(jnp.sum(lq2[...] * lk2[...], axis=-1, keepdims=True)) + lambda_init)
    lane = lax.broadcasted_iota(jnp.int32, (tq, HEAD_PAIR), 1)
    first = lane < DIFF_HEAD_DIM
    nt = (((1,), (1,)), ((), ()))
    gain = g_ref[...] * (1.0 - lambda_init)
    zero = jnp.zeros((), BF16)

    def scores(s, h):
        cols = slice(h * HEAD_PAIR, (h + 1) * HEAD_PAIR)
        q = q_ref[s, :, cols]
        q2 = jnp.concatenate([jnp.where(first, q, zero), jnp.where(first, zero, q)], axis=0)
        kh = k_all[s, :, cols] if past else k_ref[s, :, cols]
        return lax.dot_general(q2, kh, nt, preferred_element_type=F32)

    def normalize_store(s, h, o):
        o = o * lax.rsqrt(jnp.mean(o * o, axis=-1, keepdims=True) + LN_EPS) * gain
        o_ref[s, :, h * HEAD_PAIR:(h + 1) * HEAD_PAIR] = o.astype(o_ref.dtype)

    for s in range(nb):
        if past:
            for h in range(DIFF_HEADS):
                sc = scores(s, h)
                e = jnp.exp2(sc - jnp.max(sc, axis=-1, keepdims=True)).astype(BF16)
                p = _dot(e, v_aug[s, :, 2 * h * HEAD_PAIR:2 * (h + 1) * HEAD_PAIR])
                p0, p1 = p[:tq], p[tq:]
                normalize_store(s, h, p0[:, :HEAD_PAIR] * (1.0 / p0[:, HEAD_PAIR:HEAD_PAIR + 1])
                                - p1[:, :HEAD_PAIR] * (lam / p1[:, HEAD_PAIR:HEAD_PAIR + 1]))
        else:
            sc = jnp.concatenate([scores(s, h) for h in range(DIFF_HEADS)], axis=0)
            e = jnp.exp2(sc - jnp.max(sc, axis=-1, keepdims=True))
            r = 1.0 / jnp.sum(e, axis=-1, keepdims=True)
            for h in range(DIFF_HEADS):
                r0 = slice(2 * h * tq, (2 * h + 1) * tq)
                r1 = slice((2 * h + 1) * tq, (2 * h + 2) * tq)
                a = e[r0] * r[r0] - e[r1] * (lam * r[r1])
                normalize_store(s, h, _dot(a.astype(BF16), v_ref[s, :, h * HEAD_PAIR:(h + 1) * HEAD_PAIR]))


def _attn_call(q3, k3, v3, cache_k, cache_v, layer, lam_vecs, subln_g):
    B, L, _ = q3.shape
    past = 0 if cache_k is None else cache_k.shape[-1]
    nb = _seqs_per_step(B, L)
    tq = min(Q_TILE, L)
    tile = pl.BlockSpec((nb, tq, ATT_WIDTH), lambda b, i: (b, i, 0))
    seq = pl.BlockSpec((nb, L, ATT_WIDTH), lambda b, i: (b, 0, 0))
    in_specs = [tile, seq, seq]
    args = [q3, k3, v3]
    scratch = []
    if past:
        for cache in (cache_k, cache_v):
            in_specs.append(pl.BlockSpec((nb, 1) + cache.shape[2:], lambda b, i: (b, layer, 0, 0, 0)))
            args.append(cache)
        scratch = [pltpu.VMEM((nb, past + L, ATT_WIDTH), BF16),
                   pltpu.VMEM((nb, past + L, 2 * ATT_WIDTH), BF16)]
    in_specs += [_layer_spec((1, DIFF_HEAD_DIM), layer)] * 4 + [_layer_spec((1, HEAD_PAIR), layer)]
    args += list(lam_vecs) + [subln_g]
    return pl.pallas_call(
        functools.partial(_attn_kernel, lambda_init=_lambda_init(layer), past=past),
        grid=(B // nb, L // tq),
        in_specs=in_specs,
        out_specs=tile,
        out_shape=jax.ShapeDtypeStruct((B, L, ATT_WIDTH), BF16),
        scratch_shapes=scratch,
        compiler_params=_params(dimension_semantics=("arbitrary", "arbitrary")),
        name="diff_attn",
    )(*args)


def _mlp_kernel(x_ref, hy_ref, cf_ref, at_ref, ada_ref, wo_ref, w1_ref, w2_ref,
                g1_ref, b1_ref, g2_ref, b2_ref, o_ref):
    gate1 = ada_ref[0, 0, 2:3, :]
    sh2 = ada_ref[0, 0, 3:4, :]
    sc2 = ada_ref[0, 0, 4:5, :]
    gate2 = ada_ref[0, 0, 5:6, :]
    mix = jnp.concatenate([hy_ref[...], cf_ref[...], at_ref[...]], axis=1)
    y = _dot(mix, wo_ref[...])
    x = _layer_norm_rows(DEEPNORM_ALPHA * x_ref[...] + gate1 * y, g1_ref[...], b1_ref[...])
    h = (x * (1.0 + sc2) + sh2).astype(BF16)
    a = jnp.maximum(_dot(h, w1_ref[...]), 0.0)
    m = _dot((a * a).astype(BF16), w2_ref[...])
    o_ref[...] = _layer_norm_rows(DEEPNORM_ALPHA * x + gate2 * m, g2_ref[...], b2_ref[...])


def _mlp_call(x2d, y_hy, y_cf, y_at, ada4, wo_bf, w1_bf, w2_bf, ln1_g, ln1_b, ln2_g, ln2_b,
              layer, cond_row_of_tile):
    T = x2d.shape[0]
    rows = lambda w: pl.BlockSpec((ROW_TILE, w), lambda i: (i, 0))
    vec = _layer_spec((1, D_MODEL), layer)
    return pl.pallas_call(
        _mlp_kernel,
        grid=(T // ROW_TILE,),
        in_specs=[
            rows(D_MODEL), rows(HY_WIDTH), rows(CF_WIDTH), rows(ATT_WIDTH),
            pl.BlockSpec((1, 1, N_ADA, D_MODEL), lambda i: (layer, cond_row_of_tile(i), 0, 0)),
            _layer_spec((MIX_WIDTH, D_MODEL), layer),
            _layer_spec((D_MODEL, D_FF), layer),
            _layer_spec((D_FF, D_MODEL), layer),
            vec, vec, vec, vec,
        ],
        out_specs=rows(D_MODEL),
        out_shape=jax.ShapeDtypeStruct((T, D_MODEL), F32),
        compiler_params=_params(dimension_semantics=("arbitrary",)),
        name="outproj_mlp",
    )(x2d, y_hy, y_cf, y_at, ada4, wo_bf, w1_bf, w2_bf, ln1_g, ln1_b, ln2_g, ln2_b)


def _layer(x2d, batch, seq, layer, ada4, wts, hy_tabs, cf_tabs, cond_row_of_tile, rope_tabs, cache_kv,
           new_caches):
    z = _inproj_call(x2d, ada4, wts, layer, batch, seq, cond_row_of_tile, rope_tabs, new_caches,
                     (hy_tabs, cf_tabs) if cf_tabs is not None else None)
    as_seq = lambda a: a.reshape(batch, seq, a.shape[-1])
    flat = lambda a: a.reshape(batch * seq, a.shape[-1])
    if "yhy" in z:
        y_hy, y_cf = z["yhy"], z["ycf"]
    else:
        y_hy = flat(_hyena_call(as_seq(z["hy"]), wts["hy_conv_w"], wts["hy_conv_b"], hy_tabs,
                                wts["hy_bias"], layer))
        y_cf = flat(_conformer_call(as_seq(z["cf"]), wts["cf_conv_w"], wts["cf_conv_b"], wts["cf_ln_g"],
                                    wts["cf_ln_b"], layer))
    ck, cv = cache_kv if cache_kv is not None else (None, None)
    lam_vecs = (wts["lam_q1"], wts["lam_k1"], wts["lam_q2"], wts["lam_k2"])
    y_at = _attn_call(as_seq(z["q"]), as_seq(z["k"]), as_seq(z["v"]), ck, cv, layer, lam_vecs,
                      wts["subln_g"])
    x_out = _mlp_call(x2d, y_hy, y_cf, flat(y_at), ada4,
                      wts["w_out"], wts["w_mlp1"], wts["w_mlp2"],
                      wts["ln1_g"], wts["ln1_b"], wts["ln2_g"], wts["ln2_b"], layer, cond_row_of_tile)
    return x_out, (z["kc"], z["vc"]) if "kc" in z else None


def kernel(x_prompt, x_sample, cache_k, cache_v, c, c_ctx, w_ada, b_ada, w_in, hy_conv_w, hy_conv_b, hf_w1, hf_b1, hf_freq, hf_w2, hf_b2, hf_w3, hy_bias, cf_conv_w, cf_conv_b, cf_ln_g, cf_ln_b, lam_q1, lam_k1, lam_q2, lam_k2, subln_g, w_out, ln1_g, ln1_b, w_mlp1, w_mlp2, ln2_g, ln2_b):
    batch, seq, _ = x_prompt.shape
    dec_batch, dec_seq, _ = x_sample.shape
    assert 1 + dec_batch <= COND_ROWS
    assert ROW_TILE % seq == 0 and (batch * seq) % ROW_TILE == 0
    assert dec_seq % ROW_TILE == 0 and dec_seq % Q_TILE == 0

    cond = jnp.concatenate(
        [c_ctx[None, :], c, jnp.zeros((COND_ROWS - 1 - dec_batch, D_MODEL), F32)], axis=0)
    ada4 = _ada_call(cond, w_ada, b_ada).reshape(DEPTH, COND_ROWS, N_ADA, D_MODEL)

    row_vec = lambda a: a.reshape(DEPTH, 1, a.shape[-1])
    wts = dict(
        w_in=w_in.astype(BF16), w_out=w_out.astype(BF16),
        w_mlp1=w_mlp1.astype(BF16), w_mlp2=w_mlp2.astype(BF16),
        hy_conv_w=hy_conv_w, hy_conv_b=row_vec(hy_conv_b), hy_bias=row_vec(hy_bias),
        cf_conv_w=cf_conv_w, cf_conv_b=row_vec(cf_conv_b),
        cf_ln_g=row_vec(cf_ln_g), cf_ln_b=row_vec(cf_ln_b),
        lam_q1=row_vec(lam_q1), lam_k1=row_vec(lam_k1), lam_q2=row_vec(lam_q2), lam_k2=row_vec(lam_k2),
        subln_g=row_vec(subln_g),
        ln1_g=row_vec(ln1_g), ln1_b=row_vec(ln1_b), ln2_g=row_vec(ln2_g), ln2_b=row_vec(ln2_b))

    rope_tabs = tuple(jnp.asarray(t) for t in _rope_tables(dec_seq))
    cache_kt = jnp.transpose(cache_k, (0, 1, 3, 4, 2))
    dec_tiles_per_seq = dec_seq // ROW_TILE

    xp = x_prompt.reshape(batch * seq, D_MODEL)
    xs = x_sample.reshape(dec_batch * dec_seq, D_MODEL)
    new_caches = ()
    filt = (hf_w1, hf_b1, hf_freq, hf_w2, hf_b2, hf_w3)
    hy_ctx = _filter_call(seq, *filt)
    hy_dec = _filter_call(dec_seq, *filt)
    fused = lambda n: n <= FUSED_MIXER_MAX_SEQ and ROW_TILE % n == 0
    cf_ctx = _conv_filter_call(seq, cf_conv_w) if fused(seq) else None
    cf_dec = _conv_filter_call(dec_seq, cf_conv_w) if fused(dec_seq) else None
    for l in range(DEPTH):
        xp, new_caches = _layer(xp, batch, seq, l, ada4, wts, hy_ctx, cf_ctx, lambda i: 0, None, None,
                                new_caches)
        xs, _ = _layer(xs, dec_batch, dec_seq, l, ada4, wts, hy_dec, cf_dec,
                       lambda i: 1 + i // dec_tiles_per_seq, rope_tabs, (cache_kt, cache_v), None)
    new_cache_kt, new_cache_v = new_caches
    new_cache_k = jnp.transpose(new_cache_kt, (0, 1, 4, 2, 3))
    return (xp.reshape(batch, seq, D_MODEL), xs.reshape(dec_batch, dec_seq, D_MODEL),
            new_cache_k, new_cache_v)
```

```python
import functools
import math

import numpy as np
import jax
import jax.numpy as jnp
from jax import lax
from jax.experimental import pallas as pl
from jax.experimental.pallas import tpu as pltpu

D_MODEL = 1024
DEPTH = 2
GRID_W = 64
HY_WIDTH = D_MODEL // 4
CF_WIDTH = D_MODEL // 4
ATT_WIDTH = D_MODEL // 2
DIFF_HEAD_DIM = 64
DIFF_HEADS = ATT_WIDTH // (2 * DIFF_HEAD_DIM)
HEAD_PAIR = 2 * DIFF_HEAD_DIM
MIX_WIDTH = HY_WIDTH + CF_WIDTH + ATT_WIDTH
IN_WIDTH = 3 * HY_WIDTH + 2 * CF_WIDTH + 3 * ATT_WIDTH
HY_SHORT_K = 3
HY_FILTER_EMB = 33
HY_FILTER_HIDDEN = 64
HY_FAST_DECAY_PCT = 0.3
HY_SLOW_DECAY_PCT = 1.5
HY_DECAY_TARGET = 1e-2
CF_CONV_K = 31
D_FF = 4 * D_MODEL
ROPE_BASE = 10000.0
AX_DIM = DIFF_HEAD_DIM // 2
DEEPNORM_ALPHA = (2 * DEPTH) ** 0.25
LN_EPS = 1e-5
N_ADA = 6

LANES = 128
SUBLANES = 8
COND_ROWS = SUBLANES
VMEM_LIMIT = 56 * 1024 * 1024

ROW_TILE = 512
ADA_COL_TILE = 1536
Q_TILE = 512
MIXER_ROWS = 1024
FUSED_MIXER_MAX_SEQ = 256

F32 = jnp.float32
BF16 = jnp.bfloat16


def _lambda_init(l):
    return 0.8 - 0.6 * math.exp(-0.3 * l)


def _params(**kw):
    return pltpu.CompilerParams(vmem_limit_bytes=VMEM_LIMIT, **kw)


def _const_spec(shape):
    zeros = (0,) * len(shape)
    return pl.BlockSpec(shape, lambda *_: zeros, pipeline_mode=pl.Buffered(1))


def _layer_spec(shape, layer):
    zeros = (0,) * len(shape)
    return pl.BlockSpec((None,) + tuple(shape), lambda *_: (layer,) + zeros,
                        pipeline_mode=pl.Buffered(1))


def _dot(a, b):
    return jnp.dot(a, b, preferred_element_type=F32)


def _dot_f32(a, b):
    return jnp.dot(a, b, preferred_element_type=F32, precision=lax.Precision.HIGHEST)


def _split_bf16(x):
    hi = x.astype(BF16)
    return hi, (x - hi.astype(F32)).astype(BF16)


def _layer_norm_rows(x, g, b):
    mu = jnp.mean(x, axis=-1, keepdims=True)
    xc = x - mu
    var = jnp.mean(xc * xc, axis=-1, keepdims=True)
    return xc * lax.rsqrt(var + LN_EPS) * g + b


def _seqs_per_step(batch, seq):
    return max(1, min(batch, MIXER_ROWS // seq))


@functools.lru_cache(maxsize=None)
def _dft_tables(L):
    n = 2 * L
    k = np.arange(L, dtype=np.float64)[:, None]
    s = np.arange(L, dtype=np.float64)[None, :]
    ang = 2.0 * np.pi * k * s / n
    fwd = np.concatenate([np.cos(ang), -np.sin(ang)], axis=0)
    fwd[L, :] = np.cos(np.pi * np.arange(L))
    t = np.arange(L, dtype=np.float64)[:, None]
    kk = np.arange(L, dtype=np.float64)[None, :]
    ang_i = 2.0 * np.pi * t * kk / n
    inv_re = (2.0 / n) * np.cos(ang_i)
    inv_re[:, 0] = 1.0 / n
    inv_im = -(2.0 / n) * np.sin(ang_i)
    inv_im[:, 0] = np.cos(np.pi * np.arange(L)) / n
    inv = np.concatenate([inv_re, inv_im], axis=1)
    return fwd.astype(np.float32), inv.astype(np.float32)


@functools.lru_cache(maxsize=None)
def _filter_tables(L):
    bands = (HY_FILTER_EMB - 1) // 2
    t = np.linspace(0.0, 1.0, L)[:, None]
    w = (2.0 * np.pi / L) * np.arange(L, dtype=np.float64)[:, None]
    fr = np.linspace(1e-4, bands - 1, bands)[None, :]
    feats = np.concatenate([t, np.cos(fr * w), -np.sin(fr * w)], -1)
    feats_p = np.zeros((L, LANES), np.float64)
    feats_p[:, :HY_FILTER_EMB] = feats
    deltas = np.abs(np.linspace(math.log(HY_DECAY_TARGET) / HY_FAST_DECAY_PCT,
                                math.log(HY_DECAY_TARGET) / HY_SLOW_DECAY_PCT, HY_WIDTH))
    decay = np.exp(-t * deltas[None, :])
    return feats_p.astype(np.float32), decay.astype(np.float32)


@functools.lru_cache(maxsize=None)
def _rope_tables(n):
    half = AX_DIM // 2
    pos = np.arange(n)
    row = (pos // GRID_W).astype(np.float64)[:, None]
    col = (pos % GRID_W).astype(np.float64)[:, None]
    inv = ROPE_BASE ** (-np.arange(0, AX_DIM, 2, dtype=np.float64) / AX_DIM)[None, :]
    zero = np.zeros((n, half))
    cr, sr = np.cos(row * inv), np.sin(row * inv)
    cc, sc = np.cos(col * inv), np.sin(col * inv)
    c = np.concatenate([cr, cr, cc, cc], -1)
    s_up = np.concatenate([-sr, zero, -sc, zero], -1)
    s_dn = np.concatenate([zero, sr, zero, sc], -1)
    reps = LANES // DIFF_HEAD_DIM
    tile = lambda a: np.tile(a, (1, reps)).astype(np.float32)
    return tile(c), tile(s_up), tile(s_dn)


def _ada_kernel(cond_ref, w_ref, b_ref, o_ref):
    c = cond_ref[...]
    s = c * jax.nn.sigmoid(c)
    o_ref[0] = _dot(s.astype(BF16), w_ref[0].astype(BF16)) + b_ref[0]


def _ada_call(cond, w_ada, b_ada):
    n_out = N_ADA * D_MODEL
    return pl.pallas_call(
        _ada_kernel,
        grid=(DEPTH, n_out // ADA_COL_TILE),
        in_specs=[
            pl.BlockSpec((COND_ROWS, D_MODEL), lambda l, j: (0, 0)),
            pl.BlockSpec((1, D_MODEL, ADA_COL_TILE), lambda l, j: (l, 0, j)),
            pl.BlockSpec((1, 1, ADA_COL_TILE), lambda l, j: (l, 0, j)),
        ],
        out_specs=pl.BlockSpec((1, COND_ROWS, ADA_COL_TILE), lambda l, j: (l, 0, j)),
        out_shape=jax.ShapeDtypeStruct((DEPTH, COND_ROWS, n_out), F32),
        compiler_params=_params(dimension_semantics=("arbitrary", "arbitrary")),
        name="ada",
    )(cond, w_ada, b_ada.reshape(DEPTH, 1, n_out))


def _rope_lanes(x, c, s_up, s_dn):
    outs = []
    for p in range(ATT_WIDTH // LANES):
        xb = x[:, p * LANES:(p + 1) * LANES]
        up = pltpu.roll(xb, LANES - AX_DIM // 2, 1)
        dn = pltpu.roll(xb, AX_DIM // 2, 1)
        outs.append(xb * c + up * s_up + dn * s_dn)
    return jnp.concatenate(outs, axis=1)


Q_SCALE = DIFF_HEAD_DIM ** -0.5 * math.log2(math.e)


def _inproj_kernel(*refs, names, seq, cache_slot):
    r = dict(zip(names, refs))
    sh1 = r["ada"][0, 0, 0:1, :]
    sc1 = r["ada"][0, 0, 1:2, :]
    h = (r["x"][...] * (1.0 + sc1) + sh1).astype(BF16)
    w_ref = r["w"]
    o1 = 3 * HY_WIDTH
    o2 = o1 + 2 * CF_WIDTH
    o3 = o2 + ATT_WIDTH
    o4 = o3 + ATT_WIDTH
    hy = _dot(h, w_ref[:, 0:o1])
    cf = _dot(h, w_ref[:, o1:o2])
    q = _dot(h, w_ref[:, o2:o3])
    k = _dot(h, w_ref[:, o3:o4])
    v = _dot(h, w_ref[:, o4:IN_WIDTH])
    if "yhy" in r:
        seqs = hy.shape[0] // seq
        pieces = [slice(b * seq, (b + 1) * seq) for b in range(seqs)]
        y_hy, y_cf = _mix_sequences([hy[p] for p in pieces], [cf[p] for p in pieces], r)
        for b, p in enumerate(pieces):
            r["yhy"][p, :] = y_hy[b].astype(BF16)
            r["ycf"][p, :] = y_cf[b].astype(BF16)
    else:
        r["hy"][...] = hy
        r["cf"][...] = cf
    if "rope_c" in r:
        c, su, sd = r["rope_c"][...], r["rope_up"][...], r["rope_dn"][...]
        q = _rope_lanes(q, c, su, sd)
        k = _rope_lanes(k, c, su, sd)
    r["q"][...] = (q * Q_SCALE).astype(BF16)
    r["k"][...] = k.astype(BF16)
    r["v"][...] = v.astype(BF16)
    if "kc" in r:
        kc_ref, vc_ref = r["kc"], r["vc"]
        for b in range(kc_ref.shape[0]):
            rows = slice(b * seq, (b + 1) * seq)
            for slot in range(kc_ref.shape[1]):
                if slot == cache_slot:
                    kc_ref[b, slot] = k[rows].T.reshape(2 * DIFF_HEADS, DIFF_HEAD_DIM, seq)
                    vc_ref[b, slot] = v[rows].reshape(seq, DIFF_HEADS, 2 * DIFF_HEAD_DIM)
                else:
                    kc_ref[b, slot] = jnp.zeros(kc_ref.shape[2:], F32)
                    vc_ref[b, slot] = jnp.zeros(vc_ref.shape[2:], F32)


def _inproj_call(x2d, ada4, wts, layer, batch, seq, cond_row_of_tile, rope_tabs, caches, mix_tabs):
    T = x2d.shape[0]
    rows = lambda w: pl.BlockSpec((ROW_TILE, w), lambda i: (i, 0))
    ins = [
        ("x", x2d, rows(D_MODEL)),
        ("ada", ada4, pl.BlockSpec((1, 1, N_ADA, D_MODEL), lambda i: (layer, cond_row_of_tile(i), 0, 0))),
        ("w", wts["w_in"], _layer_spec((D_MODEL, IN_WIDTH), layer)),
    ]
    outs = []
    scratch = []
    if rope_tabs is not None:
        tiles_per_seq = seq // ROW_TILE
        for name, tab in zip(("rope_c", "rope_up", "rope_dn"), rope_tabs):
            ins.append((name, tab, pl.BlockSpec((ROW_TILE, LANES), lambda i: (i % tiles_per_seq, 0))))
    if mix_tabs is not None:
        ins += _mixer_inputs(seq, wts, layer, *mix_tabs)
        outs.append(("yhy", rows(HY_WIDTH), jax.ShapeDtypeStruct((T, HY_WIDTH), BF16)))
        outs.append(("ycf", rows(CF_WIDTH), jax.ShapeDtypeStruct((T, CF_WIDTH), BF16)))
        scratch.append(("hy_pad", pltpu.VMEM((ROW_TILE // seq, seq + 2 * SUBLANES, 3 * HY_WIDTH), F32)))
    else:
        outs.append(("hy", rows(3 * HY_WIDTH), jax.ShapeDtypeStruct((T, 3 * HY_WIDTH), F32)))
        outs.append(("cf", rows(2 * CF_WIDTH), jax.ShapeDtypeStruct((T, 2 * CF_WIDTH), F32)))
    for name in ("q", "k", "v"):
        outs.append((name, rows(ATT_WIDTH), jax.ShapeDtypeStruct((T, ATT_WIDTH), BF16)))
    aliases = {}
    cache_slot = 0
    if caches is not None:
        seqs = ROW_TILE // seq
        first_call = len(caches) == 0
        cache_slot = layer if first_call else 0
        shapes = (("kc", (batch, DEPTH, 2 * DIFF_HEADS, DIFF_HEAD_DIM, seq)),
                  ("vc", (batch, DEPTH, seq, DIFF_HEADS, 2 * DIFF_HEAD_DIM)))
        for j, (name, shape) in enumerate(shapes):
            if first_call:
                spec = pl.BlockSpec((seqs, DEPTH) + shape[2:], lambda i: (i, 0, 0, 0, 0))
            else:
                spec = pl.BlockSpec((seqs, 1) + shape[2:], lambda i: (i, layer, 0, 0, 0))
                aliases[len(ins)] = len(outs)
                ins.append((name + "_in", caches[j], pl.BlockSpec(memory_space=pl.ANY)))
            outs.append((name, spec, jax.ShapeDtypeStruct(shape, F32)))
    names = tuple(n for n, _, _ in ins) + tuple(n for n, _, _ in outs) + tuple(n for n, _ in scratch)
    results = pl.pallas_call(
        functools.partial(_inproj_kernel, names=names, seq=seq, cache_slot=cache_slot),
        grid=(T // ROW_TILE,),
        in_specs=[spec for _, _, spec in ins],
        out_specs=[spec for _, spec, _ in outs],
        out_shape=[shape for _, _, shape in outs],
        scratch_shapes=[s for _, s in scratch],
        input_output_aliases=aliases,
        compiler_params=_params(dimension_semantics=("arbitrary",)),
        name="inproj",
    )(*[a for _, a, _ in ins])
    return dict(zip((n for n, _, _ in outs), results))


def _store_spectrum_tables(spec, layer, ga_ref, gbc_ref, gd_ref):
    L = spec.shape[0] // 2
    W = spec.shape[1] // 2
    row = lax.broadcasted_iota(jnp.int32, (L, W), 0)
    p = spec[:L]
    q = spec[L:]
    g_re = p[:, :W] + p[:, W:]
    g_im = q[:, :W] - q[:, W:]
    nyq = q[0:1, :W] + q[0:1, W:]
    ga_ref[layer] = g_re
    gbc_ref[layer] = jnp.where(row == 0, 0.0, g_im)
    gd_ref[layer] = jnp.where(row == 0, nyq, g_re)


def _filter_kernel(feats_ref, w1_ref, b1_ref, fq_ref, w2_ref, b2_ref, w3_ref, decay_ref,
                   fwd_hi_ref, fwd_lo_ref, ga_ref, gbc_ref, gd_ref):
    L = feats_ref.shape[0]
    decay = decay_ref[...]
    row = lax.broadcasted_iota(jnp.int32, (L, HY_WIDTH), 0)
    taps = []
    for l in range(DEPTH):
        fq = fq_ref[l]
        hid = jnp.sin(fq * (_dot_f32(feats_ref[...], w1_ref[l]) + b1_ref[l]))
        hid = jnp.sin(fq * (_dot_f32(hid, w2_ref[l]) + b2_ref[l]))
        hf = _dot_f32(hid, w3_ref[l])
        taps.append(hf[:, :HY_WIDTH] * decay)
        taps.append(jnp.where(row == 0, 0.0, hf[:, HY_WIDTH:] * decay))
    h_hi, h_lo = _split_bf16(jnp.concatenate(taps, axis=1))
    fwd_hi = fwd_hi_ref[...]
    spec = _dot(fwd_hi, h_hi) + (_dot(fwd_hi, h_lo) + _dot(fwd_lo_ref[...], h_hi))
    for l in range(DEPTH):
        _store_spectrum_tables(spec[:, 2 * l * HY_WIDTH:2 * (l + 1) * HY_WIDTH], l, ga_ref, gbc_ref, gd_ref)


def _pad_to(a, shape):
    return jnp.pad(a, [(0, s - d) for s, d in zip(shape, a.shape)])


def _filter_call(L, w1, b1, fq, w2, b2, w3):
    feats, decay = _filter_tables(L)
    fwd, _ = _dft_tables(L)
    row_vec = lambda a: _pad_to(a.reshape(DEPTH, 1, -1), (DEPTH, 1, LANES))
    args = [
        jnp.asarray(feats),
        _pad_to(w1, (DEPTH, LANES, LANES)),
        row_vec(b1),
        row_vec(fq),
        _pad_to(w2, (DEPTH, LANES, LANES)),
        row_vec(b2),
        _pad_to(w3, (DEPTH, LANES, 2 * HY_WIDTH)),
        jnp.asarray(decay),
        *_split_bf16(jnp.asarray(fwd)),
    ]
    out = jax.ShapeDtypeStruct((DEPTH, L, HY_WIDTH), F32)
    return pl.pallas_call(
        _filter_kernel,
        out_shape=[out, out, out],
        compiler_params=_params(),
        name="hyena_filter",
    )(*args)


def _conv_filter_kernel(taps_ref, fwd_ref, ga_ref, gbc_ref, gd_ref):
    spec = _dot_f32(fwd_ref[...], taps_ref[...])
    for l in range(DEPTH):
        _store_spectrum_tables(spec[:, 2 * l * CF_WIDTH:2 * (l + 1) * CF_WIDTH], l, ga_ref, gbc_ref, gd_ref)


def _conv_filter_call(L, conv_w):
    half = CF_CONV_K // 2
    fwd, _ = _dft_tables(L)
    lag_pos = conv_w[:, half::-1, :]
    lag_neg = _pad_to(conv_w[:, half + 1:, :], (DEPTH, half + 1, CF_WIDTH))
    lag_neg = jnp.roll(lag_neg, 1, axis=1)
    taps = jnp.concatenate([lag_pos, lag_neg], axis=2)
    taps = _pad_to(taps, (DEPTH, LANES, 2 * CF_WIDTH))
    taps = jnp.transpose(taps, (1, 0, 2)).reshape(LANES, DEPTH * 2 * CF_WIDTH)
    out = jax.ShapeDtypeStruct((DEPTH, L, CF_WIDTH), F32)
    return pl.pallas_call(
        _conv_filter_kernel,
        out_shape=[out, out, out],
        compiler_params=_params(),
        name="conv_filter",
    )(taps, jnp.asarray(fwd[:, :LANES]))


def _dft_conv(us, tabs, fwd_ref, inv_ref):
    L, W = us[0].shape
    u_all = jnp.concatenate(us, axis=1).astype(BF16)
    spec = _dot(fwd_ref[...], u_all)
    y_re, y_im = [], []
    for s, (ga, gbc, gd) in enumerate(tabs):
        p = spec[:L, s * W:(s + 1) * W]
        q = spec[L:, s * W:(s + 1) * W]
        y_re.append(p * ga - q * gbc)
        y_im.append(p * gbc + q * gd)
    y_spec = jnp.concatenate(
        [jnp.concatenate(y_re, axis=1), jnp.concatenate(y_im, axis=1)], axis=0).astype(BF16)
    y = _dot(inv_ref[...], y_spec)
    return [y[:, s * W:(s + 1) * W] for s in range(len(us))]


def _hyena_front(hy, pad_ref, cw_ref, cb_ref):
    L, width = hy.shape
    zeros = jnp.zeros((SUBLANES, width), F32)
    pad_ref[0:SUBLANES, :] = zeros
    pad_ref[SUBLANES + L:2 * SUBLANES + L, :] = zeros
    pad_ref[SUBLANES:SUBLANES + L, :] = hy
    conv = cb_ref[...]
    for j in range(HY_SHORT_K):
        start = SUBLANES + j - HY_SHORT_K // 2
        conv = conv + cw_ref[j:j + 1, :] * pad_ref[start:start + L, :]
    return conv[:, :HY_WIDTH], conv[:, HY_WIDTH:2 * HY_WIDTH] * conv[:, 2 * HY_WIDTH:]


def _mix_sequences(hys, cfs, r):
    n = len(hys)
    x0s, us, tabs = [], [], []
    for s in range(n):
        x0, u = _hyena_front(hys[s], r["hy_pad"].at[s], r["hy_cw"], r["hy_cb"])
        x0s.append(x0)
        us.append(u)
        tabs.append((r["hy_ga"][...], r["hy_gbc"][...], r["hy_gd"][...]))
    for s in range(n):
        us.append(cfs[s][:, :CF_WIDTH] * jax.nn.sigmoid(cfs[s][:, CF_WIDTH:]))
        tabs.append((r["cf_ga"][...], r["cf_gbc"][...], r["cf_gd"][...]))
    ys = _dft_conv(us, tabs, r["fwd"], r["inv"])
    y_hy = [x0s[s] * (ys[s] + us[s] * r["hy_bias"][...]) for s in range(n)]
    y_cf = []
    for s in range(n):
        y = _layer_norm_rows(ys[n + s] + r["cf_cb"][...], r["cf_g"][...], r["cf_b"][...])
        y_cf.append(y * jax.nn.sigmoid(y))
    return y_hy, y_cf


def _mixer_inputs(seq, wts, layer, hy_tabs, cf_tabs):
    fwd, inv = _dft_tables(seq)
    ins = [("fwd", jnp.asarray(fwd).astype(BF16), _const_spec((2 * seq, seq))),
           ("inv", jnp.asarray(inv).astype(BF16), _const_spec((seq, 2 * seq)))]
    for name, tab in zip(("hy_ga", "hy_gbc", "hy_gd"), hy_tabs):
        ins.append((name, tab, _layer_spec((seq, HY_WIDTH), layer)))
    for name, tab in zip(("cf_ga", "cf_gbc", "cf_gd"), cf_tabs):
        ins.append((name, tab, _layer_spec((seq, CF_WIDTH), layer)))
    ins.append(("hy_cw", wts["hy_conv_w"], _layer_spec((HY_SHORT_K, 3 * HY_WIDTH), layer)))
    ins.append(("hy_cb", wts["hy_conv_b"], _layer_spec((1, 3 * HY_WIDTH), layer)))
    ins.append(("hy_bias", wts["hy_bias"], _layer_spec((1, HY_WIDTH), layer)))
    for name, key in (("cf_cb", "cf_conv_b"), ("cf_g", "cf_ln_g"), ("cf_b", "cf_ln_b")):
        ins.append((name, wts[key], _layer_spec((1, CF_WIDTH), layer)))
    return ins


def _mixer_kernel(*refs, names):
    r = dict(zip(names, refs))
    nb = r["hy"].shape[0]
    y_hy, y_cf = _mix_sequences([r["hy"][s] for s in range(nb)], [r["cf"][s] for s in range(nb)], r)
    for s in range(nb):
        r["yhy"][s] = y_hy[s].astype(BF16)
        r["ycf"][s] = y_cf[s].astype(BF16)


def _mixer_call(hy3, cf3, wts, layer, hy_tabs, cf_tabs):
    B, L, _ = hy3.shape
    nb = _seqs_per_step(B, L)
    seqs = lambda w: pl.BlockSpec((nb, L, w), lambda b: (b, 0, 0))
    ins = [("hy", hy3, seqs(3 * HY_WIDTH)), ("cf", cf3, seqs(2 * CF_WIDTH))]
    ins += _mixer_inputs(L, wts, layer, hy_tabs, cf_tabs)
    names = tuple(n for n, _, _ in ins) + ("yhy", "ycf", "hy_pad")
    return pl.pallas_call(
        functools.partial(_mixer_kernel, names=names),
        grid=(B // nb,),
        in_specs=[spec for _, _, spec in ins],
        out_specs=[seqs(HY_WIDTH), seqs(CF_WIDTH)],
        out_shape=[jax.ShapeDtypeStruct((B, L, HY_WIDTH), BF16), jax.ShapeDtypeStruct((B, L, CF_WIDTH), BF16)],
        scratch_shapes=[pltpu.VMEM((nb, L + 2 * SUBLANES, 3 * HY_WIDTH), F32)],
        compiler_params=_params(dimension_semantics=("arbitrary",)),
        name="mixers",
    )(*[a for _, a, _ in ins])


def _attn_kernel(*refs, lambda_init, past):
    if past:
        (q_ref, k_ref, v_ref, ck_ref, cv_ref, lq1, lk1, lq2, lk2, g_ref, o_ref, k_all, v_aug) = refs
    else:
        (q_ref, k_ref, v_ref, lq1, lk1, lq2, lk2, g_ref, o_ref) = refs
    nb, tq, _ = q_ref.shape

    if past:
        keys = v_aug.shape[1]

        @pl.when(pl.program_id(1) == 0)
        def _():
            ones = jnp.ones((keys, HEAD_PAIR), BF16)
            for s in range(nb):
                k_all[s, 0:past, :] = ck_ref[s, 0].reshape(ATT_WIDTH, past).T.astype(BF16)
                k_all[s, past:keys, :] = k_ref[s]
                v_old = cv_ref[s, 0].reshape(past, ATT_WIDTH).astype(BF16)
                for h in range(DIFF_HEADS):
                    cols = slice(h * HEAD_PAIR, (h + 1) * HEAD_PAIR)
                    base = 2 * h * HEAD_PAIR
                    v_aug[s, 0:past, base:base + HEAD_PAIR] = v_old[:, cols]
                    v_aug[s, past:keys, base:base + HEAD_PAIR] = v_ref[s, :, cols]
                    v_aug[s, :, base + HEAD_PAIR:base + 2 * HEAD_PAIR] = ones

    lam = (jnp.exp(jnp.sum(lq1[...] * lk1[...], axis=-1, keepdims=True))
           - jnp.exp(jnp.sum(lq2[...] * lk2[...], axis=-1, keepdims=True)) + lambda_init)
    lane = lax.broadcasted_iota(jnp.int32, (tq, HEAD_PAIR), 1)
    first = lane < DIFF_HEAD_DIM
    nt = (((1,), (1,)), ((), ()))
    gain = g_ref[...] * (1.0 - lambda_init)
    zero = jnp.zeros((), BF16)

    def scores(s, h):
        cols = slice(h * HEAD_PAIR, (h + 1) * HEAD_PAIR)
        q = q_ref[s, :, cols]
        q2 = jnp.concatenate([jnp.where(first, q, zero), jnp.where(first, zero, q)], axis=0)
        kh = k_all[s, :, cols] if past else k_ref[s, :, cols]
        return lax.dot_general(q2, kh, nt, preferred_element_type=F32)

    def normalize_store(s, h, o):
        o = o * lax.rsqrt(jnp.mean(o * o, axis=-1, keepdims=True) + LN_EPS) * gain
        o_ref[s, :, h * HEAD_PAIR:(h + 1) * HEAD_PAIR] = o.astype(o_ref.dtype)

    for s in range(nb):
        if past:
            for h in range(DIFF_HEADS):
                sc = scores(s, h)
                e = jnp.exp2(sc - jnp.max(sc, axis=-1, keepdims=True)).astype(BF16)
                p = _dot(e, v_aug[s, :, 2 * h * HEAD_PAIR:2 * (h + 1) * HEAD_PAIR])
                p0, p1 = p[:tq], p[tq:]
                normalize_store(s, h, p0[:, :HEAD_PAIR] * (1.0 / p0[:, HEAD_PAIR:HEAD_PAIR + 1])
                                - p1[:, :HEAD_PAIR] * (lam / p1[:, HEAD_PAIR:HEAD_PAIR + 1]))
        else:
            sc = jnp.concatenate([scores(s, h) for h in range(DIFF_HEADS)], axis=0)
            e = jnp.exp2(sc - jnp.max(sc, axis=-1, keepdims=True))
            r = 1.0 / jnp.sum(e, axis=-1, keepdims=True)
            for h in range(DIFF_HEADS):
                r0 = slice(2 * h * tq, (2 * h + 1) * tq)
                r1 = slice((2 * h + 1) * tq, (2 * h + 2) * tq)
                a = e[r0] * r[r0] - e[r1] * (lam * r[r1])
                normalize_store(s, h, _dot(a.astype(BF16), v_ref[s, :, h * HEAD_PAIR:(h + 1) * HEAD_PAIR]))


def _attn_call(q3, k3, v3, cache_k, cache_v, layer, lam_vecs, subln_g):
    B, L, _ = q3.shape
    past = 0 if cache_k is None else cache_k.shape[-1]
    nb = _seqs_per_step(B, L)
    tq = min(Q_TILE, L)
    tile = pl.BlockSpec((nb, tq, ATT_WIDTH), lambda b, i: (b, i, 0))
    seq = pl.BlockSpec((nb, L, ATT_WIDTH), lambda b, i: (b, 0, 0))
    in_specs = [tile, seq, seq]
    args = [q3, k3, v3]
    scratch = []
    if past:
        for cache in (cache_k, cache_v):
            in_specs.append(pl.BlockSpec((nb, 1) + cache.shape[2:], lambda b, i: (b, layer, 0, 0, 0)))
            args.append(cache)
        scratch = [pltpu.VMEM((nb, past + L, ATT_WIDTH), BF16),
                   pltpu.VMEM((nb, past + L, 2 * ATT_WIDTH), BF16)]
    in_specs += [_layer_spec((1, DIFF_HEAD_DIM), layer)] * 4 + [_layer_spec((1, HEAD_PAIR), layer)]
    args += list(lam_vecs) + [subln_g]
    return pl.pallas_call(
        functools.partial(_attn_kernel, lambda_init=_lambda_init(layer), past=past),
        grid=(B // nb, L // tq),
        in_specs=in_specs,
        out_specs=tile,
        out_shape=jax.ShapeDtypeStruct((B, L, ATT_WIDTH), BF16),
        scratch_shapes=scratch,
        compiler_params=_params(dimension_semantics=("arbitrary", "arbitrary")),
        name="diff_attn",
    )(*args)


def _mlp_kernel(x_ref, hy_ref, cf_ref, at_ref, ada_ref, wo_ref, w1_ref, w2_ref,
                g1_ref, b1_ref, g2_ref, b2_ref, o_ref):
    gate1 = ada_ref[0, 0, 2:3, :]
    sh2 = ada_ref[0, 0, 3:4, :]
    sc2 = ada_ref[0, 0, 4:5, :]
    gate2 = ada_ref[0, 0, 5:6, :]
    mix = jnp.concatenate([hy_ref[...], cf_ref[...], at_ref[...]], axis=1)
    y = _dot(mix, wo_ref[...])
    x = _layer_norm_rows(DEEPNORM_ALPHA * x_ref[...] + gate1 * y, g1_ref[...], b1_ref[...])
    h = (x * (1.0 + sc2) + sh2).astype(BF16)
    a = jnp.maximum(_dot(h, w1_ref[...]), 0.0)
    m = _dot((a * a).astype(BF16), w2_ref[...])
    o_ref[...] = _layer_norm_rows(DEEPNORM_ALPHA * x + gate2 * m, g2_ref[...], b2_ref[...])


def _mlp_call(x2d, y_hy, y_cf, y_at, ada4, wo_bf, w1_bf, w2_bf, ln1_g, ln1_b, ln2_g, ln2_b,
              layer, cond_row_of_tile):
    T = x2d.shape[0]
    rows = lambda w: pl.BlockSpec((ROW_TILE, w), lambda i: (i, 0))
    vec = _layer_spec((1, D_MODEL), layer)
    return pl.pallas_call(
        _mlp_kernel,
        grid=(T // ROW_TILE,),
        in_specs=[
            rows(D_MODEL), rows(HY_WIDTH), rows(CF_WIDTH), rows(ATT_WIDTH),
            pl.BlockSpec((1, 1, N_ADA, D_MODEL), lambda i: (layer, cond_row_of_tile(i), 0, 0)),
            _layer_spec((MIX_WIDTH, D_MODEL), layer),
            _layer_spec((D_MODEL, D_FF), layer),
            _layer_spec((D_FF, D_MODEL), layer),
            vec, vec, vec, vec,
        ],
        out_specs=rows(D_MODEL),
        out_shape=jax.ShapeDtypeStruct((T, D_MODEL), F32),
        compiler_params=_params(dimension_semantics=("arbitrary",)),
        name="outproj_mlp",
    )(x2d, y_hy, y_cf, y_at, ada4, wo_bf, w1_bf, w2_bf, ln1_g, ln1_b, ln2_g, ln2_b)


def _layer(x2d, batch, seq, layer, ada4, wts, hy_tabs, cf_tabs, cond_row_of_tile, rope_tabs, cache_kv,
           new_caches):
    fused = seq <= FUSED_MIXER_MAX_SEQ and ROW_TILE % seq == 0
    z = _inproj_call(x2d, ada4, wts, layer, batch, seq, cond_row_of_tile, rope_tabs, new_caches,
                     (hy_tabs, cf_tabs) if fused else None)
    as_seq = lambda a: a.reshape(batch, seq, a.shape[-1])
    flat = lambda a: a.reshape(batch * seq, a.shape[-1])
    if fused:
        y_hy, y_cf = z["yhy"], z["ycf"]
    else:
        y_hy, y_cf = map(flat, _mixer_call(as_seq(z["hy"]), as_seq(z["cf"]), wts, layer, hy_tabs, cf_tabs))
    ck, cv = cache_kv if cache_kv is not None else (None, None)
    lam_vecs = (wts["lam_q1"], wts["lam_k1"], wts["lam_q2"], wts["lam_k2"])
    y_at = _attn_call(as_seq(z["q"]), as_seq(z["k"]), as_seq(z["v"]), ck, cv, layer, lam_vecs,
                      wts["subln_g"])
    x_out = _mlp_call(x2d, y_hy, y_cf, flat(y_at), ada4,
                      wts["w_out"], wts["w_mlp1"], wts["w_mlp2"],
                      wts["ln1_g"], wts["ln1_b"], wts["ln2_g"], wts["ln2_b"], layer, cond_row_of_tile)
    return x_out, (z["kc"], z["vc"]) if "kc" in z else None


def kernel(x_prompt, x_sample, cache_k, cache_v, c, c_ctx, w_ada, b_ada, w_in, hy_conv_w, hy_conv_b, hf_w1, hf_b1, hf_freq, hf_w2, hf_b2, hf_w3, hy_bias, cf_conv_w, cf_conv_b, cf_ln_g, cf_ln_b, lam_q1, lam_k1, lam_q2, lam_k2, subln_g, w_out, ln1_g, ln1_b, w_mlp1, w_mlp2, ln2_g, ln2_b):
    batch, seq, _ = x_prompt.shape
    dec_batch, dec_seq, _ = x_sample.shape
    assert 1 + dec_batch <= COND_ROWS
    assert ROW_TILE % seq == 0 and (batch * seq) % ROW_TILE == 0
    assert dec_seq % ROW_TILE == 0 and dec_seq % Q_TILE == 0

    cond = jnp.concatenate(
        [c_ctx[None, :], c, jnp.zeros((COND_ROWS - 1 - dec_batch, D_MODEL), F32)], axis=0)
    ada4 = _ada_call(cond, w_ada, b_ada).reshape(DEPTH, COND_ROWS, N_ADA, D_MODEL)

    row_vec = lambda a: a.reshape(DEPTH, 1, a.shape[-1])
    wts = dict(
        w_in=w_in.astype(BF16), w_out=w_out.astype(BF16),
        w_mlp1=w_mlp1.astype(BF16), w_mlp2=w_mlp2.astype(BF16),
        hy_conv_w=hy_conv_w, hy_conv_b=row_vec(hy_conv_b), hy_bias=row_vec(hy_bias),
        cf_conv_w=cf_conv_w, cf_conv_b=row_vec(cf_conv_b),
        cf_ln_g=row_vec(cf_ln_g), cf_ln_b=row_vec(cf_ln_b),
        lam_q1=row_vec(lam_q1), lam_k1=row_vec(lam_k1), lam_q2=row_vec(lam_q2), lam_k2=row_vec(lam_k2),
        subln_g=row_vec(subln_g),
        ln1_g=row_vec(ln1_g), ln1_b=row_vec(ln1_b), ln2_g=row_vec(ln2_g), ln2_b=row_vec(ln2_b))

    rope_tabs = tuple(jnp.asarray(t) for t in _rope_tables(dec_seq))
    cache_kt = jnp.transpose(cache_k, (0, 1, 3, 4, 2))
    dec_tiles_per_seq = dec_seq // ROW_TILE

    xp = x_prompt.reshape(batch * seq, D_MODEL)
    xs = x_sample.reshape(dec_batch * dec_seq, D_MODEL)
    new_caches = ()
    filt = (hf_w1, hf_b1, hf_freq, hf_w2, hf_b2, hf_w3)
    hy_ctx = _filter_call(seq, *filt)
    hy_dec = _filter_call(dec_seq, *filt)
    cf_ctx = _conv_filter_call(seq, cf_conv_w)
    cf_dec = _conv_filter_call(dec_seq, cf_conv_w)
    for l in range(DEPTH):
        xp, new_caches = _layer(xp, batch, seq, l, ada4, wts, hy_ctx, cf_ctx, lambda i: 0, None, None,
                                new_caches)
        xs, _ = _layer(xs, dec_batch, dec_seq, l, ada4, wts, hy_dec, cf_dec,
                       lambda i: 1 + i // dec_tiles_per_seq, rope_tabs, (cache_kt, cache_v), None)
    new_cache_kt, new_cache_v = new_caches
    new_cache_k = jnp.transpose(new_cache_kt, (0, 1, 4, 2, 3))
    return (xp.reshape(batch, seq, D_MODEL), xs.reshape(dec_batch, dec_seq, D_MODEL),
            new_cache_k, new_cache_v)
```

```python
import functools
import math

import numpy as np
import jax
import jax.numpy as jnp
from jax import lax
from jax.experimental import pallas as pl
from jax.experimental.pallas import tpu as pltpu

D_MODEL = 1024
DEPTH = 2
GRID_W = 64
HY_WIDTH = D_MODEL // 4
CF_WIDTH = D_MODEL // 4
ATT_WIDTH = D_MODEL // 2
DIFF_HEAD_DIM = 64
DIFF_HEADS = ATT_WIDTH // (2 * DIFF_HEAD_DIM)
HEAD_PAIR = 2 * DIFF_HEAD_DIM
MIX_WIDTH = HY_WIDTH + CF_WIDTH + ATT_WIDTH
IN_WIDTH = 3 * HY_WIDTH + 2 * CF_WIDTH + 3 * ATT_WIDTH
HY_SHORT_K = 3
HY_FILTER_EMB = 33
HY_FILTER_HIDDEN = 64
HY_FAST_DECAY_PCT = 0.3
HY_SLOW_DECAY_PCT = 1.5
HY_DECAY_TARGET = 1e-2
CF_CONV_K = 31
D_FF = 4 * D_MODEL
ROPE_BASE = 10000.0
AX_DIM = DIFF_HEAD_DIM // 2
DEEPNORM_ALPHA = (2 * DEPTH) ** 0.25
LN_EPS = 1e-5
N_ADA = 6

LANES = 128
SUBLANES = 8
COND_ROWS = SUBLANES
VMEM_LIMIT = 56 * 1024 * 1024

ROW_TILE = 512
ADA_COL_TILE = 1536
Q_TILE = 512
MIXER_ROWS = 1024
HYENA_LONG_SEQS = 2
FUSED_MIXER_MAX_SEQ = 256
CF_ROW_CHUNK = 128

F32 = jnp.float32
BF16 = jnp.bfloat16


def _lambda_init(l):
    return 0.8 - 0.6 * math.exp(-0.3 * l)


def _params(**kw):
    return pltpu.CompilerParams(vmem_limit_bytes=VMEM_LIMIT, **kw)


def _const_spec(shape):
    zeros = (0,) * len(shape)
    return pl.BlockSpec(shape, lambda *_: zeros, pipeline_mode=pl.Buffered(1))


def _layer_spec(shape, layer):
    zeros = (0,) * len(shape)
    return pl.BlockSpec((None,) + tuple(shape), lambda *_: (layer,) + zeros,
                        pipeline_mode=pl.Buffered(1))


def _dot(a, b):
    return jnp.dot(a, b, preferred_element_type=F32)


def _dot_f32(a, b):
    return jnp.dot(a, b, preferred_element_type=F32, precision=lax.Precision.HIGHEST)


def _split_bf16(x):
    hi = x.astype(BF16)
    return hi, (x - hi.astype(F32)).astype(BF16)


def _layer_norm_rows(x, g, b):
    mu = jnp.mean(x, axis=-1, keepdims=True)
    xc = x - mu
    var = jnp.mean(xc * xc, axis=-1, keepdims=True)
    return xc * lax.rsqrt(var + LN_EPS) * g + b


def _seqs_per_step(batch, seq):
    return max(1, min(batch, MIXER_ROWS // seq))


@functools.lru_cache(maxsize=None)
def _dft_tables(L):
    n = 2 * L
    k = np.arange(L, dtype=np.float64)[:, None]
    s = np.arange(L, dtype=np.float64)[None, :]
    ang = 2.0 * np.pi * k * s / n
    fwd = np.concatenate([np.cos(ang), -np.sin(ang)], axis=0)
    fwd[L, :] = np.cos(np.pi * np.arange(L))
    t = np.arange(L, dtype=np.float64)[:, None]
    kk = np.arange(L, dtype=np.float64)[None, :]
    ang_i = 2.0 * np.pi * t * kk / n
    inv_re = (2.0 / n) * np.cos(ang_i)
    inv_re[:, 0] = 1.0 / n
    inv_im = -(2.0 / n) * np.sin(ang_i)
    inv_im[:, 0] = np.cos(np.pi * np.arange(L)) / n
    inv = np.concatenate([inv_re, inv_im], axis=1)
    return fwd.astype(np.float32), inv.astype(np.float32)


@functools.lru_cache(maxsize=None)
def _filter_tables(L):
    bands = (HY_FILTER_EMB - 1) // 2
    t = np.linspace(0.0, 1.0, L)[:, None]
    w = (2.0 * np.pi / L) * np.arange(L, dtype=np.float64)[:, None]
    fr = np.linspace(1e-4, bands - 1, bands)[None, :]
    feats = np.concatenate([t, np.cos(fr * w), -np.sin(fr * w)], -1)
    feats_p = np.zeros((L, LANES), np.float64)
    feats_p[:, :HY_FILTER_EMB] = feats
    deltas = np.abs(np.linspace(math.log(HY_DECAY_TARGET) / HY_FAST_DECAY_PCT,
                                math.log(HY_DECAY_TARGET) / HY_SLOW_DECAY_PCT, HY_WIDTH))
    decay = np.exp(-t * deltas[None, :])
    return feats_p.astype(np.float32), decay.astype(np.float32)


@functools.lru_cache(maxsize=None)
def _rope_tables(n):
    half = AX_DIM // 2
    pos = np.arange(n)
    row = (pos // GRID_W).astype(np.float64)[:, None]
    col = (pos % GRID_W).astype(np.float64)[:, None]
    inv = ROPE_BASE ** (-np.arange(0, AX_DIM, 2, dtype=np.float64) / AX_DIM)[None, :]
    zero = np.zeros((n, half))
    cr, sr = np.cos(row * inv), np.sin(row * inv)
    cc, sc = np.cos(col * inv), np.sin(col * inv)
    c = np.concatenate([cr, cr, cc, cc], -1)
    s_up = np.concatenate([-sr, zero, -sc, zero], -1)
    s_dn = np.concatenate([zero, sr, zero, sc], -1)
    reps = LANES // DIFF_HEAD_DIM
    tile = lambda a: np.tile(a, (1, reps)).astype(np.float32)
    return tile(c), tile(s_up), tile(s_dn)


def _ada_kernel(cond_ref, w_ref, b_ref, o_ref):
    c = cond_ref[...]
    s = c * jax.nn.sigmoid(c)
    o_ref[0] = _dot(s.astype(BF16), w_ref[0].astype(BF16)) + b_ref[0]


def _ada_call(cond, w_ada, b_ada):
    n_out = N_ADA * D_MODEL
    return pl.pallas_call(
        _ada_kernel,
        grid=(DEPTH, n_out // ADA_COL_TILE),
        in_specs=[
            pl.BlockSpec((COND_ROWS, D_MODEL), lambda l, j: (0, 0)),
            pl.BlockSpec((1, D_MODEL, ADA_COL_TILE), lambda l, j: (l, 0, j)),
            pl.BlockSpec((1, 1, ADA_COL_TILE), lambda l, j: (l, 0, j)),
        ],
        out_specs=pl.BlockSpec((1, COND_ROWS, ADA_COL_TILE), lambda l, j: (l, 0, j)),
        out_shape=jax.ShapeDtypeStruct((DEPTH, COND_ROWS, n_out), F32),
        compiler_params=_params(dimension_semantics=("arbitrary", "arbitrary")),
        name="ada",
    )(cond, w_ada, b_ada.reshape(DEPTH, 1, n_out))


def _rope_lanes(x, c, s_up, s_dn):
    outs = []
    for p in range(ATT_WIDTH // LANES):
        xb = x[:, p * LANES:(p + 1) * LANES]
        up = pltpu.roll(xb, LANES - AX_DIM // 2, 1)
        dn = pltpu.roll(xb, AX_DIM // 2, 1)
        outs.append(xb * c + up * s_up + dn * s_dn)
    return jnp.concatenate(outs, axis=1)


Q_SCALE = DIFF_HEAD_DIM ** -0.5 * math.log2(math.e)


def _inproj_kernel(*refs, names, seq, cache_slot):
    r = dict(zip(names, refs))
    sh1 = r["ada"][0, 0, 0:1, :]
    sc1 = r["ada"][0, 0, 1:2, :]
    h = (r["x"][...] * (1.0 + sc1) + sh1).astype(BF16)
    w_ref = r["w"]
    o1 = 3 * HY_WIDTH
    o2 = o1 + 2 * CF_WIDTH
    o3 = o2 + ATT_WIDTH
    o4 = o3 + ATT_WIDTH
    hy = _dot(h, w_ref[:, 0:o1])
    cf = _dot(h, w_ref[:, o1:o2])
    q = _dot(h, w_ref[:, o2:o3])
    k = _dot(h, w_ref[:, o3:o4])
    v = _dot(h, w_ref[:, o4:IN_WIDTH])
    if "yhy" in r:
        seqs = hy.shape[0] // seq
        pieces = [slice(b * seq, (b + 1) * seq) for b in range(seqs)]
        y_hy, y_cf = _mix_sequences([hy[p] for p in pieces], [cf[p] for p in pieces], r)
        for b, p in enumerate(pieces):
            r["yhy"][p, :] = y_hy[b].astype(BF16)
            r["ycf"][p, :] = y_cf[b].astype(BF16)
    else:
        r["hy"][...] = hy
        r["cf"][...] = cf
    if "rope_c" in r:
        c, su, sd = r["rope_c"][...], r["rope_up"][...], r["rope_dn"][...]
        q = _rope_lanes(q, c, su, sd)
        k = _rope_lanes(k, c, su, sd)
    r["q"][...] = (q * Q_SCALE).astype(BF16)
    r["k"][...] = k.astype(BF16)
    r["v"][...] = v.astype(BF16)
    if "kc" in r:
        kc_ref, vc_ref = r["kc"], r["vc"]
        for b in range(kc_ref.shape[0]):
            rows = slice(b * seq, (b + 1) * seq)
            for slot in range(kc_ref.shape[1]):
                if slot == cache_slot:
                    kc_ref[b, slot] = k[rows].T.reshape(2 * DIFF_HEADS, DIFF_HEAD_DIM, seq)
                    vc_ref[b, slot] = v[rows].reshape(seq, DIFF_HEADS, 2 * DIFF_HEAD_DIM)
                else:
                    kc_ref[b, slot] = jnp.zeros(kc_ref.shape[2:], F32)
                    vc_ref[b, slot] = jnp.zeros(vc_ref.shape[2:], F32)


def _inproj_call(x2d, ada4, wts, layer, batch, seq, cond_row_of_tile, rope_tabs, caches, mix_tabs):
    T = x2d.shape[0]
    rows = lambda w: pl.BlockSpec((ROW_TILE, w), lambda i: (i, 0))
    ins = [
        ("x", x2d, rows(D_MODEL)),
        ("ada", ada4, pl.BlockSpec((1, 1, N_ADA, D_MODEL), lambda i: (layer, cond_row_of_tile(i), 0, 0))),
        ("w", wts["w_in"][layer], _layer_spec((D_MODEL, IN_WIDTH), 0)),
    ]
    outs = []
    scratch = []
    if rope_tabs is not None:
        tiles_per_seq = seq // ROW_TILE
        for name, tab in zip(("rope_c", "rope_up", "rope_dn"), rope_tabs):
            ins.append((name, tab, pl.BlockSpec((ROW_TILE, LANES), lambda i: (i % tiles_per_seq, 0))))
    if mix_tabs is not None:
        ins += _mixer_inputs(seq, wts, layer, *mix_tabs)
        outs.append(("yhy", rows(HY_WIDTH), jax.ShapeDtypeStruct((T, HY_WIDTH), BF16)))
        outs.append(("ycf", rows(CF_WIDTH), jax.ShapeDtypeStruct((T, CF_WIDTH), BF16)))
        scratch.append(("hy_pad", pltpu.VMEM((ROW_TILE // seq, seq + 2 * SUBLANES, 3 * HY_WIDTH), F32)))
    else:
        outs.append(("hy", rows(3 * HY_WIDTH), jax.ShapeDtypeStruct((T, 3 * HY_WIDTH), F32)))
        outs.append(("cf", rows(2 * CF_WIDTH), jax.ShapeDtypeStruct((T, 2 * CF_WIDTH), F32)))
    for name in ("q", "k", "v"):
        outs.append((name, rows(ATT_WIDTH), jax.ShapeDtypeStruct((T, ATT_WIDTH), BF16)))
    aliases = {}
    cache_slot = 0
    if caches is not None:
        seqs = ROW_TILE // seq
        first_call = len(caches) == 0
        cache_slot = layer if first_call else 0
        shapes = (("kc", (batch, DEPTH, 2 * DIFF_HEADS, DIFF_HEAD_DIM, seq)),
                  ("vc", (batch, DEPTH, seq, DIFF_HEADS, 2 * DIFF_HEAD_DIM)))
        for j, (name, shape) in enumerate(shapes):
            if first_call:
                spec = pl.BlockSpec((seqs, DEPTH) + shape[2:], lambda i: (i, 0, 0, 0, 0))
            else:
                spec = pl.BlockSpec((seqs, 1) + shape[2:], lambda i: (i, layer, 0, 0, 0))
                aliases[len(ins)] = len(outs)
                ins.append((name + "_in", caches[j], pl.BlockSpec(memory_space=pl.ANY)))
            outs.append((name, spec, jax.ShapeDtypeStruct(shape, F32)))
    names = tuple(n for n, _, _ in ins) + tuple(n for n, _, _ in outs) + tuple(n for n, _ in scratch)
    results = pl.pallas_call(
        functools.partial(_inproj_kernel, names=names, seq=seq, cache_slot=cache_slot),
        grid=(T // ROW_TILE,),
        in_specs=[spec for _, _, spec in ins],
        out_specs=[spec for _, spec, _ in outs],
        out_shape=[shape for _, _, shape in outs],
        scratch_shapes=[s for _, s in scratch],
        input_output_aliases=aliases,
        compiler_params=_params(dimension_semantics=("arbitrary",)),
        name="inproj",
    )(*[a for _, a, _ in ins])
    return dict(zip((n for n, _, _ in outs), results))


def _store_spectrum_tables(spec, layer, ga_ref, gbc_ref, gd_ref):
    L = spec.shape[0] // 2
    W = spec.shape[1] // 2
    row = lax.broadcasted_iota(jnp.int32, (L, W), 0)
    p = spec[:L]
    q = spec[L:]
    g_re = p[:, :W] + p[:, W:]
    g_im = q[:, :W] - q[:, W:]
    nyq = q[0:1, :W] + q[0:1, W:]
    ga_ref[layer] = g_re
    gbc_ref[layer] = jnp.where(row == 0, 0.0, g_im)
    gd_ref[layer] = jnp.where(row == 0, nyq, g_re)


def _filter_kernel(feats_ref, w1_ref, b1_ref, fq_ref, w2_ref, b2_ref, w3_ref, decay_ref,
                   fwd_hi_ref, fwd_lo_ref, ga_ref, gbc_ref, gd_ref):
    L = feats_ref.shape[0]
    decay = decay_ref[...]
    row = lax.broadcasted_iota(jnp.int32, (L, HY_WIDTH), 0)
    taps = []
    for l in range(DEPTH):
        fq = fq_ref[l]
        hid = jnp.sin(fq * (_dot_f32(feats_ref[...], w1_ref[l]) + b1_ref[l]))
        hid = jnp.sin(fq * (_dot_f32(hid, w2_ref[l]) + b2_ref[l]))
        hf = _dot_f32(hid, w3_ref[l])
        taps.append(hf[:, :HY_WIDTH] * decay)
        taps.append(jnp.where(row == 0, 0.0, hf[:, HY_WIDTH:] * decay))
    h_hi, h_lo = _split_bf16(jnp.concatenate(taps, axis=1))
    fwd_hi = fwd_hi_ref[...]
    spec = _dot(fwd_hi, h_hi) + (_dot(fwd_hi, h_lo) + _dot(fwd_lo_ref[...], h_hi))
    for l in range(DEPTH):
        _store_spectrum_tables(spec[:, 2 * l * HY_WIDTH:2 * (l + 1) * HY_WIDTH], l, ga_ref, gbc_ref, gd_ref)


def _pad_to(a, shape):
    return jnp.pad(a, [(0, s - d) for s, d in zip(shape, a.shape)])


def _filter_call(L, w1, b1, fq, w2, b2, w3):
    feats, decay = _filter_tables(L)
    fwd, _ = _dft_tables(L)
    row_vec = lambda a: _pad_to(a.reshape(DEPTH, 1, -1), (DEPTH, 1, LANES))
    args = [
        jnp.asarray(feats),
        _pad_to(w1, (DEPTH, LANES, LANES)),
        row_vec(b1),
        row_vec(fq),
        _pad_to(w2, (DEPTH, LANES, LANES)),
        row_vec(b2),
        _pad_to(w3, (DEPTH, LANES, 2 * HY_WIDTH)),
        jnp.asarray(decay),
        *_split_bf16(jnp.asarray(fwd)),
    ]
    out = jax.ShapeDtypeStruct((DEPTH, L, HY_WIDTH), F32)
    return pl.pallas_call(
        _filter_kernel,
        out_shape=[out, out, out],
        compiler_params=_params(),
        name="hyena_filter",
    )(*args)


def _conv_filter_kernel(taps_ref, fwd_ref, ga_ref, gbc_ref, gd_ref):
    spec = _dot_f32(fwd_ref[...], taps_ref[...])
    for l in range(DEPTH):
        _store_spectrum_tables(spec[:, 2 * l * CF_WIDTH:2 * (l + 1) * CF_WIDTH], l, ga_ref, gbc_ref, gd_ref)


def _conv_filter_call(L, conv_w):
    half = CF_CONV_K // 2
    fwd, _ = _dft_tables(L)
    lag_pos = conv_w[:, half::-1, :]
    lag_neg = _pad_to(conv_w[:, half + 1:, :], (DEPTH, half + 1, CF_WIDTH))
    lag_neg = jnp.roll(lag_neg, 1, axis=1)
    taps = jnp.concatenate([lag_pos, lag_neg], axis=2)
    taps = _pad_to(taps, (DEPTH, LANES, 2 * CF_WIDTH))
    taps = jnp.transpose(taps, (1, 0, 2)).reshape(LANES, DEPTH * 2 * CF_WIDTH)
    out = jax.ShapeDtypeStruct((DEPTH, L, CF_WIDTH), F32)
    return pl.pallas_call(
        _conv_filter_kernel,
        out_shape=[out, out, out],
        compiler_params=_params(),
        name="conv_filter",
    )(taps, jnp.asarray(fwd[:, :LANES]))


def _dft_conv(us, tabs, fwd_ref, inv_ref):
    L, W = us[0].shape
    u_all = jnp.concatenate(us, axis=1).astype(BF16)
    spec = _dot(fwd_ref[...], u_all)
    y_re, y_im = [], []
    for s, (ga, gbc, gd) in enumerate(tabs):
        p = spec[:L, s * W:(s + 1) * W]
        q = spec[L:, s * W:(s + 1) * W]
        y_re.append(p * ga - q * gbc)
        y_im.append(p * gbc + q * gd)
    y_spec = jnp.concatenate(
        [jnp.concatenate(y_re, axis=1), jnp.concatenate(y_im, axis=1)], axis=0).astype(BF16)
    y = _dot(inv_ref[...], y_spec)
    return [y[:, s * W:(s + 1) * W] for s in range(len(us))]


def _hyena_front(hy, pad_ref, cw_ref, cb_ref):
    L, width = hy.shape
    zeros = jnp.zeros((SUBLANES, width), F32)
    pad_ref[0:SUBLANES, :] = zeros
    pad_ref[SUBLANES + L:2 * SUBLANES + L, :] = zeros
    pad_ref[SUBLANES:SUBLANES + L, :] = hy
    conv = cb_ref[...]
    for j in range(HY_SHORT_K):
        start = SUBLANES + j - HY_SHORT_K // 2
        conv = conv + cw_ref[j:j + 1, :] * pad_ref[start:start + L, :]
    return conv[:, :HY_WIDTH], conv[:, HY_WIDTH:2 * HY_WIDTH] * conv[:, 2 * HY_WIDTH:]


def _mix_sequences(hys, cfs, r):
    n = len(hys)
    x0s, us, tabs = [], [], []
    for s in range(n):
        x0, u = _hyena_front(hys[s], r["hy_pad"].at[s], r["hy_cw"], r["hy_cb"])
        x0s.append(x0)
        us.append(u)
        tabs.append((r["hy_ga"][...], r["hy_gbc"][...], r["hy_gd"][...]))
    for s in range(n):
        us.append(cfs[s][:, :CF_WIDTH] * jax.nn.sigmoid(cfs[s][:, CF_WIDTH:]))
        tabs.append((r["cf_ga"][...], r["cf_gbc"][...], r["cf_gd"][...]))
    ys = _dft_conv(us, tabs, r["fwd"], r["inv"])
    y_hy = [x0s[s] * (ys[s] + us[s] * r["hy_bias"][...]) for s in range(n)]
    y_cf = []
    for s in range(n):
        y = _layer_norm_rows(ys[n + s] + r["cf_cb"][...], r["cf_g"][...], r["cf_b"][...])
        y_cf.append(y * jax.nn.sigmoid(y))
    return y_hy, y_cf


def _mixer_inputs(seq, wts, layer, hy_tabs, cf_tabs):
    fwd, inv = _dft_tables(seq)
    ins = [("fwd", jnp.asarray(fwd).astype(BF16), _const_spec((2 * seq, seq))),
           ("inv", jnp.asarray(inv).astype(BF16), _const_spec((seq, 2 * seq)))]
    for name, tab in zip(("hy_ga", "hy_gbc", "hy_gd"), hy_tabs):
        ins.append((name, tab, _layer_spec((seq, HY_WIDTH), layer)))
    for name, tab in zip(("cf_ga", "cf_gbc", "cf_gd"), cf_tabs):
        ins.append((name, tab, _layer_spec((seq, CF_WIDTH), layer)))
    ins.append(("hy_cw", wts["hy_conv_w"], _layer_spec((HY_SHORT_K, 3 * HY_WIDTH), layer)))
    ins.append(("hy_cb", wts["hy_conv_b"], _layer_spec((1, 3 * HY_WIDTH), layer)))
    ins.append(("hy_bias", wts["hy_bias"], _layer_spec((1, HY_WIDTH), layer)))
    for name, key in (("cf_cb", "cf_conv_b"), ("cf_g", "cf_ln_g"), ("cf_b", "cf_ln_b")):
        ins.append((name, wts[key], _layer_spec((1, CF_WIDTH), layer)))
    return ins


def _hyena_kernel(hy_ref, cw_ref, cb_ref, fwd_ref, inv_ref, ga_ref, gbc_ref, gd_ref, hb_ref,
                  o_ref, pad_ref):
    nb = hy_ref.shape[0]
    x0s, us = zip(*[_hyena_front(hy_ref[s], pad_ref.at[s], cw_ref, cb_ref) for s in range(nb)])
    tabs = [(ga_ref[...], gbc_ref[...], gd_ref[...])] * nb
    ys = _dft_conv(us, tabs, fwd_ref, inv_ref)
    for s in range(nb):
        o_ref[s] = (x0s[s] * (ys[s] + us[s] * hb_ref[...])).astype(o_ref.dtype)


def _hyena_call(hy3, conv_w, conv_b, g_tabs, hy_bias, layer):
    B, L, width = hy3.shape
    nb = _seqs_per_step(B, L) if L < MIXER_ROWS else min(B, HYENA_LONG_SEQS)
    fwd, inv = _dft_tables(L)
    fwd_bf = jnp.asarray(fwd).astype(BF16)
    inv_bf = jnp.asarray(inv).astype(BF16)
    tab = _layer_spec((L, HY_WIDTH), layer)
    return pl.pallas_call(
        _hyena_kernel,
        grid=(B // nb,),
        in_specs=[
            pl.BlockSpec((nb, L, width), lambda b: (b, 0, 0)),
            _layer_spec((HY_SHORT_K, width), layer),
            _layer_spec((1, width), layer),
            _const_spec((2 * L, L)),
            _const_spec((L, 2 * L)),
            tab, tab, tab,
            _layer_spec((1, HY_WIDTH), layer),
        ],
        out_specs=pl.BlockSpec((nb, L, HY_WIDTH), lambda b: (b, 0, 0)),
        out_shape=jax.ShapeDtypeStruct((B, L, HY_WIDTH), BF16),
        scratch_shapes=[pltpu.VMEM((nb, L + 2 * SUBLANES, width), F32)],
        compiler_params=_params(dimension_semantics=("arbitrary",)),
        name="hyena",
    )(hy3, conv_w, conv_b, fwd_bf, inv_bf, *g_tabs, hy_bias)


CF_PAD = 2 * SUBLANES


def _conformer_kernel(cf_ref, cw_ref, cb_ref, g_ref, b_ref, o_ref, pad_ref):
    nb, L, _ = cf_ref.shape
    padded = L + 2 * CF_PAD
    shifted = padded - SUBLANES
    zeros = jnp.zeros((CF_PAD, CF_WIDTH), F32)
    half = CF_CONV_K // 2
    for s in range(nb):
        cf = cf_ref[s]
        pad_ref[s, 0, 0:CF_PAD, :] = zeros
        pad_ref[s, 0, CF_PAD + L:padded, :] = zeros
        pad_ref[s, 0, CF_PAD:CF_PAD + L, :] = cf[:, :CF_WIDTH] * jax.nn.sigmoid(cf[:, CF_WIDTH:])
        for r in range(1, SUBLANES):
            pad_ref[s, r, 0:shifted, :] = pad_ref[s, 0, r:r + shifted, :]
        for c0 in range(0, L, CF_ROW_CHUNK):
            acc = jnp.zeros((CF_ROW_CHUNK, CF_WIDTH), F32) + cb_ref[...]
            for j in range(CF_CONV_K):
                off = CF_PAD + j - half
                start = c0 + (off // SUBLANES) * SUBLANES
                acc = acc + cw_ref[j:j + 1, :] * pad_ref[s, off % SUBLANES, start:start + CF_ROW_CHUNK, :]
            y = _layer_norm_rows(acc, g_ref[...], b_ref[...])
            o_ref[s, c0:c0 + CF_ROW_CHUNK, :] = (y * jax.nn.sigmoid(y)).astype(o_ref.dtype)


def _conformer_call(cf3, conv_w, conv_b, ln_g, ln_b, layer):
    B, L, width = cf3.shape
    nb = _seqs_per_step(B, L)
    row = _layer_spec((1, CF_WIDTH), layer)
    return pl.pallas_call(
        _conformer_kernel,
        grid=(B // nb,),
        in_specs=[
            pl.BlockSpec((nb, L, width), lambda b: (b, 0, 0)),
            _layer_spec((CF_CONV_K, CF_WIDTH), layer),
            row, row, row,
        ],
        out_specs=pl.BlockSpec((nb, L, CF_WIDTH), lambda b: (b, 0, 0)),
        out_shape=jax.ShapeDtypeStruct((B, L, CF_WIDTH), BF16),
        scratch_shapes=[pltpu.VMEM((nb, SUBLANES, L + 2 * CF_PAD, CF_WIDTH), F32)],
        compiler_params=_params(dimension_semantics=("arbitrary",)),
        name="conformer",
    )(cf3, conv_w, conv_b, ln_g, ln_b)


def _attn_kernel(*refs, lambda_init, past):
    if past:
        (q_ref, k_ref, v_ref, ck_ref, cv_ref, lq1, lk1, lq2, lk2, g_ref, o_ref, k_all, v_aug) = refs
    else:
        (q_ref, k_ref, v_ref, lq1, lk1, lq2, lk2, g_ref, o_ref) = refs
    nb, tq, _ = q_ref.shape

    if past:
        keys = v_aug.shape[1]

        @pl.when(pl.program_id(1) == 0)
        def _():
            ones = jnp.ones((keys, HEAD_PAIR), BF16)
            for s in range(nb):
                k_all[s, 0:past, :] = ck_ref[s, 0].reshape(ATT_WIDTH, past).T.astype(BF16)
                k_all[s, past:keys, :] = k_ref[s]
                v_old = cv_ref[s, 0].reshape(past, ATT_WIDTH).astype(BF16)
                for h in range(DIFF_HEADS):
                    cols = slice(h * HEAD_PAIR, (h + 1) * HEAD_PAIR)
                    base = 2 * h * HEAD_PAIR
                    v_aug[s, 0:past, base:base + HEAD_PAIR] = v_old[:, cols]
                    v_aug[s, past:keys, base:base + HEAD_PAIR] = v_ref[s, :, cols]
                    v_aug[s, :, base + HEAD_PAIR:base + 2 * HEAD_PAIR] = ones

    lam = (jnp.exp(jnp.sum(lq1[...] * lk1[...], axis=-1, keepdims=True))
           - jnp.exp(jnp.sum(lq2[...] * lk2[...], axis=-1, keepdims=True)) + lambda_init)
    lane = lax.broadcasted_iota(jnp.int32, (tq, HEAD_PAIR), 1)
    first = lane < DIFF_HEAD_DIM
    nt = (((1,), (1,)), ((), ()))
    gain = g_ref[...] * (1.0 - lambda_init)
    zero = jnp.zeros((), BF16)

    def scores(s, h):
        cols = slice(h * HEAD_PAIR, (h + 1) * HEAD_PAIR)
        q = q_ref[s, :, cols]
        q2 = jnp.concatenate([jnp.where(first, q, zero), jnp.where(first, zero, q)], axis=0)
        kh = k_all[s, :, cols] if past else k_ref[s, :, cols]
        return lax.dot_general(q2, kh, nt, preferred_element_type=F32)

    def normalize_store(s, h, o):
        o = o * lax.rsqrt(jnp.mean(o * o, axis=-1, keepdims=True) + LN_EPS) * gain
        o_ref[s, :, h * HEAD_PAIR:(h + 1) * HEAD_PAIR] = o.astype(o_ref.dtype)

    for s in range(nb):
        if past:
            for h in range(DIFF_HEADS):
                sc = scores(s, h)
                e = jnp.exp2(sc - jnp.max(sc, axis=-1, keepdims=True)).astype(BF16)
                p = _dot(e, v_aug[s, :, 2 * h * HEAD_PAIR:2 * (h + 1) * HEAD_PAIR])
                p0, p1 = p[:tq], p[tq:]
                normalize_store(s, h, p0[:, :HEAD_PAIR] * (1.0 / p0[:, HEAD_PAIR:HEAD_PAIR + 1])
                                - p1[:, :HEAD_PAIR] * (lam / p1[:, HEAD_PAIR:HEAD_PAIR + 1]))
        else:
            sc = jnp.concatenate([scores(s, h) for h in range(DIFF_HEADS)], axis=0)
            e = jnp.exp2(sc - jnp.max(sc, axis=-1, keepdims=True))
            r = 1.0 / jnp.sum(e, axis=-1, keepdims=True)
            for h in range(DIFF_HEADS):
                r0 = slice(2 * h * tq, (2 * h + 1) * tq)
                r1 = slice((2 * h + 1) * tq, (2 * h + 2) * tq)
                a = e[r0] * r[r0] - e[r1] * (lam * r[r1])
                normalize_store(s, h, _dot(a.astype(BF16), v_ref[s, :, h * HEAD_PAIR:(h + 1) * HEAD_PAIR]))


def _attn_call(q3, k3, v3, cache_k, cache_v, layer, lam_vecs, subln_g):
    B, L, _ = q3.shape
    past = 0 if cache_k is None else cache_k.shape[-1]
    nb = _seqs_per_step(B, L)
    tq = min(Q_TILE, L)
    tile = pl.BlockSpec((nb, tq, ATT_WIDTH), lambda b, i: (b, i, 0))
    seq = pl.BlockSpec((nb, L, ATT_WIDTH), lambda b, i: (b, 0, 0))
    in_specs = [tile, seq, seq]
    args = [q3, k3, v3]
    scratch = []
    if past:
        for cache in (cache_k, cache_v):
            in_specs.append(pl.BlockSpec((nb, 1) + cache.shape[2:], lambda b, i: (b, layer, 0, 0, 0)))
            args.append(cache)
        scratch = [pltpu.VMEM((nb, past + L, ATT_WIDTH), BF16),
                   pltpu.VMEM((nb, past + L, 2 * ATT_WIDTH), BF16)]
    in_specs += [_layer_spec((1, DIFF_HEAD_DIM), layer)] * 4 + [_layer_spec((1, HEAD_PAIR), layer)]
    args += list(lam_vecs) + [subln_g]
    return pl.pallas_call(
        functools.partial(_attn_kernel, lambda_init=_lambda_init(layer), past=past),
        grid=(B // nb, L // tq),
        in_specs=in_specs,
        out_specs=tile,
        out_shape=jax.ShapeDtypeStruct((B, L, ATT_WIDTH), BF16),
        scratch_shapes=scratch,
        compiler_params=_params(dimension_semantics=("arbitrary", "arbitrary")),
        name="diff_attn",
    )(*args)


def _mlp_kernel(x_ref, hy_ref, cf_ref, at_ref, ada_ref, wo_ref, w1_ref, w2_ref,
                g1_ref, b1_ref, g2_ref, b2_ref, *rest):
    n_cast = (len(rest) - 1) // 2
    o_ref = rest[n_cast]
    for src, dst in zip(rest[:n_cast], rest[n_cast + 1:]):
        dst[...] = src[...].astype(BF16)
    gate1 = ada_ref[0, 0, 2:3, :]
    sh2 = ada_ref[0, 0, 3:4, :]
    sc2 = ada_ref[0, 0, 4:5, :]
    gate2 = ada_ref[0, 0, 5:6, :]
    mix = jnp.concatenate([hy_ref[...], cf_ref[...], at_ref[...]], axis=1)
    y = _dot(mix, wo_ref[...])
    x = _layer_norm_rows(DEEPNORM_ALPHA * x_ref[...] + gate1 * y, g1_ref[...], b1_ref[...])
    h = (x * (1.0 + sc2) + sh2).astype(BF16)
    a = jnp.maximum(_dot(h, w1_ref[...]), 0.0)
    m = _dot((a * a).astype(BF16), w2_ref[...])
    o_ref[...] = _layer_norm_rows(DEEPNORM_ALPHA * x + gate2 * m, g2_ref[...], b2_ref[...])


def _mlp_call(x2d, y_hy, y_cf, y_at, ada4, wts, layer, cond_row_of_tile, cast_next):
    T = x2d.shape[0]
    steps = T // ROW_TILE
    rows = lambda w: pl.BlockSpec((ROW_TILE, w), lambda i: (i, 0))
    vec = _layer_spec((1, D_MODEL), layer)
    in_specs = [
        rows(D_MODEL), rows(HY_WIDTH), rows(CF_WIDTH), rows(ATT_WIDTH),
        pl.BlockSpec((1, 1, N_ADA, D_MODEL), lambda i: (layer, cond_row_of_tile(i), 0, 0)),
        _layer_spec((MIX_WIDTH, D_MODEL), 0),
        _layer_spec((D_MODEL, D_FF), 0),
        _layer_spec((D_FF, D_MODEL), 0),
        vec, vec, vec, vec,
    ]
    args = [x2d, y_hy, y_cf, y_at, ada4, wts["w_out"][layer], wts["w_mlp1"][layer], wts["w_mlp2"][layer],
            wts["ln1_g"], wts["ln1_b"], wts["ln2_g"], wts["ln2_b"]]
    out_specs = [rows(D_MODEL)]
    out_shape = [jax.ShapeDtypeStruct((T, D_MODEL), F32)]
    for w in cast_next:
        _, n_rows, n_cols = w.shape
        chunk = n_rows // steps
        assert chunk * steps == n_rows and chunk % (2 * SUBLANES) == 0
        in_specs.append(pl.BlockSpec((1, chunk, n_cols), lambda i: (layer + 1, i, 0)))
        args.append(w)
        out_specs.append(pl.BlockSpec((1, chunk, n_cols), lambda i: (0, i, 0)))
        out_shape.append(jax.ShapeDtypeStruct((1, n_rows, n_cols), BF16))
    return pl.pallas_call(
        _mlp_kernel,
        grid=(steps,),
        in_specs=in_specs,
        out_specs=out_specs,
        out_shape=out_shape,
        compiler_params=_params(dimension_semantics=("arbitrary",)),
        name="outproj_mlp",
    )(*args)


def _layer(x2d, batch, seq, layer, ada4, wts, hy_tabs, cf_tabs, cond_row_of_tile, rope_tabs, cache_kv,
           new_caches, cast_next=()):
    fused = seq <= FUSED_MIXER_MAX_SEQ and ROW_TILE % seq == 0
    z = _inproj_call(x2d, ada4, wts, layer, batch, seq, cond_row_of_tile, rope_tabs, new_caches,
                     (hy_tabs, cf_tabs) if fused else None)
    as_seq = lambda a: a.reshape(batch, seq, a.shape[-1])
    flat = lambda a: a.reshape(batch * seq, a.shape[-1])
    if fused:
        y_hy, y_cf = z["yhy"], z["ycf"]
    else:
        y_hy = flat(_hyena_call(as_seq(z["hy"]), wts["hy_conv_w"], wts["hy_conv_b"], hy_tabs,
                                wts["hy_bias"], layer))
        y_cf = flat(_conformer_call(as_seq(z["cf"]), wts["cf_conv_w"], wts["cf_conv_b"], wts["cf_ln_g"],
                                    wts["cf_ln_b"], layer))
    ck, cv = cache_kv if cache_kv is not None else (None, None)
    lam_vecs = (wts["lam_q1"], wts["lam_k1"], wts["lam_q2"], wts["lam_k2"])
    y_at = _attn_call(as_seq(z["q"]), as_seq(z["k"]), as_seq(z["v"]), ck, cv, layer, lam_vecs,
                      wts["subln_g"])
    x_out, *casts = _mlp_call(x2d, y_hy, y_cf, flat(y_at), ada4, wts, layer, cond_row_of_tile, cast_next)
    return x_out, (z["kc"], z["vc"]) if "kc" in z else None, casts


def kernel(x_prompt, x_sample, cache_k, cache_v, c, c_ctx, w_ada, b_ada, w_in, hy_conv_w, hy_conv_b, hf_w1, hf_b1, hf_freq, hf_w2, hf_b2, hf_w3, hy_bias, cf_conv_w, cf_conv_b, cf_ln_g, cf_ln_b, lam_q1, lam_k1, lam_q2, lam_k2, subln_g, w_out, ln1_g, ln1_b, w_mlp1, w_mlp2, ln2_g, ln2_b):
    batch, seq, _ = x_prompt.shape
    dec_batch, dec_seq, _ = x_sample.shape
    assert 1 + dec_batch <= COND_ROWS
    assert ROW_TILE % seq == 0 and (batch * seq) % ROW_TILE == 0
    assert dec_seq % ROW_TILE == 0 and dec_seq % Q_TILE == 0

    cond = jnp.concatenate(
        [c_ctx[None, :], c, jnp.zeros((COND_ROWS - 1 - dec_batch, D_MODEL), F32)], axis=0)
    ada4 = _ada_call(cond, w_ada, b_ada).reshape(DEPTH, COND_ROWS, N_ADA, D_MODEL)

    row_vec = lambda a: a.reshape(DEPTH, 1, a.shape[-1])
    big = dict(w_in=w_in, w_out=w_out, w_mlp1=w_mlp1, w_mlp2=w_mlp2)
    wts = {name: [w[0:1].astype(BF16)] for name, w in big.items()}
    wts.update(
        hy_conv_w=hy_conv_w, hy_conv_b=row_vec(hy_conv_b), hy_bias=row_vec(hy_bias),
        cf_conv_w=cf_conv_w, cf_conv_b=row_vec(cf_conv_b),
        cf_ln_g=row_vec(cf_ln_g), cf_ln_b=row_vec(cf_ln_b),
        lam_q1=row_vec(lam_q1), lam_k1=row_vec(lam_k1), lam_q2=row_vec(lam_q2), lam_k2=row_vec(lam_k2),
        subln_g=row_vec(subln_g),
        ln1_g=row_vec(ln1_g), ln1_b=row_vec(ln1_b), ln2_g=row_vec(ln2_g), ln2_b=row_vec(ln2_b))

    rope_tabs = tuple(jnp.asarray(t) for t in _rope_tables(dec_seq))
    cache_kt = jnp.transpose(cache_k, (0, 1, 3, 4, 2))
    dec_tiles_per_seq = dec_seq // ROW_TILE

    xp = x_prompt.reshape(batch * seq, D_MODEL)
    xs = x_sample.reshape(dec_batch * dec_seq, D_MODEL)
    new_caches = ()
    filt = (hf_w1, hf_b1, hf_freq, hf_w2, hf_b2, hf_w3)
    hy_ctx = _filter_call(seq, *filt)
    hy_dec = _filter_call(dec_seq, *filt)
    fused = lambda n: n <= FUSED_MIXER_MAX_SEQ and ROW_TILE % n == 0
    cf_ctx = _conv_filter_call(seq, cf_conv_w) if fused(seq) else None
    cf_dec = _conv_filter_call(dec_seq, cf_conv_w) if fused(dec_seq) else None
    for l in range(DEPTH):
        cast_next = tuple(big.values()) if l + 1 < DEPTH else ()
        xp, new_caches, casts = _layer(xp, batch, seq, l, ada4, wts, hy_ctx, cf_ctx, lambda i: 0, None, None,
                                       new_caches, cast_next)
        for name, w_bf in zip(big, casts):
            wts[name].append(w_bf)
        xs, _, _ = _layer(xs, dec_batch, dec_seq, l, ada4, wts, hy_dec, cf_dec,
                          lambda i: 1 + i // dec_tiles_per_seq, rope_tabs, (cache_kt, cache_v), None)
    new_cache_kt, new_cache_v = new_caches
    new_cache_k = jnp.transpose(new_cache_kt, (0, 1, 4, 2, 3))
    return (xp.reshape(batch, seq, D_MODEL), xs.reshape(dec_batch, dec_seq, D_MODEL),
            new_cache_k, new_cache_v)
```

```python
import functools
import math

import numpy as np
import jax
import jax.numpy as jnp
from jax import lax
from jax.experimental import pallas as pl
from jax.experimental.pallas import tpu as pltpu

D_MODEL = 1024
DEPTH = 2
GRID_W = 64
HY_WIDTH = D_MODEL // 4
CF_WIDTH = D_MODEL // 4
ATT_WIDTH = D_MODEL // 2
DIFF_HEAD_DIM = 64
DIFF_HEADS = ATT_WIDTH // (2 * DIFF_HEAD_DIM)
HEAD_PAIR = 2 * DIFF_HEAD_DIM
MIX_WIDTH = HY_WIDTH + CF_WIDTH + ATT_WIDTH
IN_WIDTH = 3 * HY_WIDTH + 2 * CF_WIDTH + 3 * ATT_WIDTH
HY_SHORT_K = 3
HY_FILTER_EMB = 33
HY_FILTER_HIDDEN = 64
HY_FAST_DECAY_PCT = 0.3
HY_SLOW_DECAY_PCT = 1.5
HY_DECAY_TARGET = 1e-2
CF_CONV_K = 31
D_FF = 4 * D_MODEL
ROPE_BASE = 10000.0
AX_DIM = DIFF_HEAD_DIM // 2
DEEPNORM_ALPHA = (2 * DEPTH) ** 0.25
LN_EPS = 1e-5
N_ADA = 6

LANES = 128
SUBLANES = 8
COND_ROWS = SUBLANES
VMEM_LIMIT = 56 * 1024 * 1024

ROW_TILE = 512
ADA_COL_TILE = 1536
Q_TILE = 512
MIXER_ROWS = 1024
HYENA_LONG_SEQS = 2
FUSED_MIXER_MAX_SEQ = 256
CF_ROW_CHUNK = 128

F32 = jnp.float32
BF16 = jnp.bfloat16


def _lambda_init(l):
    return 0.8 - 0.6 * math.exp(-0.3 * l)


def _params(**kw):
    return pltpu.CompilerParams(vmem_limit_bytes=VMEM_LIMIT, **kw)


def _const_spec(shape):
    zeros = (0,) * len(shape)
    return pl.BlockSpec(shape, lambda *_: zeros, pipeline_mode=pl.Buffered(1))


def _layer_spec(shape, layer):
    zeros = (0,) * len(shape)
    return pl.BlockSpec((None,) + tuple(shape), lambda *_: (layer,) + zeros,
                        pipeline_mode=pl.Buffered(1))


def _dot(a, b):
    return jnp.dot(a, b, preferred_element_type=F32)


def _dot_f32(a, b):
    return jnp.dot(a, b, preferred_element_type=F32, precision=lax.Precision.HIGHEST)


def _split_bf16(x):
    hi = x.astype(BF16)
    return hi, (x - hi.astype(F32)).astype(BF16)


def _layer_norm_rows(x, g, b):
    mu = jnp.mean(x, axis=-1, keepdims=True)
    xc = x - mu
    var = jnp.mean(xc * xc, axis=-1, keepdims=True)
    return xc * lax.rsqrt(var + LN_EPS) * g + b


def _seqs_per_step(batch, seq):
    return max(1, min(batch, MIXER_ROWS // seq))


def _cast_plan(stacks, layer, steps, chunk_of_step):
    in_specs, out_specs, out_shapes = [], [], []
    for w in stacks:
        _, n_rows, n_cols = w.shape
        chunk = n_rows // steps
        assert chunk * steps == n_rows and chunk % (2 * SUBLANES) == 0
        in_specs.append(pl.BlockSpec((1, chunk, n_cols), lambda *g: (layer, chunk_of_step(*g), 0)))
        out_specs.append(pl.BlockSpec((1, chunk, n_cols), lambda *g: (0, chunk_of_step(*g), 0)))
        out_shapes.append(jax.ShapeDtypeStruct((1, n_rows, n_cols), BF16))
    return in_specs, list(stacks), out_specs, out_shapes


def _cast_chunks(srcs, dsts):
    for src, dst in zip(srcs, dsts):
        dst[...] = src[...].astype(BF16)


@functools.lru_cache(maxsize=None)
def _dft_tables(L):
    n = 2 * L
    k = np.arange(L, dtype=np.float64)[:, None]
    s = np.arange(L, dtype=np.float64)[None, :]
    ang = 2.0 * np.pi * k * s / n
    fwd = np.concatenate([np.cos(ang), -np.sin(ang)], axis=0)
    fwd[L, :] = np.cos(np.pi * np.arange(L))
    t = np.arange(L, dtype=np.float64)[:, None]
    kk = np.arange(L, dtype=np.float64)[None, :]
    ang_i = 2.0 * np.pi * t * kk / n
    inv_re = (2.0 / n) * np.cos(ang_i)
    inv_re[:, 0] = 1.0 / n
    inv_im = -(2.0 / n) * np.sin(ang_i)
    inv_im[:, 0] = np.cos(np.pi * np.arange(L)) / n
    inv = np.concatenate([inv_re, inv_im], axis=1)
    return fwd.astype(np.float32), inv.astype(np.float32)


@functools.lru_cache(maxsize=None)
def _filter_tables(L):
    bands = (HY_FILTER_EMB - 1) // 2
    t = np.linspace(0.0, 1.0, L)[:, None]
    w = (2.0 * np.pi / L) * np.arange(L, dtype=np.float64)[:, None]
    fr = np.linspace(1e-4, bands - 1, bands)[None, :]
    feats = np.concatenate([t, np.cos(fr * w), -np.sin(fr * w)], -1)
    feats_p = np.zeros((L, LANES), np.float64)
    feats_p[:, :HY_FILTER_EMB] = feats
    deltas = np.abs(np.linspace(math.log(HY_DECAY_TARGET) / HY_FAST_DECAY_PCT,
                                math.log(HY_DECAY_TARGET) / HY_SLOW_DECAY_PCT, HY_WIDTH))
    decay = np.exp(-t * deltas[None, :])
    return feats_p.astype(np.float32), decay.astype(np.float32)


@functools.lru_cache(maxsize=None)
def _rope_tables(n):
    half = AX_DIM // 2
    pos = np.arange(n)
    row = (pos // GRID_W).astype(np.float64)[:, None]
    col = (pos % GRID_W).astype(np.float64)[:, None]
    inv = ROPE_BASE ** (-np.arange(0, AX_DIM, 2, dtype=np.float64) / AX_DIM)[None, :]
    zero = np.zeros((n, half))
    cr, sr = np.cos(row * inv), np.sin(row * inv)
    cc, sc = np.cos(col * inv), np.sin(col * inv)
    c = np.concatenate([cr, cr, cc, cc], -1)
    s_up = np.concatenate([-sr, zero, -sc, zero], -1)
    s_dn = np.concatenate([zero, sr, zero, sc], -1)
    reps = LANES // DIFF_HEAD_DIM
    tile = lambda a: np.tile(a, (1, reps)).astype(np.float32)
    return tile(c), tile(s_up), tile(s_dn)


def _ada_kernel(cond_ref, w_ref, b_ref, o_ref):
    c = cond_ref[...]
    s = c * jax.nn.sigmoid(c)
    o_ref[0] = _dot(s.astype(BF16), w_ref[0].astype(BF16)) + b_ref[0]


def _ada_call(cond, w_ada, b_ada):
    n_out = N_ADA * D_MODEL
    return pl.pallas_call(
        _ada_kernel,
        grid=(DEPTH, n_out // ADA_COL_TILE),
        in_specs=[
            pl.BlockSpec((COND_ROWS, D_MODEL), lambda l, j: (0, 0)),
            pl.BlockSpec((1, D_MODEL, ADA_COL_TILE), lambda l, j: (l, 0, j)),
            pl.BlockSpec((1, 1, ADA_COL_TILE), lambda l, j: (l, 0, j)),
        ],
        out_specs=pl.BlockSpec((1, COND_ROWS, ADA_COL_TILE), lambda l, j: (l, 0, j)),
        out_shape=jax.ShapeDtypeStruct((DEPTH, COND_ROWS, n_out), F32),
        compiler_params=_params(dimension_semantics=("arbitrary", "arbitrary")),
        name="ada",
    )(cond, w_ada, b_ada.reshape(DEPTH, 1, n_out))


def _rope_lanes(x, c, s_up, s_dn):
    outs = []
    for p in range(ATT_WIDTH // LANES):
        xb = x[:, p * LANES:(p + 1) * LANES]
        up = pltpu.roll(xb, LANES - AX_DIM // 2, 1)
        dn = pltpu.roll(xb, AX_DIM // 2, 1)
        outs.append(xb * c + up * s_up + dn * s_dn)
    return jnp.concatenate(outs, axis=1)


Q_SCALE = DIFF_HEAD_DIM ** -0.5 * math.log2(math.e)


def _inproj_kernel(*refs, names, seq, cache_slot):
    r = dict(zip(names, refs))
    sh1 = r["ada"][0, 0, 0:1, :]
    sc1 = r["ada"][0, 0, 1:2, :]
    h = (r["x"][...] * (1.0 + sc1) + sh1).astype(BF16)
    w_ref = r["w"]
    o1 = 3 * HY_WIDTH
    o2 = o1 + 2 * CF_WIDTH
    o3 = o2 + ATT_WIDTH
    o4 = o3 + ATT_WIDTH
    hy = _dot(h, w_ref[:, 0:o1])
    cf = _dot(h, w_ref[:, o1:o2])
    q = _dot(h, w_ref[:, o2:o3])
    k = _dot(h, w_ref[:, o3:o4])
    v = _dot(h, w_ref[:, o4:IN_WIDTH])
    if "yhy" in r:
        seqs = hy.shape[0] // seq
        pieces = [slice(b * seq, (b + 1) * seq) for b in range(seqs)]
        y_hy, y_cf = _mix_sequences([hy[p] for p in pieces], [cf[p] for p in pieces], r)
        for b, p in enumerate(pieces):
            r["yhy"][p, :] = y_hy[b].astype(BF16)
            r["ycf"][p, :] = y_cf[b].astype(BF16)
    else:
        r["hy"][...] = hy
        r["cf"][...] = cf
    if "rope_c" in r:
        c, su, sd = r["rope_c"][...], r["rope_up"][...], r["rope_dn"][...]
        q = _rope_lanes(q, c, su, sd)
        k = _rope_lanes(k, c, su, sd)
    r["q"][...] = (q * Q_SCALE).astype(BF16)
    r["k"][...] = k.astype(BF16)
    r["v"][...] = v.astype(BF16)
    if "kc" in r:
        kc_ref, vc_ref = r["kc"], r["vc"]
        for b in range(kc_ref.shape[0]):
            rows = slice(b * seq, (b + 1) * seq)
            for slot in range(kc_ref.shape[1]):
                if slot == cache_slot:
                    kc_ref[b, slot] = k[rows].T.reshape(2 * DIFF_HEADS, DIFF_HEAD_DIM, seq)
                    vc_ref[b, slot] = v[rows].reshape(seq, DIFF_HEADS, 2 * DIFF_HEAD_DIM)
                else:
                    kc_ref[b, slot] = jnp.zeros(kc_ref.shape[2:], F32)
                    vc_ref[b, slot] = jnp.zeros(vc_ref.shape[2:], F32)


def _inproj_call(x2d, ada4, wts, layer, batch, seq, cond_row_of_tile, rope_tabs, caches, mix_tabs):
    T = x2d.shape[0]
    rows = lambda w: pl.BlockSpec((ROW_TILE, w), lambda i: (i, 0))
    ins = [
        ("x", x2d, rows(D_MODEL)),
        ("ada", ada4, pl.BlockSpec((1, 1, N_ADA, D_MODEL), lambda i: (layer, cond_row_of_tile(i), 0, 0))),
        ("w", wts["w_in"][layer], _layer_spec((D_MODEL, IN_WIDTH), 0)),
    ]
    outs = []
    scratch = []
    if rope_tabs is not None:
        tiles_per_seq = seq // ROW_TILE
        for name, tab in zip(("rope_c", "rope_up", "rope_dn"), rope_tabs):
            ins.append((name, tab, pl.BlockSpec((ROW_TILE, LANES), lambda i: (i % tiles_per_seq, 0))))
    if mix_tabs is not None:
        ins += _mixer_inputs(seq, wts, layer, *mix_tabs)
        outs.append(("yhy", rows(HY_WIDTH), jax.ShapeDtypeStruct((T, HY_WIDTH), BF16)))
        outs.append(("ycf", rows(CF_WIDTH), jax.ShapeDtypeStruct((T, CF_WIDTH), BF16)))
        scratch.append(("hy_pad", pltpu.VMEM((ROW_TILE // seq, seq + 2 * SUBLANES, 3 * HY_WIDTH), F32)))
    else:
        outs.append(("hy", rows(3 * HY_WIDTH), jax.ShapeDtypeStruct((T, 3 * HY_WIDTH), F32)))
        outs.append(("cf", rows(2 * CF_WIDTH), jax.ShapeDtypeStruct((T, 2 * CF_WIDTH), F32)))
    for name in ("q", "k", "v"):
        outs.append((name, rows(ATT_WIDTH), jax.ShapeDtypeStruct((T, ATT_WIDTH), BF16)))
    aliases = {}
    cache_slot = 0
    if caches is not None:
        seqs = ROW_TILE // seq
        first_call = len(caches) == 0
        cache_slot = layer if first_call else 0
        shapes = (("kc", (batch, DEPTH, 2 * DIFF_HEADS, DIFF_HEAD_DIM, seq)),
                  ("vc", (batch, DEPTH, seq, DIFF_HEADS, 2 * DIFF_HEAD_DIM)))
        for j, (name, shape) in enumerate(shapes):
            if first_call:
                spec = pl.BlockSpec((seqs, DEPTH) + shape[2:], lambda i: (i, 0, 0, 0, 0))
            else:
                spec = pl.BlockSpec((seqs, 1) + shape[2:], lambda i: (i, layer, 0, 0, 0))
                aliases[len(ins)] = len(outs)
                ins.append((name + "_in", caches[j], pl.BlockSpec(memory_space=pl.ANY)))
            outs.append((name, spec, jax.ShapeDtypeStruct(shape, F32)))
    names = tuple(n for n, _, _ in ins) + tuple(n for n, _, _ in outs) + tuple(n for n, _ in scratch)
    results = pl.pallas_call(
        functools.partial(_inproj_kernel, names=names, seq=seq, cache_slot=cache_slot),
        grid=(T // ROW_TILE,),
        in_specs=[spec for _, _, spec in ins],
        out_specs=[spec for _, spec, _ in outs],
        out_shape=[shape for _, _, shape in outs],
        scratch_shapes=[s for _, s in scratch],
        input_output_aliases=aliases,
        compiler_params=_params(dimension_semantics=("arbitrary",)),
        name="inproj",
    )(*[a for _, a, _ in ins])
    return dict(zip((n for n, _, _ in outs), results))


def _store_spectrum_tables(spec, layer, ga_ref, gbc_ref, gd_ref):
    L = spec.shape[0] // 2
    W = spec.shape[1] // 2
    row = lax.broadcasted_iota(jnp.int32, (L, W), 0)
    p = spec[:L]
    q = spec[L:]
    g_re = p[:, :W] + p[:, W:]
    g_im = q[:, :W] - q[:, W:]
    nyq = q[0:1, :W] + q[0:1, W:]
    ga_ref[layer] = g_re
    gbc_ref[layer] = jnp.where(row == 0, 0.0, g_im)
    gd_ref[layer] = jnp.where(row == 0, nyq, g_re)


def _filter_kernel(feats_ref, w1_ref, b1_ref, fq_ref, w2_ref, b2_ref, w3_ref, decay_ref,
                   fwd_hi_ref, fwd_lo_ref, ga_ref, gbc_ref, gd_ref):
    L = feats_ref.shape[0]
    decay = decay_ref[...]
    row = lax.broadcasted_iota(jnp.int32, (L, HY_WIDTH), 0)
    taps = []
    for l in range(DEPTH):
        fq = fq_ref[l]
        hid = jnp.sin(fq * (_dot_f32(feats_ref[...], w1_ref[l]) + b1_ref[l]))
        hid = jnp.sin(fq * (_dot_f32(hid, w2_ref[l]) + b2_ref[l]))
        hf = _dot_f32(hid, w3_ref[l])
        taps.append(hf[:, :HY_WIDTH] * decay)
        taps.append(jnp.where(row == 0, 0.0, hf[:, HY_WIDTH:] * decay))
    h_hi, h_lo = _split_bf16(jnp.concatenate(taps, axis=1))
    fwd_hi = fwd_hi_ref[...]
    spec = _dot(fwd_hi, h_hi) + (_dot(fwd_hi, h_lo) + _dot(fwd_lo_ref[...], h_hi))
    for l in range(DEPTH):
        _store_spectrum_tables(spec[:, 2 * l * HY_WIDTH:2 * (l + 1) * HY_WIDTH], l, ga_ref, gbc_ref, gd_ref)


def _pad_to(a, shape):
    return jnp.pad(a, [(0, s - d) for s, d in zip(shape, a.shape)])


def _filter_call(L, w1, b1, fq, w2, b2, w3):
    feats, decay = _filter_tables(L)
    fwd, _ = _dft_tables(L)
    row_vec = lambda a: _pad_to(a.reshape(DEPTH, 1, -1), (DEPTH, 1, LANES))
    args = [
        jnp.asarray(feats),
        _pad_to(w1, (DEPTH, LANES, LANES)),
        row_vec(b1),
        row_vec(fq),
        _pad_to(w2, (DEPTH, LANES, LANES)),
        row_vec(b2),
        _pad_to(w3, (DEPTH, LANES, 2 * HY_WIDTH)),
        jnp.asarray(decay),
        *_split_bf16(jnp.asarray(fwd)),
    ]
    out = jax.ShapeDtypeStruct((DEPTH, L, HY_WIDTH), F32)
    return pl.pallas_call(
        _filter_kernel,
        out_shape=[out, out, out],
        compiler_params=_params(),
        name="hyena_filter",
    )(*args)


def _conv_filter_kernel(taps_ref, fwd_ref, ga_ref, gbc_ref, gd_ref):
    spec = _dot_f32(fwd_ref[...], taps_ref[...])
    for l in range(DEPTH):
        _store_spectrum_tables(spec[:, 2 * l * CF_WIDTH:2 * (l + 1) * CF_WIDTH], l, ga_ref, gbc_ref, gd_ref)


def _conv_filter_call(L, conv_w):
    half = CF_CONV_K // 2
    fwd, _ = _dft_tables(L)
    lag_pos = conv_w[:, half::-1, :]
    lag_neg = _pad_to(conv_w[:, half + 1:, :], (DEPTH, half + 1, CF_WIDTH))
    lag_neg = jnp.roll(lag_neg, 1, axis=1)
    taps = jnp.concatenate([lag_pos, lag_neg], axis=2)
    taps = _pad_to(taps, (DEPTH, LANES, 2 * CF_WIDTH))
    taps = jnp.transpose(taps, (1, 0, 2)).reshape(LANES, DEPTH * 2 * CF_WIDTH)
    out = jax.ShapeDtypeStruct((DEPTH, L, CF_WIDTH), F32)
    return pl.pallas_call(
        _conv_filter_kernel,
        out_shape=[out, out, out],
        compiler_params=_params(),
        name="conv_filter",
    )(taps, jnp.asarray(fwd[:, :LANES]))


def _dft_conv(us, tabs, fwd_ref, inv_ref):
    L, W = us[0].shape
    u_all = jnp.concatenate(us, axis=1).astype(BF16)
    spec = _dot(fwd_ref[...], u_all)
    y_re, y_im = [], []
    for s, (ga, gbc, gd) in enumerate(tabs):
        p = spec[:L, s * W:(s + 1) * W]
        q = spec[L:, s * W:(s + 1) * W]
        y_re.append(p * ga - q * gbc)
        y_im.append(p * gbc + q * gd)
    y_spec = jnp.concatenate(
        [jnp.concatenate(y_re, axis=1), jnp.concatenate(y_im, axis=1)], axis=0).astype(BF16)
    y = _dot(inv_ref[...], y_spec)
    return [y[:, s * W:(s + 1) * W] for s in range(len(us))]


def _hyena_front(hy, pad_ref, cw_ref, cb_ref):
    L, width = hy.shape
    zeros = jnp.zeros((SUBLANES, width), F32)
    pad_ref[0:SUBLANES, :] = zeros
    pad_ref[SUBLANES + L:2 * SUBLANES + L, :] = zeros
    pad_ref[SUBLANES:SUBLANES + L, :] = hy
    conv = cb_ref[...]
    for j in range(HY_SHORT_K):
        start = SUBLANES + j - HY_SHORT_K // 2
        conv = conv + cw_ref[j:j + 1, :] * pad_ref[start:start + L, :]
    return conv[:, :HY_WIDTH], conv[:, HY_WIDTH:2 * HY_WIDTH] * conv[:, 2 * HY_WIDTH:]


def _mix_sequences(hys, cfs, r):
    n = len(hys)
    x0s, us, tabs = [], [], []
    for s in range(n):
        x0, u = _hyena_front(hys[s], r["hy_pad"].at[s], r["hy_cw"], r["hy_cb"])
        x0s.append(x0)
        us.append(u)
        tabs.append((r["hy_ga"][...], r["hy_gbc"][...], r["hy_gd"][...]))
    for s in range(n):
        us.append(cfs[s][:, :CF_WIDTH] * jax.nn.sigmoid(cfs[s][:, CF_WIDTH:]))
        tabs.append((r["cf_ga"][...], r["cf_gbc"][...], r["cf_gd"][...]))
    ys = _dft_conv(us, tabs, r["fwd"], r["inv"])
    y_hy = [x0s[s] * (ys[s] + us[s] * r["hy_bias"][...]) for s in range(n)]
    y_cf = []
    for s in range(n):
        y = _layer_norm_rows(ys[n + s] + r["cf_cb"][...], r["cf_g"][...], r["cf_b"][...])
        y_cf.append(y * jax.nn.sigmoid(y))
    return y_hy, y_cf


def _mixer_inputs(seq, wts, layer, hy_tabs, cf_tabs):
    fwd, inv = _dft_tables(seq)
    ins = [("fwd", jnp.asarray(fwd).astype(BF16), _const_spec((2 * seq, seq))),
           ("inv", jnp.asarray(inv).astype(BF16), _const_spec((seq, 2 * seq)))]
    for name, tab in zip(("hy_ga", "hy_gbc", "hy_gd"), hy_tabs):
        ins.append((name, tab, _layer_spec((seq, HY_WIDTH), layer)))
    for name, tab in zip(("cf_ga", "cf_gbc", "cf_gd"), cf_tabs):
        ins.append((name, tab, _layer_spec((seq, CF_WIDTH), layer)))
    ins.append(("hy_cw", wts["hy_conv_w"], _layer_spec((HY_SHORT_K, 3 * HY_WIDTH), layer)))
    ins.append(("hy_cb", wts["hy_conv_b"], _layer_spec((1, 3 * HY_WIDTH), layer)))
    ins.append(("hy_bias", wts["hy_bias"], _layer_spec((1, HY_WIDTH), layer)))
    for name, key in (("cf_cb", "cf_conv_b"), ("cf_g", "cf_ln_g"), ("cf_b", "cf_ln_b")):
        ins.append((name, wts[key], _layer_spec((1, CF_WIDTH), layer)))
    return ins


def _hyena_kernel(hy_ref, cw_ref, cb_ref, fwd_ref, inv_ref, ga_ref, gbc_ref, gd_ref, hb_ref,
                  o_ref, pad_ref):
    nb = hy_ref.shape[0]
    x0s, us = zip(*[_hyena_front(hy_ref[s], pad_ref.at[s], cw_ref, cb_ref) for s in range(nb)])
    tabs = [(ga_ref[...], gbc_ref[...], gd_ref[...])] * nb
    ys = _dft_conv(us, tabs, fwd_ref, inv_ref)
    for s in range(nb):
        o_ref[s] = (x0s[s] * (ys[s] + us[s] * hb_ref[...])).astype(o_ref.dtype)


def _hyena_call(hy3, conv_w, conv_b, g_tabs, hy_bias, layer):
    B, L, width = hy3.shape
    nb = _seqs_per_step(B, L) if L < MIXER_ROWS else min(B, HYENA_LONG_SEQS)
    fwd, inv = _dft_tables(L)
    fwd_bf = jnp.asarray(fwd).astype(BF16)
    inv_bf = jnp.asarray(inv).astype(BF16)
    tab = _layer_spec((L, HY_WIDTH), layer)
    return pl.pallas_call(
        _hyena_kernel,
        grid=(B // nb,),
        in_specs=[
            pl.BlockSpec((nb, L, width), lambda b: (b, 0, 0)),
            _layer_spec((HY_SHORT_K, width), layer),
            _layer_spec((1, width), layer),
            _const_spec((2 * L, L)),
            _const_spec((L, 2 * L)),
            tab, tab, tab,
            _layer_spec((1, HY_WIDTH), layer),
        ],
        out_specs=pl.BlockSpec((nb, L, HY_WIDTH), lambda b: (b, 0, 0)),
        out_shape=jax.ShapeDtypeStruct((B, L, HY_WIDTH), BF16),
        scratch_shapes=[pltpu.VMEM((nb, L + 2 * SUBLANES, width), F32)],
        compiler_params=_params(dimension_semantics=("arbitrary",)),
        name="hyena",
    )(hy3, conv_w, conv_b, fwd_bf, inv_bf, *g_tabs, hy_bias)


CF_PAD = 2 * SUBLANES


def _conformer_kernel(cf_ref, cw_ref, cb_ref, g_ref, b_ref, o_ref, pad_ref):
    nb, L, _ = cf_ref.shape
    padded = L + 2 * CF_PAD
    shifted = padded - SUBLANES
    zeros = jnp.zeros((CF_PAD, CF_WIDTH), F32)
    half = CF_CONV_K // 2
    for s in range(nb):
        cf = cf_ref[s]
        pad_ref[s, 0, 0:CF_PAD, :] = zeros
        pad_ref[s, 0, CF_PAD + L:padded, :] = zeros
        pad_ref[s, 0, CF_PAD:CF_PAD + L, :] = cf[:, :CF_WIDTH] * jax.nn.sigmoid(cf[:, CF_WIDTH:])
        for r in range(1, SUBLANES):
            pad_ref[s, r, 0:shifted, :] = pad_ref[s, 0, r:r + shifted, :]
        for c0 in range(0, L, CF_ROW_CHUNK):
            acc = jnp.zeros((CF_ROW_CHUNK, CF_WIDTH), F32) + cb_ref[...]
            for j in range(CF_CONV_K):
                off = CF_PAD + j - half
                start = c0 + (off // SUBLANES) * SUBLANES
                acc = acc + cw_ref[j:j + 1, :] * pad_ref[s, off % SUBLANES, start:start + CF_ROW_CHUNK, :]
            y = _layer_norm_rows(acc, g_ref[...], b_ref[...])
            o_ref[s, c0:c0 + CF_ROW_CHUNK, :] = (y * jax.nn.sigmoid(y)).astype(o_ref.dtype)


def _conformer_call(cf3, conv_w, conv_b, ln_g, ln_b, layer):
    B, L, width = cf3.shape
    nb = _seqs_per_step(B, L)
    row = _layer_spec((1, CF_WIDTH), layer)
    return pl.pallas_call(
        _conformer_kernel,
        grid=(B // nb,),
        in_specs=[
            pl.BlockSpec((nb, L, width), lambda b: (b, 0, 0)),
            _layer_spec((CF_CONV_K, CF_WIDTH), layer),
            row, row, row,
        ],
        out_specs=pl.BlockSpec((nb, L, CF_WIDTH), lambda b: (b, 0, 0)),
        out_shape=jax.ShapeDtypeStruct((B, L, CF_WIDTH), BF16),
        scratch_shapes=[pltpu.VMEM((nb, SUBLANES, L + 2 * CF_PAD, CF_WIDTH), F32)],
        compiler_params=_params(dimension_semantics=("arbitrary",)),
        name="conformer",
    )(cf3, conv_w, conv_b, ln_g, ln_b)


def _attn_kernel(*refs, lambda_init, past, n_cast):
    q_ref, k_ref, v_ref = refs[:3]
    n_in = 3
    if past:
        ck_ref, cv_ref = refs[3:5]
        k_all, v_aug = refs[-2:]
        n_in = 5
    lq1, lk1, lq2, lk2, g_ref = refs[n_in:n_in + 5]
    cast_in = refs[n_in + 5:n_in + 5 + n_cast]
    o_ref = refs[n_in + 5 + n_cast]
    cast_out = refs[n_in + 6 + n_cast:n_in + 6 + 2 * n_cast]
    _cast_chunks(cast_in, cast_out)
    nb, tq, _ = q_ref.shape

    if past:
        keys = v_aug.shape[1]

        @pl.when(pl.program_id(1) == 0)
        def _():
            ones = jnp.ones((keys, HEAD_PAIR), BF16)
            for s in range(nb):
                k_all[s, 0:past, :] = ck_ref[s, 0].reshape(ATT_WIDTH, past).T.astype(BF16)
                k_all[s, past:keys, :] = k_ref[s]
                v_old = cv_ref[s, 0].reshape(past, ATT_WIDTH).astype(BF16)
                for h in range(DIFF_HEADS):
                    cols = slice(h * HEAD_PAIR, (h + 1) * HEAD_PAIR)
                    base = 2 * h * HEAD_PAIR
                    v_aug[s, 0:past, base:base + HEAD_PAIR] = v_old[:, cols]
                    v_aug[s, past:keys, base:base + HEAD_PAIR] = v_ref[s, :, cols]
                    v_aug[s, :, base + HEAD_PAIR:base + 2 * HEAD_PAIR] = ones

    lam = (jnp.exp(jnp.sum(lq1[...] * lk1[...], axis=-1, keepdims=True))
           - jnp.exp(jnp.sum(lq2[...] * lk2[...], axis=-1, keepdims=True)) + lambda_init)
    lane = lax.broadcasted_iota(jnp.int32, (tq, HEAD_PAIR), 1)
    first = lane < DIFF_HEAD_DIM
    nt = (((1,), (1,)), ((), ()))
    gain = g_ref[...] * (1.0 - lambda_init)
    zero = jnp.zeros((), BF16)

    def scores(s, h):
        cols = slice(h * HEAD_PAIR, (h + 1) * HEAD_PAIR)
        q = q_ref[s, :, cols]
        q2 = jnp.concatenate([jnp.where(first, q, zero), jnp.where(first, zero, q)], axis=0)
        kh = k_all[s, :, cols] if past else k_ref[s, :, cols]
        return lax.dot_general(q2, kh, nt, preferred_element_type=F32)

    def normalize_store(s, h, o):
        o = o * lax.rsqrt(jnp.mean(o * o, axis=-1, keepdims=True) + LN_EPS) * gain
        o_ref[s, :, h * HEAD_PAIR:(h + 1) * HEAD_PAIR] = o.astype(o_ref.dtype)

    for s in range(nb):
        if past:
            for h in range(DIFF_HEADS):
                sc = scores(s, h)
                e = jnp.exp2(sc - jnp.max(sc, axis=-1, keepdims=True)).astype(BF16)
                p = _dot(e, v_aug[s, :, 2 * h * HEAD_PAIR:2 * (h + 1) * HEAD_PAIR])
                p0, p1 = p[:tq], p[tq:]
                normalize_store(s, h, p0[:, :HEAD_PAIR] * (1.0 / p0[:, HEAD_PAIR:HEAD_PAIR + 1])
                                - p1[:, :HEAD_PAIR] * (lam / p1[:, HEAD_PAIR:HEAD_PAIR + 1]))
        else:
            sc = jnp.concatenate([scores(s, h) for h in range(DIFF_HEADS)], axis=0)
            e = jnp.exp2(sc - jnp.max(sc, axis=-1, keepdims=True))
            r = 1.0 / jnp.sum(e, axis=-1, keepdims=True)
            for h in range(DIFF_HEADS):
                r0 = slice(2 * h * tq, (2 * h + 1) * tq)
                r1 = slice((2 * h + 1) * tq, (2 * h + 2) * tq)
                a = e[r0] * r[r0] - e[r1] * (lam * r[r1])
                normalize_store(s, h, _dot(a.astype(BF16), v_ref[s, :, h * HEAD_PAIR:(h + 1) * HEAD_PAIR]))


def _attn_call(q3, k3, v3, cache_k, cache_v, layer, lam_vecs, subln_g, cast=()):
    B, L, _ = q3.shape
    past = 0 if cache_k is None else cache_k.shape[-1]
    nb = _seqs_per_step(B, L)
    tq = min(Q_TILE, L)
    cast_in, cast_args, cast_out_specs, cast_out_shapes = _cast_plan(
        cast, layer, (B // nb) * (L // tq), lambda b, i: b * (L // tq) + i)
    tile = pl.BlockSpec((nb, tq, ATT_WIDTH), lambda b, i: (b, i, 0))
    seq = pl.BlockSpec((nb, L, ATT_WIDTH), lambda b, i: (b, 0, 0))
    in_specs = [tile, seq, seq]
    args = [q3, k3, v3]
    scratch = []
    if past:
        for cache in (cache_k, cache_v):
            in_specs.append(pl.BlockSpec((nb, 1) + cache.shape[2:], lambda b, i: (b, layer, 0, 0, 0)))
            args.append(cache)
        scratch = [pltpu.VMEM((nb, past + L, ATT_WIDTH), BF16),
                   pltpu.VMEM((nb, past + L, 2 * ATT_WIDTH), BF16)]
    in_specs += [_layer_spec((1, DIFF_HEAD_DIM), layer)] * 4 + [_layer_spec((1, HEAD_PAIR), layer)]
    args += list(lam_vecs) + [subln_g]
    return pl.pallas_call(
        functools.partial(_attn_kernel, lambda_init=_lambda_init(layer), past=past, n_cast=len(cast)),
        grid=(B // nb, L // tq),
        in_specs=in_specs + cast_in,
        out_specs=[tile] + cast_out_specs,
        out_shape=[jax.ShapeDtypeStruct((B, L, ATT_WIDTH), BF16)] + cast_out_shapes,
        scratch_shapes=scratch,
        compiler_params=_params(dimension_semantics=("arbitrary", "arbitrary")),
        name="diff_attn",
    )(*args, *cast_args)


def _mlp_kernel(x_ref, hy_ref, cf_ref, at_ref, ada_ref, wo_ref, w1_ref, w2_ref,
                g1_ref, b1_ref, g2_ref, b2_ref, *rest):
    n_cast = (len(rest) - 1) // 2
    o_ref = rest[n_cast]
    _cast_chunks(rest[:n_cast], rest[n_cast + 1:])
    gate1 = ada_ref[0, 0, 2:3, :]
    sh2 = ada_ref[0, 0, 3:4, :]
    sc2 = ada_ref[0, 0, 4:5, :]
    gate2 = ada_ref[0, 0, 5:6, :]
    mix = jnp.concatenate([hy_ref[...], cf_ref[...], at_ref[...]], axis=1)
    y = _dot(mix, wo_ref[...])
    x = _layer_norm_rows(DEEPNORM_ALPHA * x_ref[...] + gate1 * y, g1_ref[...], b1_ref[...])
    h = (x * (1.0 + sc2) + sh2).astype(BF16)
    a = jnp.maximum(_dot(h, w1_ref[...]), 0.0)
    m = _dot((a * a).astype(BF16), w2_ref[...])
    o_ref[...] = _layer_norm_rows(DEEPNORM_ALPHA * x + gate2 * m, g2_ref[...], b2_ref[...])


def _mlp_call(x2d, y_hy, y_cf, y_at, ada4, wts, layer, cond_row_of_tile, cast_next):
    T = x2d.shape[0]
    steps = T // ROW_TILE
    cast_in, cast_args, cast_out_specs, cast_out_shapes = _cast_plan(cast_next, layer + 1, steps, lambda i: i)
    rows = lambda w: pl.BlockSpec((ROW_TILE, w), lambda i: (i, 0))
    vec = _layer_spec((1, D_MODEL), layer)
    in_specs = [
        rows(D_MODEL), rows(HY_WIDTH), rows(CF_WIDTH), rows(ATT_WIDTH),
        pl.BlockSpec((1, 1, N_ADA, D_MODEL), lambda i: (layer, cond_row_of_tile(i), 0, 0)),
        _layer_spec((MIX_WIDTH, D_MODEL), 0),
        _layer_spec((D_MODEL, D_FF), 0),
        _layer_spec((D_FF, D_MODEL), 0),
        vec, vec, vec, vec,
    ]
    args = [x2d, y_hy, y_cf, y_at, ada4, wts["w_out"][layer], wts["w_mlp1"][layer], wts["w_mlp2"][layer],
            wts["ln1_g"], wts["ln1_b"], wts["ln2_g"], wts["ln2_b"]]
    return pl.pallas_call(
        _mlp_kernel,
        grid=(steps,),
        in_specs=in_specs + cast_in,
        out_specs=[rows(D_MODEL)] + cast_out_specs,
        out_shape=[jax.ShapeDtypeStruct((T, D_MODEL), F32)] + cast_out_shapes,
        compiler_params=_params(dimension_semantics=("arbitrary",)),
        name="outproj_mlp",
    )(*args, *cast_args)


MLP_WEIGHTS = ("w_out", "w_mlp1", "w_mlp2")


def _layer(x2d, batch, seq, layer, ada4, wts, f32_weights, hy_tabs, cf_tabs, cond_row_of_tile, rope_tabs,
           cache_kv, new_caches, casts):
    fused = seq <= FUSED_MIXER_MAX_SEQ and ROW_TILE % seq == 0
    z = _inproj_call(x2d, ada4, wts, layer, batch, seq, cond_row_of_tile, rope_tabs, new_caches,
                     (hy_tabs, cf_tabs) if fused else None)
    as_seq = lambda a: a.reshape(batch, seq, a.shape[-1])
    flat = lambda a: a.reshape(batch * seq, a.shape[-1])
    if fused:
        y_hy, y_cf = z["yhy"], z["ycf"]
    else:
        y_hy = flat(_hyena_call(as_seq(z["hy"]), wts["hy_conv_w"], wts["hy_conv_b"], hy_tabs,
                                wts["hy_bias"], layer))
        y_cf = flat(_conformer_call(as_seq(z["cf"]), wts["cf_conv_w"], wts["cf_conv_b"], wts["cf_ln_g"],
                                    wts["cf_ln_b"], layer))
    ck, cv = cache_kv if cache_kv is not None else (None, None)
    lam_vecs = (wts["lam_q1"], wts["lam_k1"], wts["lam_q2"], wts["lam_k2"])
    y_at, *mlp_weights = _attn_call(as_seq(z["q"]), as_seq(z["k"]), as_seq(z["v"]), ck, cv, layer, lam_vecs,
                                    wts["subln_g"], [f32_weights[n] for n in MLP_WEIGHTS] if casts else ())
    for name, w_bf in zip(MLP_WEIGHTS, mlp_weights):
        wts[name].append(w_bf)
    cast_next = (f32_weights["w_in"],) if casts and layer + 1 < DEPTH else ()
    x_out, *w_in_next = _mlp_call(x2d, y_hy, y_cf, flat(y_at), ada4, wts, layer, cond_row_of_tile, cast_next)
    wts["w_in"] += w_in_next
    return x_out, (z["kc"], z["vc"]) if "kc" in z else None


def kernel(x_prompt, x_sample, cache_k, cache_v, c, c_ctx, w_ada, b_ada, w_in, hy_conv_w, hy_conv_b, hf_w1, hf_b1, hf_freq, hf_w2, hf_b2, hf_w3, hy_bias, cf_conv_w, cf_conv_b, cf_ln_g, cf_ln_b, lam_q1, lam_k1, lam_q2, lam_k2, subln_g, w_out, ln1_g, ln1_b, w_mlp1, w_mlp2, ln2_g, ln2_b):
    batch, seq, _ = x_prompt.shape
    dec_batch, dec_seq, _ = x_sample.shape
    assert 1 + dec_batch <= COND_ROWS
    assert ROW_TILE % seq == 0 and (batch * seq) % ROW_TILE == 0
    assert dec_seq % ROW_TILE == 0 and dec_seq % Q_TILE == 0

    cond = jnp.concatenate(
        [c_ctx[None, :], c, jnp.zeros((COND_ROWS - 1 - dec_batch, D_MODEL), F32)], axis=0)
    ada4 = _ada_call(cond, w_ada, b_ada).reshape(DEPTH, COND_ROWS, N_ADA, D_MODEL)

    row_vec = lambda a: a.reshape(DEPTH, 1, a.shape[-1])
    f32_weights = dict(w_in=w_in, w_out=w_out, w_mlp1=w_mlp1, w_mlp2=w_mlp2)
    wts = dict(w_in=[w_in[0:1].astype(BF16)], w_out=[], w_mlp1=[], w_mlp2=[])
    wts.update(
        hy_conv_w=hy_conv_w, hy_conv_b=row_vec(hy_conv_b), hy_bias=row_vec(hy_bias),
        cf_conv_w=cf_conv_w, cf_conv_b=row_vec(cf_conv_b),
        cf_ln_g=row_vec(cf_ln_g), cf_ln_b=row_vec(cf_ln_b),
        lam_q1=row_vec(lam_q1), lam_k1=row_vec(lam_k1), lam_q2=row_vec(lam_q2), lam_k2=row_vec(lam_k2),
        subln_g=row_vec(subln_g),
        ln1_g=row_vec(ln1_g), ln1_b=row_vec(ln1_b), ln2_g=row_vec(ln2_g), ln2_b=row_vec(ln2_b))

    rope_tabs = tuple(jnp.asarray(t) for t in _rope_tables(dec_seq))
    cache_kt = jnp.transpose(cache_k, (0, 1, 3, 4, 2))
    dec_tiles_per_seq = dec_seq // ROW_TILE

    xp = x_prompt.reshape(batch * seq, D_MODEL)
    xs = x_sample.reshape(dec_batch * dec_seq, D_MODEL)
    new_caches = ()
    filt = (hf_w1, hf_b1, hf_freq, hf_w2, hf_b2, hf_w3)
    hy_ctx = _filter_call(seq, *filt)
    hy_dec = _filter_call(dec_seq, *filt)
    fused = lambda n: n <= FUSED_MIXER_MAX_SEQ and ROW_TILE % n == 0
    cf_ctx = _conv_filter_call(seq, cf_conv_w) if fused(seq) else None
    cf_dec = _conv_filter_call(dec_seq, cf_conv_w) if fused(dec_seq) else None
    for l in range(DEPTH):
        xp, new_caches = _layer(xp, batch, seq, l, ada4, wts, f32_weights, hy_ctx, cf_ctx, lambda i: 0,
                                None, None, new_caches, True)
        xs, _ = _layer(xs, dec_batch, dec_seq, l, ada4, wts, f32_weights, hy_dec, cf_dec,
                       lambda i: 1 + i // dec_tiles_per_seq, rope_tabs, (cache_kt, cache_v), None, False)
    new_cache_kt, new_cache_v = new_caches
    new_cache_k = jnp.transpose(new_cache_kt, (0, 1, 4, 2, 3))
    return (xp.reshape(batch, seq, D_MODEL), xs.reshape(dec_batch, dec_seq, D_MODEL),
            new_cache_k, new_cache_v)
```

```python
import functools
import math

import numpy as np
import jax
import jax.numpy as jnp
from jax import lax
from jax.experimental import pallas as pl
from jax.experimental.pallas import tpu as pltpu

D_MODEL = 1024
DEPTH = 2
GRID_W = 64
HY_WIDTH = D_MODEL // 4
CF_WIDTH = D_MODEL // 4
ATT_WIDTH = D_MODEL // 2
DIFF_HEAD_DIM = 64
DIFF_HEADS = ATT_WIDTH // (2 * DIFF_HEAD_DIM)
HEAD_PAIR = 2 * DIFF_HEAD_DIM
MIX_WIDTH = HY_WIDTH + CF_WIDTH + ATT_WIDTH
IN_WIDTH = 3 * HY_WIDTH + 2 * CF_WIDTH + 3 * ATT_WIDTH
HY_SHORT_K = 3
HY_FILTER_EMB = 33
HY_FILTER_HIDDEN = 64
HY_FAST_DECAY_PCT = 0.3
HY_SLOW_DECAY_PCT = 1.5
HY_DECAY_TARGET = 1e-2
CF_CONV_K = 31
D_FF = 4 * D_MODEL
ROPE_BASE = 10000.0
AX_DIM = DIFF_HEAD_DIM // 2
DEEPNORM_ALPHA = (2 * DEPTH) ** 0.25
LN_EPS = 1e-5
N_ADA = 6

LANES = 128
SUBLANES = 8
COND_ROWS = SUBLANES
VMEM_LIMIT = 56 * 1024 * 1024

ROW_TILE = 512
ADA_COL_TILE = 1536
Q_TILE = 512
MIXER_ROWS = 1024
HYENA_LONG_SEQS = 2
FUSED_MIXER_MAX_SEQ = 256
CF_ROW_CHUNK = 128

F32 = jnp.float32
BF16 = jnp.bfloat16


def _lambda_init(l):
    return 0.8 - 0.6 * math.exp(-0.3 * l)


def _params(**kw):
    return pltpu.CompilerParams(vmem_limit_bytes=VMEM_LIMIT, **kw)


def _const_spec(shape):
    zeros = (0,) * len(shape)
    return pl.BlockSpec(shape, lambda *_: zeros, pipeline_mode=pl.Buffered(1))


def _layer_spec(shape, layer):
    zeros = (0,) * len(shape)
    return pl.BlockSpec((None,) + tuple(shape), lambda *_: (layer,) + zeros,
                        pipeline_mode=pl.Buffered(1))


VEC_SLOTS = (
    ("ln1_g", D_MODEL), ("ln1_b", D_MODEL), ("ln2_g", D_MODEL), ("ln2_b", D_MODEL),
    ("hy_conv_b", D_MODEL),
    ("hy_bias", HY_WIDTH), ("cf_conv_b", CF_WIDTH), ("cf_ln_g", CF_WIDTH), ("cf_ln_b", CF_WIDTH),
    ("subln_g", HEAD_PAIR), ("lam_q1", LANES), ("lam_k1", LANES), ("lam_q2", LANES), ("lam_k2", LANES),
)


def _vec_offsets():
    offsets, off = {}, 0
    for name, width in VEC_SLOTS:
        assert off % width == 0
        offsets[name] = (off, width)
        off += width
    return offsets, off


def _pack_vectors(params):
    pieces = [jnp.pad(params[name], ((0, 0), (0, width - params[name].shape[1]))) for name, width in VEC_SLOTS]
    return jnp.concatenate(pieces, axis=1).reshape(DEPTH, 1, -1)


def _vec_spec(name, layer):
    off, width = _vec_offsets()[0][name]
    return pl.BlockSpec((None, 1, width), lambda *_: (layer, 0, off // width), pipeline_mode=pl.Buffered(1))


def _dot(a, b):
    return jnp.dot(a, b, preferred_element_type=F32)


def _dot_f32(a, b):
    return jnp.dot(a, b, preferred_element_type=F32, precision=lax.Precision.HIGHEST)


def _split_bf16(x):
    hi = x.astype(BF16)
    return hi, (x - hi.astype(F32)).astype(BF16)


def _layer_norm_rows(x, g, b):
    mu = jnp.mean(x, axis=-1, keepdims=True)
    xc = x - mu
    var = jnp.mean(xc * xc, axis=-1, keepdims=True)
    return xc * lax.rsqrt(var + LN_EPS) * g + b


def _seqs_per_step(batch, seq):
    return max(1, min(batch, MIXER_ROWS // seq))


def _cast_plan(stacks, layer, steps, chunk_of_step):
    in_specs, out_specs, out_shapes = [], [], []
    for w in stacks:
        _, n_rows, n_cols = w.shape
        chunk = n_rows // steps
        assert chunk * steps == n_rows and chunk % (2 * SUBLANES) == 0
        in_specs.append(pl.BlockSpec((1, chunk, n_cols), lambda *g: (layer, chunk_of_step(*g), 0)))
        out_specs.append(pl.BlockSpec((1, chunk, n_cols), lambda *g: (0, chunk_of_step(*g), 0)))
        out_shapes.append(jax.ShapeDtypeStruct((1, n_rows, n_cols), BF16))
    return in_specs, list(stacks), out_specs, out_shapes


def _cast_chunks(srcs, dsts):
    for src, dst in zip(srcs, dsts):
        dst[...] = src[...].astype(BF16)


@functools.lru_cache(maxsize=None)
def _dft_tables(L):
    n = 2 * L
    k = np.arange(L, dtype=np.float64)[:, None]
    s = np.arange(L, dtype=np.float64)[None, :]
    ang = 2.0 * np.pi * k * s / n
    fwd = np.concatenate([np.cos(ang), -np.sin(ang)], axis=0)
    fwd[L, :] = np.cos(np.pi * np.arange(L))
    t = np.arange(L, dtype=np.float64)[:, None]
    kk = np.arange(L, dtype=np.float64)[None, :]
    ang_i = 2.0 * np.pi * t * kk / n
    inv_re = (2.0 / n) * np.cos(ang_i)
    inv_re[:, 0] = 1.0 / n
    inv_im = -(2.0 / n) * np.sin(ang_i)
    inv_im[:, 0] = np.cos(np.pi * np.arange(L)) / n
    inv = np.concatenate([inv_re, inv_im], axis=1)
    return fwd.astype(np.float32), inv.astype(np.float32)


@functools.lru_cache(maxsize=None)
def _filter_tables(L):
    bands = (HY_FILTER_EMB - 1) // 2
    t = np.linspace(0.0, 1.0, L)[:, None]
    w = (2.0 * np.pi / L) * np.arange(L, dtype=np.float64)[:, None]
    fr = np.linspace(1e-4, bands - 1, bands)[None, :]
    feats = np.concatenate([t, np.cos(fr * w), -np.sin(fr * w)], -1)
    feats_p = np.zeros((L, LANES), np.float64)
    feats_p[:, :HY_FILTER_EMB] = feats
    deltas = np.abs(np.linspace(math.log(HY_DECAY_TARGET) / HY_FAST_DECAY_PCT,
                                math.log(HY_DECAY_TARGET) / HY_SLOW_DECAY_PCT, HY_WIDTH))
    decay = np.exp(-t * deltas[None, :])
    return feats_p.astype(np.float32), decay.astype(np.float32)


@functools.lru_cache(maxsize=None)
def _rope_tables(n):
    half = AX_DIM // 2
    pos = np.arange(n)
    row = (pos // GRID_W).astype(np.float64)[:, None]
    col = (pos % GRID_W).astype(np.float64)[:, None]
    inv = ROPE_BASE ** (-np.arange(0, AX_DIM, 2, dtype=np.float64) / AX_DIM)[None, :]
    zero = np.zeros((n, half))
    cr, sr = np.cos(row * inv), np.sin(row * inv)
    cc, sc = np.cos(col * inv), np.sin(col * inv)
    c = np.concatenate([cr, cr, cc, cc], -1)
    s_up = np.concatenate([-sr, zero, -sc, zero], -1)
    s_dn = np.concatenate([zero, sr, zero, sc], -1)
    reps = LANES // DIFF_HEAD_DIM
    tile = lambda a: np.tile(a, (1, reps)).astype(np.float32)
    return tile(c), tile(s_up), tile(s_dn)


def _ada_kernel(cond_ref, w_ref, b_ref, o_ref):
    c = cond_ref[...]
    s = c * jax.nn.sigmoid(c)
    o_ref[0] = _dot(s.astype(BF16), w_ref[0].astype(BF16)) + b_ref[0]


def _ada_call(cond, w_ada, b_ada):
    n_out = N_ADA * D_MODEL
    return pl.pallas_call(
        _ada_kernel,
        grid=(DEPTH, n_out // ADA_COL_TILE),
        in_specs=[
            pl.BlockSpec((COND_ROWS, D_MODEL), lambda l, j: (0, 0)),
            pl.BlockSpec((1, D_MODEL, ADA_COL_TILE), lambda l, j: (l, 0, j)),
            pl.BlockSpec((1, 1, ADA_COL_TILE), lambda l, j: (l, 0, j)),
        ],
        out_specs=pl.BlockSpec((1, COND_ROWS, ADA_COL_TILE), lambda l, j: (l, 0, j)),
        out_shape=jax.ShapeDtypeStruct((DEPTH, COND_ROWS, n_out), F32),
        compiler_params=_params(dimension_semantics=("arbitrary", "arbitrary")),
        name="ada",
    )(cond, w_ada, b_ada.reshape(DEPTH, 1, n_out))


def _rope_lanes(x, c, s_up, s_dn):
    outs = []
    for p in range(ATT_WIDTH // LANES):
        xb = x[:, p * LANES:(p + 1) * LANES]
        up = pltpu.roll(xb, LANES - AX_DIM // 2, 1)
        dn = pltpu.roll(xb, AX_DIM // 2, 1)
        outs.append(xb * c + up * s_up + dn * s_dn)
    return jnp.concatenate(outs, axis=1)


Q_SCALE = DIFF_HEAD_DIM ** -0.5 * math.log2(math.e)


def _inproj_kernel(*refs, names, seq, cache_slot):
    r = dict(zip(names, refs))
    sh1 = r["ada"][0, 0, 0:1, :]
    sc1 = r["ada"][0, 0, 1:2, :]
    h = (r["x"][...] * (1.0 + sc1) + sh1).astype(BF16)
    w_ref = r["w"]
    o1 = 3 * HY_WIDTH
    o2 = o1 + 2 * CF_WIDTH
    o3 = o2 + ATT_WIDTH
    o4 = o3 + ATT_WIDTH
    hy = _dot(h, w_ref[:, 0:o1])
    cf = _dot(h, w_ref[:, o1:o2])
    q = _dot(h, w_ref[:, o2:o3])
    k = _dot(h, w_ref[:, o3:o4])
    v = _dot(h, w_ref[:, o4:IN_WIDTH])
    if "yhy" in r:
        seqs = hy.shape[0] // seq
        pieces = [slice(b * seq, (b + 1) * seq) for b in range(seqs)]
        y_hy, y_cf = _mix_sequences([hy[p] for p in pieces], [cf[p] for p in pieces], r)
        for b, p in enumerate(pieces):
            r["yhy"][p, :] = y_hy[b].astype(BF16)
            r["ycf"][p, :] = y_cf[b].astype(BF16)
    else:
        r["hy"][...] = hy
        r["cf"][...] = cf
    if "rope_c" in r:
        c, su, sd = r["rope_c"][...], r["rope_up"][...], r["rope_dn"][...]
        q = _rope_lanes(q, c, su, sd)
        k = _rope_lanes(k, c, su, sd)
    r["q"][...] = (q * Q_SCALE).astype(BF16)
    r["k"][...] = k.astype(BF16)
    r["v"][...] = v.astype(BF16)
    if "kc" in r:
        kc_ref, vc_ref = r["kc"], r["vc"]
        for b in range(kc_ref.shape[0]):
            rows = slice(b * seq, (b + 1) * seq)
            for slot in range(kc_ref.shape[1]):
                if slot == cache_slot:
                    kc_ref[b, slot] = k[rows].T.reshape(2 * DIFF_HEADS, DIFF_HEAD_DIM, seq)
                    vc_ref[b, slot] = v[rows].reshape(seq, DIFF_HEADS, 2 * DIFF_HEAD_DIM)
                else:
                    kc_ref[b, slot] = jnp.zeros(kc_ref.shape[2:], F32)
                    vc_ref[b, slot] = jnp.zeros(vc_ref.shape[2:], F32)


def _inproj_call(x2d, ada4, wts, layer, batch, seq, cond_row_of_tile, rope_tabs, caches, mix_tabs):
    T = x2d.shape[0]
    rows = lambda w: pl.BlockSpec((ROW_TILE, w), lambda i: (i, 0))
    ins = [
        ("x", x2d, rows(D_MODEL)),
        ("ada", ada4, pl.BlockSpec((1, 1, N_ADA, D_MODEL), lambda i: (layer, cond_row_of_tile(i), 0, 0))),
        ("w", wts["w_in"][layer], _layer_spec((D_MODEL, IN_WIDTH), 0)),
    ]
    outs = []
    scratch = []
    if rope_tabs is not None:
        tiles_per_seq = seq // ROW_TILE
        for name, tab in zip(("rope_c", "rope_up", "rope_dn"), rope_tabs):
            ins.append((name, tab, pl.BlockSpec((ROW_TILE, LANES), lambda i: (i % tiles_per_seq, 0))))
    if mix_tabs is not None:
        ins += _mixer_inputs(seq, wts, layer, *mix_tabs)
        outs.append(("yhy", rows(HY_WIDTH), jax.ShapeDtypeStruct((T, HY_WIDTH), BF16)))
        outs.append(("ycf", rows(CF_WIDTH), jax.ShapeDtypeStruct((T, CF_WIDTH), BF16)))
        scratch.append(("hy_pad", pltpu.VMEM((ROW_TILE // seq, seq + 2 * SUBLANES, 3 * HY_WIDTH), F32)))
    else:
        outs.append(("hy", rows(3 * HY_WIDTH), jax.ShapeDtypeStruct((T, 3 * HY_WIDTH), F32)))
        outs.append(("cf", rows(2 * CF_WIDTH), jax.ShapeDtypeStruct((T, 2 * CF_WIDTH), F32)))
    for name in ("q", "k", "v"):
        outs.append((name, rows(ATT_WIDTH), jax.ShapeDtypeStruct((T, ATT_WIDTH), BF16)))
    aliases = {}
    cache_slot = 0
    if caches is not None:
        seqs = ROW_TILE // seq
        first_call = len(caches) == 0
        cache_slot = layer if first_call else 0
        shapes = (("kc", (batch, DEPTH, 2 * DIFF_HEADS, DIFF_HEAD_DIM, seq)),
                  ("vc", (batch, DEPTH, seq, DIFF_HEADS, 2 * DIFF_HEAD_DIM)))
        for j, (name, shape) in enumerate(shapes):
            if first_call:
                spec = pl.BlockSpec((seqs, DEPTH) + shape[2:], lambda i: (i, 0, 0, 0, 0))
            else:
                spec = pl.BlockSpec((seqs, 1) + shape[2:], lambda i: (i, layer, 0, 0, 0))
                aliases[len(ins)] = len(outs)
                ins.append((name + "_in", caches[j], pl.BlockSpec(memory_space=pl.ANY)))
            outs.append((name, spec, jax.ShapeDtypeStruct(shape, F32)))
    names = tuple(n for n, _, _ in ins) + tuple(n for n, _, _ in outs) + tuple(n for n, _ in scratch)
    results = pl.pallas_call(
        functools.partial(_inproj_kernel, names=names, seq=seq, cache_slot=cache_slot),
        grid=(T // ROW_TILE,),
        in_specs=[spec for _, _, spec in ins],
        out_specs=[spec for _, spec, _ in outs],
        out_shape=[shape for _, _, shape in outs],
        scratch_shapes=[s for _, s in scratch],
        input_output_aliases=aliases,
        compiler_params=_params(dimension_semantics=("arbitrary",)),
        name="inproj",
    )(*[a for _, a, _ in ins])
    return dict(zip((n for n, _, _ in outs), results))


def _store_spectrum_tables(spec, layer, ga_ref, gbc_ref, gd_ref):
    L = spec.shape[0] // 2
    W = spec.shape[1] // 2
    row = lax.broadcasted_iota(jnp.int32, (L, W), 0)
    p = spec[:L]
    q = spec[L:]
    g_re = p[:, :W] + p[:, W:]
    g_im = q[:, :W] - q[:, W:]
    nyq = q[0:1, :W] + q[0:1, W:]
    ga_ref[layer] = g_re
    gbc_ref[layer] = jnp.where(row == 0, 0.0, g_im)
    gd_ref[layer] = jnp.where(row == 0, nyq, g_re)


def _filter_kernel(feats_ref, w1_ref, b1_ref, fq_ref, w2_ref, b2_ref, w3_ref, decay_ref,
                   fwd_hi_ref, fwd_lo_ref, ga_ref, gbc_ref, gd_ref):
    L = feats_ref.shape[0]
    decay = decay_ref[...]
    row = lax.broadcasted_iota(jnp.int32, (L, HY_WIDTH), 0)
    taps = []
    for l in range(DEPTH):
        fq = fq_ref[l]
        hid = jnp.sin(fq * (_dot_f32(feats_ref[...], w1_ref[l]) + b1_ref[l]))
        hid = jnp.sin(fq * (_dot_f32(hid, w2_ref[l]) + b2_ref[l]))
        hf = _dot_f32(hid, w3_ref[l])
        taps.append(hf[:, :HY_WIDTH] * decay)
        taps.append(jnp.where(row == 0, 0.0, hf[:, HY_WIDTH:] * decay))
    h_hi, h_lo = _split_bf16(jnp.concatenate(taps, axis=1))
    fwd_hi = fwd_hi_ref[...]
    spec = _dot(fwd_hi, h_hi) + (_dot(fwd_hi, h_lo) + _dot(fwd_lo_ref[...], h_hi))
    for l in range(DEPTH):
        _store_spectrum_tables(spec[:, 2 * l * HY_WIDTH:2 * (l + 1) * HY_WIDTH], l, ga_ref, gbc_ref, gd_ref)


def _pad_to(a, shape):
    return jnp.pad(a, [(0, s - d) for s, d in zip(shape, a.shape)])


def _filter_call(L, w1, b1, fq, w2, b2, w3):
    feats, decay = _filter_tables(L)
    fwd, _ = _dft_tables(L)
    row_vec = lambda a: _pad_to(a.reshape(DEPTH, 1, -1), (DEPTH, 1, LANES))
    args = [
        jnp.asarray(feats),
        _pad_to(w1, (DEPTH, LANES, LANES)),
        row_vec(b1),
        row_vec(fq),
        _pad_to(w2, (DEPTH, LANES, LANES)),
        row_vec(b2),
        _pad_to(w3, (DEPTH, LANES, 2 * HY_WIDTH)),
        jnp.asarray(decay),
        *_split_bf16(jnp.asarray(fwd)),
    ]
    out = jax.ShapeDtypeStruct((DEPTH, L, HY_WIDTH), F32)
    return pl.pallas_call(
        _filter_kernel,
        out_shape=[out, out, out],
        compiler_params=_params(),
        name="hyena_filter",
    )(*args)


def _conv_filter_kernel(taps_ref, fwd_ref, ga_ref, gbc_ref, gd_ref):
    spec = _dot_f32(fwd_ref[...], taps_ref[...])
    for l in range(DEPTH):
        _store_spectrum_tables(spec[:, 2 * l * CF_WIDTH:2 * (l + 1) * CF_WIDTH], l, ga_ref, gbc_ref, gd_ref)


def _conv_filter_call(L, conv_w):
    half = CF_CONV_K // 2
    fwd, _ = _dft_tables(L)
    lag_pos = conv_w[:, half::-1, :]
    lag_neg = _pad_to(conv_w[:, half + 1:, :], (DEPTH, half + 1, CF_WIDTH))
    lag_neg = jnp.roll(lag_neg, 1, axis=1)
    taps = jnp.concatenate([lag_pos, lag_neg], axis=2)
    taps = _pad_to(taps, (DEPTH, LANES, 2 * CF_WIDTH))
    taps = jnp.transpose(taps, (1, 0, 2)).reshape(LANES, DEPTH * 2 * CF_WIDTH)
    out = jax.ShapeDtypeStruct((DEPTH, L, CF_WIDTH), F32)
    return pl.pallas_call(
        _conv_filter_kernel,
        out_shape=[out, out, out],
        compiler_params=_params(),
        name="conv_filter",
    )(taps, jnp.asarray(fwd[:, :LANES]))


def _dft_conv(us, tabs, fwd_ref, inv_ref):
    L, W = us[0].shape
    u_all = jnp.concatenate(us, axis=1).astype(BF16)
    spec = _dot(fwd_ref[...], u_all)
    y_re, y_im = [], []
    for s, (ga, gbc, gd) in enumerate(tabs):
        p = spec[:L, s * W:(s + 1) * W]
        q = spec[L:, s * W:(s + 1) * W]
        y_re.append(p * ga - q * gbc)
        y_im.append(p * gbc + q * gd)
    y_spec = jnp.concatenate(
        [jnp.concatenate(y_re, axis=1), jnp.concatenate(y_im, axis=1)], axis=0).astype(BF16)
    y = _dot(inv_ref[...], y_spec)
    return [y[:, s * W:(s + 1) * W] for s in range(len(us))]


def _hyena_front(hy, pad_ref, cw_ref, cb_ref):
    L, width = hy.shape
    zeros = jnp.zeros((SUBLANES, width), F32)
    pad_ref[0:SUBLANES, :] = zeros
    pad_ref[SUBLANES + L:2 * SUBLANES + L, :] = zeros
    pad_ref[SUBLANES:SUBLANES + L, :] = hy
    conv = cb_ref[:, :width]
    for j in range(HY_SHORT_K):
        start = SUBLANES + j - HY_SHORT_K // 2
        conv = conv + cw_ref[j:j + 1, :] * pad_ref[start:start + L, :]
    return conv[:, :HY_WIDTH], conv[:, HY_WIDTH:2 * HY_WIDTH] * conv[:, 2 * HY_WIDTH:]


def _mix_sequences(hys, cfs, r):
    n = len(hys)
    x0s, us, tabs = [], [], []
    for s in range(n):
        x0, u = _hyena_front(hys[s], r["hy_pad"].at[s], r["hy_cw"], r["hy_cb"])
        x0s.append(x0)
        us.append(u)
        tabs.append((r["hy_ga"][...], r["hy_gbc"][...], r["hy_gd"][...]))
    for s in range(n):
        us.append(cfs[s][:, :CF_WIDTH] * jax.nn.sigmoid(cfs[s][:, CF_WIDTH:]))
        tabs.append((r["cf_ga"][...], r["cf_gbc"][...], r["cf_gd"][...]))
    ys = _dft_conv(us, tabs, r["fwd"], r["inv"])
    y_hy = [x0s[s] * (ys[s] + us[s] * r["hy_bias"][...]) for s in range(n)]
    y_cf = []
    for s in range(n):
        y = _layer_norm_rows(ys[n + s] + r["cf_cb"][...], r["cf_g"][...], r["cf_b"][...])
        y_cf.append(y * jax.nn.sigmoid(y))
    return y_hy, y_cf


def _mixer_inputs(seq, wts, layer, hy_tabs, cf_tabs):
    fwd, inv = _dft_tables(seq)
    ins = [("fwd", jnp.asarray(fwd).astype(BF16), _const_spec((2 * seq, seq))),
           ("inv", jnp.asarray(inv).astype(BF16), _const_spec((seq, 2 * seq)))]
    for name, tab in zip(("hy_ga", "hy_gbc", "hy_gd"), hy_tabs):
        ins.append((name, tab, _layer_spec((seq, HY_WIDTH), layer)))
    for name, tab in zip(("cf_ga", "cf_gbc", "cf_gd"), cf_tabs):
        ins.append((name, tab, _layer_spec((seq, CF_WIDTH), layer)))
    ins.append(("hy_cw", wts["hy_conv_w"], _layer_spec((HY_SHORT_K, 3 * HY_WIDTH), layer)))
    for name, key in (("hy_cb", "hy_conv_b"), ("hy_bias", "hy_bias"),
                      ("cf_cb", "cf_conv_b"), ("cf_g", "cf_ln_g"), ("cf_b", "cf_ln_b")):
        ins.append((name, wts["vecs"], _vec_spec(key, layer)))
    return ins


def _hyena_kernel(hy_ref, cw_ref, cb_ref, fwd_ref, inv_ref, ga_ref, gbc_ref, gd_ref, hb_ref,
                  o_ref, pad_ref):
    nb = hy_ref.shape[0]
    x0s, us = zip(*[_hyena_front(hy_ref[s], pad_ref.at[s], cw_ref, cb_ref) for s in range(nb)])
    tabs = [(ga_ref[...], gbc_ref[...], gd_ref[...])] * nb
    ys = _dft_conv(us, tabs, fwd_ref, inv_ref)
    for s in range(nb):
        o_ref[s] = (x0s[s] * (ys[s] + us[s] * hb_ref[...])).astype(o_ref.dtype)


def _hyena_call(hy3, conv_w, conv_b, g_tabs, hy_bias, layer):
    B, L, width = hy3.shape
    nb = _seqs_per_step(B, L) if L < MIXER_ROWS else min(B, HYENA_LONG_SEQS)
    fwd, inv = _dft_tables(L)
    fwd_bf = jnp.asarray(fwd).astype(BF16)
    inv_bf = jnp.asarray(inv).astype(BF16)
    tab = _layer_spec((L, HY_WIDTH), layer)
    return pl.pallas_call(
        _hyena_kernel,
        grid=(B // nb,),
        in_specs=[
            pl.BlockSpec((nb, L, width), lambda b: (b, 0, 0)),
            _layer_spec((HY_SHORT_K, width), layer),
            _vec_spec("hy_conv_b", layer),
            _const_spec((2 * L, L)),
            _const_spec((L, 2 * L)),
            tab, tab, tab,
            _vec_spec("hy_bias", layer),
        ],
        out_specs=pl.BlockSpec((nb, L, HY_WIDTH), lambda b: (b, 0, 0)),
        out_shape=jax.ShapeDtypeStruct((B, L, HY_WIDTH), BF16),
        scratch_shapes=[pltpu.VMEM((nb, L + 2 * SUBLANES, width), F32)],
        compiler_params=_params(dimension_semantics=("arbitrary",)),
        name="hyena",
    )(hy3, conv_w, conv_b, fwd_bf, inv_bf, *g_tabs, hy_bias)


CF_PAD = 2 * SUBLANES


def _conformer_kernel(cf_ref, cw_ref, cb_ref, g_ref, b_ref, o_ref, pad_ref):
    nb, L, _ = cf_ref.shape
    padded = L + 2 * CF_PAD
    shifted = padded - SUBLANES
    zeros = jnp.zeros((CF_PAD, CF_WIDTH), F32)
    half = CF_CONV_K // 2
    for s in range(nb):
        cf = cf_ref[s]
        pad_ref[s, 0, 0:CF_PAD, :] = zeros
        pad_ref[s, 0, CF_PAD + L:padded, :] = zeros
        pad_ref[s, 0, CF_PAD:CF_PAD + L, :] = cf[:, :CF_WIDTH] * jax.nn.sigmoid(cf[:, CF_WIDTH:])
        for r in range(1, SUBLANES):
            pad_ref[s, r, 0:shifted, :] = pad_ref[s, 0, r:r + shifted, :]
        for c0 in range(0, L, CF_ROW_CHUNK):
            acc = jnp.zeros((CF_ROW_CHUNK, CF_WIDTH), F32) + cb_ref[...]
            for j in range(CF_CONV_K):
                off = CF_PAD + j - half
                start = c0 + (off // SUBLANES) * SUBLANES
                acc = acc + cw_ref[j:j + 1, :] * pad_ref[s, off % SUBLANES, start:start + CF_ROW_CHUNK, :]
            y = _layer_norm_rows(acc, g_ref[...], b_ref[...])
            o_ref[s, c0:c0 + CF_ROW_CHUNK, :] = (y * jax.nn.sigmoid(y)).astype(o_ref.dtype)


def _conformer_call(cf3, conv_w, conv_b, ln_g, ln_b, layer):
    B, L, width = cf3.shape
    nb = _seqs_per_step(B, L)
    return pl.pallas_call(
        _conformer_kernel,
        grid=(B // nb,),
        in_specs=[
            pl.BlockSpec((nb, L, width), lambda b: (b, 0, 0)),
            _layer_spec((CF_CONV_K, CF_WIDTH), layer),
            _vec_spec("cf_conv_b", layer), _vec_spec("cf_ln_g", layer), _vec_spec("cf_ln_b", layer),
        ],
        out_specs=pl.BlockSpec((nb, L, CF_WIDTH), lambda b: (b, 0, 0)),
        out_shape=jax.ShapeDtypeStruct((B, L, CF_WIDTH), BF16),
        scratch_shapes=[pltpu.VMEM((nb, SUBLANES, L + 2 * CF_PAD, CF_WIDTH), F32)],
        compiler_params=_params(dimension_semantics=("arbitrary",)),
        name="conformer",
    )(cf3, conv_w, conv_b, ln_g, ln_b)


def _attn_kernel(*refs, lambda_init, past, n_cast):
    q_ref, k_ref, v_ref = refs[:3]
    n_in = 3
    if past:
        ck_ref, cv_ref = refs[3:5]
        k_all, v_aug = refs[-2:]
        n_in = 5
    lq1, lk1, lq2, lk2, g_ref = refs[n_in:n_in + 5]
    cast_in = refs[n_in + 5:n_in + 5 + n_cast]
    o_ref = refs[n_in + 5 + n_cast]
    cast_out = refs[n_in + 6 + n_cast:n_in + 6 + 2 * n_cast]
    _cast_chunks(cast_in, cast_out)
    nb, tq, _ = q_ref.shape

    if past:
        keys = v_aug.shape[1]

        @pl.when(pl.program_id(1) == 0)
        def _():
            ones = jnp.ones((keys, HEAD_PAIR), BF16)
            for s in range(nb):
                k_all[s, 0:past, :] = ck_ref[s, 0].reshape(ATT_WIDTH, past).T.astype(BF16)
                k_all[s, past:keys, :] = k_ref[s]
                v_old = cv_ref[s, 0].reshape(past, ATT_WIDTH).astype(BF16)
                for h in range(DIFF_HEADS):
                    cols = slice(h * HEAD_PAIR, (h + 1) * HEAD_PAIR)
                    base = 2 * h * HEAD_PAIR
                    v_aug[s, 0:past, base:base + HEAD_PAIR] = v_old[:, cols]
                    v_aug[s, past:keys, base:base + HEAD_PAIR] = v_ref[s, :, cols]
                    v_aug[s, :, base + HEAD_PAIR:base + 2 * HEAD_PAIR] = ones

    lam = (jnp.exp(jnp.sum(lq1[...] * lk1[...], axis=-1, keepdims=True))
           - jnp.exp(jnp.sum(lq2[...] * lk2[...], axis=-1, keepdims=True)) + lambda_init)
    lane = lax.broadcasted_iota(jnp.int32, (tq, HEAD_PAIR), 1)
    first = lane < DIFF_HEAD_DIM
    nt = (((1,), (1,)), ((), ()))
    gain = g_ref[...] * (1.0 - lambda_init)
    zero = jnp.zeros((), BF16)

    def scores(s, h):
        cols = slice(h * HEAD_PAIR, (h + 1) * HEAD_PAIR)
        q = q_ref[s, :, cols]
        q2 = jnp.concatenate([jnp.where(first, q, zero), jnp.where(first, zero, q)], axis=0)
        kh = k_all[s, :, cols] if past else k_ref[s, :, cols]
        return lax.dot_general(q2, kh, nt, preferred_element_type=F32)

    def normalize_store(s, h, o):
        o = o * lax.rsqrt(jnp.mean(o * o, axis=-1, keepdims=True) + LN_EPS) * gain
        o_ref[s, :, h * HEAD_PAIR:(h + 1) * HEAD_PAIR] = o.astype(o_ref.dtype)

    for s in range(nb):
        if past:
            for h in range(DIFF_HEADS):
                sc = scores(s, h)
                e = jnp.exp2(sc - jnp.max(sc, axis=-1, keepdims=True)).astype(BF16)
                p = _dot(e, v_aug[s, :, 2 * h * HEAD_PAIR:2 * (h + 1) * HEAD_PAIR])
                p0, p1 = p[:tq], p[tq:]
                normalize_store(s, h, p0[:, :HEAD_PAIR] * (1.0 / p0[:, HEAD_PAIR:HEAD_PAIR + 1])
                                - p1[:, :HEAD_PAIR] * (lam / p1[:, HEAD_PAIR:HEAD_PAIR + 1]))
        else:
            sc = jnp.concatenate([scores(s, h) for h in range(DIFF_HEADS)], axis=0)
            e = jnp.exp2(sc - jnp.max(sc, axis=-1, keepdims=True))
            r = 1.0 / jnp.sum(e, axis=-1, keepdims=True)
            for h in range(DIFF_HEADS):
                r0 = slice(2 * h * tq, (2 * h + 1) * tq)
                r1 = slice((2 * h + 1) * tq, (2 * h + 2) * tq)
                a = e[r0] * r[r0] - e[r1] * (lam * r[r1])
                normalize_store(s, h, _dot(a.astype(BF16), v_ref[s, :, h * HEAD_PAIR:(h + 1) * HEAD_PAIR]))


def _attn_call(q3, k3, v3, cache_k, cache_v, layer, vecs, cast=()):
    B, L, _ = q3.shape
    past = 0 if cache_k is None else cache_k.shape[-1]
    nb = _seqs_per_step(B, L)
    tq = min(Q_TILE, L)
    cast_in, cast_args, cast_out_specs, cast_out_shapes = _cast_plan(
        cast, layer, (B // nb) * (L // tq), lambda b, i: b * (L // tq) + i)
    tile = pl.BlockSpec((nb, tq, ATT_WIDTH), lambda b, i: (b, i, 0))
    seq = pl.BlockSpec((nb, L, ATT_WIDTH), lambda b, i: (b, 0, 0))
    in_specs = [tile, seq, seq]
    args = [q3, k3, v3]
    scratch = []
    if past:
        for cache in (cache_k, cache_v):
            in_specs.append(pl.BlockSpec((nb, 1) + cache.shape[2:], lambda b, i: (b, layer, 0, 0, 0)))
            args.append(cache)
        scratch = [pltpu.VMEM((nb, past + L, ATT_WIDTH), BF16),
                   pltpu.VMEM((nb, past + L, 2 * ATT_WIDTH), BF16)]
    in_specs += [_vec_spec(n, layer) for n in ("lam_q1", "lam_k1", "lam_q2", "lam_k2", "subln_g")]
    args += [vecs] * 5
    return pl.pallas_call(
        functools.partial(_attn_kernel, lambda_init=_lambda_init(layer), past=past, n_cast=len(cast)),
        grid=(B // nb, L // tq),
        in_specs=in_specs + cast_in,
        out_specs=[tile] + cast_out_specs,
        out_shape=[jax.ShapeDtypeStruct((B, L, ATT_WIDTH), BF16)] + cast_out_shapes,
        scratch_shapes=scratch,
        compiler_params=_params(dimension_semantics=("arbitrary", "arbitrary")),
        name="diff_attn",
    )(*args, *cast_args)


def _mlp_kernel(x_ref, hy_ref, cf_ref, at_ref, ada_ref, wo_ref, w1_ref, w2_ref,
                g1_ref, b1_ref, g2_ref, b2_ref, *rest):
    n_cast = (len(rest) - 1) // 2
    o_ref = rest[n_cast]
    _cast_chunks(rest[:n_cast], rest[n_cast + 1:])
    gate1 = ada_ref[0, 0, 2:3, :]
    sh2 = ada_ref[0, 0, 3:4, :]
    sc2 = ada_ref[0, 0, 4:5, :]
    gate2 = ada_ref[0, 0, 5:6, :]
    mix = jnp.concatenate([hy_ref[...], cf_ref[...], at_ref[...]], axis=1)
    y = _dot(mix, wo_ref[...])
    x = _layer_norm_rows(DEEPNORM_ALPHA * x_ref[...] + gate1 * y, g1_ref[...], b1_ref[...])
    h = (x * (1.0 + sc2) + sh2).astype(BF16)
    a = jnp.maximum(_dot(h, w1_ref[...]), 0.0)
    m = _dot((a * a).astype(BF16), w2_ref[...])
    o_ref[...] = _layer_norm_rows(DEEPNORM_ALPHA * x + gate2 * m, g2_ref[...], b2_ref[...])


def _mlp_call(x2d, y_hy, y_cf, y_at, ada4, wts, layer, cond_row_of_tile, cast_next):
    T = x2d.shape[0]
    steps = T // ROW_TILE
    cast_in, cast_args, cast_out_specs, cast_out_shapes = _cast_plan(cast_next, layer + 1, steps, lambda i: i)
    rows = lambda w: pl.BlockSpec((ROW_TILE, w), lambda i: (i, 0))
    in_specs = [
        rows(D_MODEL), rows(HY_WIDTH), rows(CF_WIDTH), rows(ATT_WIDTH),
        pl.BlockSpec((1, 1, N_ADA, D_MODEL), lambda i: (layer, cond_row_of_tile(i), 0, 0)),
        _layer_spec((MIX_WIDTH, D_MODEL), 0),
        _layer_spec((D_MODEL, D_FF), 0),
        _layer_spec((D_FF, D_MODEL), 0),
        _vec_spec("ln1_g", layer), _vec_spec("ln1_b", layer), _vec_spec("ln2_g", layer), _vec_spec("ln2_b", layer),
    ]
    args = [x2d, y_hy, y_cf, y_at, ada4, wts["w_out"][layer], wts["w_mlp1"][layer], wts["w_mlp2"][layer]]
    args += [wts["vecs"]] * 4
    return pl.pallas_call(
        _mlp_kernel,
        grid=(steps,),
        in_specs=in_specs + cast_in,
        out_specs=[rows(D_MODEL)] + cast_out_specs,
        out_shape=[jax.ShapeDtypeStruct((T, D_MODEL), F32)] + cast_out_shapes,
        compiler_params=_params(dimension_semantics=("arbitrary",)),
        name="outproj_mlp",
    )(*args, *cast_args)


MLP_WEIGHTS = ("w_out", "w_mlp1", "w_mlp2")


def _layer(x2d, batch, seq, layer, ada4, wts, f32_weights, hy_tabs, cf_tabs, cond_row_of_tile, rope_tabs,
           cache_kv, new_caches, casts):
    fused = seq <= FUSED_MIXER_MAX_SEQ and ROW_TILE % seq == 0
    z = _inproj_call(x2d, ada4, wts, layer, batch, seq, cond_row_of_tile, rope_tabs, new_caches,
                     (hy_tabs, cf_tabs) if fused else None)
    as_seq = lambda a: a.reshape(batch, seq, a.shape[-1])
    flat = lambda a: a.reshape(batch * seq, a.shape[-1])
    if fused:
        y_hy, y_cf = z["yhy"], z["ycf"]
    else:
        vecs = wts["vecs"]
        y_hy = flat(_hyena_call(as_seq(z["hy"]), wts["hy_conv_w"], vecs, hy_tabs, vecs, layer))
        y_cf = flat(_conformer_call(as_seq(z["cf"]), wts["cf_conv_w"], vecs, vecs, vecs, layer))
    ck, cv = cache_kv if cache_kv is not None else (None, None)
    y_at, *mlp_weights = _attn_call(as_seq(z["q"]), as_seq(z["k"]), as_seq(z["v"]), ck, cv, layer, wts["vecs"],
                                    [f32_weights[n] for n in MLP_WEIGHTS] if casts else ())
    for name, w_bf in zip(MLP_WEIGHTS, mlp_weights):
        wts[name].append(w_bf)
    cast_next = (f32_weights["w_in"],) if casts and layer + 1 < DEPTH else ()
    x_out, *w_in_next = _mlp_call(x2d, y_hy, y_cf, flat(y_at), ada4, wts, layer, cond_row_of_tile, cast_next)
    wts["w_in"] += w_in_next
    return x_out, (z["kc"], z["vc"]) if "kc" in z else None


def kernel(x_prompt, x_sample, cache_k, cache_v, c, c_ctx, w_ada, b_ada, w_in, hy_conv_w, hy_conv_b, hf_w1, hf_b1, hf_freq, hf_w2, hf_b2, hf_w3, hy_bias, cf_conv_w, cf_conv_b, cf_ln_g, cf_ln_b, lam_q1, lam_k1, lam_q2, lam_k2, subln_g, w_out, ln1_g, ln1_b, w_mlp1, w_mlp2, ln2_g, ln2_b):
    batch, seq, _ = x_prompt.shape
    dec_batch, dec_seq, _ = x_sample.shape
    assert 1 + dec_batch <= COND_ROWS
    assert ROW_TILE % seq == 0 and (batch * seq) % ROW_TILE == 0
    assert dec_seq % ROW_TILE == 0 and dec_seq % Q_TILE == 0

    cond = jnp.concatenate(
        [c_ctx[None, :], c, jnp.zeros((COND_ROWS - 1 - dec_batch, D_MODEL), F32)], axis=0)
    ada4 = _ada_call(cond, w_ada, b_ada).reshape(DEPTH, COND_ROWS, N_ADA, D_MODEL)

    f32_weights = dict(w_in=w_in, w_out=w_out, w_mlp1=w_mlp1, w_mlp2=w_mlp2)
    wts = dict(w_in=[w_in[0:1].astype(BF16)], w_out=[], w_mlp1=[], w_mlp2=[])
    wts.update(
        hy_conv_w=hy_conv_w, cf_conv_w=cf_conv_w,
        vecs=_pack_vectors(dict(
            ln1_g=ln1_g, ln1_b=ln1_b, ln2_g=ln2_g, ln2_b=ln2_b, hy_conv_b=hy_conv_b, hy_bias=hy_bias,
            cf_conv_b=cf_conv_b, cf_ln_g=cf_ln_g, cf_ln_b=cf_ln_b, subln_g=subln_g,
            lam_q1=lam_q1, lam_k1=lam_k1, lam_q2=lam_q2, lam_k2=lam_k2)))

    rope_tabs = tuple(jnp.asarray(t) for t in _rope_tables(dec_seq))
    cache_kt = jnp.transpose(cache_k, (0, 1, 3, 4, 2))
    dec_tiles_per_seq = dec_seq // ROW_TILE

    xp = x_prompt.reshape(batch * seq, D_MODEL)
    xs = x_sample.reshape(dec_batch * dec_seq, D_MODEL)
    new_caches = ()
    filt = (hf_w1, hf_b1, hf_freq, hf_w2, hf_b2, hf_w3)
    hy_ctx = _filter_call(seq, *filt)
    hy_dec = _filter_call(dec_seq, *filt)
    fused = lambda n: n <= FUSED_MIXER_MAX_SEQ and ROW_TILE % n == 0
    cf_ctx = _conv_filter_call(seq, cf_conv_w) if fused(seq) else None
    cf_dec = _conv_filter_call(dec_seq, cf_conv_w) if fused(dec_seq) else None
    for l in range(DEPTH):
        xp, new_caches = _layer(xp, batch, seq, l, ada4, wts, f32_weights, hy_ctx, cf_ctx, lambda i: 0,
                                None, None, new_caches, True)
        xs, _ = _layer(xs, dec_batch, dec_seq, l, ada4, wts, f32_weights, hy_dec, cf_dec,
                       lambda i: 1 + i // dec_tiles_per_seq, rope_tabs, (cache_kt, cache_v), None, False)
    new_cache_kt, new_cache_v = new_caches
    new_cache_k = jnp.transpose(new_cache_kt, (0, 1, 4, 2, 3))
    return (xp.reshape(batch, seq, D_MODEL), xs.reshape(dec_batch, dec_seq, D_MODEL),
            new_cache_k, new_cache_v)
```

```python
import functools
import math

import numpy as np
import jax
import jax.numpy as jnp
from jax import lax
from jax.experimental import pallas as pl
from jax.experimental.pallas import tpu as pltpu

D_MODEL = 1024
DEPTH = 2
GRID_W = 64
HY_WIDTH = D_MODEL // 4
CF_WIDTH = D_MODEL // 4
ATT_WIDTH = D_MODEL // 2
DIFF_HEAD_DIM = 64
DIFF_HEADS = ATT_WIDTH // (2 * DIFF_HEAD_DIM)
HEAD_PAIR = 2 * DIFF_HEAD_DIM
MIX_WIDTH = HY_WIDTH + CF_WIDTH + ATT_WIDTH
IN_WIDTH = 3 * HY_WIDTH + 2 * CF_WIDTH + 3 * ATT_WIDTH
HY_SHORT_K = 3
HY_FILTER_EMB = 33
HY_FILTER_HIDDEN = 64
HY_FAST_DECAY_PCT = 0.3
HY_SLOW_DECAY_PCT = 1.5
HY_DECAY_TARGET = 1e-2
CF_CONV_K = 31
D_FF = 4 * D_MODEL
ROPE_BASE = 10000.0
AX_DIM = DIFF_HEAD_DIM // 2
DEEPNORM_ALPHA = (2 * DEPTH) ** 0.25
LN_EPS = 1e-5
N_ADA = 6

LANES = 128
SUBLANES = 8
COND_ROWS = SUBLANES
VMEM_LIMIT = 56 * 1024 * 1024
MLP_VMEM_LIMIT = 62 * 1024 * 1024

ROW_TILE = 512
MLP_ROW_TILE = 1024
MLP_GROUP_ROWS = 256
ADA_COL_TILE = 1536
Q_TILE = 512
MIXER_ROWS = 1024
HYENA_LONG_SEQS = 2
FUSED_MIXER_MAX_SEQ = 256
CF_ROW_CHUNK = 128

F32 = jnp.float32
BF16 = jnp.bfloat16


def _lambda_init(l):
    return 0.8 - 0.6 * math.exp(-0.3 * l)


def _params(vmem_limit_bytes=VMEM_LIMIT, **kw):
    return pltpu.CompilerParams(vmem_limit_bytes=vmem_limit_bytes, **kw)


def _const_spec(shape):
    zeros = (0,) * len(shape)
    return pl.BlockSpec(shape, lambda *_: zeros, pipeline_mode=pl.Buffered(1))


def _layer_spec(shape, layer):
    zeros = (0,) * len(shape)
    return pl.BlockSpec((None,) + tuple(shape), lambda *_: (layer,) + zeros,
                        pipeline_mode=pl.Buffered(1))


VEC_SLOTS = (
    ("ln1_g", D_MODEL), ("ln1_b", D_MODEL), ("ln2_g", D_MODEL), ("ln2_b", D_MODEL),
    ("hy_conv_b", D_MODEL),
    ("hy_bias", HY_WIDTH), ("cf_conv_b", CF_WIDTH), ("cf_ln_g", CF_WIDTH), ("cf_ln_b", CF_WIDTH),
    ("subln_g", HEAD_PAIR), ("lam_q1", LANES), ("lam_k1", LANES), ("lam_q2", LANES), ("lam_k2", LANES),
)


def _vec_offsets():
    offsets, off = {}, 0
    for name, width in VEC_SLOTS:
        assert off % width == 0
        offsets[name] = (off, width)
        off += width
    return offsets, off


def _pack_vectors(params):
    pieces = [jnp.pad(params[name], ((0, 0), (0, width - params[name].shape[1]))) for name, width in VEC_SLOTS]
    return jnp.concatenate(pieces, axis=1).reshape(DEPTH, 1, -1)


def _vec_spec(name, layer):
    off, width = _vec_offsets()[0][name]
    return pl.BlockSpec((None, 1, width), lambda *_: (layer, 0, off // width), pipeline_mode=pl.Buffered(1))


def _dot(a, b):
    return jnp.dot(a, b, preferred_element_type=F32)


def _dot_f32(a, b):
    return jnp.dot(a, b, preferred_element_type=F32, precision=lax.Precision.HIGHEST)


def _split_bf16(x):
    hi = x.astype(BF16)
    return hi, (x - hi.astype(F32)).astype(BF16)


def _layer_norm_rows(x, g, b):
    mu = jnp.mean(x, axis=-1, keepdims=True)
    xc = x - mu
    var = jnp.mean(xc * xc, axis=-1, keepdims=True)
    return xc * lax.rsqrt(var + LN_EPS) * g + b


def _seqs_per_step(batch, seq):
    return max(1, min(batch, MIXER_ROWS // seq))


def _cast_plan(items, steps, chunk_of_step):
    in_specs, out_specs, out_shapes = [], [], []
    for w, layer in items:
        _, n_rows, n_cols = w.shape
        chunk = n_rows // steps
        assert chunk * steps == n_rows and chunk % (2 * SUBLANES) == 0
        in_specs.append(pl.BlockSpec((1, chunk, n_cols), lambda *g, layer=layer: (layer, chunk_of_step(*g), 0)))
        out_specs.append(pl.BlockSpec((1, chunk, n_cols), lambda *g: (0, chunk_of_step(*g), 0)))
        out_shapes.append(jax.ShapeDtypeStruct((1, n_rows, n_cols), BF16))
    return in_specs, [w for w, _ in items], out_specs, out_shapes


def _cast_chunks(srcs, dsts):
    for src, dst in zip(srcs, dsts):
        dst[...] = src[...].astype(BF16)


@functools.lru_cache(maxsize=None)
def _dft_tables(L):
    n = 2 * L
    k = np.arange(L, dtype=np.float64)[:, None]
    s = np.arange(L, dtype=np.float64)[None, :]
    ang = 2.0 * np.pi * k * s / n
    fwd = np.concatenate([np.cos(ang), -np.sin(ang)], axis=0)
    fwd[L, :] = np.cos(np.pi * np.arange(L))
    t = np.arange(L, dtype=np.float64)[:, None]
    kk = np.arange(L, dtype=np.float64)[None, :]
    ang_i = 2.0 * np.pi * t * kk / n
    inv_re = (2.0 / n) * np.cos(ang_i)
    inv_re[:, 0] = 1.0 / n
    inv_im = -(2.0 / n) * np.sin(ang_i)
    inv_im[:, 0] = np.cos(np.pi * np.arange(L)) / n
    inv = np.concatenate([inv_re, inv_im], axis=1)
    return fwd.astype(np.float32), inv.astype(np.float32)


@functools.lru_cache(maxsize=None)
def _filter_tables(L):
    bands = (HY_FILTER_EMB - 1) // 2
    t = np.linspace(0.0, 1.0, L)[:, None]
    w = (2.0 * np.pi / L) * np.arange(L, dtype=np.float64)[:, None]
    fr = np.linspace(1e-4, bands - 1, bands)[None, :]
    feats = np.concatenate([t, np.cos(fr * w), -np.sin(fr * w)], -1)
    feats_p = np.zeros((L, LANES), np.float64)
    feats_p[:, :HY_FILTER_EMB] = feats
    deltas = np.abs(np.linspace(math.log(HY_DECAY_TARGET) / HY_FAST_DECAY_PCT,
                                math.log(HY_DECAY_TARGET) / HY_SLOW_DECAY_PCT, HY_WIDTH))
    decay = np.exp(-t * deltas[None, :])
    return feats_p.astype(np.float32), decay.astype(np.float32)


@functools.lru_cache(maxsize=None)
def _rope_tables(n):
    half = AX_DIM // 2
    pos = np.arange(n)
    row = (pos // GRID_W).astype(np.float64)[:, None]
    col = (pos % GRID_W).astype(np.float64)[:, None]
    inv = ROPE_BASE ** (-np.arange(0, AX_DIM, 2, dtype=np.float64) / AX_DIM)[None, :]
    zero = np.zeros((n, half))
    cr, sr = np.cos(row * inv), np.sin(row * inv)
    cc, sc = np.cos(col * inv), np.sin(col * inv)
    c = np.concatenate([cr, cr, cc, cc], -1)
    s_up = np.concatenate([-sr, zero, -sc, zero], -1)
    s_dn = np.concatenate([zero, sr, zero, sc], -1)
    reps = LANES // DIFF_HEAD_DIM
    tile = lambda a: np.tile(a, (1, reps)).astype(np.float32)
    return tile(c), tile(s_up), tile(s_dn)


def _ada_kernel(cond_ref, w_ref, b_ref, o_ref):
    c = cond_ref[...]
    s = c * jax.nn.sigmoid(c)
    o_ref[0] = _dot(s.astype(BF16), w_ref[0].astype(BF16)) + b_ref[0]


def _ada_call(cond, w_ada, b_ada):
    n_out = N_ADA * D_MODEL
    return pl.pallas_call(
        _ada_kernel,
        grid=(DEPTH, n_out // ADA_COL_TILE),
        in_specs=[
            pl.BlockSpec((COND_ROWS, D_MODEL), lambda l, j: (0, 0)),
            pl.BlockSpec((1, D_MODEL, ADA_COL_TILE), lambda l, j: (l, 0, j)),
            pl.BlockSpec((1, 1, ADA_COL_TILE), lambda l, j: (l, 0, j)),
        ],
        out_specs=pl.BlockSpec((1, COND_ROWS, ADA_COL_TILE), lambda l, j: (l, 0, j)),
        out_shape=jax.ShapeDtypeStruct((DEPTH, COND_ROWS, n_out), F32),
        compiler_params=_params(dimension_semantics=("arbitrary", "arbitrary")),
        name="ada",
    )(cond, w_ada, b_ada.reshape(DEPTH, 1, n_out))


def _rope_lanes(x, c, s_up, s_dn):
    outs = []
    for p in range(ATT_WIDTH // LANES):
        xb = x[:, p * LANES:(p + 1) * LANES]
        up = pltpu.roll(xb, LANES - AX_DIM // 2, 1)
        dn = pltpu.roll(xb, AX_DIM // 2, 1)
        outs.append(xb * c + up * s_up + dn * s_dn)
    return jnp.concatenate(outs, axis=1)


Q_SCALE = DIFF_HEAD_DIM ** -0.5 * math.log2(math.e)


def _inproj_kernel(*refs, names, seq, cache_slot):
    r = dict(zip(names, refs))
    sh1 = r["ada"][0, 0, 0:1, :]
    sc1 = r["ada"][0, 0, 1:2, :]
    h = (r["x"][...] * (1.0 + sc1) + sh1).astype(BF16)
    w_ref = r["w"]
    o1 = 3 * HY_WIDTH
    o2 = o1 + 2 * CF_WIDTH
    o3 = o2 + ATT_WIDTH
    o4 = o3 + ATT_WIDTH
    hy = _dot(h, w_ref[:, 0:o1])
    cf = _dot(h, w_ref[:, o1:o2])
    q = _dot(h, w_ref[:, o2:o3])
    k = _dot(h, w_ref[:, o3:o4])
    v = _dot(h, w_ref[:, o4:IN_WIDTH])
    if "yhy" in r:
        seqs = hy.shape[0] // seq
        pieces = [slice(b * seq, (b + 1) * seq) for b in range(seqs)]
        y_hy, y_cf = _mix_sequences([hy[p] for p in pieces], [cf[p] for p in pieces], r)
        for b, p in enumerate(pieces):
            r["yhy"][p, :] = y_hy[b].astype(BF16)
            r["ycf"][p, :] = y_cf[b].astype(BF16)
    else:
        r["hy"][...] = hy
        r["cf"][...] = cf
    if "rope_c" in r:
        c, su, sd = r["rope_c"][...], r["rope_up"][...], r["rope_dn"][...]
        q = _rope_lanes(q, c, su, sd)
        k = _rope_lanes(k, c, su, sd)
    r["q"][...] = (q * Q_SCALE).astype(BF16)
    r["k"][...] = k.astype(BF16)
    r["v"][...] = v.astype(BF16)
    if "kc" in r:
        kc_ref, vc_ref = r["kc"], r["vc"]
        for b in range(kc_ref.shape[0]):
            rows = slice(b * seq, (b + 1) * seq)
            for slot in range(kc_ref.shape[1]):
                if slot == cache_slot:
                    kc_ref[b, slot] = k[rows].T.reshape(2 * DIFF_HEADS, DIFF_HEAD_DIM, seq)
                    vc_ref[b, slot] = v[rows].reshape(seq, DIFF_HEADS, 2 * DIFF_HEAD_DIM)
                else:
                    kc_ref[b, slot] = jnp.zeros(kc_ref.shape[2:], F32)
                    vc_ref[b, slot] = jnp.zeros(vc_ref.shape[2:], F32)


def _inproj_call(x2d, ada4, wts, layer, batch, seq, cond_row_of_row, rope_tabs, caches, mix_tabs):
    T = x2d.shape[0]
    rows = lambda w: pl.BlockSpec((ROW_TILE, w), lambda i: (i, 0))
    ins = [
        ("x", x2d, rows(D_MODEL)),
        ("ada", ada4, pl.BlockSpec((1, 1, N_ADA, D_MODEL), lambda i: (layer, cond_row_of_row(i * ROW_TILE), 0, 0))),
        ("w", wts["w_in"][layer], _layer_spec((D_MODEL, IN_WIDTH), 0)),
    ]
    outs = []
    scratch = []
    if rope_tabs is not None:
        tiles_per_seq = seq // ROW_TILE
        for name, tab in zip(("rope_c", "rope_up", "rope_dn"), rope_tabs):
            ins.append((name, tab, pl.BlockSpec((ROW_TILE, LANES), lambda i: (i % tiles_per_seq, 0))))
    if mix_tabs is not None:
        ins += _mixer_inputs(seq, wts, layer, *mix_tabs)
        outs.append(("yhy", rows(HY_WIDTH), jax.ShapeDtypeStruct((T, HY_WIDTH), BF16)))
        outs.append(("ycf", rows(CF_WIDTH), jax.ShapeDtypeStruct((T, CF_WIDTH), BF16)))
        scratch.append(("hy_pad", pltpu.VMEM((ROW_TILE // seq, seq + 2 * SUBLANES, 3 * HY_WIDTH), F32)))
    else:
        outs.append(("hy", rows(3 * HY_WIDTH), jax.ShapeDtypeStruct((T, 3 * HY_WIDTH), F32)))
        outs.append(("cf", rows(2 * CF_WIDTH), jax.ShapeDtypeStruct((T, 2 * CF_WIDTH), F32)))
    for name in ("q", "k", "v"):
        outs.append((name, rows(ATT_WIDTH), jax.ShapeDtypeStruct((T, ATT_WIDTH), BF16)))
    aliases = {}
    cache_slot = 0
    if caches is not None:
        seqs = ROW_TILE // seq
        first_call = len(caches) == 0
        cache_slot = layer if first_call else 0
        shapes = (("kc", (batch, DEPTH, 2 * DIFF_HEADS, DIFF_HEAD_DIM, seq)),
                  ("vc", (batch, DEPTH, seq, DIFF_HEADS, 2 * DIFF_HEAD_DIM)))
        for j, (name, shape) in enumerate(shapes):
            if first_call:
                spec = pl.BlockSpec((seqs, DEPTH) + shape[2:], lambda i: (i, 0, 0, 0, 0))
            else:
                spec = pl.BlockSpec((seqs, 1) + shape[2:], lambda i: (i, layer, 0, 0, 0))
                aliases[len(ins)] = len(outs)
                ins.append((name + "_in", caches[j], pl.BlockSpec(memory_space=pl.ANY)))
            outs.append((name, spec, jax.ShapeDtypeStruct(shape, F32)))
    names = tuple(n for n, _, _ in ins) + tuple(n for n, _, _ in outs) + tuple(n for n, _ in scratch)
    results = pl.pallas_call(
        functools.partial(_inproj_kernel, names=names, seq=seq, cache_slot=cache_slot),
        grid=(T // ROW_TILE,),
        in_specs=[spec for _, _, spec in ins],
        out_specs=[spec for _, spec, _ in outs],
        out_shape=[shape for _, _, shape in outs],
        scratch_shapes=[s for _, s in scratch],
        input_output_aliases=aliases,
        compiler_params=_params(dimension_semantics=("arbitrary",)),
        name="inproj",
    )(*[a for _, a, _ in ins])
    return dict(zip((n for n, _, _ in outs), results))


def _store_spectrum_tables(spec, layer, ga_ref, gbc_ref, gd_ref):
    L = spec.shape[0] // 2
    W = spec.shape[1] // 2
    row = lax.broadcasted_iota(jnp.int32, (L, W), 0)
    p = spec[:L]
    q = spec[L:]
    g_re = p[:, :W] + p[:, W:]
    g_im = q[:, :W] - q[:, W:]
    nyq = q[0:1, :W] + q[0:1, W:]
    ga_ref[layer] = g_re
    gbc_ref[layer] = jnp.where(row == 0, 0.0, g_im)
    gd_ref[layer] = jnp.where(row == 0, nyq, g_re)


def _filter_kernel(feats_ref, w1_ref, b1_ref, fq_ref, w2_ref, b2_ref, w3_ref, decay_ref,
                   fwd_hi_ref, fwd_lo_ref, ga_ref, gbc_ref, gd_ref):
    L = feats_ref.shape[0]
    decay = decay_ref[...]
    row = lax.broadcasted_iota(jnp.int32, (L, HY_WIDTH), 0)
    taps = []
    for l in range(DEPTH):
        fq = fq_ref[l]
        hid = jnp.sin(fq * (_dot_f32(feats_ref[...], w1_ref[l]) + b1_ref[l]))
        hid = jnp.sin(fq * (_dot_f32(hid, w2_ref[l]) + b2_ref[l]))
        hf = _dot_f32(hid, w3_ref[l])
        taps.append(hf[:, :HY_WIDTH] * decay)
        taps.append(jnp.where(row == 0, 0.0, hf[:, HY_WIDTH:] * decay))
    h_hi, h_lo = _split_bf16(jnp.concatenate(taps, axis=1))
    fwd_hi = fwd_hi_ref[...]
    spec = _dot(fwd_hi, h_hi) + (_dot(fwd_hi, h_lo) + _dot(fwd_lo_ref[...], h_hi))
    for l in range(DEPTH):
        _store_spectrum_tables(spec[:, 2 * l * HY_WIDTH:2 * (l + 1) * HY_WIDTH], l, ga_ref, gbc_ref, gd_ref)


def _pad_to(a, shape):
    return jnp.pad(a, [(0, s - d) for s, d in zip(shape, a.shape)])


def _filter_call(L, w1, b1, fq, w2, b2, w3):
    feats, decay = _filter_tables(L)
    fwd, _ = _dft_tables(L)
    row_vec = lambda a: _pad_to(a.reshape(DEPTH, 1, -1), (DEPTH, 1, LANES))
    args = [
        jnp.asarray(feats),
        _pad_to(w1, (DEPTH, LANES, LANES)),
        row_vec(b1),
        row_vec(fq),
        _pad_to(w2, (DEPTH, LANES, LANES)),
        row_vec(b2),
        _pad_to(w3, (DEPTH, LANES, 2 * HY_WIDTH)),
        jnp.asarray(decay),
        *_split_bf16(jnp.asarray(fwd)),
    ]
    out = jax.ShapeDtypeStruct((DEPTH, L, HY_WIDTH), F32)
    return pl.pallas_call(
        _filter_kernel,
        out_shape=[out, out, out],
        compiler_params=_params(),
        name="hyena_filter",
    )(*args)


def _conv_filter_kernel(taps_ref, fwd_ref, ga_ref, gbc_ref, gd_ref):
    spec = _dot_f32(fwd_ref[...], taps_ref[...])
    for l in range(DEPTH):
        _store_spectrum_tables(spec[:, 2 * l * CF_WIDTH:2 * (l + 1) * CF_WIDTH], l, ga_ref, gbc_ref, gd_ref)


def _conv_filter_call(L, conv_w):
    half = CF_CONV_K // 2
    fwd, _ = _dft_tables(L)
    lag_pos = conv_w[:, half::-1, :]
    lag_neg = _pad_to(conv_w[:, half + 1:, :], (DEPTH, half + 1, CF_WIDTH))
    lag_neg = jnp.roll(lag_neg, 1, axis=1)
    taps = jnp.concatenate([lag_pos, lag_neg], axis=2)
    taps = _pad_to(taps, (DEPTH, LANES, 2 * CF_WIDTH))
    taps = jnp.transpose(taps, (1, 0, 2)).reshape(LANES, DEPTH * 2 * CF_WIDTH)
    out = jax.ShapeDtypeStruct((DEPTH, L, CF_WIDTH), F32)
    return pl.pallas_call(
        _conv_filter_kernel,
        out_shape=[out, out, out],
        compiler_params=_params(),
        name="conv_filter",
    )(taps, jnp.asarray(fwd[:, :LANES]))


def _dft_conv(us, tabs, fwd_ref, inv_ref):
    L, W = us[0].shape
    u_all = jnp.concatenate(us, axis=1).astype(BF16)
    spec = _dot(fwd_ref[...], u_all)
    y_re, y_im = [], []
    for s, (ga, gbc, gd) in enumerate(tabs):
        p = spec[:L, s * W:(s + 1) * W]
        q = spec[L:, s * W:(s + 1) * W]
        y_re.append(p * ga - q * gbc)
        y_im.append(p * gbc + q * gd)
    y_spec = jnp.concatenate(
        [jnp.concatenate(y_re, axis=1), jnp.concatenate(y_im, axis=1)], axis=0).astype(BF16)
    y = _dot(inv_ref[...], y_spec)
    return [y[:, s * W:(s + 1) * W] for s in range(len(us))]


def _hyena_front(hy, pad_ref, cw_ref, cb_ref):
    L, width = hy.shape
    zeros = jnp.zeros((SUBLANES, width), F32)
    pad_ref[0:SUBLANES, :] = zeros
    pad_ref[SUBLANES + L:2 * SUBLANES + L, :] = zeros
    pad_ref[SUBLANES:SUBLANES + L, :] = hy
    conv = cb_ref[:, :width]
    for j in range(HY_SHORT_K):
        start = SUBLANES + j - HY_SHORT_K // 2
        conv = conv + cw_ref[j:j + 1, :] * pad_ref[start:start + L, :]
    return conv[:, :HY_WIDTH], conv[:, HY_WIDTH:2 * HY_WIDTH] * conv[:, 2 * HY_WIDTH:]


def _mix_sequences(hys, cfs, r):
    n = len(hys)
    x0s, us, tabs = [], [], []
    for s in range(n):
        x0, u = _hyena_front(hys[s], r["hy_pad"].at[s], r["hy_cw"], r["hy_cb"])
        x0s.append(x0)
        us.append(u)
        tabs.append((r["hy_ga"][...], r["hy_gbc"][...], r["hy_gd"][...]))
    for s in range(n):
        us.append(cfs[s][:, :CF_WIDTH] * jax.nn.sigmoid(cfs[s][:, CF_WIDTH:]))
        tabs.append((r["cf_ga"][...], r["cf_gbc"][...], r["cf_gd"][...]))
    ys = _dft_conv(us, tabs, r["fwd"], r["inv"])
    y_hy = [x0s[s] * (ys[s] + us[s] * r["hy_bias"][...]) for s in range(n)]
    y_cf = []
    for s in range(n):
        y = _layer_norm_rows(ys[n + s] + r["cf_cb"][...], r["cf_g"][...], r["cf_b"][...])
        y_cf.append(y * jax.nn.sigmoid(y))
    return y_hy, y_cf


def _mixer_inputs(seq, wts, layer, hy_tabs, cf_tabs):
    fwd, inv = _dft_tables(seq)
    ins = [("fwd", jnp.asarray(fwd).astype(BF16), _const_spec((2 * seq, seq))),
           ("inv", jnp.asarray(inv).astype(BF16), _const_spec((seq, 2 * seq)))]
    for name, tab in zip(("hy_ga", "hy_gbc", "hy_gd"), hy_tabs):
        ins.append((name, tab, _layer_spec((seq, HY_WIDTH), layer)))
    for name, tab in zip(("cf_ga", "cf_gbc", "cf_gd"), cf_tabs):
        ins.append((name, tab, _layer_spec((seq, CF_WIDTH), layer)))
    ins.append(("hy_cw", wts["hy_conv_w"], _layer_spec((HY_SHORT_K, 3 * HY_WIDTH), layer)))
    for name, key in (("hy_cb", "hy_conv_b"), ("hy_bias", "hy_bias"),
                      ("cf_cb", "cf_conv_b"), ("cf_g", "cf_ln_g"), ("cf_b", "cf_ln_b")):
        ins.append((name, wts["vecs"], _vec_spec(key, layer)))
    return ins


def _hyena_kernel(hy_ref, cw_ref, cb_ref, fwd_ref, inv_ref, ga_ref, gbc_ref, gd_ref, hb_ref,
                  o_ref, pad_ref):
    nb = hy_ref.shape[0]
    x0s, us = zip(*[_hyena_front(hy_ref[s], pad_ref.at[s], cw_ref, cb_ref) for s in range(nb)])
    tabs = [(ga_ref[...], gbc_ref[...], gd_ref[...])] * nb
    ys = _dft_conv(us, tabs, fwd_ref, inv_ref)
    for s in range(nb):
        o_ref[s] = (x0s[s] * (ys[s] + us[s] * hb_ref[...])).astype(o_ref.dtype)


def _hyena_call(hy3, conv_w, conv_b, g_tabs, hy_bias, layer):
    B, L, width = hy3.shape
    nb = _seqs_per_step(B, L) if L < MIXER_ROWS else min(B, HYENA_LONG_SEQS)
    fwd, inv = _dft_tables(L)
    fwd_bf = jnp.asarray(fwd).astype(BF16)
    inv_bf = jnp.asarray(inv).astype(BF16)
    tab = _layer_spec((L, HY_WIDTH), layer)
    return pl.pallas_call(
        _hyena_kernel,
        grid=(B // nb,),
        in_specs=[
            pl.BlockSpec((nb, L, width), lambda b: (b, 0, 0)),
            _layer_spec((HY_SHORT_K, width), layer),
            _vec_spec("hy_conv_b", layer),
            _const_spec((2 * L, L)),
            _const_spec((L, 2 * L)),
            tab, tab, tab,
            _vec_spec("hy_bias", layer),
        ],
        out_specs=pl.BlockSpec((nb, L, HY_WIDTH), lambda b: (b, 0, 0)),
        out_shape=jax.ShapeDtypeStruct((B, L, HY_WIDTH), BF16),
        scratch_shapes=[pltpu.VMEM((nb, L + 2 * SUBLANES, width), F32)],
        compiler_params=_params(dimension_semantics=("arbitrary",)),
        name="hyena",
    )(hy3, conv_w, conv_b, fwd_bf, inv_bf, *g_tabs, hy_bias)


CF_PAD = 2 * SUBLANES


def _conformer_kernel(cf_ref, cw_ref, cb_ref, g_ref, b_ref, o_ref, pad_ref):
    nb, L, _ = cf_ref.shape
    padded = L + 2 * CF_PAD
    shifted = padded - SUBLANES
    zeros = jnp.zeros((CF_PAD, CF_WIDTH), F32)
    half = CF_CONV_K // 2
    for s in range(nb):
        cf = cf_ref[s]
        pad_ref[s, 0, 0:CF_PAD, :] = zeros
        pad_ref[s, 0, CF_PAD + L:padded, :] = zeros
        pad_ref[s, 0, CF_PAD:CF_PAD + L, :] = cf[:, :CF_WIDTH] * jax.nn.sigmoid(cf[:, CF_WIDTH:])
        for r in range(1, SUBLANES):
            pad_ref[s, r, 0:shifted, :] = pad_ref[s, 0, r:r + shifted, :]
        for c0 in range(0, L, CF_ROW_CHUNK):
            acc = jnp.zeros((CF_ROW_CHUNK, CF_WIDTH), F32) + cb_ref[...]
            for j in range(CF_CONV_K):
                off = CF_PAD + j - half
                start = c0 + (off // SUBLANES) * SUBLANES
                acc = acc + cw_ref[j:j + 1, :] * pad_ref[s, off % SUBLANES, start:start + CF_ROW_CHUNK, :]
            y = _layer_norm_rows(acc, g_ref[...], b_ref[...])
            o_ref[s, c0:c0 + CF_ROW_CHUNK, :] = (y * jax.nn.sigmoid(y)).astype(o_ref.dtype)


def _conformer_call(cf3, conv_w, conv_b, ln_g, ln_b, layer):
    B, L, width = cf3.shape
    nb = _seqs_per_step(B, L)
    return pl.pallas_call(
        _conformer_kernel,
        grid=(B // nb,),
        in_specs=[
            pl.BlockSpec((nb, L, width), lambda b: (b, 0, 0)),
            _layer_spec((CF_CONV_K, CF_WIDTH), layer),
            _vec_spec("cf_conv_b", layer), _vec_spec("cf_ln_g", layer), _vec_spec("cf_ln_b", layer),
        ],
        out_specs=pl.BlockSpec((nb, L, CF_WIDTH), lambda b: (b, 0, 0)),
        out_shape=jax.ShapeDtypeStruct((B, L, CF_WIDTH), BF16),
        scratch_shapes=[pltpu.VMEM((nb, SUBLANES, L + 2 * CF_PAD, CF_WIDTH), F32)],
        compiler_params=_params(dimension_semantics=("arbitrary",)),
        name="conformer",
    )(cf3, conv_w, conv_b, ln_g, ln_b)


def _attn_kernel(*refs, lambda_init, past, n_cast):
    q_ref, k_ref, v_ref = refs[:3]
    n_in = 3
    if past:
        ck_ref, cv_ref = refs[3:5]
        k_all, v_aug = refs[-2:]
        n_in = 5
    lq1, lk1, lq2, lk2, g_ref = refs[n_in:n_in + 5]
    cast_in = refs[n_in + 5:n_in + 5 + n_cast]
    o_ref = refs[n_in + 5 + n_cast]
    cast_out = refs[n_in + 6 + n_cast:n_in + 6 + 2 * n_cast]
    _cast_chunks(cast_in, cast_out)
    nb, tq, _ = q_ref.shape

    if past:
        keys = v_aug.shape[1]

        @pl.when(pl.program_id(1) == 0)
        def _():
            ones = jnp.ones((keys, HEAD_PAIR), BF16)
            for s in range(nb):
                k_all[s, 0:past, :] = ck_ref[s, 0].reshape(ATT_WIDTH, past).T.astype(BF16)
                k_all[s, past:keys, :] = k_ref[s]
                v_old = cv_ref[s, 0].reshape(past, ATT_WIDTH).astype(BF16)
                for h in range(DIFF_HEADS):
                    cols = slice(h * HEAD_PAIR, (h + 1) * HEAD_PAIR)
                    base = 2 * h * HEAD_PAIR
                    v_aug[s, 0:past, base:base + HEAD_PAIR] = v_old[:, cols]
                    v_aug[s, past:keys, base:base + HEAD_PAIR] = v_ref[s, :, cols]
                    v_aug[s, :, base + HEAD_PAIR:base + 2 * HEAD_PAIR] = ones

    lam = (jnp.exp(jnp.sum(lq1[...] * lk1[...], axis=-1, keepdims=True))
           - jnp.exp(jnp.sum(lq2[...] * lk2[...], axis=-1, keepdims=True)) + lambda_init)
    lane = lax.broadcasted_iota(jnp.int32, (tq, HEAD_PAIR), 1)
    first = lane < DIFF_HEAD_DIM
    nt = (((1,), (1,)), ((), ()))
    gain = g_ref[...] * (1.0 - lambda_init)
    zero = jnp.zeros((), BF16)

    def scores(s, h):
        cols = slice(h * HEAD_PAIR, (h + 1) * HEAD_PAIR)
        q = q_ref[s, :, cols]
        q2 = jnp.concatenate([jnp.where(first, q, zero), jnp.where(first, zero, q)], axis=0)
        kh = k_all[s, :, cols] if past else k_ref[s, :, cols]
        return lax.dot_general(q2, kh, nt, preferred_element_type=F32)

    def normalize_store(s, h, o):
        o = o * lax.rsqrt(jnp.mean(o * o, axis=-1, keepdims=True) + LN_EPS) * gain
        o_ref[s, :, h * HEAD_PAIR:(h + 1) * HEAD_PAIR] = o.astype(o_ref.dtype)

    for s in range(nb):
        if past:
            for h in range(DIFF_HEADS):
                sc = scores(s, h)
                e = jnp.exp2(sc - jnp.max(sc, axis=-1, keepdims=True)).astype(BF16)
                p = _dot(e, v_aug[s, :, 2 * h * HEAD_PAIR:2 * (h + 1) * HEAD_PAIR])
                p0, p1 = p[:tq], p[tq:]
                normalize_store(s, h, p0[:, :HEAD_PAIR] * (1.0 / p0[:, HEAD_PAIR:HEAD_PAIR + 1])
                                - p1[:, :HEAD_PAIR] * (lam / p1[:, HEAD_PAIR:HEAD_PAIR + 1]))
        else:
            sc = jnp.concatenate([scores(s, h) for h in range(DIFF_HEADS)], axis=0)
            e = jnp.exp2(sc - jnp.max(sc, axis=-1, keepdims=True))
            r = 1.0 / jnp.sum(e, axis=-1, keepdims=True)
            for h in range(DIFF_HEADS):
                r0 = slice(2 * h * tq, (2 * h + 1) * tq)
                r1 = slice((2 * h + 1) * tq, (2 * h + 2) * tq)
                a = e[r0] * r[r0] - e[r1] * (lam * r[r1])
                normalize_store(s, h, _dot(a.astype(BF16), v_ref[s, :, h * HEAD_PAIR:(h + 1) * HEAD_PAIR]))


def _attn_call(q3, k3, v3, cache_k, cache_v, layer, vecs, cast=()):
    B, L, _ = q3.shape
    past = 0 if cache_k is None else cache_k.shape[-1]
    nb = _seqs_per_step(B, L)
    tq = min(Q_TILE, L)
    cast_in, cast_args, cast_out_specs, cast_out_shapes = _cast_plan(
        cast, (B // nb) * (L // tq), lambda b, i: b * (L // tq) + i)
    tile = pl.BlockSpec((nb, tq, ATT_WIDTH), lambda b, i: (b, i, 0))
    seq = pl.BlockSpec((nb, L, ATT_WIDTH), lambda b, i: (b, 0, 0))
    in_specs = [tile, seq, seq]
    args = [q3, k3, v3]
    scratch = []
    if past:
        for cache in (cache_k, cache_v):
            in_specs.append(pl.BlockSpec((nb, 1) + cache.shape[2:], lambda b, i: (b, layer, 0, 0, 0)))
            args.append(cache)
        scratch = [pltpu.VMEM((nb, past + L, ATT_WIDTH), BF16),
                   pltpu.VMEM((nb, past + L, 2 * ATT_WIDTH), BF16)]
    in_specs += [_vec_spec(n, layer) for n in ("lam_q1", "lam_k1", "lam_q2", "lam_k2", "subln_g")]
    args += [vecs] * 5
    return pl.pallas_call(
        functools.partial(_attn_kernel, lambda_init=_lambda_init(layer), past=past, n_cast=len(cast)),
        grid=(B // nb, L // tq),
        in_specs=in_specs + cast_in,
        out_specs=[tile] + cast_out_specs,
        out_shape=[jax.ShapeDtypeStruct((B, L, ATT_WIDTH), BF16)] + cast_out_shapes,
        scratch_shapes=scratch,
        compiler_params=_params(dimension_semantics=("arbitrary", "arbitrary")),
        name="diff_attn",
    )(*args, *cast_args)


def _mlp_kernel(x_ref, hy_ref, cf_ref, at_ref, ada_ref, wo_ref, w1_ref, w2_ref,
                g1_ref, b1_ref, g2_ref, b2_ref, *rest):
    n_cast = (len(rest) - 1) // 2
    o_ref = rest[n_cast]
    _cast_chunks(rest[:n_cast], rest[n_cast + 1:])
    gate1 = ada_ref[0, 0, 2:3, :]
    sh2 = ada_ref[0, 0, 3:4, :]
    sc2 = ada_ref[0, 0, 4:5, :]
    gate2 = ada_ref[0, 0, 5:6, :]
    groups = [slice(r, r + MLP_GROUP_ROWS) for r in range(0, x_ref.shape[0], MLP_GROUP_ROWS)]
    ys = [_dot(jnp.concatenate([hy_ref[g, :], cf_ref[g, :], at_ref[g, :]], axis=1), wo_ref[...])
          for g in groups]
    xs, acts = [], []
    for g, y in zip(groups, ys):
        x = _layer_norm_rows(DEEPNORM_ALPHA * x_ref[g, :] + gate1 * y, g1_ref[...], b1_ref[...])
        h = (x * (1.0 + sc2) + sh2).astype(BF16)
        xs.append(x)
        acts.append(jnp.maximum(_dot(h, w1_ref[...]), 0.0))
    ms = [_dot((a * a).astype(BF16), w2_ref[...]) for a in acts]
    for g, x, m in zip(groups, xs, ms):
        o_ref[g, :] = _layer_norm_rows(DEEPNORM_ALPHA * x + gate2 * m, g2_ref[...], b2_ref[...])


def _mlp_call(x2d, y_hy, y_cf, y_at, ada4, wts, layer, cond_row_of_row, cast_next):
    T = x2d.shape[0]
    steps = T // MLP_ROW_TILE
    cast_in, cast_args, cast_out_specs, cast_out_shapes = _cast_plan(cast_next, steps, lambda i: i)
    rows = lambda w: pl.BlockSpec((MLP_ROW_TILE, w), lambda i: (i, 0))
    in_specs = [
        rows(D_MODEL), rows(HY_WIDTH), rows(CF_WIDTH), rows(ATT_WIDTH),
        pl.BlockSpec((1, 1, N_ADA, D_MODEL), lambda i: (layer, cond_row_of_row(i * MLP_ROW_TILE), 0, 0)),
        _layer_spec((MIX_WIDTH, D_MODEL), 0),
        _layer_spec((D_MODEL, D_FF), 0),
        _layer_spec((D_FF, D_MODEL), 0),
        _vec_spec("ln1_g", layer), _vec_spec("ln1_b", layer), _vec_spec("ln2_g", layer), _vec_spec("ln2_b", layer),
    ]
    args = [x2d, y_hy, y_cf, y_at, ada4, wts["w_out"][layer], wts["w_mlp1"][layer], wts["w_mlp2"][layer]]
    args += [wts["vecs"]] * 4
    return pl.pallas_call(
        _mlp_kernel,
        grid=(steps,),
        in_specs=in_specs + cast_in,
        out_specs=[rows(D_MODEL)] + cast_out_specs,
        out_shape=[jax.ShapeDtypeStruct((T, D_MODEL), F32)] + cast_out_shapes,
        compiler_params=_params(MLP_VMEM_LIMIT, dimension_semantics=("arbitrary",)),
        name="outproj_mlp",
    )(*args, *cast_args)


MLP_WEIGHTS = ("w_out", "w_mlp1", "w_mlp2")


def _layer(x2d, batch, seq, layer, ada4, wts, f32_weights, hy_tabs, cf_tabs, cond_row_of_row, rope_tabs,
           cache_kv, new_caches, casts):
    fused = seq <= FUSED_MIXER_MAX_SEQ and ROW_TILE % seq == 0
    z = _inproj_call(x2d, ada4, wts, layer, batch, seq, cond_row_of_row, rope_tabs, new_caches,
                     (hy_tabs, cf_tabs) if fused else None)
    as_seq = lambda a: a.reshape(batch, seq, a.shape[-1])
    flat = lambda a: a.reshape(batch * seq, a.shape[-1])
    if fused:
        y_hy, y_cf = z["yhy"], z["ycf"]
    else:
        vecs = wts["vecs"]
        y_hy = flat(_hyena_call(as_seq(z["hy"]), wts["hy_conv_w"], vecs, hy_tabs, vecs, layer))
        y_cf = flat(_conformer_call(as_seq(z["cf"]), wts["cf_conv_w"], vecs, vecs, vecs, layer))
    ck, cv = cache_kv if cache_kv is not None else (None, None)
    cast = []
    if casts:
        cast = [(n, f32_weights[n], layer) for n in MLP_WEIGHTS]
        if layer + 1 < DEPTH:
            cast.append(("w_in", f32_weights["w_in"], layer + 1))
    y_at, *cast_weights = _attn_call(as_seq(z["q"]), as_seq(z["k"]), as_seq(z["v"]), ck, cv, layer, wts["vecs"],
                                     [(w, l) for _, w, l in cast])
    for (name, _, _), w_bf in zip(cast, cast_weights):
        wts[name].append(w_bf)
    x_out, = _mlp_call(x2d, y_hy, y_cf, flat(y_at), ada4, wts, layer, cond_row_of_row, ())
    return x_out, (z["kc"], z["vc"]) if "kc" in z else None


def kernel(x_prompt, x_sample, cache_k, cache_v, c, c_ctx, w_ada, b_ada, w_in, hy_conv_w, hy_conv_b, hf_w1, hf_b1, hf_freq, hf_w2, hf_b2, hf_w3, hy_bias, cf_conv_w, cf_conv_b, cf_ln_g, cf_ln_b, lam_q1, lam_k1, lam_q2, lam_k2, subln_g, w_out, ln1_g, ln1_b, w_mlp1, w_mlp2, ln2_g, ln2_b):
    batch, seq, _ = x_prompt.shape
    dec_batch, dec_seq, _ = x_sample.shape
    assert 1 + dec_batch <= COND_ROWS
    assert ROW_TILE % seq == 0 and (batch * seq) % ROW_TILE == 0
    assert dec_seq % ROW_TILE == 0 and dec_seq % Q_TILE == 0
    assert (batch * seq) % MLP_ROW_TILE == 0 and dec_seq % MLP_ROW_TILE == 0

    cond = jnp.concatenate(
        [c_ctx[None, :], c, jnp.zeros((COND_ROWS - 1 - dec_batch, D_MODEL), F32)], axis=0)
    ada4 = _ada_call(cond, w_ada, b_ada).reshape(DEPTH, COND_ROWS, N_ADA, D_MODEL)

    f32_weights = dict(w_in=w_in, w_out=w_out, w_mlp1=w_mlp1, w_mlp2=w_mlp2)
    wts = dict(w_in=[w_in[0:1].astype(BF16)], w_out=[], w_mlp1=[], w_mlp2=[])
    wts.update(
        hy_conv_w=hy_conv_w, cf_conv_w=cf_conv_w,
        vecs=_pack_vectors(dict(
            ln1_g=ln1_g, ln1_b=ln1_b, ln2_g=ln2_g, ln2_b=ln2_b, hy_conv_b=hy_conv_b, hy_bias=hy_bias,
            cf_conv_b=cf_conv_b, cf_ln_g=cf_ln_g, cf_ln_b=cf_ln_b, subln_g=subln_g,
            lam_q1=lam_q1, lam_k1=lam_k1, lam_q2=lam_q2, lam_k2=lam_k2)))

    rope_tabs = tuple(jnp.asarray(t) for t in _rope_tables(dec_seq))
    cache_kt = jnp.transpose(cache_k, (0, 1, 3, 4, 2))

    xp = x_prompt.reshape(batch * seq, D_MODEL)
    xs = x_sample.reshape(dec_batch * dec_seq, D_MODEL)
    new_caches = ()
    filt = (hf_w1, hf_b1, hf_freq, hf_w2, hf_b2, hf_w3)
    hy_ctx = _filter_call(seq, *filt)
    hy_dec = _filter_call(dec_seq, *filt)
    fused = lambda n: n <= FUSED_MIXER_MAX_SEQ and ROW_TILE % n == 0
    cf_ctx = _conv_filter_call(seq, cf_conv_w) if fused(seq) else None
    cf_dec = _conv_filter_call(dec_seq, cf_conv_w) if fused(dec_seq) else None
    for l in range(DEPTH):
        xp, new_caches = _layer(xp, batch, seq, l, ada4, wts, f32_weights, hy_ctx, cf_ctx, lambda row: 0,
                                None, None, new_caches, True)
        xs, _ = _layer(xs, dec_batch, dec_seq, l, ada4, wts, f32_weights, hy_dec, cf_dec,
                       lambda row: 1 + row // dec_seq, rope_tabs, (cache_kt, cache_v), None, False)
    new_cache_kt, new_cache_v = new_caches
    new_cache_k = jnp.transpose(new_cache_kt, (0, 1, 4, 2, 3))
    return (xp.reshape(batch, seq, D_MODEL), xs.reshape(dec_batch, dec_seq, D_MODEL),
            new_cache_k, new_cache_v)
```

```python
import functools
import math

import numpy as np
import jax
import jax.numpy as jnp
from jax import lax
from jax.experimental import pallas as pl
from jax.experimental.pallas import tpu as pltpu

D_MODEL = 1024
DEPTH = 2
GRID_W = 64
HY_WIDTH = D_MODEL // 4
CF_WIDTH = D_MODEL // 4
ATT_WIDTH = D_MODEL // 2
DIFF_HEAD_DIM = 64
DIFF_HEADS = ATT_WIDTH // (2 * DIFF_HEAD_DIM)
HEAD_PAIR = 2 * DIFF_HEAD_DIM
MIX_WIDTH = HY_WIDTH + CF_WIDTH + ATT_WIDTH
IN_WIDTH = 3 * HY_WIDTH + 2 * CF_WIDTH + 3 * ATT_WIDTH
HY_SHORT_K = 3
HY_FILTER_EMB = 33
HY_FILTER_HIDDEN = 64
HY_FAST_DECAY_PCT = 0.3
HY_SLOW_DECAY_PCT = 1.5
HY_DECAY_TARGET = 1e-2
CF_CONV_K = 31
D_FF = 4 * D_MODEL
ROPE_BASE = 10000.0
AX_DIM = DIFF_HEAD_DIM // 2
DEEPNORM_ALPHA = (2 * DEPTH) ** 0.25
LN_EPS = 1e-5
N_ADA = 6

LANES = 128
SUBLANES = 8
COND_ROWS = SUBLANES
VMEM_LIMIT = 56 * 1024 * 1024
MLP_VMEM_LIMIT = VMEM_LIMIT

ROW_TILE = 512
MLP_ROW_TILE = 512
MLP_GROUP_ROWS = 256
ADA_COL_TILE = 1536
Q_TILE = 512
MIXER_ROWS = 1024
HYENA_LONG_SEQS = 2
FUSED_MIXER_MAX_SEQ = 256
CF_ROW_CHUNK = 128

F32 = jnp.float32
BF16 = jnp.bfloat16


def _lambda_init(l):
    return 0.8 - 0.6 * math.exp(-0.3 * l)


def _params(vmem_limit_bytes=VMEM_LIMIT, **kw):
    return pltpu.CompilerParams(vmem_limit_bytes=vmem_limit_bytes, **kw)


def _const_spec(shape):
    zeros = (0,) * len(shape)
    return pl.BlockSpec(shape, lambda *_: zeros, pipeline_mode=pl.Buffered(1))


def _layer_spec(shape, layer):
    zeros = (0,) * len(shape)
    return pl.BlockSpec((None,) + tuple(shape), lambda *_: (layer,) + zeros,
                        pipeline_mode=pl.Buffered(1))


VEC_SLOTS = (
    ("ln1_g", D_MODEL), ("ln1_b", D_MODEL), ("ln2_g", D_MODEL), ("ln2_b", D_MODEL),
    ("hy_conv_b", D_MODEL),
    ("hy_bias", HY_WIDTH), ("cf_conv_b", CF_WIDTH), ("cf_ln_g", CF_WIDTH), ("cf_ln_b", CF_WIDTH),
    ("subln_g", HEAD_PAIR), ("lam_q1", LANES), ("lam_k1", LANES), ("lam_q2", LANES), ("lam_k2", LANES),
)


def _vec_offsets():
    offsets, off = {}, 0
    for name, width in VEC_SLOTS:
        assert off % width == 0
        offsets[name] = (off, width)
        off += width
    return offsets, off


def _pack_vectors(params):
    pieces = [jnp.pad(params[name], ((0, 0), (0, width - params[name].shape[1]))) for name, width in VEC_SLOTS]
    return jnp.concatenate(pieces, axis=1).reshape(DEPTH, 1, -1)


def _vec_spec(name, layer):
    off, width = _vec_offsets()[0][name]
    return pl.BlockSpec((None, 1, width), lambda *_: (layer, 0, off // width), pipeline_mode=pl.Buffered(1))


def _dot(a, b):
    return jnp.dot(a, b, preferred_element_type=F32)


def _dot_f32(a, b):
    return jnp.dot(a, b, preferred_element_type=F32, precision=lax.Precision.HIGHEST)


def _split_bf16(x):
    hi = x.astype(BF16)
    return hi, (x - hi.astype(F32)).astype(BF16)


def _layer_norm_rows(x, g, b):
    mu = jnp.mean(x, axis=-1, keepdims=True)
    xc = x - mu
    var = jnp.mean(xc * xc, axis=-1, keepdims=True)
    return xc * lax.rsqrt(var + LN_EPS) * g + b


def _seqs_per_step(batch, seq):
    return max(1, min(batch, MIXER_ROWS // seq))


def _cast_plan(items, steps, chunk_of_step):
    in_specs, out_specs, out_shapes = [], [], []
    for w, layer in items:
        _, n_rows, n_cols = w.shape
        chunk = n_rows // steps
        assert chunk * steps == n_rows and chunk % (2 * SUBLANES) == 0
        in_specs.append(pl.BlockSpec((1, chunk, n_cols), lambda *g, layer=layer: (layer, chunk_of_step(*g), 0)))
        out_specs.append(pl.BlockSpec((1, chunk, n_cols), lambda *g: (0, chunk_of_step(*g), 0)))
        out_shapes.append(jax.ShapeDtypeStruct((1, n_rows, n_cols), BF16))
    return in_specs, [w for w, _ in items], out_specs, out_shapes


def _cast_chunks(srcs, dsts):
    for src, dst in zip(srcs, dsts):
        dst[...] = src[...].astype(BF16)


@functools.lru_cache(maxsize=None)
def _dft_tables(L):
    n = 2 * L
    k = np.arange(L, dtype=np.float64)[:, None]
    s = np.arange(L, dtype=np.float64)[None, :]
    ang = 2.0 * np.pi * k * s / n
    fwd = np.concatenate([np.cos(ang), -np.sin(ang)], axis=0)
    fwd[L, :] = np.cos(np.pi * np.arange(L))
    t = np.arange(L, dtype=np.float64)[:, None]
    kk = np.arange(L, dtype=np.float64)[None, :]
    ang_i = 2.0 * np.pi * t * kk / n
    inv_re = (2.0 / n) * np.cos(ang_i)
    inv_re[:, 0] = 1.0 / n
    inv_im = -(2.0 / n) * np.sin(ang_i)
    inv_im[:, 0] = np.cos(np.pi * np.arange(L)) / n
    inv = np.concatenate([inv_re, inv_im], axis=1)
    return fwd.astype(np.float32), inv.astype(np.float32)


@functools.lru_cache(maxsize=None)
def _filter_tables(L):
    bands = (HY_FILTER_EMB - 1) // 2
    t = np.linspace(0.0, 1.0, L)[:, None]
    w = (2.0 * np.pi / L) * np.arange(L, dtype=np.float64)[:, None]
    fr = np.linspace(1e-4, bands - 1, bands)[None, :]
    feats = np.concatenate([t, np.cos(fr * w), -np.sin(fr * w)], -1)
    feats_p = np.zeros((L, LANES), np.float64)
    feats_p[:, :HY_FILTER_EMB] = feats
    deltas = np.abs(np.linspace(math.log(HY_DECAY_TARGET) / HY_FAST_DECAY_PCT,
                                math.log(HY_DECAY_TARGET) / HY_SLOW_DECAY_PCT, HY_WIDTH))
    decay = np.exp(-t * deltas[None, :])
    return feats_p.astype(np.float32), decay.astype(np.float32)


@functools.lru_cache(maxsize=None)
def _rope_tables(n):
    half = AX_DIM // 2
    pos = np.arange(n)
    row = (pos // GRID_W).astype(np.float64)[:, None]
    col = (pos % GRID_W).astype(np.float64)[:, None]
    inv = ROPE_BASE ** (-np.arange(0, AX_DIM, 2, dtype=np.float64) / AX_DIM)[None, :]
    zero = np.zeros((n, half))
    cr, sr = np.cos(row * inv), np.sin(row * inv)
    cc, sc = np.cos(col * inv), np.sin(col * inv)
    c = np.concatenate([cr, cr, cc, cc], -1)
    s_up = np.concatenate([-sr, zero, -sc, zero], -1)
    s_dn = np.concatenate([zero, sr, zero, sc], -1)
    reps = LANES // DIFF_HEAD_DIM
    tile = lambda a: np.tile(a, (1, reps)).astype(np.float32)
    return tile(c), tile(s_up), tile(s_dn)


def _ada_kernel(cond_ref, w_ref, b_ref, o_ref):
    c = cond_ref[...]
    s = c * jax.nn.sigmoid(c)
    o_ref[0] = _dot(s.astype(BF16), w_ref[0].astype(BF16)) + b_ref[0]


def _ada_call(cond, w_ada, b_ada):
    n_out = N_ADA * D_MODEL
    return pl.pallas_call(
        _ada_kernel,
        grid=(DEPTH, n_out // ADA_COL_TILE),
        in_specs=[
            pl.BlockSpec((COND_ROWS, D_MODEL), lambda l, j: (0, 0)),
            pl.BlockSpec((1, D_MODEL, ADA_COL_TILE), lambda l, j: (l, 0, j)),
            pl.BlockSpec((1, 1, ADA_COL_TILE), lambda l, j: (l, 0, j)),
        ],
        out_specs=pl.BlockSpec((1, COND_ROWS, ADA_COL_TILE), lambda l, j: (l, 0, j)),
        out_shape=jax.ShapeDtypeStruct((DEPTH, COND_ROWS, n_out), F32),
        compiler_params=_params(dimension_semantics=("arbitrary", "arbitrary")),
        name="ada",
    )(cond, w_ada, b_ada.reshape(DEPTH, 1, n_out))


def _rope_lanes(x, c, s_up, s_dn):
    outs = []
    for p in range(ATT_WIDTH // LANES):
        xb = x[:, p * LANES:(p + 1) * LANES]
        up = pltpu.roll(xb, LANES - AX_DIM // 2, 1)
        dn = pltpu.roll(xb, AX_DIM // 2, 1)
        outs.append(xb * c + up * s_up + dn * s_dn)
    return jnp.concatenate(outs, axis=1)


Q_SCALE = DIFF_HEAD_DIM ** -0.5 * math.log2(math.e)


def _inproj_kernel(*refs, names, seq, cache_slot):
    r = dict(zip(names, refs))
    sh1 = r["ada"][0, 0, 0:1, :]
    sc1 = r["ada"][0, 0, 1:2, :]
    h = (r["x"][...] * (1.0 + sc1) + sh1).astype(BF16)
    w_ref = r["w"]
    o1 = 3 * HY_WIDTH
    o2 = o1 + 2 * CF_WIDTH
    o3 = o2 + ATT_WIDTH
    o4 = o3 + ATT_WIDTH
    hy = _dot(h, w_ref[:, 0:o1])
    cf = _dot(h, w_ref[:, o1:o2])
    q = _dot(h, w_ref[:, o2:o3])
    k = _dot(h, w_ref[:, o3:o4])
    v = _dot(h, w_ref[:, o4:IN_WIDTH])
    if "yhy" in r:
        seqs = hy.shape[0] // seq
        pieces = [slice(b * seq, (b + 1) * seq) for b in range(seqs)]
        y_hy, y_cf = _mix_sequences([hy[p] for p in pieces], [cf[p] for p in pieces], r)
        for b, p in enumerate(pieces):
            r["yhy"][p, :] = y_hy[b].astype(BF16)
            r["ycf"][p, :] = y_cf[b].astype(BF16)
    else:
        r["hy"][...] = hy
        r["cf"][...] = cf
    if "rope_c" in r:
        c, su, sd = r["rope_c"][...], r["rope_up"][...], r["rope_dn"][...]
        q = _rope_lanes(q, c, su, sd)
        k = _rope_lanes(k, c, su, sd)
    r["q"][...] = (q * Q_SCALE).astype(BF16)
    r["k"][...] = k.astype(BF16)
    r["v"][...] = v.astype(BF16)
    if "kc" in r:
        kc_ref, vc_ref = r["kc"], r["vc"]
        for b in range(kc_ref.shape[0]):
            rows = slice(b * seq, (b + 1) * seq)
            for slot in range(kc_ref.shape[1]):
                if slot == cache_slot:
                    kc_ref[b, slot] = k[rows].T.reshape(2 * DIFF_HEADS, DIFF_HEAD_DIM, seq)
                    vc_ref[b, slot] = v[rows].reshape(seq, DIFF_HEADS, 2 * DIFF_HEAD_DIM)
                else:
                    kc_ref[b, slot] = jnp.zeros(kc_ref.shape[2:], F32)
                    vc_ref[b, slot] = jnp.zeros(vc_ref.shape[2:], F32)


def _inproj_call(x2d, ada4, wts, layer, batch, seq, cond_row_of_row, rope_tabs, caches, mix_tabs):
    T = x2d.shape[0]
    rows = lambda w: pl.BlockSpec((ROW_TILE, w), lambda i: (i, 0))
    ins = [
        ("x", x2d, rows(D_MODEL)),
        ("ada", ada4, pl.BlockSpec((1, 1, N_ADA, D_MODEL), lambda i: (layer, cond_row_of_row(i * ROW_TILE), 0, 0))),
        ("w", wts["w_in"][layer], _layer_spec((D_MODEL, IN_WIDTH), 0)),
    ]
    outs = []
    scratch = []
    if rope_tabs is not None:
        tiles_per_seq = seq // ROW_TILE
        for name, tab in zip(("rope_c", "rope_up", "rope_dn"), rope_tabs):
            ins.append((name, tab, pl.BlockSpec((ROW_TILE, LANES), lambda i: (i % tiles_per_seq, 0))))
    if mix_tabs is not None:
        ins += _mixer_inputs(seq, wts, layer, *mix_tabs)
        outs.append(("yhy", rows(HY_WIDTH), jax.ShapeDtypeStruct((T, HY_WIDTH), BF16)))
        outs.append(("ycf", rows(CF_WIDTH), jax.ShapeDtypeStruct((T, CF_WIDTH), BF16)))
        scratch.append(("hy_pad", pltpu.VMEM((ROW_TILE // seq, seq + 2 * SUBLANES, 3 * HY_WIDTH), F32)))
    else:
        outs.append(("hy", rows(3 * HY_WIDTH), jax.ShapeDtypeStruct((T, 3 * HY_WIDTH), F32)))
        outs.append(("cf", rows(2 * CF_WIDTH), jax.ShapeDtypeStruct((T, 2 * CF_WIDTH), F32)))
    for name in ("q", "k", "v"):
        outs.append((name, rows(ATT_WIDTH), jax.ShapeDtypeStruct((T, ATT_WIDTH), BF16)))
    aliases = {}
    cache_slot = 0
    if caches is not None:
        seqs = ROW_TILE // seq
        first_call = len(caches) == 0
        cache_slot = layer if first_call else 0
        shapes = (("kc", (batch, DEPTH, 2 * DIFF_HEADS, DIFF_HEAD_DIM, seq)),
                  ("vc", (batch, DEPTH, seq, DIFF_HEADS, 2 * DIFF_HEAD_DIM)))
        for j, (name, shape) in enumerate(shapes):
            if first_call:
                spec = pl.BlockSpec((seqs, DEPTH) + shape[2:], lambda i: (i, 0, 0, 0, 0))
            else:
                spec = pl.BlockSpec((seqs, 1) + shape[2:], lambda i: (i, layer, 0, 0, 0))
                aliases[len(ins)] = len(outs)
                ins.append((name + "_in", caches[j], pl.BlockSpec(memory_space=pl.ANY)))
            outs.append((name, spec, jax.ShapeDtypeStruct(shape, F32)))
    names = tuple(n for n, _, _ in ins) + tuple(n for n, _, _ in outs) + tuple(n for n, _ in scratch)
    results = pl.pallas_call(
        functools.partial(_inproj_kernel, names=names, seq=seq, cache_slot=cache_slot),
        grid=(T // ROW_TILE,),
        in_specs=[spec for _, _, spec in ins],
        out_specs=[spec for _, spec, _ in outs],
        out_shape=[shape for _, _, shape in outs],
        scratch_shapes=[s for _, s in scratch],
        input_output_aliases=aliases,
        compiler_params=_params(dimension_semantics=("arbitrary",)),
        name="inproj",
    )(*[a for _, a, _ in ins])
    return dict(zip((n for n, _, _ in outs), results))


def _store_spectrum_tables(spec, layer, ga_ref, gbc_ref, gd_ref):
    L = spec.shape[0] // 2
    W = spec.shape[1] // 2
    row = lax.broadcasted_iota(jnp.int32, (L, W), 0)
    p = spec[:L]
    q = spec[L:]
    g_re = p[:, :W] + p[:, W:]
    g_im = q[:, :W] - q[:, W:]
    nyq = q[0:1, :W] + q[0:1, W:]
    ga_ref[layer] = g_re
    gbc_ref[layer] = jnp.where(row == 0, 0.0, g_im)
    gd_ref[layer] = jnp.where(row == 0, nyq, g_re)


def _filter_kernel(feats_ref, w1_ref, b1_ref, fq_ref, w2_ref, b2_ref, w3_ref, decay_ref,
                   fwd_hi_ref, fwd_lo_ref, ga_ref, gbc_ref, gd_ref):
    L = feats_ref.shape[0]
    decay = decay_ref[...]
    row = lax.broadcasted_iota(jnp.int32, (L, HY_WIDTH), 0)
    taps = []
    for l in range(DEPTH):
        fq = fq_ref[l]
        hid = jnp.sin(fq * (_dot_f32(feats_ref[...], w1_ref[l]) + b1_ref[l]))
        hid = jnp.sin(fq * (_dot_f32(hid, w2_ref[l]) + b2_ref[l]))
        hf = _dot_f32(hid, w3_ref[l])
        taps.append(hf[:, :HY_WIDTH] * decay)
        taps.append(jnp.where(row == 0, 0.0, hf[:, HY_WIDTH:] * decay))
    h_hi, h_lo = _split_bf16(jnp.concatenate(taps, axis=1))
    fwd_hi = fwd_hi_ref[...]
    spec = _dot(fwd_hi, h_hi) + (_dot(fwd_hi, h_lo) + _dot(fwd_lo_ref[...], h_hi))
    for l in range(DEPTH):
        _store_spectrum_tables(spec[:, 2 * l * HY_WIDTH:2 * (l + 1) * HY_WIDTH], l, ga_ref, gbc_ref, gd_ref)


def _pad_to(a, shape):
    return jnp.pad(a, [(0, s - d) for s, d in zip(shape, a.shape)])


def _filter_call(L, w1, b1, fq, w2, b2, w3):
    feats, decay = _filter_tables(L)
    fwd, _ = _dft_tables(L)
    row_vec = lambda a: _pad_to(a.reshape(DEPTH, 1, -1), (DEPTH, 1, LANES))
    args = [
        jnp.asarray(feats),
        _pad_to(w1, (DEPTH, LANES, LANES)),
        row_vec(b1),
        row_vec(fq),
        _pad_to(w2, (DEPTH, LANES, LANES)),
        row_vec(b2),
        _pad_to(w3, (DEPTH, LANES, 2 * HY_WIDTH)),
        jnp.asarray(decay),
        *_split_bf16(jnp.asarray(fwd)),
    ]
    out = jax.ShapeDtypeStruct((DEPTH, L, HY_WIDTH), F32)
    return pl.pallas_call(
        _filter_kernel,
        out_shape=[out, out, out],
        compiler_params=_params(),
        name="hyena_filter",
    )(*args)


def _conv_filter_kernel(taps_ref, fwd_ref, ga_ref, gbc_ref, gd_ref):
    spec = _dot_f32(fwd_ref[...], taps_ref[...])
    for l in range(DEPTH):
        _store_spectrum_tables(spec[:, 2 * l * CF_WIDTH:2 * (l + 1) * CF_WIDTH], l, ga_ref, gbc_ref, gd_ref)


def _conv_filter_call(L, conv_w):
    half = CF_CONV_K // 2
    fwd, _ = _dft_tables(L)
    lag_pos = conv_w[:, half::-1, :]
    lag_neg = _pad_to(conv_w[:, half + 1:, :], (DEPTH, half + 1, CF_WIDTH))
    lag_neg = jnp.roll(lag_neg, 1, axis=1)
    taps = jnp.concatenate([lag_pos, lag_neg], axis=2)
    taps = _pad_to(taps, (DEPTH, LANES, 2 * CF_WIDTH))
    taps = jnp.transpose(taps, (1, 0, 2)).reshape(LANES, DEPTH * 2 * CF_WIDTH)
    out = jax.ShapeDtypeStruct((DEPTH, L, CF_WIDTH), F32)
    return pl.pallas_call(
        _conv_filter_kernel,
        out_shape=[out, out, out],
        compiler_params=_params(),
        name="conv_filter",
    )(taps, jnp.asarray(fwd[:, :LANES]))


def _dft_conv(us, tabs, fwd_ref, inv_ref):
    L, W = us[0].shape
    u_all = jnp.concatenate(us, axis=1).astype(BF16)
    spec = _dot(fwd_ref[...], u_all)
    y_re, y_im = [], []
    for s, (ga, gbc, gd) in enumerate(tabs):
        p = spec[:L, s * W:(s + 1) * W]
        q = spec[L:, s * W:(s + 1) * W]
        y_re.append(p * ga - q * gbc)
        y_im.append(p * gbc + q * gd)
    y_spec = jnp.concatenate(
        [jnp.concatenate(y_re, axis=1), jnp.concatenate(y_im, axis=1)], axis=0).astype(BF16)
    y = _dot(inv_ref[...], y_spec)
    return [y[:, s * W:(s + 1) * W] for s in range(len(us))]


def _hyena_front(hy, pad_ref, cw_ref, cb_ref):
    L, width = hy.shape
    zeros = jnp.zeros((SUBLANES, width), F32)
    pad_ref[0:SUBLANES, :] = zeros
    pad_ref[SUBLANES + L:2 * SUBLANES + L, :] = zeros
    pad_ref[SUBLANES:SUBLANES + L, :] = hy
    conv = cb_ref[:, :width]
    for j in range(HY_SHORT_K):
        start = SUBLANES + j - HY_SHORT_K // 2
        conv = conv + cw_ref[j:j + 1, :] * pad_ref[start:start + L, :]
    return conv[:, :HY_WIDTH], conv[:, HY_WIDTH:2 * HY_WIDTH] * conv[:, 2 * HY_WIDTH:]


def _mix_sequences(hys, cfs, r):
    n = len(hys)
    x0s, us, tabs = [], [], []
    for s in range(n):
        x0, u = _hyena_front(hys[s], r["hy_pad"].at[s], r["hy_cw"], r["hy_cb"])
        x0s.append(x0)
        us.append(u)
        tabs.append((r["hy_ga"][...], r["hy_gbc"][...], r["hy_gd"][...]))
    for s in range(n):
        us.append(cfs[s][:, :CF_WIDTH] * jax.nn.sigmoid(cfs[s][:, CF_WIDTH:]))
        tabs.append((r["cf_ga"][...], r["cf_gbc"][...], r["cf_gd"][...]))
    ys = _dft_conv(us, tabs, r["fwd"], r["inv"])
    y_hy = [x0s[s] * (ys[s] + us[s] * r["hy_bias"][...]) for s in range(n)]
    y_cf = []
    for s in range(n):
        y = _layer_norm_rows(ys[n + s] + r["cf_cb"][...], r["cf_g"][...], r["cf_b"][...])
        y_cf.append(y * jax.nn.sigmoid(y))
    return y_hy, y_cf


def _mixer_inputs(seq, wts, layer, hy_tabs, cf_tabs):
    fwd, inv = _dft_tables(seq)
    ins = [("fwd", jnp.asarray(fwd).astype(BF16), _const_spec((2 * seq, seq))),
           ("inv", jnp.asarray(inv).astype(BF16), _const_spec((seq, 2 * seq)))]
    for name, tab in zip(("hy_ga", "hy_gbc", "hy_gd"), hy_tabs):
        ins.append((name, tab, _layer_spec((seq, HY_WIDTH), layer)))
    for name, tab in zip(("cf_ga", "cf_gbc", "cf_gd"), cf_tabs):
        ins.append((name, tab, _layer_spec((seq, CF_WIDTH), layer)))
    ins.append(("hy_cw", wts["hy_conv_w"], _layer_spec((HY_SHORT_K, 3 * HY_WIDTH), layer)))
    for name, key in (("hy_cb", "hy_conv_b"), ("hy_bias", "hy_bias"),
                      ("cf_cb", "cf_conv_b"), ("cf_g", "cf_ln_g"), ("cf_b", "cf_ln_b")):
        ins.append((name, wts["vecs"], _vec_spec(key, layer)))
    return ins


def _hyena_kernel(hy_ref, cw_ref, cb_ref, fwd_ref, inv_ref, ga_ref, gbc_ref, gd_ref, hb_ref,
                  o_ref, pad_ref):
    nb = hy_ref.shape[0]
    x0s, us = zip(*[_hyena_front(hy_ref[s], pad_ref.at[s], cw_ref, cb_ref) for s in range(nb)])
    tabs = [(ga_ref[...], gbc_ref[...], gd_ref[...])] * nb
    ys = _dft_conv(us, tabs, fwd_ref, inv_ref)
    for s in range(nb):
        o_ref[s] = (x0s[s] * (ys[s] + us[s] * hb_ref[...])).astype(o_ref.dtype)


def _hyena_call(hy3, conv_w, conv_b, g_tabs, hy_bias, layer):
    B, L, width = hy3.shape
    nb = _seqs_per_step(B, L) if L < MIXER_ROWS else min(B, HYENA_LONG_SEQS)
    fwd, inv = _dft_tables(L)
    fwd_bf = jnp.asarray(fwd).astype(BF16)
    inv_bf = jnp.asarray(inv).astype(BF16)
    tab = _layer_spec((L, HY_WIDTH), layer)
    return pl.pallas_call(
        _hyena_kernel,
        grid=(B // nb,),
        in_specs=[
            pl.BlockSpec((nb, L, width), lambda b: (b, 0, 0)),
            _layer_spec((HY_SHORT_K, width), layer),
            _vec_spec("hy_conv_b", layer),
            _const_spec((2 * L, L)),
            _const_spec((L, 2 * L)),
            tab, tab, tab,
            _vec_spec("hy_bias", layer),
        ],
        out_specs=pl.BlockSpec((nb, L, HY_WIDTH), lambda b: (b, 0, 0)),
        out_shape=jax.ShapeDtypeStruct((B, L, HY_WIDTH), BF16),
        scratch_shapes=[pltpu.VMEM((nb, L + 2 * SUBLANES, width), F32)],
        compiler_params=_params(dimension_semantics=("arbitrary",)),
        name="hyena",
    )(hy3, conv_w, conv_b, fwd_bf, inv_bf, *g_tabs, hy_bias)


CF_PAD = 2 * SUBLANES


def _conformer_kernel(cf_ref, cw_ref, cb_ref, g_ref, b_ref, o_ref, pad_ref):
    nb, L, _ = cf_ref.shape
    padded = L + 2 * CF_PAD
    shifted = padded - SUBLANES
    zeros = jnp.zeros((CF_PAD, CF_WIDTH), F32)
    half = CF_CONV_K // 2
    for s in range(nb):
        cf = cf_ref[s]
        pad_ref[s, 0, 0:CF_PAD, :] = zeros
        pad_ref[s, 0, CF_PAD + L:padded, :] = zeros
        pad_ref[s, 0, CF_PAD:CF_PAD + L, :] = cf[:, :CF_WIDTH] * jax.nn.sigmoid(cf[:, CF_WIDTH:])
        for r in range(1, SUBLANES):
            pad_ref[s, r, 0:shifted, :] = pad_ref[s, 0, r:r + shifted, :]
        for c0 in range(0, L, CF_ROW_CHUNK):
            acc = jnp.zeros((CF_ROW_CHUNK, CF_WIDTH), F32) + cb_ref[...]
            for j in range(CF_CONV_K):
                off = CF_PAD + j - half
                start = c0 + (off // SUBLANES) * SUBLANES
                acc = acc + cw_ref[j:j + 1, :] * pad_ref[s, off % SUBLANES, start:start + CF_ROW_CHUNK, :]
            y = _layer_norm_rows(acc, g_ref[...], b_ref[...])
            o_ref[s, c0:c0 + CF_ROW_CHUNK, :] = (y * jax.nn.sigmoid(y)).astype(o_ref.dtype)


def _conformer_call(cf3, conv_w, conv_b, ln_g, ln_b, layer):
    B, L, width = cf3.shape
    nb = _seqs_per_step(B, L)
    return pl.pallas_call(
        _conformer_kernel,
        grid=(B // nb,),
        in_specs=[
            pl.BlockSpec((nb, L, width), lambda b: (b, 0, 0)),
            _layer_spec((CF_CONV_K, CF_WIDTH), layer),
            _vec_spec("cf_conv_b", layer), _vec_spec("cf_ln_g", layer), _vec_spec("cf_ln_b", layer),
        ],
        out_specs=pl.BlockSpec((nb, L, CF_WIDTH), lambda b: (b, 0, 0)),
        out_shape=jax.ShapeDtypeStruct((B, L, CF_WIDTH), BF16),
        scratch_shapes=[pltpu.VMEM((nb, SUBLANES, L + 2 * CF_PAD, CF_WIDTH), F32)],
        compiler_params=_params(dimension_semantics=("arbitrary",)),
        name="conformer",
    )(cf3, conv_w, conv_b, ln_g, ln_b)


def _attn_kernel(*refs, lambda_init, past, n_cast):
    q_ref, k_ref, v_ref = refs[:3]
    n_in = 3
    if past:
        ck_ref, cv_ref = refs[3:5]
        k_all, v_aug = refs[-2:]
        n_in = 5
    lq1, lk1, lq2, lk2, g_ref = refs[n_in:n_in + 5]
    cast_in = refs[n_in + 5:n_in + 5 + n_cast]
    o_ref = refs[n_in + 5 + n_cast]
    cast_out = refs[n_in + 6 + n_cast:n_in + 6 + 2 * n_cast]
    _cast_chunks(cast_in, cast_out)
    nb, tq, _ = q_ref.shape

    if past:
        keys = v_aug.shape[1]

        @pl.when(pl.program_id(1) == 0)
        def _():
            ones = jnp.ones((keys, HEAD_PAIR), BF16)
            for s in range(nb):
                k_all[s, 0:past, :] = ck_ref[s, 0].reshape(ATT_WIDTH, past).T.astype(BF16)
                k_all[s, past:keys, :] = k_ref[s]
                v_old = cv_ref[s, 0].reshape(past, ATT_WIDTH).astype(BF16)
                for h in range(DIFF_HEADS):
                    cols = slice(h * HEAD_PAIR, (h + 1) * HEAD_PAIR)
                    base = 2 * h * HEAD_PAIR
                    v_aug[s, 0:past, base:base + HEAD_PAIR] = v_old[:, cols]
                    v_aug[s, past:keys, base:base + HEAD_PAIR] = v_ref[s, :, cols]
                    v_aug[s, :, base + HEAD_PAIR:base + 2 * HEAD_PAIR] = ones

    lam = (jnp.exp(jnp.sum(lq1[...] * lk1[...], axis=-1, keepdims=True))
           - jnp.exp(jnp.sum(lq2[...] * lk2[...], axis=-1, keepdims=True)) + lambda_init)
    lane = lax.broadcasted_iota(jnp.int32, (tq, HEAD_PAIR), 1)
    first = lane < DIFF_HEAD_DIM
    nt = (((1,), (1,)), ((), ()))
    gain = g_ref[...] * (1.0 - lambda_init)
    zero = jnp.zeros((), BF16)

    def scores(s, h):
        cols = slice(h * HEAD_PAIR, (h + 1) * HEAD_PAIR)
        q = q_ref[s, :, cols]
        q2 = jnp.concatenate([jnp.where(first, q, zero), jnp.where(first, zero, q)], axis=0)
        kh = k_all[s, :, cols] if past else k_ref[s, :, cols]
        return lax.dot_general(q2, kh, nt, preferred_element_type=F32)

    def normalize_store(s, h, o):
        o = o * lax.rsqrt(jnp.mean(o * o, axis=-1, keepdims=True) + LN_EPS) * gain
        o_ref[s, :, h * HEAD_PAIR:(h + 1) * HEAD_PAIR] = o.astype(o_ref.dtype)

    for s in range(nb):
        if past:
            for h in range(DIFF_HEADS):
                sc = scores(s, h)
                e = jnp.exp2(sc - jnp.max(sc, axis=-1, keepdims=True)).astype(BF16)
                p = _dot(e, v_aug[s, :, 2 * h * HEAD_PAIR:2 * (h + 1) * HEAD_PAIR])
                p0, p1 = p[:tq], p[tq:]
                normalize_store(s, h, p0[:, :HEAD_PAIR] * (1.0 / p0[:, HEAD_PAIR:HEAD_PAIR + 1])
                                - p1[:, :HEAD_PAIR] * (lam / p1[:, HEAD_PAIR:HEAD_PAIR + 1]))
        else:
            sc = jnp.concatenate([scores(s, h) for h in range(DIFF_HEADS)], axis=0)
            e = jnp.exp2(sc - jnp.max(sc, axis=-1, keepdims=True))
            r = 1.0 / jnp.sum(e, axis=-1, keepdims=True)
            for h in range(DIFF_HEADS):
                r0 = slice(2 * h * tq, (2 * h + 1) * tq)
                r1 = slice((2 * h + 1) * tq, (2 * h + 2) * tq)
                a = e[r0] * r[r0] - e[r1] * (lam * r[r1])
                normalize_store(s, h, _dot(a.astype(BF16), v_ref[s, :, h * HEAD_PAIR:(h + 1) * HEAD_PAIR]))


def _attn_call(q3, k3, v3, cache_k, cache_v, layer, vecs, cast=()):
    B, L, _ = q3.shape
    past = 0 if cache_k is None else cache_k.shape[-1]
    nb = _seqs_per_step(B, L)
    tq = min(Q_TILE, L)
    cast_in, cast_args, cast_out_specs, cast_out_shapes = _cast_plan(
        cast, (B // nb) * (L // tq), lambda b, i: b * (L // tq) + i)
    tile = pl.BlockSpec((nb, tq, ATT_WIDTH), lambda b, i: (b, i, 0))
    seq = pl.BlockSpec((nb, L, ATT_WIDTH), lambda b, i: (b, 0, 0))
    in_specs = [tile, seq, seq]
    args = [q3, k3, v3]
    scratch = []
    if past:
        for cache in (cache_k, cache_v):
            in_specs.append(pl.BlockSpec((nb, 1) + cache.shape[2:], lambda b, i: (b, layer, 0, 0, 0)))
            args.append(cache)
        scratch = [pltpu.VMEM((nb, past + L, ATT_WIDTH), BF16),
                   pltpu.VMEM((nb, past + L, 2 * ATT_WIDTH), BF16)]
    in_specs += [_vec_spec(n, layer) for n in ("lam_q1", "lam_k1", "lam_q2", "lam_k2", "subln_g")]
    args += [vecs] * 5
    return pl.pallas_call(
        functools.partial(_attn_kernel, lambda_init=_lambda_init(layer), past=past, n_cast=len(cast)),
        grid=(B // nb, L // tq),
        in_specs=in_specs + cast_in,
        out_specs=[tile] + cast_out_specs,
        out_shape=[jax.ShapeDtypeStruct((B, L, ATT_WIDTH), BF16)] + cast_out_shapes,
        scratch_shapes=scratch,
        compiler_params=_params(dimension_semantics=("arbitrary", "arbitrary")),
        name="diff_attn",
    )(*args, *cast_args)


def _mlp_kernel(x_ref, hy_ref, cf_ref, at_ref, ada_ref, wo_ref, w1_ref, w2_ref,
                g1_ref, b1_ref, g2_ref, b2_ref, *rest):
    n_cast = (len(rest) - 1) // 2
    o_ref = rest[n_cast]
    _cast_chunks(rest[:n_cast], rest[n_cast + 1:])
    gate1 = ada_ref[0, 0, 2:3, :]
    sh2 = ada_ref[0, 0, 3:4, :]
    sc2 = ada_ref[0, 0, 4:5, :]
    gate2 = ada_ref[0, 0, 5:6, :]
    groups = [slice(r, r + MLP_GROUP_ROWS) for r in range(0, x_ref.shape[0], MLP_GROUP_ROWS)]
    ys = [_dot(jnp.concatenate([hy_ref[g, :], cf_ref[g, :], at_ref[g, :]], axis=1), wo_ref[...])
          for g in groups]
    xs, acts = [], []
    for g, y in zip(groups, ys):
        x = _layer_norm_rows(DEEPNORM_ALPHA * x_ref[g, :] + gate1 * y, g1_ref[...], b1_ref[...])
        h = (x * (1.0 + sc2) + sh2).astype(BF16)
        xs.append(x)
        acts.append(jnp.maximum(_dot(h, w1_ref[...]), 0.0))
    ms = [_dot((a * a).astype(BF16), w2_ref[...]) for a in acts]
    for g, x, m in zip(groups, xs, ms):
        o_ref[g, :] = _layer_norm_rows(DEEPNORM_ALPHA * x + gate2 * m, g2_ref[...], b2_ref[...])


def _mlp_call(x2d, y_hy, y_cf, y_at, ada4, wts, layer, cond_row_of_row, cast_next):
    T = x2d.shape[0]
    steps = T // MLP_ROW_TILE
    cast_in, cast_args, cast_out_specs, cast_out_shapes = _cast_plan(cast_next, steps, lambda i: i)
    rows = lambda w: pl.BlockSpec((MLP_ROW_TILE, w), lambda i: (i, 0))
    in_specs = [
        rows(D_MODEL), rows(HY_WIDTH), rows(CF_WIDTH), rows(ATT_WIDTH),
        pl.BlockSpec((1, 1, N_ADA, D_MODEL), lambda i: (layer, cond_row_of_row(i * MLP_ROW_TILE), 0, 0)),
        _layer_spec((MIX_WIDTH, D_MODEL), 0),
        _layer_spec((D_MODEL, D_FF), 0),
        _layer_spec((D_FF, D_MODEL), 0),
        _vec_spec("ln1_g", layer), _vec_spec("ln1_b", layer), _vec_spec("ln2_g", layer), _vec_spec("ln2_b", layer),
    ]
    args = [x2d, y_hy, y_cf, y_at, ada4, wts["w_out"][layer], wts["w_mlp1"][layer], wts["w_mlp2"][layer]]
    args += [wts["vecs"]] * 4
    return pl.pallas_call(
        _mlp_kernel,
        grid=(steps,),
        in_specs=in_specs + cast_in,
        out_specs=[rows(D_MODEL)] + cast_out_specs,
        out_shape=[jax.ShapeDtypeStruct((T, D_MODEL), F32)] + cast_out_shapes,
        compiler_params=_params(MLP_VMEM_LIMIT, dimension_semantics=("arbitrary",)),
        name="outproj_mlp",
    )(*args, *cast_args)


MLP_WEIGHTS = ("w_out", "w_mlp1", "w_mlp2")


def _layer(x2d, batch, seq, layer, ada4, wts, f32_weights, hy_tabs, cf_tabs, cond_row_of_row, rope_tabs,
           cache_kv, new_caches, casts):
    fused = seq <= FUSED_MIXER_MAX_SEQ and ROW_TILE % seq == 0
    z = _inproj_call(x2d, ada4, wts, layer, batch, seq, cond_row_of_row, rope_tabs, new_caches,
                     (hy_tabs, cf_tabs) if fused else None)
    as_seq = lambda a: a.reshape(batch, seq, a.shape[-1])
    flat = lambda a: a.reshape(batch * seq, a.shape[-1])
    if fused:
        y_hy, y_cf = z["yhy"], z["ycf"]
    else:
        vecs = wts["vecs"]
        y_hy = flat(_hyena_call(as_seq(z["hy"]), wts["hy_conv_w"], vecs, hy_tabs, vecs, layer))
        y_cf = flat(_conformer_call(as_seq(z["cf"]), wts["cf_conv_w"], vecs, vecs, vecs, layer))
    ck, cv = cache_kv if cache_kv is not None else (None, None)
    cast = []
    if casts:
        cast = [(n, f32_weights[n], layer) for n in MLP_WEIGHTS]
        if layer + 1 < DEPTH:
            cast.append(("w_in", f32_weights["w_in"], layer + 1))
    y_at, *cast_weights = _attn_call(as_seq(z["q"]), as_seq(z["k"]), as_seq(z["v"]), ck, cv, layer, wts["vecs"],
                                     [(w, l) for _, w, l in cast])
    for (name, _, _), w_bf in zip(cast, cast_weights):
        wts[name].append(w_bf)
    x_out, = _mlp_call(x2d, y_hy, y_cf, flat(y_at), ada4, wts, layer, cond_row_of_row, ())
    return x_out, (z["kc"], z["vc"]) if "kc" in z else None


def kernel(x_prompt, x_sample, cache_k, cache_v, c, c_ctx, w_ada, b_ada, w_in, hy_conv_w, hy_conv_b, hf_w1, hf_b1, hf_freq, hf_w2, hf_b2, hf_w3, hy_bias, cf_conv_w, cf_conv_b, cf_ln_g, cf_ln_b, lam_q1, lam_k1, lam_q2, lam_k2, subln_g, w_out, ln1_g, ln1_b, w_mlp1, w_mlp2, ln2_g, ln2_b):
    batch, seq, _ = x_prompt.shape
    dec_batch, dec_seq, _ = x_sample.shape
    assert 1 + dec_batch <= COND_ROWS
    assert ROW_TILE % seq == 0 and (batch * seq) % ROW_TILE == 0
    assert dec_seq % ROW_TILE == 0 and dec_seq % Q_TILE == 0
    assert (batch * seq) % MLP_ROW_TILE == 0 and dec_seq % MLP_ROW_TILE == 0

    cond = jnp.concatenate(
        [c_ctx[None, :], c, jnp.zeros((COND_ROWS - 1 - dec_batch, D_MODEL), F32)], axis=0)
    ada4 = _ada_call(cond, w_ada, b_ada).reshape(DEPTH, COND_ROWS, N_ADA, D_MODEL)

    f32_weights = dict(w_in=w_in, w_out=w_out, w_mlp1=w_mlp1, w_mlp2=w_mlp2)
    wts = dict(w_in=[w_in[0:1].astype(BF16)], w_out=[], w_mlp1=[], w_mlp2=[])
    wts.update(
        hy_conv_w=hy_conv_w, cf_conv_w=cf_conv_w,
        vecs=_pack_vectors(dict(
            ln1_g=ln1_g, ln1_b=ln1_b, ln2_g=ln2_g, ln2_b=ln2_b, hy_conv_b=hy_conv_b, hy_bias=hy_bias,
            cf_conv_b=cf_conv_b, cf_ln_g=cf_ln_g, cf_ln_b=cf_ln_b, subln_g=subln_g,
            lam_q1=lam_q1, lam_k1=lam_k1, lam_q2=lam_q2, lam_k2=lam_k2)))

    rope_tabs = tuple(jnp.asarray(t) for t in _rope_tables(dec_seq))
    cache_kt = jnp.transpose(cache_k, (0, 1, 3, 4, 2))

    xp = x_prompt.reshape(batch * seq, D_MODEL)
    xs = x_sample.reshape(dec_batch * dec_seq, D_MODEL)
    new_caches = ()
    filt = (hf_w1, hf_b1, hf_freq, hf_w2, hf_b2, hf_w3)
    hy_ctx = _filter_call(seq, *filt)
    hy_dec = _filter_call(dec_seq, *filt)
    fused = lambda n: n <= FUSED_MIXER_MAX_SEQ and ROW_TILE % n == 0
    cf_ctx = _conv_filter_call(seq, cf_conv_w) if fused(seq) else None
    cf_dec = _conv_filter_call(dec_seq, cf_conv_w) if fused(dec_seq) else None
    for l in range(DEPTH):
        xp, new_caches = _layer(xp, batch, seq, l, ada4, wts, f32_weights, hy_ctx, cf_ctx, lambda row: 0,
                                None, None, new_caches, True)
        xs, _ = _layer(xs, dec_batch, dec_seq, l, ada4, wts, f32_weights, hy_dec, cf_dec,
                       lambda row: 1 + row // dec_seq, rope_tabs, (cache_kt, cache_v), None, False)
    new_cache_kt, new_cache_v = new_caches
    new_cache_k = jnp.transpose(new_cache_kt, (0, 1, 4, 2, 3))
    return (xp.reshape(batch, seq, D_MODEL), xs.reshape(dec_batch, dec_seq, D_MODEL),
            new_cache_k, new_cache_v)
```

```python
import functools
import math

import numpy as np
import jax
import jax.numpy as jnp
from jax import lax
from jax.experimental import pallas as pl
from jax.experimental.pallas import tpu as pltpu

D_MODEL = 1024
DEPTH = 2
GRID_W = 64
HY_WIDTH = D_MODEL // 4
CF_WIDTH = D_MODEL // 4
ATT_WIDTH = D_MODEL // 2
DIFF_HEAD_DIM = 64
DIFF_HEADS = ATT_WIDTH // (2 * DIFF_HEAD_DIM)
HEAD_PAIR = 2 * DIFF_HEAD_DIM
MIX_WIDTH = HY_WIDTH + CF_WIDTH + ATT_WIDTH
IN_WIDTH = 3 * HY_WIDTH + 2 * CF_WIDTH + 3 * ATT_WIDTH
HY_SHORT_K = 3
HY_FILTER_EMB = 33
HY_FILTER_HIDDEN = 64
HY_FAST_DECAY_PCT = 0.3
HY_SLOW_DECAY_PCT = 1.5
HY_DECAY_TARGET = 1e-2
CF_CONV_K = 31
D_FF = 4 * D_MODEL
ROPE_BASE = 10000.0
AX_DIM = DIFF_HEAD_DIM // 2
DEEPNORM_ALPHA = (2 * DEPTH) ** 0.25
LN_EPS = 1e-5
N_ADA = 6

LANES = 128
SUBLANES = 8
COND_ROWS = SUBLANES
VMEM_LIMIT = 56 * 1024 * 1024

ROW_TILE = 512
MLP_ROW_TILE = 512
MLP_GROUP_ROWS = 256
ADA_COL_TILE = 1536
Q_TILE = 512
MIXER_ROWS = 1024
HYENA_LONG_SEQS = 2
FUSED_MIXER_MAX_SEQ = 256
CF_ROW_CHUNK = 128

F32 = jnp.float32
BF16 = jnp.bfloat16


def _lambda_init(l):
    return 0.8 - 0.6 * math.exp(-0.3 * l)


def _params(**kw):
    return pltpu.CompilerParams(vmem_limit_bytes=VMEM_LIMIT, **kw)


def _const_spec(shape):
    zeros = (0,) * len(shape)
    return pl.BlockSpec(shape, lambda *_: zeros, pipeline_mode=pl.Buffered(1))


def _layer_spec(shape, layer):
    zeros = (0,) * len(shape)
    return pl.BlockSpec((None,) + tuple(shape), lambda *_: (layer,) + zeros,
                        pipeline_mode=pl.Buffered(1))


VEC_SLOTS = (
    ("ln1_g", D_MODEL), ("ln1_b", D_MODEL), ("ln2_g", D_MODEL), ("ln2_b", D_MODEL),
    ("hy_conv_b", D_MODEL),
    ("hy_bias", HY_WIDTH), ("cf_conv_b", CF_WIDTH), ("cf_ln_g", CF_WIDTH), ("cf_ln_b", CF_WIDTH),
    ("subln_g", HEAD_PAIR), ("lam_q1", LANES), ("lam_k1", LANES), ("lam_q2", LANES), ("lam_k2", LANES),
)


def _vec_offsets():
    offsets, off = {}, 0
    for name, width in VEC_SLOTS:
        assert off % width == 0
        offsets[name] = (off, width)
        off += width
    return offsets, off


def _pack_vectors(params):
    pieces = [jnp.pad(params[name], ((0, 0), (0, width - params[name].shape[1]))) for name, width in VEC_SLOTS]
    return jnp.concatenate(pieces, axis=1).reshape(DEPTH, 1, -1)


def _vec_spec(name, layer):
    off, width = _vec_offsets()[0][name]
    return pl.BlockSpec((None, 1, width), lambda *_: (layer, 0, off // width), pipeline_mode=pl.Buffered(1))


def _dot(a, b):
    return jnp.dot(a, b, preferred_element_type=F32)


def _dot_f32(a, b):
    return jnp.dot(a, b, preferred_element_type=F32, precision=lax.Precision.HIGHEST)


def _split_bf16(x):
    hi = x.astype(BF16)
    return hi, (x - hi.astype(F32)).astype(BF16)


def _layer_norm_rows(x, g, b):
    mu = jnp.mean(x, axis=-1, keepdims=True)
    xc = x - mu
    var = jnp.mean(xc * xc, axis=-1, keepdims=True)
    return xc * lax.rsqrt(var + LN_EPS) * g + b


def _seqs_per_step(batch, seq):
    return max(1, min(batch, MIXER_ROWS // seq))


def _cast_plan(items, steps, chunk_of_step):
    in_specs, out_specs, out_shapes = [], [], []
    for w, layer in items:
        _, n_rows, n_cols = w.shape
        chunk = n_rows // steps
        assert chunk * steps == n_rows and chunk % (2 * SUBLANES) == 0
        in_specs.append(pl.BlockSpec((1, chunk, n_cols), lambda *g, layer=layer: (layer, chunk_of_step(*g), 0)))
        out_specs.append(pl.BlockSpec((1, chunk, n_cols), lambda *g: (0, chunk_of_step(*g), 0)))
        out_shapes.append(jax.ShapeDtypeStruct((1, n_rows, n_cols), BF16))
    return in_specs, [w for w, _ in items], out_specs, out_shapes


def _cast_chunks(srcs, dsts):
    for src, dst in zip(srcs, dsts):
        dst[...] = src[...].astype(BF16)


@functools.lru_cache(maxsize=None)
def _dft_tables(L):
    n = 2 * L
    k = np.arange(L, dtype=np.float64)[:, None]
    s = np.arange(L, dtype=np.float64)[None, :]
    ang = 2.0 * np.pi * k * s / n
    fwd = np.concatenate([np.cos(ang), -np.sin(ang)], axis=0)
    fwd[L, :] = np.cos(np.pi * np.arange(L))
    t = np.arange(L, dtype=np.float64)[:, None]
    kk = np.arange(L, dtype=np.float64)[None, :]
    ang_i = 2.0 * np.pi * t * kk / n
    inv_re = (2.0 / n) * np.cos(ang_i)
    inv_re[:, 0] = 1.0 / n
    inv_im = -(2.0 / n) * np.sin(ang_i)
    inv_im[:, 0] = np.cos(np.pi * np.arange(L)) / n
    inv = np.concatenate([inv_re, inv_im], axis=1)
    return fwd.astype(np.float32), inv.astype(np.float32)


@functools.lru_cache(maxsize=None)
def _filter_tables(L):
    bands = (HY_FILTER_EMB - 1) // 2
    t = np.linspace(0.0, 1.0, L)[:, None]
    w = (2.0 * np.pi / L) * np.arange(L, dtype=np.float64)[:, None]
    fr = np.linspace(1e-4, bands - 1, bands)[None, :]
    feats = np.concatenate([t, np.cos(fr * w), -np.sin(fr * w)], -1)
    feats_p = np.zeros((L, LANES), np.float64)
    feats_p[:, :HY_FILTER_EMB] = feats
    deltas = np.abs(np.linspace(math.log(HY_DECAY_TARGET) / HY_FAST_DECAY_PCT,
                                math.log(HY_DECAY_TARGET) / HY_SLOW_DECAY_PCT, HY_WIDTH))
    decay = np.exp(-t * deltas[None, :])
    return feats_p.astype(np.float32), decay.astype(np.float32)


@functools.lru_cache(maxsize=None)
def _rope_tables(n):
    half = AX_DIM // 2
    pos = np.arange(n)
    row = (pos // GRID_W).astype(np.float64)[:, None]
    col = (pos % GRID_W).astype(np.float64)[:, None]
    inv = ROPE_BASE ** (-np.arange(0, AX_DIM, 2, dtype=np.float64) / AX_DIM)[None, :]
    zero = np.zeros((n, half))
    cr, sr = np.cos(row * inv), np.sin(row * inv)
    cc, sc = np.cos(col * inv), np.sin(col * inv)
    c = np.concatenate([cr, cr, cc, cc], -1)
    s_up = np.concatenate([-sr, zero, -sc, zero], -1)
    s_dn = np.concatenate([zero, sr, zero, sc], -1)
    reps = LANES // DIFF_HEAD_DIM
    tile = lambda a: np.tile(a, (1, reps)).astype(np.float32)
    return tile(c), tile(s_up), tile(s_dn)


def _ada_kernel(cond_ref, w_ref, b_ref, o_ref):
    c = cond_ref[...]
    s = c * jax.nn.sigmoid(c)
    o_ref[0] = _dot(s.astype(BF16), w_ref[0].astype(BF16)) + b_ref[0]


def _ada_call(cond, w_ada, b_ada):
    n_out = N_ADA * D_MODEL
    return pl.pallas_call(
        _ada_kernel,
        grid=(DEPTH, n_out // ADA_COL_TILE),
        in_specs=[
            pl.BlockSpec((COND_ROWS, D_MODEL), lambda l, j: (0, 0)),
            pl.BlockSpec((1, D_MODEL, ADA_COL_TILE), lambda l, j: (l, 0, j)),
            pl.BlockSpec((1, 1, ADA_COL_TILE), lambda l, j: (l, 0, j)),
        ],
        out_specs=pl.BlockSpec((1, COND_ROWS, ADA_COL_TILE), lambda l, j: (l, 0, j)),
        out_shape=jax.ShapeDtypeStruct((DEPTH, COND_ROWS, n_out), F32),
        compiler_params=_params(dimension_semantics=("arbitrary", "arbitrary")),
        name="ada",
    )(cond, w_ada, b_ada.reshape(DEPTH, 1, n_out))


def _rope_lanes(x, c, s_up, s_dn):
    outs = []
    for p in range(ATT_WIDTH // LANES):
        xb = x[:, p * LANES:(p + 1) * LANES]
        up = pltpu.roll(xb, LANES - AX_DIM // 2, 1)
        dn = pltpu.roll(xb, AX_DIM // 2, 1)
        outs.append(xb * c + up * s_up + dn * s_dn)
    return jnp.concatenate(outs, axis=1)


Q_SCALE = DIFF_HEAD_DIM ** -0.5 * math.log2(math.e)


def _inproj_kernel(*refs, names, seq, cache_slot):
    r = dict(zip(names, refs))
    sh1 = r["ada"][0, 0, 0:1, :]
    sc1 = r["ada"][0, 0, 1:2, :]
    h = (r["x"][...] * (1.0 + sc1) + sh1).astype(BF16)
    w_ref = r["w"]
    o1 = 3 * HY_WIDTH
    o2 = o1 + 2 * CF_WIDTH
    o3 = o2 + ATT_WIDTH
    o4 = o3 + ATT_WIDTH
    hy = _dot(h, w_ref[:, 0:o1])
    cf = _dot(h, w_ref[:, o1:o2])
    q = _dot(h, w_ref[:, o2:o3])
    k = _dot(h, w_ref[:, o3:o4])
    v = _dot(h, w_ref[:, o4:IN_WIDTH])
    if "yhy" in r:
        seqs = hy.shape[0] // seq
        pieces = [slice(b * seq, (b + 1) * seq) for b in range(seqs)]
        y_hy, y_cf = _mix_sequences([hy[p] for p in pieces], [cf[p] for p in pieces], r)
        for b, p in enumerate(pieces):
            r["yhy"][p, :] = y_hy[b].astype(BF16)
            r["ycf"][p, :] = y_cf[b].astype(BF16)
    else:
        r["hy"][...] = hy
        r["cf"][...] = cf
    if "rope_c" in r:
        c, su, sd = r["rope_c"][...], r["rope_up"][...], r["rope_dn"][...]
        q = _rope_lanes(q, c, su, sd)
        k = _rope_lanes(k, c, su, sd)
    r["q"][...] = (q * Q_SCALE).astype(BF16)
    r["k"][...] = k.astype(BF16)
    r["v"][...] = v.astype(BF16)
    if "kc" in r:
        kc_ref, vc_ref = r["kc"], r["vc"]
        for b in range(kc_ref.shape[0]):
            rows = slice(b * seq, (b + 1) * seq)
            for slot in range(kc_ref.shape[1]):
                if slot == cache_slot:
                    kc_ref[b, slot] = k[rows].T.reshape(2 * DIFF_HEADS, DIFF_HEAD_DIM, seq)
                    vc_ref[b, slot] = v[rows].reshape(seq, DIFF_HEADS, 2 * DIFF_HEAD_DIM)
                else:
                    kc_ref[b, slot] = jnp.zeros(kc_ref.shape[2:], F32)
                    vc_ref[b, slot] = jnp.zeros(vc_ref.shape[2:], F32)


def _inproj_call(x2d, ada4, wts, layer, batch, seq, cond_row_of_row, rope_tabs, caches, mix_tabs):
    T = x2d.shape[0]
    rows = lambda w: pl.BlockSpec((ROW_TILE, w), lambda i: (i, 0))
    ins = [
        ("x", x2d, rows(D_MODEL)),
        ("ada", ada4, pl.BlockSpec((1, 1, N_ADA, D_MODEL), lambda i: (layer, cond_row_of_row(i * ROW_TILE), 0, 0))),
        ("w", wts["w_in"][layer], _layer_spec((D_MODEL, IN_WIDTH), 0)),
    ]
    outs = []
    scratch = []
    if rope_tabs is not None:
        tiles_per_seq = seq // ROW_TILE
        for name, tab in zip(("rope_c", "rope_up", "rope_dn"), rope_tabs):
            ins.append((name, tab, pl.BlockSpec((ROW_TILE, LANES), lambda i: (i % tiles_per_seq, 0))))
    if mix_tabs is not None:
        ins += _mixer_inputs(seq, wts, layer, *mix_tabs)
        outs.append(("yhy", rows(HY_WIDTH), jax.ShapeDtypeStruct((T, HY_WIDTH), BF16)))
        outs.append(("ycf", rows(CF_WIDTH), jax.ShapeDtypeStruct((T, CF_WIDTH), BF16)))
        scratch.append(("hy_pad", pltpu.VMEM((ROW_TILE // seq, seq + 2 * SUBLANES, 3 * HY_WIDTH), F32)))
    else:
        outs.append(("hy", rows(3 * HY_WIDTH), jax.ShapeDtypeStruct((T, 3 * HY_WIDTH), F32)))
        outs.append(("cf", rows(2 * CF_WIDTH), jax.ShapeDtypeStruct((T, 2 * CF_WIDTH), F32)))
    for name in ("q", "k", "v"):
        outs.append((name, rows(ATT_WIDTH), jax.ShapeDtypeStruct((T, ATT_WIDTH), BF16)))
    aliases = {}
    cache_slot = 0
    if caches is not None:
        seqs = ROW_TILE // seq
        first_call = len(caches) == 0
        cache_slot = layer if first_call else 0
        shapes = (("kc", (batch, DEPTH, 2 * DIFF_HEADS, DIFF_HEAD_DIM, seq)),
                  ("vc", (batch, DEPTH, seq, DIFF_HEADS, 2 * DIFF_HEAD_DIM)))
        for j, (name, shape) in enumerate(shapes):
            if first_call:
                spec = pl.BlockSpec((seqs, DEPTH) + shape[2:], lambda i: (i, 0, 0, 0, 0))
            else:
                spec = pl.BlockSpec((seqs, 1) + shape[2:], lambda i: (i, layer, 0, 0, 0))
                aliases[len(ins)] = len(outs)
                ins.append((name + "_in", caches[j], pl.BlockSpec(memory_space=pl.ANY)))
            outs.append((name, spec, jax.ShapeDtypeStruct(shape, F32)))
    names = tuple(n for n, _, _ in ins) + tuple(n for n, _, _ in outs) + tuple(n for n, _ in scratch)
    results = pl.pallas_call(
        functools.partial(_inproj_kernel, names=names, seq=seq, cache_slot=cache_slot),
        grid=(T // ROW_TILE,),
        in_specs=[spec for _, _, spec in ins],
        out_specs=[spec for _, spec, _ in outs],
        out_shape=[shape for _, _, shape in outs],
        scratch_shapes=[s for _, s in scratch],
        input_output_aliases=aliases,
        compiler_params=_params(dimension_semantics=("arbitrary",)),
        name="inproj",
    )(*[a for _, a, _ in ins])
    return dict(zip((n for n, _, _ in outs), results))


def _store_spectrum_tables(spec, layer, ga_ref, gbc_ref, gd_ref):
    L = spec.shape[0] // 2
    W = spec.shape[1] // 2
    row = lax.broadcasted_iota(jnp.int32, (L, W), 0)
    p = spec[:L]
    q = spec[L:]
    g_re = p[:, :W] + p[:, W:]
    g_im = q[:, :W] - q[:, W:]
    nyq = q[0:1, :W] + q[0:1, W:]
    ga_ref[layer] = g_re
    gbc_ref[layer] = jnp.where(row == 0, 0.0, g_im)
    gd_ref[layer] = jnp.where(row == 0, nyq, g_re)


def _filter_kernel(feats_ref, w1_ref, b1_ref, fq_ref, w2_ref, b2_ref, w3_ref, decay_ref,
                   fwd_hi_ref, fwd_lo_ref, ga_ref, gbc_ref, gd_ref):
    L = feats_ref.shape[0]
    decay = decay_ref[...]
    row = lax.broadcasted_iota(jnp.int32, (L, HY_WIDTH), 0)
    taps = []
    for l in range(DEPTH):
        fq = fq_ref[l]
        hid = jnp.sin(fq * (_dot_f32(feats_ref[...], w1_ref[l]) + b1_ref[l]))
        hid = jnp.sin(fq * (_dot_f32(hid, w2_ref[l]) + b2_ref[l]))
        hf = _dot_f32(hid, w3_ref[l])
        taps.append(hf[:, :HY_WIDTH] * decay)
        taps.append(jnp.where(row == 0, 0.0, hf[:, HY_WIDTH:] * decay))
    h_hi, h_lo = _split_bf16(jnp.concatenate(taps, axis=1))
    fwd_hi = fwd_hi_ref[...]
    spec = _dot(fwd_hi, h_hi) + (_dot(fwd_hi, h_lo) + _dot(fwd_lo_ref[...], h_hi))
    for l in range(DEPTH):
        _store_spectrum_tables(spec[:, 2 * l * HY_WIDTH:2 * (l + 1) * HY_WIDTH], l, ga_ref, gbc_ref, gd_ref)


def _pad_to(a, shape):
    return jnp.pad(a, [(0, s - d) for s, d in zip(shape, a.shape)])


def _filter_call(L, w1, b1, fq, w2, b2, w3):
    feats, decay = _filter_tables(L)
    fwd, _ = _dft_tables(L)
    row_vec = lambda a: _pad_to(a.reshape(DEPTH, 1, -1), (DEPTH, 1, LANES))
    args = [
        jnp.asarray(feats),
        _pad_to(w1, (DEPTH, LANES, LANES)),
        row_vec(b1),
        row_vec(fq),
        _pad_to(w2, (DEPTH, LANES, LANES)),
        row_vec(b2),
        _pad_to(w3, (DEPTH, LANES, 2 * HY_WIDTH)),
        jnp.asarray(decay),
        *_split_bf16(jnp.asarray(fwd)),
    ]
    out = jax.ShapeDtypeStruct((DEPTH, L, HY_WIDTH), F32)
    return pl.pallas_call(
        _filter_kernel,
        out_shape=[out, out, out],
        compiler_params=_params(),
        name="hyena_filter",
    )(*args)


def _conv_filter_kernel(taps_ref, fwd_ref, ga_ref, gbc_ref, gd_ref):
    spec = _dot_f32(fwd_ref[...], taps_ref[...])
    for l in range(DEPTH):
        _store_spectrum_tables(spec[:, 2 * l * CF_WIDTH:2 * (l + 1) * CF_WIDTH], l, ga_ref, gbc_ref, gd_ref)


def _conv_filter_call(L, conv_w):
    half = CF_CONV_K // 2
    fwd, _ = _dft_tables(L)
    lag_pos = conv_w[:, half::-1, :]
    lag_neg = _pad_to(conv_w[:, half + 1:, :], (DEPTH, half + 1, CF_WIDTH))
    lag_neg = jnp.roll(lag_neg, 1, axis=1)
    taps = jnp.concatenate([lag_pos, lag_neg], axis=2)
    taps = _pad_to(taps, (DEPTH, LANES, 2 * CF_WIDTH))
    taps = jnp.transpose(taps, (1, 0, 2)).reshape(LANES, DEPTH * 2 * CF_WIDTH)
    out = jax.ShapeDtypeStruct((DEPTH, L, CF_WIDTH), F32)
    return pl.pallas_call(
        _conv_filter_kernel,
        out_shape=[out, out, out],
        compiler_params=_params(),
        name="conv_filter",
    )(taps, jnp.asarray(fwd[:, :LANES]))


def _dft_conv(us, tabs, fwd_ref, inv_ref):
    L, W = us[0].shape
    u_all = jnp.concatenate(us, axis=1).astype(BF16)
    spec = _dot(fwd_ref[...], u_all)
    y_re, y_im = [], []
    for s, (ga, gbc, gd) in enumerate(tabs):
        p = spec[:L, s * W:(s + 1) * W]
        q = spec[L:, s * W:(s + 1) * W]
        y_re.append(p * ga - q * gbc)
        y_im.append(p * gbc + q * gd)
    y_spec = jnp.concatenate(
        [jnp.concatenate(y_re, axis=1), jnp.concatenate(y_im, axis=1)], axis=0).astype(BF16)
    y = _dot(inv_ref[...], y_spec)
    return [y[:, s * W:(s + 1) * W] for s in range(len(us))]


def _hyena_front(hy, pad_ref, cw_ref, cb_ref):
    L, width = hy.shape
    zeros = jnp.zeros((SUBLANES, width), F32)
    pad_ref[0:SUBLANES, :] = zeros
    pad_ref[SUBLANES + L:2 * SUBLANES + L, :] = zeros
    pad_ref[SUBLANES:SUBLANES + L, :] = hy
    conv = cb_ref[:, :width]
    for j in range(HY_SHORT_K):
        start = SUBLANES + j - HY_SHORT_K // 2
        conv = conv + cw_ref[j:j + 1, :] * pad_ref[start:start + L, :]
    return conv[:, :HY_WIDTH], conv[:, HY_WIDTH:2 * HY_WIDTH] * conv[:, 2 * HY_WIDTH:]


def _mix_sequences(hys, cfs, r):
    n = len(hys)
    x0s, us, tabs = [], [], []
    for s in range(n):
        x0, u = _hyena_front(hys[s], r["hy_pad"].at[s], r["hy_cw"], r["hy_cb"])
        x0s.append(x0)
        us.append(u)
        tabs.append((r["hy_ga"][...], r["hy_gbc"][...], r["hy_gd"][...]))
    for s in range(n):
        us.append(cfs[s][:, :CF_WIDTH] * jax.nn.sigmoid(cfs[s][:, CF_WIDTH:]))
        tabs.append((r["cf_ga"][...], r["cf_gbc"][...], r["cf_gd"][...]))
    ys = _dft_conv(us, tabs, r["fwd"], r["inv"])
    y_hy = [x0s[s] * (ys[s] + us[s] * r["hy_bias"][...]) for s in range(n)]
    y_cf = []
    for s in range(n):
        y = _layer_norm_rows(ys[n + s] + r["cf_cb"][...], r["cf_g"][...], r["cf_b"][...])
        y_cf.append(y * jax.nn.sigmoid(y))
    return y_hy, y_cf


def _mixer_inputs(seq, wts, layer, hy_tabs, cf_tabs):
    fwd, inv = _dft_tables(seq)
    ins = [("fwd", jnp.asarray(fwd).astype(BF16), _const_spec((2 * seq, seq))),
           ("inv", jnp.asarray(inv).astype(BF16), _const_spec((seq, 2 * seq)))]
    for name, tab in zip(("hy_ga", "hy_gbc", "hy_gd"), hy_tabs):
        ins.append((name, tab, _layer_spec((seq, HY_WIDTH), layer)))
    for name, tab in zip(("cf_ga", "cf_gbc", "cf_gd"), cf_tabs):
        ins.append((name, tab, _layer_spec((seq, CF_WIDTH), layer)))
    ins.append(("hy_cw", wts["hy_conv_w"], _layer_spec((HY_SHORT_K, 3 * HY_WIDTH), layer)))
    for name, key in (("hy_cb", "hy_conv_b"), ("hy_bias", "hy_bias"),
                      ("cf_cb", "cf_conv_b"), ("cf_g", "cf_ln_g"), ("cf_b", "cf_ln_b")):
        ins.append((name, wts["vecs"], _vec_spec(key, layer)))
    return ins


def _hyena_kernel(hy_ref, cw_ref, cb_ref, fwd_ref, inv_ref, ga_ref, gbc_ref, gd_ref, hb_ref,
                  o_ref, pad_ref):
    nb = hy_ref.shape[0]
    x0s, us = zip(*[_hyena_front(hy_ref[s], pad_ref.at[s], cw_ref, cb_ref) for s in range(nb)])
    tabs = [(ga_ref[...], gbc_ref[...], gd_ref[...])] * nb
    ys = _dft_conv(us, tabs, fwd_ref, inv_ref)
    for s in range(nb):
        o_ref[s] = (x0s[s] * (ys[s] + us[s] * hb_ref[...])).astype(o_ref.dtype)


def _hyena_call(hy3, conv_w, conv_b, g_tabs, hy_bias, layer):
    B, L, width = hy3.shape
    nb = _seqs_per_step(B, L) if L < MIXER_ROWS else min(B, HYENA_LONG_SEQS)
    fwd, inv = _dft_tables(L)
    fwd_bf = jnp.asarray(fwd).astype(BF16)
    inv_bf = jnp.asarray(inv).astype(BF16)
    tab = _layer_spec((L, HY_WIDTH), layer)
    return pl.pallas_call(
        _hyena_kernel,
        grid=(B // nb,),
        in_specs=[
            pl.BlockSpec((nb, L, width), lambda b: (b, 0, 0)),
            _layer_spec((HY_SHORT_K, width), layer),
            _vec_spec("hy_conv_b", layer),
            _const_spec((2 * L, L)),
            _const_spec((L, 2 * L)),
            tab, tab, tab,
            _vec_spec("hy_bias", layer),
        ],
        out_specs=pl.BlockSpec((nb, L, HY_WIDTH), lambda b: (b, 0, 0)),
        out_shape=jax.ShapeDtypeStruct((B, L, HY_WIDTH), BF16),
        scratch_shapes=[pltpu.VMEM((nb, L + 2 * SUBLANES, width), F32)],
        compiler_params=_params(dimension_semantics=("arbitrary",)),
        name="hyena",
    )(hy3, conv_w, conv_b, fwd_bf, inv_bf, *g_tabs, hy_bias)


CF_PAD = 2 * SUBLANES


def _conformer_kernel(cf_ref, cw_ref, cb_ref, g_ref, b_ref, o_ref, pad_ref):
    nb, L, _ = cf_ref.shape
    padded = L + 2 * CF_PAD
    shifted = padded - SUBLANES
    zeros = jnp.zeros((CF_PAD, CF_WIDTH), F32)
    half = CF_CONV_K // 2
    for s in range(nb):
        cf = cf_ref[s]
        pad_ref[s, 0, 0:CF_PAD, :] = zeros
        pad_ref[s, 0, CF_PAD + L:padded, :] = zeros
        pad_ref[s, 0, CF_PAD:CF_PAD + L, :] = cf[:, :CF_WIDTH] * jax.nn.sigmoid(cf[:, CF_WIDTH:])
        for r in range(1, SUBLANES):
            pad_ref[s, r, 0:shifted, :] = pad_ref[s, 0, r:r + shifted, :]
        for c0 in range(0, L, CF_ROW_CHUNK):
            acc = jnp.zeros((CF_ROW_CHUNK, CF_WIDTH), F32) + cb_ref[...]
            for j in range(CF_CONV_K):
                off = CF_PAD + j - half
                start = c0 + (off // SUBLANES) * SUBLANES
                acc = acc + cw_ref[j:j + 1, :] * pad_ref[s, off % SUBLANES, start:start + CF_ROW_CHUNK, :]
            y = _layer_norm_rows(acc, g_ref[...], b_ref[...])
            o_ref[s, c0:c0 + CF_ROW_CHUNK, :] = (y * jax.nn.sigmoid(y)).astype(o_ref.dtype)


def _conformer_call(cf3, conv_w, conv_b, ln_g, ln_b, layer):
    B, L, width = cf3.shape
    nb = _seqs_per_step(B, L)
    return pl.pallas_call(
        _conformer_kernel,
        grid=(B // nb,),
        in_specs=[
            pl.BlockSpec((nb, L, width), lambda b: (b, 0, 0)),
            _layer_spec((CF_CONV_K, CF_WIDTH), layer),
            _vec_spec("cf_conv_b", layer), _vec_spec("cf_ln_g", layer), _vec_spec("cf_ln_b", layer),
        ],
        out_specs=pl.BlockSpec((nb, L, CF_WIDTH), lambda b: (b, 0, 0)),
        out_shape=jax.ShapeDtypeStruct((B, L, CF_WIDTH), BF16),
        scratch_shapes=[pltpu.VMEM((nb, SUBLANES, L + 2 * CF_PAD, CF_WIDTH), F32)],
        compiler_params=_params(dimension_semantics=("arbitrary",)),
        name="conformer",
    )(cf3, conv_w, conv_b, ln_g, ln_b)


def _attn_kernel(*refs, lambda_init, past, n_cast):
    q_ref, k_ref, v_ref = refs[:3]
    n_in = 3
    if past:
        ck_ref, cv_ref = refs[3:5]
        k_all, v_aug = refs[-2:]
        n_in = 5
    lq1, lk1, lq2, lk2, g_ref = refs[n_in:n_in + 5]
    cast_in = refs[n_in + 5:n_in + 5 + n_cast]
    o_ref = refs[n_in + 5 + n_cast]
    cast_out = refs[n_in + 6 + n_cast:n_in + 6 + 2 * n_cast]
    _cast_chunks(cast_in, cast_out)
    nb, tq, _ = q_ref.shape

    if past:
        keys = v_aug.shape[1]

        @pl.when(pl.program_id(1) == 0)
        def _():
            ones = jnp.ones((keys, HEAD_PAIR), BF16)
            for s in range(nb):
                k_all[s, 0:past, :] = ck_ref[s, 0].reshape(ATT_WIDTH, past).T.astype(BF16)
                k_all[s, past:keys, :] = k_ref[s]
                v_old = cv_ref[s, 0].reshape(past, ATT_WIDTH).astype(BF16)
                for h in range(DIFF_HEADS):
                    cols = slice(h * HEAD_PAIR, (h + 1) * HEAD_PAIR)
                    base = 2 * h * HEAD_PAIR
                    v_aug[s, 0:past, base:base + HEAD_PAIR] = v_old[:, cols]
                    v_aug[s, past:keys, base:base + HEAD_PAIR] = v_ref[s, :, cols]
                    v_aug[s, :, base + HEAD_PAIR:base + 2 * HEAD_PAIR] = ones

    lam = (jnp.exp(jnp.sum(lq1[...] * lk1[...], axis=-1, keepdims=True))
           - jnp.exp(jnp.sum(lq2[...] * lk2[...], axis=-1, keepdims=True)) + lambda_init)
    lane = lax.broadcasted_iota(jnp.int32, (tq, HEAD_PAIR), 1)
    first = lane < DIFF_HEAD_DIM
    nt = (((1,), (1,)), ((), ()))
    gain = g_ref[...] * (1.0 - lambda_init)
    zero = jnp.zeros((), BF16)

    def scores(s, h):
        cols = slice(h * HEAD_PAIR, (h + 1) * HEAD_PAIR)
        q = q_ref[s, :, cols]
        q2 = jnp.concatenate([jnp.where(first, q, zero), jnp.where(first, zero, q)], axis=0)
        kh = k_all[s, :, cols] if past else k_ref[s, :, cols]
        return lax.dot_general(q2, kh, nt, preferred_element_type=F32)

    def normalize_store(s, h, o):
        o = o * lax.rsqrt(jnp.mean(o * o, axis=-1, keepdims=True) + LN_EPS) * gain
        o_ref[s, :, h * HEAD_PAIR:(h + 1) * HEAD_PAIR] = o.astype(o_ref.dtype)

    for s in range(nb):
        if past:
            def finish_long(h, sc):
                e = jnp.exp2(sc - jnp.max(sc, axis=-1, keepdims=True)).astype(BF16)
                p = _dot(e, v_aug[s, :, 2 * h * HEAD_PAIR:2 * (h + 1) * HEAD_PAIR])
                p0, p1 = p[:tq], p[tq:]
                normalize_store(s, h, p0[:, :HEAD_PAIR] * (1.0 / p0[:, HEAD_PAIR:HEAD_PAIR + 1])
                                - p1[:, :HEAD_PAIR] * (lam / p1[:, HEAD_PAIR:HEAD_PAIR + 1]))

            pending = None
            for h in range(DIFF_HEADS):
                sc = scores(s, h)
                if pending is not None:
                    finish_long(*pending)
                pending = (h, sc)
            finish_long(*pending)
        else:
            sc = jnp.concatenate([scores(s, h) for h in range(DIFF_HEADS)], axis=0)
            e = jnp.exp2(sc - jnp.max(sc, axis=-1, keepdims=True))
            r = 1.0 / jnp.sum(e, axis=-1, keepdims=True)
            for h in range(DIFF_HEADS):
                r0 = slice(2 * h * tq, (2 * h + 1) * tq)
                r1 = slice((2 * h + 1) * tq, (2 * h + 2) * tq)
                a = e[r0] * r[r0] - e[r1] * (lam * r[r1])
                normalize_store(s, h, _dot(a.astype(BF16), v_ref[s, :, h * HEAD_PAIR:(h + 1) * HEAD_PAIR]))


def _attn_call(q3, k3, v3, cache_k, cache_v, layer, vecs, cast=()):
    B, L, _ = q3.shape
    past = 0 if cache_k is None else cache_k.shape[-1]
    nb = _seqs_per_step(B, L)
    tq = min(Q_TILE, L)
    cast_in, cast_args, cast_out_specs, cast_out_shapes = _cast_plan(
        cast, (B // nb) * (L // tq), lambda b, i: b * (L // tq) + i)
    tile = pl.BlockSpec((nb, tq, ATT_WIDTH), lambda b, i: (b, i, 0))
    seq = pl.BlockSpec((nb, L, ATT_WIDTH), lambda b, i: (b, 0, 0))
    in_specs = [tile, seq, seq]
    args = [q3, k3, v3]
    scratch = []
    if past:
        for cache in (cache_k, cache_v):
            in_specs.append(pl.BlockSpec((nb, 1) + cache.shape[2:], lambda b, i: (b, layer, 0, 0, 0)))
            args.append(cache)
        scratch = [pltpu.VMEM((nb, past + L, ATT_WIDTH), BF16),
                   pltpu.VMEM((nb, past + L, 2 * ATT_WIDTH), BF16)]
    in_specs += [_vec_spec(n, layer) for n in ("lam_q1", "lam_k1", "lam_q2", "lam_k2", "subln_g")]
    args += [vecs] * 5
    return pl.pallas_call(
        functools.partial(_attn_kernel, lambda_init=_lambda_init(layer), past=past, n_cast=len(cast)),
        grid=(B // nb, L // tq),
        in_specs=in_specs + cast_in,
        out_specs=[tile] + cast_out_specs,
        out_shape=[jax.ShapeDtypeStruct((B, L, ATT_WIDTH), BF16)] + cast_out_shapes,
        scratch_shapes=scratch,
        compiler_params=_params(dimension_semantics=("arbitrary", "arbitrary")),
        name="diff_attn",
    )(*args, *cast_args)


def _mlp_kernel(x_ref, hy_ref, cf_ref, at_ref, ada_ref, wo_ref, w1_ref, w2_ref,
                g1_ref, b1_ref, g2_ref, b2_ref, o_ref):
    gate1 = ada_ref[0, 0, 2:3, :]
    sh2 = ada_ref[0, 0, 3:4, :]
    sc2 = ada_ref[0, 0, 4:5, :]
    gate2 = ada_ref[0, 0, 5:6, :]
    groups = [slice(r, r + MLP_GROUP_ROWS) for r in range(0, x_ref.shape[0], MLP_GROUP_ROWS)]
    ys = [_dot(jnp.concatenate([hy_ref[g, :], cf_ref[g, :], at_ref[g, :]], axis=1), wo_ref[...])
          for g in groups]
    xs, acts = [], []
    for g, y in zip(groups, ys):
        x = _layer_norm_rows(DEEPNORM_ALPHA * x_ref[g, :] + gate1 * y, g1_ref[...], b1_ref[...])
        h = (x * (1.0 + sc2) + sh2).astype(BF16)
        xs.append(x)
        acts.append(jnp.maximum(_dot(h, w1_ref[...]), 0.0))
    ms = [_dot((a * a).astype(BF16), w2_ref[...]) for a in acts]
    for g, x, m in zip(groups, xs, ms):
        o_ref[g, :] = _layer_norm_rows(DEEPNORM_ALPHA * x + gate2 * m, g2_ref[...], b2_ref[...])


def _mlp_call(x2d, y_hy, y_cf, y_at, ada4, wts, layer, cond_row_of_row):
    T = x2d.shape[0]
    rows = lambda w: pl.BlockSpec((MLP_ROW_TILE, w), lambda i: (i, 0))
    in_specs = [
        rows(D_MODEL), rows(HY_WIDTH), rows(CF_WIDTH), rows(ATT_WIDTH),
        pl.BlockSpec((1, 1, N_ADA, D_MODEL), lambda i: (layer, cond_row_of_row(i * MLP_ROW_TILE), 0, 0)),
        _layer_spec((MIX_WIDTH, D_MODEL), 0),
        _layer_spec((D_MODEL, D_FF), 0),
        _layer_spec((D_FF, D_MODEL), 0),
        _vec_spec("ln1_g", layer), _vec_spec("ln1_b", layer), _vec_spec("ln2_g", layer), _vec_spec("ln2_b", layer),
    ]
    args = [x2d, y_hy, y_cf, y_at, ada4, wts["w_out"][layer], wts["w_mlp1"][layer], wts["w_mlp2"][layer]]
    args += [wts["vecs"]] * 4
    return pl.pallas_call(
        _mlp_kernel,
        grid=(T // MLP_ROW_TILE,),
        in_specs=in_specs,
        out_specs=rows(D_MODEL),
        out_shape=jax.ShapeDtypeStruct((T, D_MODEL), F32),
        compiler_params=_params(dimension_semantics=("arbitrary",)),
        name="outproj_mlp",
    )(*args)


MLP_WEIGHTS = ("w_out", "w_mlp1", "w_mlp2")


def _layer(x2d, batch, seq, layer, ada4, wts, f32_weights, hy_tabs, cf_tabs, cond_row_of_row, rope_tabs,
           cache_kv, new_caches, casts):
    fused = seq <= FUSED_MIXER_MAX_SEQ and ROW_TILE % seq == 0
    z = _inproj_call(x2d, ada4, wts, layer, batch, seq, cond_row_of_row, rope_tabs, new_caches,
                     (hy_tabs, cf_tabs) if fused else None)
    as_seq = lambda a: a.reshape(batch, seq, a.shape[-1])
    flat = lambda a: a.reshape(batch * seq, a.shape[-1])
    if fused:
        y_hy, y_cf = z["yhy"], z["ycf"]
    else:
        vecs = wts["vecs"]
        y_hy = flat(_hyena_call(as_seq(z["hy"]), wts["hy_conv_w"], vecs, hy_tabs, vecs, layer))
        y_cf = flat(_conformer_call(as_seq(z["cf"]), wts["cf_conv_w"], vecs, vecs, vecs, layer))
    ck, cv = cache_kv if cache_kv is not None else (None, None)
    cast = []
    if casts:
        cast = [(n, f32_weights[n], layer) for n in MLP_WEIGHTS]
        if layer + 1 < DEPTH:
            cast.append(("w_in", f32_weights["w_in"], layer + 1))
    y_at, *cast_weights = _attn_call(as_seq(z["q"]), as_seq(z["k"]), as_seq(z["v"]), ck, cv, layer, wts["vecs"],
                                     [(w, l) for _, w, l in cast])
    for (name, _, _), w_bf in zip(cast, cast_weights):
        wts[name].append(w_bf)
    x_out = _mlp_call(x2d, y_hy, y_cf, flat(y_at), ada4, wts, layer, cond_row_of_row)
    return x_out, (z["kc"], z["vc"]) if "kc" in z else None


def kernel(x_prompt, x_sample, cache_k, cache_v, c, c_ctx, w_ada, b_ada, w_in, hy_conv_w, hy_conv_b, hf_w1, hf_b1, hf_freq, hf_w2, hf_b2, hf_w3, hy_bias, cf_conv_w, cf_conv_b, cf_ln_g, cf_ln_b, lam_q1, lam_k1, lam_q2, lam_k2, subln_g, w_out, ln1_g, ln1_b, w_mlp1, w_mlp2, ln2_g, ln2_b):
    batch, seq, _ = x_prompt.shape
    dec_batch, dec_seq, _ = x_sample.shape
    assert 1 + dec_batch <= COND_ROWS
    assert ROW_TILE % seq == 0 and (batch * seq) % ROW_TILE == 0
    assert dec_seq % ROW_TILE == 0 and dec_seq % Q_TILE == 0
    assert (batch * seq) % MLP_ROW_TILE == 0 and dec_seq % MLP_ROW_TILE == 0

    cond = jnp.concatenate(
        [c_ctx[None, :], c, jnp.zeros((COND_ROWS - 1 - dec_batch, D_MODEL), F32)], axis=0)
    ada4 = _ada_call(cond, w_ada, b_ada).reshape(DEPTH, COND_ROWS, N_ADA, D_MODEL)

    f32_weights = dict(w_in=w_in, w_out=w_out, w_mlp1=w_mlp1, w_mlp2=w_mlp2)
    wts = dict(w_in=[w_in[0:1].astype(BF16)], w_out=[], w_mlp1=[], w_mlp2=[])
    wts.update(
        hy_conv_w=hy_conv_w, cf_conv_w=cf_conv_w,
        vecs=_pack_vectors(dict(
            ln1_g=ln1_g, ln1_b=ln1_b, ln2_g=ln2_g, ln2_b=ln2_b, hy_conv_b=hy_conv_b, hy_bias=hy_bias,
            cf_conv_b=cf_conv_b, cf_ln_g=cf_ln_g, cf_ln_b=cf_ln_b, subln_g=subln_g,
            lam_q1=lam_q1, lam_k1=lam_k1, lam_q2=lam_q2, lam_k2=lam_k2)))

    rope_tabs = tuple(jnp.asarray(t) for t in _rope_tables(dec_seq))
    cache_kt = jnp.transpose(cache_k, (0, 1, 3, 4, 2))

    xp = x_prompt.reshape(batch * seq, D_MODEL)
    xs = x_sample.reshape(dec_batch * dec_seq, D_MODEL)
    new_caches = ()
    filt = (hf_w1, hf_b1, hf_freq, hf_w2, hf_b2, hf_w3)
    hy_ctx = _filter_call(seq, *filt)
    hy_dec = _filter_call(dec_seq, *filt)
    fused = lambda n: n <= FUSED_MIXER_MAX_SEQ and ROW_TILE % n == 0
    cf_ctx = _conv_filter_call(seq, cf_conv_w) if fused(seq) else None
    cf_dec = _conv_filter_call(dec_seq, cf_conv_w) if fused(dec_seq) else None
    for l in range(DEPTH):
        xp, new_caches = _layer(xp, batch, seq, l, ada4, wts, f32_weights, hy_ctx, cf_ctx, lambda row: 0,
                                None, None, new_caches, True)
        xs, _ = _layer(xs, dec_batch, dec_seq, l, ada4, wts, f32_weights, hy_dec, cf_dec,
                       lambda row: 1 + row // dec_seq, rope_tabs, (cache_kt, cache_v), None, False)
    new_cache_kt, new_cache_v = new_caches
    new_cache_k = jnp.transpose(new_cache_kt, (0, 1, 4, 2, 3))
    return (xp.reshape(batch, seq, D_MODEL), xs.reshape(dec_batch, dec_seq, D_MODEL),
            new_cache_k, new_cache_v)
```
